```python
import jax, jax.numpy as jnp
from jax import lax
import numpy as np

D_MODEL = 1024
BATCH = 8
SEQ = 2048
DEPTH = 1

CTX_LEN = 256
GRID_W = 64
FOURIER_GROUPS = 4
FOURIER_GROUP_DIM = 128
FOURIER_WIDTH = FOURIER_GROUPS * FOURIER_GROUP_DIM
MLA_HEADS = 8
QK_NOPE_DIM = 64
QK_ROPE_DIM = 32
V_HEAD_DIM = 64
Q_LORA_RANK = 256
KV_LORA_RANK = 128
MIX_WIDTH = FOURIER_WIDTH + MLA_HEADS * V_HEAD_DIM
KV_COL = FOURIER_WIDTH + Q_LORA_RANK
IN_WIDTH = FOURIER_WIDTH + Q_LORA_RANK + KV_LORA_RANK + QK_ROPE_DIM
D_FF = -(-8 * D_MODEL // (3 * 256)) * 256
ROPE_BASE = 10000.0
NORM_EPS = 1e-6
Q_BLOCK = 128

kernel_name = "hybrid_fnet_mla_dit_block"


def rms_norm(x, g):
    x32 = x.astype(jnp.float32)
    y = x32 * lax.rsqrt(jnp.mean(x32 * x32, axis=-1, keepdims=True) + NORM_EPS)
    return (y * g.astype(jnp.float32)).astype(x.dtype)


def modulate(h, shift, scale):
    return h * (1.0 + scale) + shift


def _rotate(x, cos, sin):
    half = x.shape[-1] // 2
    x1, x2 = x[..., :half], x[..., half:]
    return jnp.concatenate([x1 * cos - x2 * sin, x2 * cos + x1 * sin], axis=-1)


def axial_rope(x, row, col):
    axis_dim = QK_ROPE_DIM // 2
    half = axis_dim // 2
    inv_freq = ROPE_BASE ** (-jnp.arange(half, dtype=jnp.float32) / half)
    shape = (row.shape[0],) + (1,) * (x.ndim - 3) + (half,)

    def tables(pos):
        ang = pos.astype(jnp.float32)[:, None] * inv_freq[None, :]
        return (jnp.cos(ang).reshape(shape).astype(x.dtype),
                jnp.sin(ang).reshape(shape).astype(x.dtype))

    xr, xc = x[..., :axis_dim], x[..., axis_dim:]
    return jnp.concatenate([_rotate(xr, *tables(row)), _rotate(xc, *tables(col))], axis=-1)


def mla_queries(q_a, g_q_a, w_q_b):
    B, T, _ = q_a.shape
    q = (rms_norm(q_a, g_q_a) @ w_q_b).reshape(B, T, MLA_HEADS, QK_NOPE_DIM + QK_ROPE_DIM)
    return q[..., :QK_NOPE_DIM], q[..., QK_NOPE_DIM:]


def mla_keys_values(kv_a, g_kv_a, w_kv_b):
    B, T, _ = kv_a.shape
    kv = (rms_norm(kv_a, g_kv_a) @ w_kv_b).reshape(B, T, MLA_HEADS, QK_NOPE_DIM + V_HEAD_DIM)
    return kv[..., :QK_NOPE_DIM], kv[..., QK_NOPE_DIM:]


def mla_attention(q_nope, q_rope, k_nope, k_rope, v):
    B, T, H, _ = q_nope.shape
    nb = T // Q_BLOCK
    scale = (QK_NOPE_DIM + QK_ROPE_DIM) ** -0.5
    qn = q_nope.reshape(B, nb, Q_BLOCK, H, QK_NOPE_DIM).transpose(1, 0, 2, 3, 4)
    qr = q_rope.reshape(B, nb, Q_BLOCK, H, QK_ROPE_DIM).transpose(1, 0, 2, 3, 4)

    def block(args):
        qn_b, qr_b = args
        s = (jnp.einsum('bqhd,bkhd->bhqk', qn_b, k_nope, preferred_element_type=jnp.float32)
             + jnp.einsum('bqhr,bkr->bhqk', qr_b, k_rope, preferred_element_type=jnp.float32))
        p = jax.nn.softmax(s * scale, axis=-1).astype(v.dtype)
        return jnp.einsum('bhqk,bkhd->bqhd', p, v)

    out = lax.map(block, (qn, qr))
    return out.transpose(1, 0, 2, 3, 4).reshape(B, T, H * V_HEAD_DIM)


def fourier_mix(u, w_fourier):
    B, T, _ = u.shape
    ug = u.reshape(B, T, FOURIER_GROUPS, FOURIER_GROUP_DIM).astype(jnp.float32)
    f = jnp.fft.fft2(ug, axes=(1, 3), norm="ortho").real.astype(u.dtype)
    return jnp.einsum('btgc,gcd->btgd', f, w_fourier).reshape(B, T, FOURIER_WIDTH)


def swiglu(h, w_gate, w_up, w_down):
    return (jax.nn.silu(h @ w_gate) * (h @ w_up)) @ w_down


def setup_inputs(seed: int = 0) -> dict:
    key = jax.random.key(seed)
    ks = jax.random.split(key, 20)
    f32 = jnp.float32

    def w(k, shape, fan_in):
        return jax.random.normal(k, shape, f32) * fan_in ** -0.5

    def gain(k, shape):
        return 1.0 + 0.1 * jax.random.normal(k, shape, f32)

    return {
        "x": jax.random.normal(ks[0], (BATCH, SEQ, D_MODEL), f32),
        "c": jax.random.normal(ks[1], (BATCH, D_MODEL), f32),
        "ctx": jax.random.normal(ks[2], (BATCH, CTX_LEN, D_MODEL), f32),
        "c_ctx": jax.random.normal(ks[3], (D_MODEL,), f32),
        "w_ada": w(ks[4], (DEPTH, D_MODEL, 6 * D_MODEL), D_MODEL),
        "b_ada": 0.02 * jax.random.normal(ks[5], (DEPTH, 6 * D_MODEL), f32),
        "g_pre_mix": gain(ks[6], (DEPTH, D_MODEL)),
        "g_post_mix": gain(ks[7], (DEPTH, D_MODEL)),
        "g_pre_ffn": gain(ks[8], (DEPTH, D_MODEL)),
        "g_post_ffn": gain(ks[9], (DEPTH, D_MODEL)),
        "w_in": w(ks[10], (DEPTH, D_MODEL, IN_WIDTH), D_MODEL),
        "g_q_a": gain(ks[11], (DEPTH, Q_LORA_RANK)),
        "w_q_b": w(ks[12], (DEPTH, Q_LORA_RANK, MLA_HEADS * (QK_NOPE_DIM + QK_ROPE_DIM)), Q_LORA_RANK),
        "g_kv_a": gain(ks[13], (DEPTH, KV_LORA_RANK)),
        "w_kv_b": w(ks[14], (DEPTH, KV_LORA_RANK, MLA_HEADS * (QK_NOPE_DIM + V_HEAD_DIM)), KV_LORA_RANK),
        "w_fourier": w(ks[15], (DEPTH, FOURIER_GROUPS, FOURIER_GROUP_DIM, FOURIER_GROUP_DIM), FOURIER_GROUP_DIM),
        "w_out": w(ks[16], (DEPTH, MIX_WIDTH, D_MODEL), MIX_WIDTH),
        "w_gate": w(ks[17], (DEPTH, D_MODEL, D_FF), D_MODEL),
        "w_up": w(ks[18], (DEPTH, D_MODEL, D_FF), D_MODEL),
        "w_down": w(ks[19], (DEPTH, D_FF, D_MODEL), D_FF),
    }


def reference(x, c, ctx, c_ctx, w_ada, b_ada, g_pre_mix, g_post_mix, g_pre_ffn, g_post_ffn,
              w_in, g_q_a, w_q_b, g_kv_a, w_kv_b, w_fourier, w_out, w_gate, w_up, w_down):
    n_lat = x.shape[1]
    ROWS = n_lat // GRID_W
    row = jnp.repeat(jnp.arange(ROWS, dtype=jnp.int32), GRID_W)
    col = jnp.tile(jnp.arange(GRID_W, dtype=jnp.int32), ROWS)

    for l in range(DEPTH):
        mod = jax.nn.silu(c) @ w_ada[l] + b_ada[l]
        sh_m, sc_m, gt_m, sh_f, sc_f, gt_f = jnp.split(mod[:, None, :], 6, axis=-1)
        mod_c = jax.nn.silu(c_ctx) @ w_ada[l] + b_ada[l]
        csh_m, csc_m, cgt_m, csh_f, csc_f, cgt_f = jnp.split(mod_c, 6, axis=-1)

        h_lat = modulate(rms_norm(x, g_pre_mix[l]), sh_m, sc_m)
        p_lat = h_lat @ w_in[l]
        u_f, q_a, kv_a, kr_raw = jnp.split(
            p_lat, [FOURIER_WIDTH, KV_COL, KV_COL + KV_LORA_RANK], axis=-1)

        h_ctx = modulate(rms_norm(ctx, g_pre_mix[l]), csh_m, csc_m)
        p_ctx_kv = h_ctx @ w_in[l][:, KV_COL:]
        kv_a_c, kr_c = p_ctx_kv[..., :KV_LORA_RANK], p_ctx_kv[..., KV_LORA_RANK:]

        kn_l, v_l = mla_keys_values(kv_a, g_kv_a[l], w_kv_b[l])
        kr_l = axial_rope(kr_raw, row, col)
        kn_c, v_c = mla_keys_values(kv_a_c, g_kv_a[l], w_kv_b[l])
        k_nope = jnp.concatenate([kn_c, kn_l], axis=1)
        k_rope = jnp.concatenate([kr_c, kr_l], axis=1)
        v_all = jnp.concatenate([v_c, v_l], axis=1)

        qn_l, qr_l = mla_queries(q_a, g_q_a[l], w_q_b[l])
        qr_l = axial_rope(qr_l, row, col)
        attn_l = mla_attention(qn_l, qr_l, k_nope, k_rope, v_all)
        four_l = fourier_mix(u_f, w_fourier[l])
        y_lat = jnp.concatenate([four_l, attn_l], axis=-1) @ w_out[l]

        if l + 1 < DEPTH:
            p_ctx_fq = h_ctx @ w_in[l][:, :KV_COL]
            u_f_c, q_a_c = p_ctx_fq[..., :FOURIER_WIDTH], p_ctx_fq[..., FOURIER_WIDTH:]
            qn_c, qr_c = mla_queries(q_a_c, g_q_a[l], w_q_b[l])
            attn_c = mla_attention(qn_c, qr_c, kn_c, kr_c, v_c)
            four_c = fourier_mix(u_f_c, w_fourier[l])
            y_ctx = jnp.concatenate([four_c, attn_c], axis=-1) @ w_out[l]
            ctx = ctx + cgt_m * rms_norm(y_ctx, g_post_mix[l])
            h2c = modulate(rms_norm(ctx, g_pre_ffn[l]), csh_f, csc_f)
            ctx = ctx + cgt_f * rms_norm(swiglu(h2c, w_gate[l], w_up[l], w_down[l]), g_post_ffn[l])

        x = x + gt_m * rms_norm(y_lat, g_post_mix[l])
        h2 = modulate(rms_norm(x, g_pre_ffn[l]), sh_f, sc_f)
        x = x + gt_f * rms_norm(swiglu(h2, w_gate[l], w_up[l], w_down[l]), g_post_ffn[l])
    return x
```

```python
import functools

import numpy as np
import jax
import jax.numpy as jnp
from jax import lax
from jax.experimental import pallas as pl
from jax.experimental.pallas import tpu as pltpu

F32 = jnp.float32
BF16 = jnp.bfloat16

D_MODEL = 1024
GRID_W = 64
FOURIER_GROUPS = 4
FOURIER_GROUP_DIM = 128
FOURIER_WIDTH = FOURIER_GROUPS * FOURIER_GROUP_DIM
MLA_HEADS = 8
QK_NOPE_DIM = 64
QK_ROPE_DIM = 32
V_HEAD_DIM = 64
Q_LORA_RANK = 256
KV_LORA_RANK = 128
KV_COL = FOURIER_WIDTH + Q_LORA_RANK
ROPE_COL = KV_COL + KV_LORA_RANK
ROPE_BASE = 10000.0
NORM_EPS = 1e-6
HEAD_SLOT = 128
ATTN_WIDTH = MLA_HEADS * V_HEAD_DIM
QK_WIDTH = MLA_HEADS * HEAD_SLOT

VMEM_LIMIT_BYTES = 56 * 1024 * 1024


def _rms(x, g):
    return x * lax.rsqrt(jnp.mean(x * x, axis=-1, keepdims=True) + NORM_EPS) * g


def _dot(a, b):
    return jnp.dot(a, b, preferred_element_type=F32)


def _dot_nt(a, b):
    return lax.dot_general(a, b, (((1,), (1,)), ((), ())), preferred_element_type=F32)


def _adaln_kernel(c_ref, w_ref, b_ref, o_ref):
    c = c_ref[...]
    a = c / (1.0 + jnp.exp(-c))
    o_ref[...] = _dot(a.astype(BF16), w_ref[...].astype(BF16)) + b_ref[...]


def _adaln(cc, w_ada, b_ada, tn=512):
    rows, d = cc.shape
    n = w_ada.shape[1]
    return pl.pallas_call(
        _adaln_kernel,
        grid=(n // tn,),
        in_specs=[
            pl.BlockSpec((rows, d), lambda j: (0, 0)),
            pl.BlockSpec((d, tn), lambda j: (0, j)),
            pl.BlockSpec((1, tn), lambda j: (0, j)),
        ],
        out_specs=pl.BlockSpec((rows, tn), lambda j: (0, j)),
        out_shape=jax.ShapeDtypeStruct((rows, n), F32),
        name="adaln",
    )(cc, w_ada, b_ada)


def _premix_kernel(x_ref, mod_ref, gpre_ref, win_ref, gq_ref, wq_ref, gkv_ref, wkv_ref,
                   cosq_ref, sinq_ref, cosk_ref, sink_ref, u_ref, q_ref, k_ref, v_ref):
    x = x_ref[0]
    shift = mod_ref[0, 0:1, :]
    scale = mod_ref[0, 1:2, :]
    h = _rms(x, gpre_ref[...]) * (1.0 + scale) + shift
    p = _dot(h.astype(BF16), win_ref[...])
    u_ref[0] = p[:, :FOURIER_WIDTH].astype(BF16)

    qn = _rms(p[:, FOURIER_WIDTH:KV_COL], gq_ref[...]).astype(BF16)
    qq = _dot(qn, wq_ref[...])
    cosq = cosq_ref[...]
    sinq = sinq_ref[...]
    for hd in range(MLA_HEADS):
        lo = hd * HEAD_SLOT
        q_ref[0, :, lo:lo + HEAD_SLOT] = (
            qq[:, lo:lo + HEAD_SLOT] * cosq + qq[:, QK_WIDTH + lo:QK_WIDTH + lo + HEAD_SLOT] * sinq
        ).astype(BF16)

    kvn = _rms(p[:, KV_COL:ROPE_COL], gkv_ref[...]).astype(BF16)
    kv = _dot(kvn, wkv_ref[...])
    kr = (p[:, ROPE_COL:ROPE_COL + HEAD_SLOT] * cosk_ref[...]
          + p[:, ROPE_COL + HEAD_SLOT:ROPE_COL + 2 * HEAD_SLOT] * sink_ref[...])
    for hd in range(MLA_HEADS):
        lo = hd * HEAD_SLOT
        k_ref[0, :, lo:lo + HEAD_SLOT] = (kv[:, lo:lo + HEAD_SLOT] + kr).astype(BF16)
    v_ref[0] = kv[:, QK_WIDTH:].astype(BF16)


def _premix(x, mod3, gpre, win, gq, wq, gkv, wkv, cosq, sinq, cosk, sink, tm=512):
    b, s, d = x.shape
    const = lambda shape: pl.BlockSpec(shape, lambda i, j: (0,) * len(shape))
    rows = lambda w: pl.BlockSpec((1, tm, w), lambda i, j: (i, j, 0))
    table = pl.BlockSpec((tm, HEAD_SLOT), lambda i, j: (j, 0))
    return pl.pallas_call(
        _premix_kernel,
        grid=(b, s // tm),
        in_specs=[
            rows(d),
            pl.BlockSpec((1, 6, d), lambda i, j: (i, 0, 0)),
            const(gpre.shape), const(win.shape), const(gq.shape), const(wq.shape),
            const(gkv.shape), const(wkv.shape),
            table, table, table, table,
        ],
        out_specs=[rows(FOURIER_WIDTH), rows(QK_WIDTH), rows(QK_WIDTH), rows(ATTN_WIDTH)],
        out_shape=[
            jax.ShapeDtypeStruct((b, s, FOURIER_WIDTH), BF16),
            jax.ShapeDtypeStruct((b, s, QK_WIDTH), BF16),
            jax.ShapeDtypeStruct((b, s, QK_WIDTH), BF16),
            jax.ShapeDtypeStruct((b, s, ATTN_WIDTH), BF16),
        ],
        compiler_params=pltpu.CompilerParams(vmem_limit_bytes=VMEM_LIMIT_BYTES),
        name="premix",
    )(x, mod3, gpre, win, gq, wq, gkv, wkv, cosq, sinq, cosk, sink)


def _ctxkv_kernel(x_ref, mod_ref, gpre_ref, win_ref, gkv_ref, wkv_ref, k_ref, v_ref):
    x = x_ref[0]
    shift = mod_ref[0, 0:1, :]
    scale = mod_ref[0, 1:2, :]
    h = _rms(x, gpre_ref[...]) * (1.0 + scale) + shift
    p = _dot(h.astype(BF16), win_ref[...])
    kvn = _rms(p[:, :KV_LORA_RANK], gkv_ref[...]).astype(BF16)
    kv = _dot(kvn, wkv_ref[...])
    kr = p[:, KV_LORA_RANK:]
    for hd in range(MLA_HEADS):
        lo = hd * HEAD_SLOT
        k_ref[0, :, lo:lo + HEAD_SLOT] = (kv[:, lo:lo + HEAD_SLOT] + kr).astype(BF16)
    v_ref[0] = kv[:, QK_WIDTH:].astype(BF16)


def _ctxkv(ctx, mod3, ctx_row, gpre, win_c, gkv, wkv):
    b, c, d = ctx.shape
    const = lambda shape: pl.BlockSpec(shape, lambda i: (0,) * len(shape))
    rows = lambda w: pl.BlockSpec((1, c, w), lambda i: (i, 0, 0))
    return pl.pallas_call(
        _ctxkv_kernel,
        grid=(b,),
        in_specs=[
            rows(d),
            pl.BlockSpec((1, 6, d), lambda i: (ctx_row, 0, 0)),
            const(gpre.shape), const(win_c.shape), const(gkv.shape), const(wkv.shape),
        ],
        out_specs=[rows(QK_WIDTH), rows(ATTN_WIDTH)],
        out_shape=[
            jax.ShapeDtypeStruct((b, c, QK_WIDTH), BF16),
            jax.ShapeDtypeStruct((b, c, ATTN_WIDTH), BF16),
        ],
        name="ctxkv",
    )(ctx, mod3, gpre, win_c, gkv, wkv)


def _attn_kernel(q_ref, kc_ref, kl_ref, vc_ref, vl_ref, o_ref):
    tq = q_ref.shape[1]
    first_half = lax.broadcasted_iota(jnp.int32, (tq, 2 * V_HEAD_DIM), 1) < V_HEAD_DIM
    for pair in range(MLA_HEADS // 2):
        vlo = pair * 2 * V_HEAD_DIM
        vc = vc_ref[0, :, vlo:vlo + 2 * V_HEAD_DIM]
        vl = vl_ref[0, :, vlo:vlo + 2 * V_HEAD_DIM]
        outs = []
        for hd in (2 * pair, 2 * pair + 1):
            lo = hd * HEAD_SLOT
            qh = q_ref[0, :, lo:lo + HEAD_SLOT]
            s_c = _dot_nt(qh, kc_ref[0, :, lo:lo + HEAD_SLOT])
            s_l = _dot_nt(qh, kl_ref[0, :, lo:lo + HEAD_SLOT])
            m = jnp.maximum(jnp.max(s_c, axis=-1, keepdims=True), jnp.max(s_l, axis=-1, keepdims=True))
            p_c = jnp.exp2(s_c - m)
            p_l = jnp.exp2(s_l - m)
            denom = jnp.sum(p_c, axis=-1, keepdims=True) + jnp.sum(p_l, axis=-1, keepdims=True)
            o = _dot(p_c.astype(BF16), vc) + _dot(p_l.astype(BF16), vl)
            outs.append(o / denom)
        o_ref[0, :, vlo:vlo + 2 * V_HEAD_DIM] = jnp.where(first_half, outs[0], outs[1]).astype(BF16)


def _attention(q, kc, kl, vc, vl, tq=512):
    b, s, _ = q.shape
    c = kc.shape[1]
    return pl.pallas_call(
        _attn_kernel,
        grid=(b, s // tq),
        in_specs=[
            pl.BlockSpec((1, tq, QK_WIDTH), lambda i, j: (i, j, 0)),
            pl.BlockSpec((1, c, QK_WIDTH), lambda i, j: (i, 0, 0)),
            pl.BlockSpec((1, s, QK_WIDTH), lambda i, j: (i, 0, 0)),
            pl.BlockSpec((1, c, ATTN_WIDTH), lambda i, j: (i, 0, 0)),
            pl.BlockSpec((1, s, ATTN_WIDTH), lambda i, j: (i, 0, 0)),
        ],
        out_specs=pl.BlockSpec((1, tq, ATTN_WIDTH), lambda i, j: (i, j, 0)),
        out_shape=jax.ShapeDtypeStruct((b, s, ATTN_WIDTH), BF16),
        compiler_params=pltpu.CompilerParams(vmem_limit_bytes=VMEM_LIMIT_BYTES),
        name="attn",
    )(q, kc, kl, vc, vl)


def _fourier_kernel(f_ref, u_ref, cc_ref, wf_ref, o_ref, fb_ref):
    @pl.when(pl.program_id(1) == 0)
    def _():
        fb_ref[...] = f_ref[...].astype(BF16)

    u = u_ref[0]
    pr = _dot(fb_ref[0], u).astype(BF16)
    pi = _dot(fb_ref[1], u).astype(BF16)
    cc = cc_ref[...].astype(BF16)
    for g in range(FOURIER_GROUPS):
        lo = g * FOURIER_GROUP_DIM
        hi = lo + FOURIER_GROUP_DIM
        f = _dot(pr[:, lo:hi], cc[0]) + _dot(pi[:, lo:hi], cc[1])
        o_ref[0, :, lo:hi] = _dot(f.astype(BF16), wf_ref[g]).astype(BF16)


def _fourier(u, fmat, cmat, wf, tm=512):
    b, s, w = u.shape
    return pl.pallas_call(
        _fourier_kernel,
        grid=(s // tm, b),
        in_specs=[
            pl.BlockSpec((2, tm, s), lambda j, i: (0, j, 0)),
            pl.BlockSpec((1, s, w), lambda j, i: (i, 0, 0)),
            pl.BlockSpec(cmat.shape, lambda j, i: (0, 0, 0)),
            pl.BlockSpec(wf.shape, lambda j, i: (0, 0, 0)),
        ],
        out_specs=pl.BlockSpec((1, tm, w), lambda j, i: (i, j, 0)),
        out_shape=jax.ShapeDtypeStruct((b, s, w), BF16),
        scratch_shapes=[pltpu.VMEM((2, tm, s), BF16)],
        compiler_params=pltpu.CompilerParams(
            dimension_semantics=("arbitrary", "arbitrary"), vmem_limit_bytes=VMEM_LIMIT_BYTES),
        name="fourier",
    )(fmat, u, cmat, wf)


def _post_kernel(x_ref, four_ref, attn_ref, mod_ref, gpm_ref, gpf_ref, gqf_ref,
                 wof_ref, woa_ref, wg_ref, wu_ref, wd_ref, o_ref):
    x = x_ref[0]
    gt_m = mod_ref[0, 2:3, :]
    sh_f = mod_ref[0, 3:4, :]
    sc_f = mod_ref[0, 4:5, :]
    gt_f = mod_ref[0, 5:6, :]
    y = _dot(four_ref[0], wof_ref[...]) + _dot(attn_ref[0], woa_ref[...])
    x1 = x + gt_m * _rms(y, gpm_ref[...])
    h2 = (_rms(x1, gpf_ref[...]) * (1.0 + sc_f) + sh_f).astype(BF16)
    g = _dot(h2, wg_ref[...])
    up = _dot(h2, wu_ref[...])
    act = (g / (1.0 + jnp.exp(-g)) * up).astype(BF16)
    ffn = _dot(act, wd_ref[...])
    o_ref[0] = x1 + gt_f * _rms(ffn, gqf_ref[...])


def _post(x, four, attn, mod3, gpm, gpf, gqf, wof, woa, wg, wu, wd, tm=256):
    b, s, d = x.shape
    const = lambda shape: pl.BlockSpec(shape, lambda i, j: (0,) * len(shape),
                                       pipeline_mode=pl.Buffered(1))
    rows = lambda w: pl.BlockSpec((1, tm, w), lambda i, j: (i, j, 0))
    return pl.pallas_call(
        _post_kernel,
        grid=(b, s // tm),
        in_specs=[
            rows(d), rows(FOURIER_WIDTH), rows(ATTN_WIDTH),
            pl.BlockSpec((1, 6, d), lambda i, j: (i, 0, 0)),
            const(gpm.shape), const(gpf.shape), const(gqf.shape),
            const(wof.shape), const(woa.shape), const(wg.shape), const(wu.shape), const(wd.shape),
        ],
        out_specs=rows(d),
        out_shape=jax.ShapeDtypeStruct((b, s, d), F32),
        compiler_params=pltpu.CompilerParams(vmem_limit_bytes=VMEM_LIMIT_BYTES),
        name="post",
    )(x, four, attn, mod3, gpm, gpf, gqf, wof, woa, wg, wu, wd)


def _rope_rotate_cols(w):
    a = QK_ROPE_DIM // 2
    hf = a // 2
    blocks = []
    for s0 in (0, a):
        blocks += [-w[..., s0 + hf:s0 + a], w[..., s0:s0 + hf]]
    return jnp.concatenate(blocks, axis=-1)


def _head_slot(nope, rope):
    pad = HEAD_SLOT - QK_NOPE_DIM - QK_ROPE_DIM
    return jnp.concatenate([nope, rope, jnp.zeros(rope.shape[:-1] + (pad,), rope.dtype)], axis=-1)


def _rope_tables(n_lat, q_scale):
    t = jnp.arange(n_lat, dtype=jnp.int32)
    row, col = t // GRID_W, t % GRID_W
    hf = QK_ROPE_DIM // 4
    inv_freq = ROPE_BASE ** (-jnp.arange(hf, dtype=F32) / hf)
    ar = row.astype(F32)[:, None] * inv_freq[None, :]
    ac = col.astype(F32)[:, None] * inv_freq[None, :]
    cos32 = jnp.concatenate([jnp.cos(ar), jnp.cos(ar), jnp.cos(ac), jnp.cos(ac)], axis=-1)
    sin32 = jnp.concatenate([jnp.sin(ar), jnp.sin(ar), jnp.sin(ac), jnp.sin(ac)], axis=-1)
    ones = jnp.ones((n_lat, QK_NOPE_DIM), F32)
    zeros = jnp.zeros((n_lat, QK_NOPE_DIM), F32)
    cosq = _head_slot(ones, cos32) * q_scale
    sinq = _head_slot(zeros, sin32) * q_scale
    cosk = _head_slot(zeros, cos32)
    sink = _head_slot(zeros, sin32)
    return cosq, sinq, cosk, sink


def _dft_tables(n_pos, n_ch):
    k = np.arange(n_pos, dtype=np.int64)
    ang = 2.0 * np.pi * ((k[:, None] * k[None, :]) % n_pos) / n_pos
    fmat = np.stack([np.cos(ang), np.sin(ang)]).astype(np.float32)
    c = np.arange(n_ch, dtype=np.int64)
    angc = 2.0 * np.pi * ((c[:, None] * c[None, :]) % n_ch) / n_ch
    norm = 1.0 / np.sqrt(float(n_pos * n_ch))
    cmat = np.stack([np.cos(angc) * norm, -np.sin(angc) * norm]).astype(np.float32)
    return jnp.asarray(fmat), jnp.asarray(cmat)


def kernel(x, c, ctx, c_ctx, w_ada, b_ada, g_pre_mix, g_post_mix, g_pre_ffn, g_post_ffn, w_in, g_q_a,
           w_q_b, g_kv_a, w_kv_b, w_fourier, w_out, w_gate, w_up, w_down):
    assert w_ada.shape[0] == 1, "single-layer block"
    batch, n_lat, d = x.shape

    mod_rows = -(-(batch + 1) // 8) * 8
    cc = jnp.concatenate([c, c_ctx[None, :], jnp.zeros((mod_rows - batch - 1, d), F32)], axis=0)
    mod = _adaln(cc, w_ada[0], b_ada[0][None, :])
    mod3 = mod.reshape(mod_rows, 6, d)

    w_in0 = w_in[0]
    w_kr = w_in0[:, ROPE_COL:]
    zeros_d = jnp.zeros((d, QK_NOPE_DIM), F32)
    kr_slot = _head_slot(zeros_d, w_kr)
    kr_rot_slot = _head_slot(zeros_d, _rope_rotate_cols(w_kr))
    win = jnp.concatenate([w_in0[:, :ROPE_COL], kr_slot, kr_rot_slot], axis=1).astype(BF16)
    win_c = jnp.concatenate([w_in0[:, KV_COL:ROPE_COL], kr_slot], axis=1).astype(BF16)

    wq3 = w_q_b[0].reshape(Q_LORA_RANK, MLA_HEADS, QK_NOPE_DIM + QK_ROPE_DIM)
    wq_nope, wq_rope = wq3[..., :QK_NOPE_DIM], wq3[..., QK_NOPE_DIM:]
    wq_a = _head_slot(wq_nope, wq_rope).reshape(Q_LORA_RANK, QK_WIDTH)
    wq_b = _head_slot(jnp.zeros_like(wq_nope), _rope_rotate_cols(wq_rope)).reshape(Q_LORA_RANK, QK_WIDTH)
    wq = jnp.concatenate([wq_a, wq_b], axis=1).astype(BF16)

    wkv3 = w_kv_b[0].reshape(KV_LORA_RANK, MLA_HEADS, QK_NOPE_DIM + V_HEAD_DIM)
    wk_nope, wv = wkv3[..., :QK_NOPE_DIM], wkv3[..., QK_NOPE_DIM:]
    wk_slots = _head_slot(wk_nope, jnp.zeros(wk_nope.shape[:-1] + (QK_ROPE_DIM,), F32))
    wkv = jnp.concatenate([wk_slots.reshape(KV_LORA_RANK, QK_WIDTH),
                           wv.reshape(KV_LORA_RANK, ATTN_WIDTH)], axis=1).astype(BF16)

    q_scale = float((QK_NOPE_DIM + QK_ROPE_DIM) ** -0.5 * np.log2(np.e))
    cosq, sinq, cosk, sink = _rope_tables(n_lat, q_scale)
    fmat, cmat = _dft_tables(n_lat, FOURIER_GROUP_DIM)

    row2 = lambda g: g[0][None, :]
    u_f, q, k_lat, v_lat = _premix(x, mod3, row2(g_pre_mix), win, row2(g_q_a), wq, row2(g_kv_a), wkv,
                                   cosq, sinq, cosk, sink)
    k_ctx, v_ctx = _ctxkv(ctx, mod3, batch, row2(g_pre_mix), win_c, row2(g_kv_a), wkv)
    attn = _attention(q, k_ctx, k_lat, v_ctx, v_lat)
    four = _fourier(u_f, fmat, cmat, w_fourier[0].astype(BF16))
    w_out0 = w_out[0].astype(BF16)
    return _post(x, four, attn, mod3, row2(g_post_mix), row2(g_pre_ffn), row2(g_post_ffn),
                 w_out0[:FOURIER_WIDTH], w_out0[FOURIER_WIDTH:],
                 w_gate[0].astype(BF16), w_up[0].astype(BF16), w_down[0].astype(BF16))
```

```python
import functools

import numpy as np
import jax
import jax.numpy as jnp
from jax import lax
from jax.experimental import pallas as pl
from jax.experimental.pallas import tpu as pltpu

F32 = jnp.float32
BF16 = jnp.bfloat16

D_MODEL = 1024
GRID_W = 64
FOURIER_GROUPS = 4
FOURIER_GROUP_DIM = 128
FOURIER_WIDTH = FOURIER_GROUPS * FOURIER_GROUP_DIM
MLA_HEADS = 8
QK_NOPE_DIM = 64
QK_ROPE_DIM = 32
V_HEAD_DIM = 64
Q_LORA_RANK = 256
KV_LORA_RANK = 128
KV_COL = FOURIER_WIDTH + Q_LORA_RANK
ROPE_COL = KV_COL + KV_LORA_RANK
ROPE_BASE = 10000.0
NORM_EPS = 1e-6
FFT_RADIX = 4
HEAD_SLOT = 128
ATTN_WIDTH = MLA_HEADS * V_HEAD_DIM
QK_WIDTH = MLA_HEADS * HEAD_SLOT

VMEM_LIMIT_BYTES = 56 * 1024 * 1024


def _rms(x, g):
    return x * lax.rsqrt(jnp.mean(x * x, axis=-1, keepdims=True) + NORM_EPS) * g


def _dot(a, b):
    return jnp.dot(a, b, preferred_element_type=F32)


def _dot_nt(a, b):
    return lax.dot_general(a, b, (((1,), (1,)), ((), ())), preferred_element_type=F32)


def _adaln_kernel(c_ref, w_ref, b_ref, o_ref):
    c = c_ref[...]
    a = c / (1.0 + jnp.exp(-c))
    o_ref[...] = _dot(a.astype(BF16), w_ref[...].astype(BF16)) + b_ref[...]


def _adaln(cc, w_ada, b_ada, tn=512):
    rows, d = cc.shape
    n = w_ada.shape[1]
    return pl.pallas_call(
        _adaln_kernel,
        grid=(n // tn,),
        in_specs=[
            pl.BlockSpec((rows, d), lambda j: (0, 0)),
            pl.BlockSpec((d, tn), lambda j: (0, j)),
            pl.BlockSpec((1, tn), lambda j: (0, j)),
        ],
        out_specs=pl.BlockSpec((rows, tn), lambda j: (0, j)),
        out_shape=jax.ShapeDtypeStruct((rows, n), F32),
        name="adaln",
    )(cc, w_ada, b_ada)


def _premix_kernel(x_ref, mod_ref, gpre_ref, win_ref, gq_ref, wq_ref, gkv_ref, wkv_ref,
                   cosq_ref, sinq_ref, cosk_ref, sink_ref, u_ref, q_ref, k_ref, v_ref, us_ref):
    x = x_ref[0]
    shift = mod_ref[0, 0:1, :]
    scale = mod_ref[0, 1:2, :]
    h = _rms(x, gpre_ref[...]) * (1.0 + scale) + shift
    p = _dot(h.astype(BF16), win_ref[...])
    sub = us_ref.shape[1] // FFT_RADIX
    for g in range(FOURIER_GROUPS):
        glo = g * FOURIER_GROUP_DIM
        us_ref[g] = p[:, glo:glo + FOURIER_GROUP_DIM]
        for r in range(FFT_RADIX):
            lo = r * FOURIER_WIDTH + glo
            u_ref[0, :, lo:lo + FOURIER_GROUP_DIM] = (
                us_ref[g, pl.ds(r, sub, stride=FFT_RADIX), :].astype(BF16))

    qn = _rms(p[:, FOURIER_WIDTH:KV_COL], gq_ref[...]).astype(BF16)
    qq = _dot(qn, wq_ref[...])
    cosq = cosq_ref[...]
    sinq = sinq_ref[...]
    for hd in range(MLA_HEADS):
        lo = hd * HEAD_SLOT
        q_ref[0, :, lo:lo + HEAD_SLOT] = (
            qq[:, lo:lo + HEAD_SLOT] * cosq + qq[:, QK_WIDTH + lo:QK_WIDTH + lo + HEAD_SLOT] * sinq
        ).astype(BF16)

    kvn = _rms(p[:, KV_COL:ROPE_COL], gkv_ref[...]).astype(BF16)
    kv = _dot(kvn, wkv_ref[...])
    kr = (p[:, ROPE_COL:ROPE_COL + HEAD_SLOT] * cosk_ref[...]
          + p[:, ROPE_COL + HEAD_SLOT:ROPE_COL + 2 * HEAD_SLOT] * sink_ref[...])
    for hd in range(MLA_HEADS):
        lo = hd * HEAD_SLOT
        k_ref[0, :, lo:lo + HEAD_SLOT] = (kv[:, lo:lo + HEAD_SLOT] + kr).astype(BF16)
    v_ref[0] = kv[:, QK_WIDTH:].astype(BF16)


def _premix(x, mod3, gpre, win, gq, wq, gkv, wkv, cosq, sinq, cosk, sink, tm=512):
    b, s, d = x.shape
    const = lambda shape: pl.BlockSpec(shape, lambda i, j: (0,) * len(shape))
    rows = lambda w: pl.BlockSpec((1, tm, w), lambda i, j: (i, j, 0))
    table = pl.BlockSpec((tm, HEAD_SLOT), lambda i, j: (j, 0))
    return pl.pallas_call(
        _premix_kernel,
        grid=(b, s // tm),
        in_specs=[
            rows(d),
            pl.BlockSpec((1, 6, d), lambda i, j: (i, 0, 0)),
            const(gpre.shape), const(win.shape), const(gq.shape), const(wq.shape),
            const(gkv.shape), const(wkv.shape),
            table, table, table, table,
        ],
        out_specs=[
            pl.BlockSpec((1, tm // FFT_RADIX, FFT_RADIX * FOURIER_WIDTH), lambda i, j: (i, j, 0)),
            rows(QK_WIDTH), rows(QK_WIDTH), rows(ATTN_WIDTH)],
        out_shape=[
            jax.ShapeDtypeStruct((b, s // FFT_RADIX, FFT_RADIX * FOURIER_WIDTH), BF16),
            jax.ShapeDtypeStruct((b, s, QK_WIDTH), BF16),
            jax.ShapeDtypeStruct((b, s, QK_WIDTH), BF16),
            jax.ShapeDtypeStruct((b, s, ATTN_WIDTH), BF16),
        ],
        scratch_shapes=[pltpu.VMEM((FOURIER_GROUPS, tm, FOURIER_GROUP_DIM), F32)],
        compiler_params=pltpu.CompilerParams(vmem_limit_bytes=VMEM_LIMIT_BYTES),
        name="premix",
    )(x, mod3, gpre, win, gq, wq, gkv, wkv, cosq, sinq, cosk, sink)


def _ctxkv_kernel(x_ref, mod_ref, gpre_ref, win_ref, gkv_ref, wkv_ref, k_ref, v_ref):
    x = x_ref[0]
    shift = mod_ref[0, 0:1, :]
    scale = mod_ref[0, 1:2, :]
    h = _rms(x, gpre_ref[...]) * (1.0 + scale) + shift
    p = _dot(h.astype(BF16), win_ref[...])
    kvn = _rms(p[:, :KV_LORA_RANK], gkv_ref[...]).astype(BF16)
    kv = _dot(kvn, wkv_ref[...])
    kr = p[:, KV_LORA_RANK:]
    for hd in range(MLA_HEADS):
        lo = hd * HEAD_SLOT
        k_ref[0, :, lo:lo + HEAD_SLOT] = (kv[:, lo:lo + HEAD_SLOT] + kr).astype(BF16)
    v_ref[0] = kv[:, QK_WIDTH:].astype(BF16)


def _ctxkv(ctx, mod3, ctx_row, gpre, win_c, gkv, wkv):
    b, c, d = ctx.shape
    const = lambda shape: pl.BlockSpec(shape, lambda i: (0,) * len(shape))
    rows = lambda w: pl.BlockSpec((1, c, w), lambda i: (i, 0, 0))
    return pl.pallas_call(
        _ctxkv_kernel,
        grid=(b,),
        in_specs=[
            rows(d),
            pl.BlockSpec((1, 6, d), lambda i: (ctx_row, 0, 0)),
            const(gpre.shape), const(win_c.shape), const(gkv.shape), const(wkv.shape),
        ],
        out_specs=[rows(QK_WIDTH), rows(ATTN_WIDTH)],
        out_shape=[
            jax.ShapeDtypeStruct((b, c, QK_WIDTH), BF16),
            jax.ShapeDtypeStruct((b, c, ATTN_WIDTH), BF16),
        ],
        name="ctxkv",
    )(ctx, mod3, gpre, win_c, gkv, wkv)


def _attn_kernel(q_ref, kc_ref, kl_ref, vc_ref, vl_ref, o_ref):
    tq = q_ref.shape[1]
    first_half = lax.broadcasted_iota(jnp.int32, (tq, 2 * V_HEAD_DIM), 1) < V_HEAD_DIM
    for pair in range(MLA_HEADS // 2):
        vlo = pair * 2 * V_HEAD_DIM
        vc = vc_ref[0, :, vlo:vlo + 2 * V_HEAD_DIM]
        vl = vl_ref[0, :, vlo:vlo + 2 * V_HEAD_DIM]
        outs = []
        for hd in (2 * pair, 2 * pair + 1):
            lo = hd * HEAD_SLOT
            qh = q_ref[0, :, lo:lo + HEAD_SLOT]
            s_c = _dot_nt(qh, kc_ref[0, :, lo:lo + HEAD_SLOT])
            s_l = _dot_nt(qh, kl_ref[0, :, lo:lo + HEAD_SLOT])
            m = jnp.maximum(jnp.max(s_c, axis=-1, keepdims=True), jnp.max(s_l, axis=-1, keepdims=True))
            p_c = jnp.exp2(s_c - m)
            p_l = jnp.exp2(s_l - m)
            denom = jnp.sum(p_c, axis=-1, keepdims=True) + jnp.sum(p_l, axis=-1, keepdims=True)
            o = _dot(p_c.astype(BF16), vc) + _dot(p_l.astype(BF16), vl)
            outs.append(o / denom)
        o_ref[0, :, vlo:vlo + 2 * V_HEAD_DIM] = jnp.where(first_half, outs[0], outs[1]).astype(BF16)


def _attention(q, kc, kl, vc, vl, tq=512):
    b, s, _ = q.shape
    c = kc.shape[1]
    return pl.pallas_call(
        _attn_kernel,
        grid=(b, s // tq),
        in_specs=[
            pl.BlockSpec((1, tq, QK_WIDTH), lambda i, j: (i, j, 0)),
            pl.BlockSpec((1, c, QK_WIDTH), lambda i, j: (i, 0, 0)),
            pl.BlockSpec((1, s, QK_WIDTH), lambda i, j: (i, 0, 0)),
            pl.BlockSpec((1, c, ATTN_WIDTH), lambda i, j: (i, 0, 0)),
            pl.BlockSpec((1, s, ATTN_WIDTH), lambda i, j: (i, 0, 0)),
        ],
        out_specs=pl.BlockSpec((1, tq, ATTN_WIDTH), lambda i, j: (i, j, 0)),
        out_shape=jax.ShapeDtypeStruct((b, s, ATTN_WIDTH), BF16),
        compiler_params=pltpu.CompilerParams(vmem_limit_bytes=VMEM_LIMIT_BYTES),
        name="attn",
    )(q, kc, kl, vc, vl)


def _fourier_kernel(t_ref, tw_ref, u_ref, cc_ref, wf_ref, o_ref, tb_ref, cw_ref):
    @pl.when(pl.program_id(0) == 0)
    def _():
        tb_ref[...] = t_ref[...].astype(BF16)
        cc = cc_ref[...].astype(BF16)
        for g in range(FOURIER_GROUPS):
            cw_ref[g] = _dot(cc, wf_ref[g]).astype(BF16)

    n_sub = tb_ref.shape[1]
    f = _dot(tb_ref[...], u_ref[0])
    gd = FOURIER_GROUP_DIM
    for g in range(FOURIER_GROUPS):
        gr, gi = [], []
        for r in range(FFT_RADIX):
            lo = r * FOURIER_WIDTH + g * gd
            a = f[:n_sub, lo:lo + gd]
            b = f[n_sub:, lo:lo + gd]
            if r == 0:
                gr.append(a)
                gi.append(b)
            else:
                c = tw_ref[r - 1, 0]
                s = tw_ref[r - 1, 1]
                gr.append(a * c - b * s)
                gi.append(a * s + b * c)
        ar, ai = gr[0] + gr[2], gi[0] + gi[2]
        br, bi = gr[0] - gr[2], gi[0] - gi[2]
        cr, ci = gr[1] + gr[3], gi[1] + gi[3]
        dr, di = gr[1] - gr[3], gi[1] - gi[3]
        xr = jnp.concatenate([ar + cr, br - di, ar - cr, br + di], axis=0)
        xi = jnp.concatenate([ai + ci, bi + dr, ai - ci, bi - dr], axis=0)
        lhs = jnp.concatenate([xr, xi], axis=1).astype(BF16)
        o_ref[0, :, g * gd:(g + 1) * gd] = _dot(lhs, cw_ref[g]).astype(BF16)


def _fourier(u4, tmat, tw, cmat, wf):
    b, n_sub, _ = u4.shape
    s = n_sub * FFT_RADIX
    full = lambda a: pl.BlockSpec(a.shape, lambda i: (0,) * a.ndim)
    return pl.pallas_call(
        _fourier_kernel,
        grid=(b,),
        in_specs=[
            full(tmat), full(tw),
            pl.BlockSpec((1, n_sub, FFT_RADIX * FOURIER_WIDTH), lambda i: (i, 0, 0)),
            full(cmat), full(wf),
        ],
        out_specs=pl.BlockSpec((1, s, FOURIER_WIDTH), lambda i: (i, 0, 0)),
        out_shape=jax.ShapeDtypeStruct((b, s, FOURIER_WIDTH), BF16),
        scratch_shapes=[
            pltpu.VMEM(tmat.shape, BF16),
            pltpu.VMEM((FOURIER_GROUPS, 2 * FOURIER_GROUP_DIM, FOURIER_GROUP_DIM), BF16),
        ],
        compiler_params=pltpu.CompilerParams(
            dimension_semantics=("arbitrary",), vmem_limit_bytes=VMEM_LIMIT_BYTES),
        name="fourier",
    )(tmat, tw, u4, cmat, wf)


def _post_kernel(x_ref, four_ref, attn_ref, mod_ref, gpm_ref, gpf_ref, gqf_ref,
                 wof_ref, woa_ref, wg_ref, wu_ref, wd_ref, o_ref):
    x = x_ref[0]
    gt_m = mod_ref[0, 2:3, :]
    sh_f = mod_ref[0, 3:4, :]
    sc_f = mod_ref[0, 4:5, :]
    gt_f = mod_ref[0, 5:6, :]
    y = _dot(four_ref[0], wof_ref[...]) + _dot(attn_ref[0], woa_ref[...])
    x1 = x + gt_m * _rms(y, gpm_ref[...])
    h2 = (_rms(x1, gpf_ref[...]) * (1.0 + sc_f) + sh_f).astype(BF16)
    g = _dot(h2, wg_ref[...])
    up = _dot(h2, wu_ref[...])
    act = (g / (1.0 + jnp.exp(-g)) * up).astype(BF16)
    ffn = _dot(act, wd_ref[...])
    o_ref[0] = x1 + gt_f * _rms(ffn, gqf_ref[...])


def _post(x, four, attn, mod3, gpm, gpf, gqf, wof, woa, wg, wu, wd, tm=512):
    b, s, d = x.shape
    const = lambda shape: pl.BlockSpec(shape, lambda i, j: (0,) * len(shape),
                                       pipeline_mode=pl.Buffered(1))
    rows = lambda w: pl.BlockSpec((1, tm, w), lambda i, j: (i, j, 0))
    return pl.pallas_call(
        _post_kernel,
        grid=(b, s // tm),
        in_specs=[
            rows(d), rows(FOURIER_WIDTH), rows(ATTN_WIDTH),
            pl.BlockSpec((1, 6, d), lambda i, j: (i, 0, 0)),
            const(gpm.shape), const(gpf.shape), const(gqf.shape),
            const(wof.shape), const(woa.shape), const(wg.shape), const(wu.shape), const(wd.shape),
        ],
        out_specs=rows(d),
        out_shape=jax.ShapeDtypeStruct((b, s, d), F32),
        compiler_params=pltpu.CompilerParams(vmem_limit_bytes=VMEM_LIMIT_BYTES),
        name="post",
    )(x, four, attn, mod3, gpm, gpf, gqf, wof, woa, wg, wu, wd)


def _rope_rotate_cols(w):
    a = QK_ROPE_DIM // 2
    hf = a // 2
    blocks = []
    for s0 in (0, a):
        blocks += [-w[..., s0 + hf:s0 + a], w[..., s0:s0 + hf]]
    return jnp.concatenate(blocks, axis=-1)


def _head_slot(nope, rope):
    pad = HEAD_SLOT - QK_NOPE_DIM - QK_ROPE_DIM
    return jnp.concatenate([nope, rope, jnp.zeros(rope.shape[:-1] + (pad,), rope.dtype)], axis=-1)


def _rope_tables(n_lat, q_scale):
    t = np.arange(n_lat)
    hf = QK_ROPE_DIM // 4
    inv_freq = ROPE_BASE ** (-np.arange(hf, dtype=np.float64) / hf)
    ar = (t // GRID_W)[:, None] * inv_freq[None, :]
    ac = (t % GRID_W)[:, None] * inv_freq[None, :]
    cos32 = np.concatenate([np.cos(ar), np.cos(ar), np.cos(ac), np.cos(ac)], axis=-1)
    sin32 = np.concatenate([np.sin(ar), np.sin(ar), np.sin(ac), np.sin(ac)], axis=-1)
    pad = np.zeros((n_lat, HEAD_SLOT - QK_NOPE_DIM - QK_ROPE_DIM))
    ones = np.ones((n_lat, QK_NOPE_DIM))
    zeros = np.zeros((n_lat, QK_NOPE_DIM))
    slot = lambda nope, rope: jnp.asarray(np.concatenate([nope, rope, pad], axis=-1).astype(np.float32))
    return (slot(ones * q_scale, cos32 * q_scale), slot(zeros, sin32 * q_scale),
            slot(zeros, cos32), slot(zeros, sin32))


def _dft_tables(n_pos, n_ch):
    n_sub = n_pos // FFT_RADIX
    m = np.arange(n_sub, dtype=np.int64)
    ang = 2.0 * np.pi * ((m[:, None] * m[None, :]) % n_sub) / n_sub
    tmat = np.concatenate([np.cos(ang), np.sin(ang)], axis=0).astype(np.float32)
    tw = np.zeros((FFT_RADIX - 1, 2, n_sub, FOURIER_GROUP_DIM), np.float32)
    for r in range(1, FFT_RADIX):
        a = 2.0 * np.pi * r * m / n_pos
        tw[r - 1, 0] = np.cos(a)[:, None]
        tw[r - 1, 1] = np.sin(a)[:, None]
    c = np.arange(n_ch, dtype=np.int64)
    angc = 2.0 * np.pi * ((c[:, None] * c[None, :]) % n_ch) / n_ch
    norm = 1.0 / np.sqrt(float(n_pos * n_ch))
    cmat = np.concatenate([np.cos(angc) * norm, -np.sin(angc) * norm], axis=0).astype(np.float32)
    return jnp.asarray(tmat), jnp.asarray(tw), jnp.asarray(cmat)


def kernel(x, c, ctx, c_ctx, w_ada, b_ada, g_pre_mix, g_post_mix, g_pre_ffn, g_post_ffn, w_in, g_q_a,
           w_q_b, g_kv_a, w_kv_b, w_fourier, w_out, w_gate, w_up, w_down):
    assert w_ada.shape[0] == 1, "single-layer block"
    batch, n_lat, d = x.shape

    mod_rows = -(-(batch + 1) // 8) * 8
    cc = jnp.concatenate([c, c_ctx[None, :], jnp.zeros((mod_rows - batch - 1, d), F32)], axis=0)
    mod = _adaln(cc, w_ada[0], b_ada[0][None, :])
    mod3 = mod.reshape(mod_rows, 6, d)

    w_in0 = w_in[0]
    w_kr = w_in0[:, ROPE_COL:]
    zeros_d = jnp.zeros((d, QK_NOPE_DIM), F32)
    kr_slot = _head_slot(zeros_d, w_kr)
    kr_rot_slot = _head_slot(zeros_d, _rope_rotate_cols(w_kr))
    win = jnp.concatenate([w_in0[:, :ROPE_COL], kr_slot, kr_rot_slot], axis=1).astype(BF16)
    win_c = jnp.concatenate([w_in0[:, KV_COL:ROPE_COL], kr_slot], axis=1).astype(BF16)

    wq3 = w_q_b[0].reshape(Q_LORA_RANK, MLA_HEADS, QK_NOPE_DIM + QK_ROPE_DIM)
    wq_nope, wq_rope = wq3[..., :QK_NOPE_DIM], wq3[..., QK_NOPE_DIM:]
    wq_a = _head_slot(wq_nope, wq_rope).reshape(Q_LORA_RANK, QK_WIDTH)
    wq_b = _head_slot(jnp.zeros_like(wq_nope), _rope_rotate_cols(wq_rope)).reshape(Q_LORA_RANK, QK_WIDTH)
    wq = jnp.concatenate([wq_a, wq_b], axis=1).astype(BF16)

    wkv3 = w_kv_b[0].reshape(KV_LORA_RANK, MLA_HEADS, QK_NOPE_DIM + V_HEAD_DIM)
    wk_nope, wv = wkv3[..., :QK_NOPE_DIM], wkv3[..., QK_NOPE_DIM:]
    wk_slots = _head_slot(wk_nope, jnp.zeros(wk_nope.shape[:-1] + (QK_ROPE_DIM,), F32))
    wkv = jnp.concatenate([wk_slots.reshape(KV_LORA_RANK, QK_WIDTH),
                           wv.reshape(KV_LORA_RANK, ATTN_WIDTH)], axis=1).astype(BF16)

    q_scale = float((QK_NOPE_DIM + QK_ROPE_DIM) ** -0.5 * np.log2(np.e))
    cosq, sinq, cosk, sink = _rope_tables(n_lat, q_scale)
    tmat, tw, cmat = _dft_tables(n_lat, FOURIER_GROUP_DIM)

    row2 = lambda g: g[0][None, :]
    u_f, q, k_lat, v_lat = _premix(x, mod3, row2(g_pre_mix), win, row2(g_q_a), wq, row2(g_kv_a), wkv,
                                   cosq, sinq, cosk, sink)
    k_ctx, v_ctx = _ctxkv(ctx, mod3, batch, row2(g_pre_mix), win_c, row2(g_kv_a), wkv)
    attn = _attention(q, k_ctx, k_lat, v_ctx, v_lat)
    four = _fourier(u_f, tmat, tw, cmat, w_fourier[0].astype(BF16))
    w_out0 = w_out[0].astype(BF16)
    return _post(x, four, attn, mod3, row2(g_post_mix), row2(g_pre_ffn), row2(g_post_ffn),
                 w_out0[:FOURIER_WIDTH], w_out0[FOURIER_WIDTH:],
                 w_gate[0].astype(BF16), w_up[0].astype(BF16), w_down[0].astype(BF16))
```

```python
import functools

import numpy as np
import jax
import jax.numpy as jnp
from jax import lax
from jax.experimental import pallas as pl
from jax.experimental.pallas import tpu as pltpu

F32 = jnp.float32
BF16 = jnp.bfloat16

D_MODEL = 1024
GRID_W = 64
FOURIER_GROUPS = 4
FOURIER_GROUP_DIM = 128
FOURIER_WIDTH = FOURIER_GROUPS * FOURIER_GROUP_DIM
MLA_HEADS = 8
QK_NOPE_DIM = 64
QK_ROPE_DIM = 32
V_HEAD_DIM = 64
Q_LORA_RANK = 256
KV_LORA_RANK = 128
KV_COL = FOURIER_WIDTH + Q_LORA_RANK
ROPE_COL = KV_COL + KV_LORA_RANK
ROPE_BASE = 10000.0
NORM_EPS = 1e-6
FFT_RADIX = 4
HEAD_SLOT = 128
ATTN_WIDTH = MLA_HEADS * V_HEAD_DIM
QK_WIDTH = MLA_HEADS * HEAD_SLOT

VMEM_LIMIT_BYTES = 56 * 1024 * 1024


def _rms(x, g):
    return x * lax.rsqrt(jnp.mean(x * x, axis=-1, keepdims=True) + NORM_EPS) * g


def _dot(a, b):
    return jnp.dot(a, b, preferred_element_type=F32)


def _dot_nt(a, b):
    return lax.dot_general(a, b, (((1,), (1,)), ((), ())), preferred_element_type=F32)


def _adaln_kernel(c_ref, w_ref, b_ref, o_ref):
    c = c_ref[...]
    a = c / (1.0 + jnp.exp(-c))
    o_ref[...] = _dot(a.astype(BF16), w_ref[...].astype(BF16)) + b_ref[...]


def _adaln(cc, w_ada, b_ada, tn=512):
    rows, d = cc.shape
    n = w_ada.shape[1]
    return pl.pallas_call(
        _adaln_kernel,
        grid=(n // tn,),
        in_specs=[
            pl.BlockSpec((rows, d), lambda j: (0, 0)),
            pl.BlockSpec((d, tn), lambda j: (0, j)),
            pl.BlockSpec((1, tn), lambda j: (0, j)),
        ],
        out_specs=pl.BlockSpec((rows, tn), lambda j: (0, j)),
        out_shape=jax.ShapeDtypeStruct((rows, n), F32),
        name="adaln",
    )(cc, w_ada, b_ada)


def _premix_kernel(x_ref, mod_ref, gpre_ref, win_ref, gq_ref, wq_ref, gkv_ref, wkv_ref,
                   cosq_ref, sinq_ref, cosk_ref, sink_ref, u_ref, q_ref, k_ref, v_ref, us_ref):
    x = x_ref[0]
    shift = mod_ref[0, 0:1, :]
    scale = mod_ref[0, 1:2, :]
    h = _rms(x, gpre_ref[...]) * (1.0 + scale) + shift
    p = _dot(h.astype(BF16), win_ref[...])
    sub = us_ref.shape[1] // FFT_RADIX
    for g in range(FOURIER_GROUPS):
        glo = g * FOURIER_GROUP_DIM
        us_ref[g] = p[:, glo:glo + FOURIER_GROUP_DIM]
        for r in range(FFT_RADIX):
            lo = r * FOURIER_WIDTH + glo
            u_ref[0, :, lo:lo + FOURIER_GROUP_DIM] = (
                us_ref[g, pl.ds(r, sub, stride=FFT_RADIX), :].astype(BF16))

    qn = _rms(p[:, FOURIER_WIDTH:KV_COL], gq_ref[...]).astype(BF16)
    qq = _dot(qn, wq_ref[...])
    cosq = cosq_ref[...]
    sinq = sinq_ref[...]
    for hd in range(MLA_HEADS):
        lo = hd * HEAD_SLOT
        q_ref[0, :, lo:lo + HEAD_SLOT] = (
            qq[:, lo:lo + HEAD_SLOT] * cosq + qq[:, QK_WIDTH + lo:QK_WIDTH + lo + HEAD_SLOT] * sinq
        ).astype(BF16)

    kvn = _rms(p[:, KV_COL:ROPE_COL], gkv_ref[...]).astype(BF16)
    kv = _dot(kvn, wkv_ref[...])
    kr = (p[:, ROPE_COL:ROPE_COL + HEAD_SLOT] * cosk_ref[...]
          + p[:, ROPE_COL + HEAD_SLOT:ROPE_COL + 2 * HEAD_SLOT] * sink_ref[...])
    for hd in range(MLA_HEADS):
        lo = hd * HEAD_SLOT
        k_ref[0, :, lo:lo + HEAD_SLOT] = (kv[:, lo:lo + HEAD_SLOT] + kr).astype(BF16)
    v_ref[0] = kv[:, QK_WIDTH:].astype(BF16)


def _premix(x, mod3, gpre, win, gq, wq, gkv, wkv, cosq, sinq, cosk, sink, tm=512):
    b, s, d = x.shape
    const = lambda shape: pl.BlockSpec(shape, lambda i, j: (0,) * len(shape))
    rows = lambda w: pl.BlockSpec((1, tm, w), lambda i, j: (i, j, 0))
    table = pl.BlockSpec((tm, HEAD_SLOT), lambda i, j: (j, 0))
    return pl.pallas_call(
        _premix_kernel,
        grid=(b, s // tm),
        in_specs=[
            rows(d),
            pl.BlockSpec((1, 6, d), lambda i, j: (i, 0, 0)),
            const(gpre.shape), const(win.shape), const(gq.shape), const(wq.shape),
            const(gkv.shape), const(wkv.shape),
            table, table, table, table,
        ],
        out_specs=[
            pl.BlockSpec((1, tm // FFT_RADIX, FFT_RADIX * FOURIER_WIDTH), lambda i, j: (i, j, 0)),
            rows(QK_WIDTH), rows(QK_WIDTH), rows(ATTN_WIDTH)],
        out_shape=[
            jax.ShapeDtypeStruct((b, s // FFT_RADIX, FFT_RADIX * FOURIER_WIDTH), BF16),
            jax.ShapeDtypeStruct((b, s, QK_WIDTH), BF16),
            jax.ShapeDtypeStruct((b, s, QK_WIDTH), BF16),
            jax.ShapeDtypeStruct((b, s, ATTN_WIDTH), BF16),
        ],
        scratch_shapes=[pltpu.VMEM((FOURIER_GROUPS, tm, FOURIER_GROUP_DIM), F32)],
        compiler_params=pltpu.CompilerParams(vmem_limit_bytes=VMEM_LIMIT_BYTES),
        name="premix",
    )(x, mod3, gpre, win, gq, wq, gkv, wkv, cosq, sinq, cosk, sink)


def _ctxkv_kernel(x_ref, mod_ref, gpre_ref, win_ref, gkv_ref, wkv_ref, k_ref, v_ref):
    x = x_ref[0]
    shift = mod_ref[0, 0:1, :]
    scale = mod_ref[0, 1:2, :]
    h = _rms(x, gpre_ref[...]) * (1.0 + scale) + shift
    p = _dot(h.astype(BF16), win_ref[...])
    kvn = _rms(p[:, :KV_LORA_RANK], gkv_ref[...]).astype(BF16)
    kv = _dot(kvn, wkv_ref[...])
    kr = p[:, KV_LORA_RANK:]
    for hd in range(MLA_HEADS):
        lo = hd * HEAD_SLOT
        k_ref[0, :, lo:lo + HEAD_SLOT] = (kv[:, lo:lo + HEAD_SLOT] + kr).astype(BF16)
    v_ref[0] = kv[:, QK_WIDTH:].astype(BF16)


def _ctxkv(ctx, mod3, ctx_row, gpre, win_c, gkv, wkv):
    b, c, d = ctx.shape
    const = lambda shape: pl.BlockSpec(shape, lambda i: (0,) * len(shape))
    rows = lambda w: pl.BlockSpec((1, c, w), lambda i: (i, 0, 0))
    return pl.pallas_call(
        _ctxkv_kernel,
        grid=(b,),
        in_specs=[
            rows(d),
            pl.BlockSpec((1, 6, d), lambda i: (ctx_row, 0, 0)),
            const(gpre.shape), const(win_c.shape), const(gkv.shape), const(wkv.shape),
        ],
        out_specs=[rows(QK_WIDTH), rows(ATTN_WIDTH)],
        out_shape=[
            jax.ShapeDtypeStruct((b, c, QK_WIDTH), BF16),
            jax.ShapeDtypeStruct((b, c, ATTN_WIDTH), BF16),
        ],
        name="ctxkv",
    )(ctx, mod3, gpre, win_c, gkv, wkv)


def _attn_kernel(q_ref, kc_ref, kl_ref, vc_ref, vl_ref, vcp_ref, vlp_ref, omain_ref, olast_ref,
                 p_ref, den_ref, oprev_ref):
    t = pl.program_id(0)
    n_blocks = pl.num_programs(0) - 1
    n_ctx = kc_ref.shape[1]
    tq = q_ref.shape[1]
    pair_w = 2 * V_HEAD_DIM
    first_half = lax.broadcasted_iota(jnp.int32, (tq, pair_w), 1) < V_HEAD_DIM
    last = MLA_HEADS - 1

    @pl.when(t == 0)
    def _():
        p_ref[...] = jnp.zeros_like(p_ref)
        den_ref[...] = jnp.ones_like(den_ref)
        oprev_ref[...] = jnp.zeros_like(oprev_ref)

    def scores(hd):
        lo = hd * HEAD_SLOT
        qh = q_ref[0, :, lo:lo + HEAD_SLOT]
        s_c = _dot_nt(qh, kc_ref[0, :, lo:lo + HEAD_SLOT])
        s_l = _dot_nt(qh, kl_ref[0, :, lo:lo + HEAD_SLOT])
        m = jnp.maximum(jnp.max(s_c, axis=-1, keepdims=True), jnp.max(s_l, axis=-1, keepdims=True))
        p_c = jnp.exp2(s_c - m)
        p_l = jnp.exp2(s_l - m)
        denom = jnp.sum(p_c, axis=-1, keepdims=True) + jnp.sum(p_l, axis=-1, keepdims=True)
        return p_c.astype(BF16), p_l.astype(BF16), denom

    def drain():
        o = _dot(p_ref[:, :n_ctx], vcp_ref[0]) + _dot(p_ref[:, n_ctx:], vlp_ref[0])
        olast_ref[0] = jnp.where(first_half, oprev_ref[...], o / den_ref[...]).astype(BF16)

    @pl.when(t < n_blocks)
    def _():
        drain()
        for pair in range(MLA_HEADS // 2):
            vlo = pair * pair_w
            vc = vc_ref[0, :, vlo:vlo + pair_w]
            vl = vl_ref[0, :, vlo:vlo + pair_w]
            outs = []
            for hd in (2 * pair, 2 * pair + 1):
                p_c, p_l, denom = scores(hd)
                if hd == last:
                    p_ref[:, :n_ctx] = p_c
                    p_ref[:, n_ctx:] = p_l
                    den_ref[...] = denom
                else:
                    outs.append((_dot(p_c, vc) + _dot(p_l, vl)) / denom)
            if len(outs) == 2:
                omain_ref[0, :, vlo:vlo + pair_w] = jnp.where(first_half, outs[0], outs[1]).astype(BF16)
            else:
                oprev_ref[...] = outs[0]

    @pl.when(t == n_blocks)
    def _():
        drain()


def _attention(q, kc, kl, vc, vl, tq=512):
    b, s, _ = q.shape
    c = kc.shape[1]
    nq = s // tq
    n_blocks = b * nq
    pair_w = 2 * V_HEAD_DIM
    main_w = ATTN_WIDTH - pair_w
    last_pair = ATTN_WIDTH // pair_w - 1
    cur = lambda t: jnp.minimum(t, n_blocks - 1)
    prev = lambda t: jnp.maximum(t - 1, 0)
    return pl.pallas_call(
        _attn_kernel,
        grid=(n_blocks + 1,),
        in_specs=[
            pl.BlockSpec((1, tq, QK_WIDTH), lambda t: (cur(t) // nq, cur(t) % nq, 0)),
            pl.BlockSpec((1, c, QK_WIDTH), lambda t: (cur(t) // nq, 0, 0)),
            pl.BlockSpec((1, s, QK_WIDTH), lambda t: (cur(t) // nq, 0, 0)),
            pl.BlockSpec((1, c, ATTN_WIDTH), lambda t: (cur(t) // nq, 0, 0)),
            pl.BlockSpec((1, s, ATTN_WIDTH), lambda t: (cur(t) // nq, 0, 0)),
            pl.BlockSpec((1, c, pair_w), lambda t: (prev(t) // nq, 0, last_pair)),
            pl.BlockSpec((1, s, pair_w), lambda t: (prev(t) // nq, 0, last_pair)),
        ],
        out_specs=[
            pl.BlockSpec((1, tq, main_w), lambda t: (cur(t) // nq, cur(t) % nq, 0)),
            pl.BlockSpec((1, tq, pair_w), lambda t: (prev(t) // nq, prev(t) % nq, 0)),
        ],
        out_shape=[
            jax.ShapeDtypeStruct((b, s, main_w), BF16),
            jax.ShapeDtypeStruct((b, s, pair_w), BF16),
        ],
        scratch_shapes=[
            pltpu.VMEM((tq, c + s), BF16),
            pltpu.VMEM((tq, 1), F32),
            pltpu.VMEM((tq, pair_w), F32),
        ],
        compiler_params=pltpu.CompilerParams(
            dimension_semantics=("arbitrary",), vmem_limit_bytes=VMEM_LIMIT_BYTES),
        name="attn",
    )(q, kc, kl, vc, vl, vc, vl)


def _fourier_kernel(t_ref, tw_ref, u_ref, cc_ref, wf_ref, o_ref, tb_ref, cw_ref):
    @pl.when(pl.program_id(0) == 0)
    def _():
        tb_ref[...] = t_ref[...].astype(BF16)
        cc = cc_ref[...].astype(BF16)
        for g in range(FOURIER_GROUPS):
            cw_ref[g] = _dot(cc, wf_ref[g]).astype(BF16)

    n_sub = tb_ref.shape[1]
    f = _dot(tb_ref[...], u_ref[0])
    gd = FOURIER_GROUP_DIM
    for g in range(FOURIER_GROUPS):
        gr, gi = [], []
        for r in range(FFT_RADIX):
            lo = r * FOURIER_WIDTH + g * gd
            a = f[:n_sub, lo:lo + gd]
            b = f[n_sub:, lo:lo + gd]
            if r == 0:
                gr.append(a)
                gi.append(b)
            else:
                c = tw_ref[r - 1, 0]
                s = tw_ref[r - 1, 1]
                gr.append(a * c - b * s)
                gi.append(a * s + b * c)
        ar, ai = gr[0] + gr[2], gi[0] + gi[2]
        br, bi = gr[0] - gr[2], gi[0] - gi[2]
        cr, ci = gr[1] + gr[3], gi[1] + gi[3]
        dr, di = gr[1] - gr[3], gi[1] - gi[3]
        xr = jnp.concatenate([ar + cr, br - di, ar - cr, br + di], axis=0)
        xi = jnp.concatenate([ai + ci, bi + dr, ai - ci, bi - dr], axis=0)
        lhs = jnp.concatenate([xr, xi], axis=1).astype(BF16)
        o_ref[0, :, g * gd:(g + 1) * gd] = _dot(lhs, cw_ref[g]).astype(BF16)


def _fourier(u4, tmat, tw, cmat, wf):
    b, n_sub, _ = u4.shape
    s = n_sub * FFT_RADIX
    full = lambda a: pl.BlockSpec(a.shape, lambda i: (0,) * a.ndim)
    return pl.pallas_call(
        _fourier_kernel,
        grid=(b,),
        in_specs=[
            full(tmat), full(tw),
            pl.BlockSpec((1, n_sub, FFT_RADIX * FOURIER_WIDTH), lambda i: (i, 0, 0)),
            full(cmat), full(wf),
        ],
        out_specs=pl.BlockSpec((1, s, FOURIER_WIDTH), lambda i: (i, 0, 0)),
        out_shape=jax.ShapeDtypeStruct((b, s, FOURIER_WIDTH), BF16),
        scratch_shapes=[
            pltpu.VMEM(tmat.shape, BF16),
            pltpu.VMEM((FOURIER_GROUPS, 2 * FOURIER_GROUP_DIM, FOURIER_GROUP_DIM), BF16),
        ],
        compiler_params=pltpu.CompilerParams(
            dimension_semantics=("arbitrary",), vmem_limit_bytes=VMEM_LIMIT_BYTES),
        name="fourier",
    )(tmat, tw, u4, cmat, wf)


def _post_kernel(x_ref, four_ref, am_ref, al_ref, mod_ref, gpm_ref, gpf_ref, gqf_ref,
                 wof_ref, woa_ref, wg_ref, wu_ref, wd_ref, o_ref):
    x = x_ref[0]
    gt_m = mod_ref[0, 2:3, :]
    sh_f = mod_ref[0, 3:4, :]
    sc_f = mod_ref[0, 4:5, :]
    gt_f = mod_ref[0, 5:6, :]
    attn = jnp.concatenate([am_ref[0], al_ref[0]], axis=1)
    y = _dot(four_ref[0], wof_ref[...]) + _dot(attn, woa_ref[...])
    x1 = x + gt_m * _rms(y, gpm_ref[...])
    h2 = (_rms(x1, gpf_ref[...]) * (1.0 + sc_f) + sh_f).astype(BF16)
    g = _dot(h2, wg_ref[...])
    up = _dot(h2, wu_ref[...])
    act = (g / (1.0 + jnp.exp(-g)) * up).astype(BF16)
    ffn = _dot(act, wd_ref[...])
    o_ref[0] = x1 + gt_f * _rms(ffn, gqf_ref[...])


def _post(x, four, attn_main, attn_last, mod3, gpm, gpf, gqf, wof, woa, wg, wu, wd, tm=512):
    b, s, d = x.shape
    const = lambda shape: pl.BlockSpec(shape, lambda i, j: (0,) * len(shape),
                                       pipeline_mode=pl.Buffered(1))
    rows = lambda w: pl.BlockSpec((1, tm, w), lambda i, j: (i, j, 0))
    return pl.pallas_call(
        _post_kernel,
        grid=(b, s // tm),
        in_specs=[
            rows(d), rows(FOURIER_WIDTH), rows(attn_main.shape[2]), rows(attn_last.shape[2]),
            pl.BlockSpec((1, 6, d), lambda i, j: (i, 0, 0)),
            const(gpm.shape), const(gpf.shape), const(gqf.shape),
            const(wof.shape), const(woa.shape), const(wg.shape), const(wu.shape), const(wd.shape),
        ],
        out_specs=rows(d),
        out_shape=jax.ShapeDtypeStruct((b, s, d), F32),
        compiler_params=pltpu.CompilerParams(vmem_limit_bytes=VMEM_LIMIT_BYTES),
        name="post",
    )(x, four, attn_main, attn_last, mod3, gpm, gpf, gqf, wof, woa, wg, wu, wd)


def _rope_rotate_cols(w):
    a = QK_ROPE_DIM // 2
    hf = a // 2
    blocks = []
    for s0 in (0, a):
        blocks += [-w[..., s0 + hf:s0 + a], w[..., s0:s0 + hf]]
    return jnp.concatenate(blocks, axis=-1)


def _head_slot(nope, rope):
    pad = HEAD_SLOT - QK_NOPE_DIM - QK_ROPE_DIM
    return jnp.concatenate([nope, rope, jnp.zeros(rope.shape[:-1] + (pad,), rope.dtype)], axis=-1)


def _rope_tables(n_lat, q_scale):
    t = np.arange(n_lat)
    hf = QK_ROPE_DIM // 4
    inv_freq = ROPE_BASE ** (-np.arange(hf, dtype=np.float64) / hf)
    ar = (t // GRID_W)[:, None] * inv_freq[None, :]
    ac = (t % GRID_W)[:, None] * inv_freq[None, :]
    cos32 = np.concatenate([np.cos(ar), np.cos(ar), np.cos(ac), np.cos(ac)], axis=-1)
    sin32 = np.concatenate([np.sin(ar), np.sin(ar), np.sin(ac), np.sin(ac)], axis=-1)
    pad = np.zeros((n_lat, HEAD_SLOT - QK_NOPE_DIM - QK_ROPE_DIM))
    ones = np.ones((n_lat, QK_NOPE_DIM))
    zeros = np.zeros((n_lat, QK_NOPE_DIM))
    slot = lambda nope, rope: jnp.asarray(np.concatenate([nope, rope, pad], axis=-1).astype(np.float32))
    return (slot(ones * q_scale, cos32 * q_scale), slot(zeros, sin32 * q_scale),
            slot(zeros, cos32), slot(zeros, sin32))


def _dft_tables(n_pos, n_ch):
    n_sub = n_pos // FFT_RADIX
    m = np.arange(n_sub, dtype=np.int64)
    ang = 2.0 * np.pi * ((m[:, None] * m[None, :]) % n_sub) / n_sub
    tmat = np.concatenate([np.cos(ang), np.sin(ang)], axis=0).astype(np.float32)
    tw = np.zeros((FFT_RADIX - 1, 2, n_sub, FOURIER_GROUP_DIM), np.float32)
    for r in range(1, FFT_RADIX):
        a = 2.0 * np.pi * r * m / n_pos
        tw[r - 1, 0] = np.cos(a)[:, None]
        tw[r - 1, 1] = np.sin(a)[:, None]
    c = np.arange(n_ch, dtype=np.int64)
    angc = 2.0 * np.pi * ((c[:, None] * c[None, :]) % n_ch) / n_ch
    norm = 1.0 / np.sqrt(float(n_pos * n_ch))
    cmat = np.concatenate([np.cos(angc) * norm, -np.sin(angc) * norm], axis=0).astype(np.float32)
    return jnp.asarray(tmat), jnp.asarray(tw), jnp.asarray(cmat)


def kernel(x, c, ctx, c_ctx, w_ada, b_ada, g_pre_mix, g_post_mix, g_pre_ffn, g_post_ffn, w_in, g_q_a,
           w_q_b, g_kv_a, w_kv_b, w_fourier, w_out, w_gate, w_up, w_down):
    assert w_ada.shape[0] == 1, "single-layer block"
    batch, n_lat, d = x.shape

    mod_rows = -(-(batch + 1) // 8) * 8
    cc = jnp.concatenate([c, c_ctx[None, :], jnp.zeros((mod_rows - batch - 1, d), F32)], axis=0)
    mod = _adaln(cc, w_ada[0], b_ada[0][None, :])
    mod3 = mod.reshape(mod_rows, 6, d)

    w_in0 = w_in[0]
    w_kr = w_in0[:, ROPE_COL:]
    zeros_d = jnp.zeros((d, QK_NOPE_DIM), F32)
    kr_slot = _head_slot(zeros_d, w_kr)
    kr_rot_slot = _head_slot(zeros_d, _rope_rotate_cols(w_kr))
    win = jnp.concatenate([w_in0[:, :ROPE_COL], kr_slot, kr_rot_slot], axis=1).astype(BF16)
    win_c = jnp.concatenate([w_in0[:, KV_COL:ROPE_COL], kr_slot], axis=1).astype(BF16)

    wq3 = w_q_b[0].reshape(Q_LORA_RANK, MLA_HEADS, QK_NOPE_DIM + QK_ROPE_DIM)
    wq_nope, wq_rope = wq3[..., :QK_NOPE_DIM], wq3[..., QK_NOPE_DIM:]
    wq_a = _head_slot(wq_nope, wq_rope).reshape(Q_LORA_RANK, QK_WIDTH)
    wq_b = _head_slot(jnp.zeros_like(wq_nope), _rope_rotate_cols(wq_rope)).reshape(Q_LORA_RANK, QK_WIDTH)
    wq = jnp.concatenate([wq_a, wq_b], axis=1).astype(BF16)

    wkv3 = w_kv_b[0].reshape(KV_LORA_RANK, MLA_HEADS, QK_NOPE_DIM + V_HEAD_DIM)
    wk_nope, wv = wkv3[..., :QK_NOPE_DIM], wkv3[..., QK_NOPE_DIM:]
    wk_slots = _head_slot(wk_nope, jnp.zeros(wk_nope.shape[:-1] + (QK_ROPE_DIM,), F32))
    wkv = jnp.concatenate([wk_slots.reshape(KV_LORA_RANK, QK_WIDTH),
                           wv.reshape(KV_LORA_RANK, ATTN_WIDTH)], axis=1).astype(BF16)

    q_scale = float((QK_NOPE_DIM + QK_ROPE_DIM) ** -0.5 * np.log2(np.e))
    cosq, sinq, cosk, sink = _rope_tables(n_lat, q_scale)
    tmat, tw, cmat = _dft_tables(n_lat, FOURIER_GROUP_DIM)

    row2 = lambda g: g[0][None, :]
    u_f, q, k_lat, v_lat = _premix(x, mod3, row2(g_pre_mix), win, row2(g_q_a), wq, row2(g_kv_a), wkv,
                                   cosq, sinq, cosk, sink)
    k_ctx, v_ctx = _ctxkv(ctx, mod3, batch, row2(g_pre_mix), win_c, row2(g_kv_a), wkv)
    attn_main, attn_last = _attention(q, k_ctx, k_lat, v_ctx, v_lat)
    four = _fourier(u_f, tmat, tw, cmat, w_fourier[0].astype(BF16))
    w_out0 = w_out[0].astype(BF16)
    return _post(x, four, attn_main, attn_last, mod3, row2(g_post_mix), row2(g_pre_ffn), row2(g_post_ffn),
                 w_out0[:FOURIER_WIDTH], w_out0[FOURIER_WIDTH:],
                 w_gate[0].astype(BF16), w_up[0].astype(BF16), w_down[0].astype(BF16))
```

```python
import functools

import numpy as np
import jax
import jax.numpy as jnp
from jax import lax
from jax.experimental import pallas as pl
from jax.experimental.pallas import tpu as pltpu

F32 = jnp.float32
BF16 = jnp.bfloat16

D_MODEL = 1024
GRID_W = 64
FOURIER_GROUPS = 4
FOURIER_GROUP_DIM = 128
FOURIER_WIDTH = FOURIER_GROUPS * FOURIER_GROUP_DIM
MLA_HEADS = 8
QK_NOPE_DIM = 64
QK_ROPE_DIM = 32
V_HEAD_DIM = 64
Q_LORA_RANK = 256
KV_LORA_RANK = 128
KV_COL = FOURIER_WIDTH + Q_LORA_RANK
ROPE_COL = KV_COL + KV_LORA_RANK
ROPE_BASE = 10000.0
NORM_EPS = 1e-6
FFT_RADIX = 4
HEAD_SLOT = 128
PV_KEY_TILE = 256
ATTN_WIDTH = MLA_HEADS * V_HEAD_DIM
QK_WIDTH = MLA_HEADS * HEAD_SLOT

VMEM_LIMIT_BYTES = 56 * 1024 * 1024


def _rms(x, g):
    return x * lax.rsqrt(jnp.mean(x * x, axis=-1, keepdims=True) + NORM_EPS) * g


def _dot(a, b):
    return jnp.dot(a, b, preferred_element_type=F32)


def _dot_nt(a, b):
    return lax.dot_general(a, b, (((1,), (1,)), ((), ())), preferred_element_type=F32)


def _denominator_lanes():
    lane = lax.broadcasted_iota(jnp.int32, (1, QK_WIDTH), 1)
    return (lane % HEAD_SLOT == V_HEAD_DIM).astype(F32)


def _adaln_kernel(c_ref, w_ref, b_ref, o_ref):
    c = c_ref[...]
    a = c / (1.0 + jnp.exp(-c))
    o_ref[...] = _dot(a.astype(BF16), w_ref[...].astype(BF16)) + b_ref[...]


def _adaln(cc, w_ada, b_ada, tn=512):
    rows, d = cc.shape
    n = w_ada.shape[1]
    return pl.pallas_call(
        _adaln_kernel,
        grid=(n // tn,),
        in_specs=[
            pl.BlockSpec((rows, d), lambda j: (0, 0)),
            pl.BlockSpec((d, tn), lambda j: (0, j)),
            pl.BlockSpec((1, tn), lambda j: (0, j)),
        ],
        out_specs=pl.BlockSpec((rows, tn), lambda j: (0, j)),
        out_shape=jax.ShapeDtypeStruct((rows, n), F32),
        name="adaln",
    )(cc, w_ada, b_ada)


def _premix_kernel(x_ref, mod_ref, gpre_ref, win_ref, gq_ref, wq_ref, gkv_ref, wkv_ref,
                   cosq_ref, sinq_ref, cosk_ref, sink_ref, u_ref, q_ref, k_ref, v_ref, us_ref):
    x = x_ref[0]
    shift = mod_ref[0, 0:1, :]
    scale = mod_ref[0, 1:2, :]
    h = _rms(x, gpre_ref[...]) * (1.0 + scale) + shift
    p = _dot(h.astype(BF16), win_ref[...])
    sub = us_ref.shape[1] // FFT_RADIX
    for g in range(FOURIER_GROUPS):
        glo = g * FOURIER_GROUP_DIM
        us_ref[g] = p[:, glo:glo + FOURIER_GROUP_DIM]
        for r in range(FFT_RADIX):
            lo = r * FOURIER_WIDTH + glo
            u_ref[0, :, lo:lo + FOURIER_GROUP_DIM] = (
                us_ref[g, pl.ds(r, sub, stride=FFT_RADIX), :].astype(BF16))

    qn = _rms(p[:, FOURIER_WIDTH:KV_COL], gq_ref[...]).astype(BF16)
    qq = _dot(qn, wq_ref[...])
    cosq = cosq_ref[...]
    sinq = sinq_ref[...]
    for hd in range(MLA_HEADS):
        lo = hd * HEAD_SLOT
        q_ref[0, :, lo:lo + HEAD_SLOT] = (
            qq[:, lo:lo + HEAD_SLOT] * cosq + qq[:, QK_WIDTH + lo:QK_WIDTH + lo + HEAD_SLOT] * sinq
        ).astype(BF16)

    kvn = _rms(p[:, KV_COL:ROPE_COL], gkv_ref[...]).astype(BF16)
    kv = _dot(kvn, wkv_ref[...])
    kr = (p[:, ROPE_COL:ROPE_COL + HEAD_SLOT] * cosk_ref[...]
          + p[:, ROPE_COL + HEAD_SLOT:ROPE_COL + 2 * HEAD_SLOT] * sink_ref[...])
    for hd in range(MLA_HEADS):
        lo = hd * HEAD_SLOT
        k_ref[0, :, lo:lo + HEAD_SLOT] = (kv[:, lo:lo + HEAD_SLOT] + kr).astype(BF16)
    v_ref[0] = (kv[:, QK_WIDTH:] + _denominator_lanes()).astype(BF16)


def _premix(x, mod3, gpre, win, gq, wq, gkv, wkv, cosq, sinq, cosk, sink, tm=512):
    b, s, d = x.shape
    const = lambda shape: pl.BlockSpec(shape, lambda i, j: (0,) * len(shape))
    rows = lambda w: pl.BlockSpec((1, tm, w), lambda i, j: (i, j, 0))
    table = pl.BlockSpec((tm, HEAD_SLOT), lambda i, j: (j, 0))
    return pl.pallas_call(
        _premix_kernel,
        grid=(b, s // tm),
        in_specs=[
            rows(d),
            pl.BlockSpec((1, 6, d), lambda i, j: (i, 0, 0)),
            const(gpre.shape), const(win.shape), const(gq.shape), const(wq.shape),
            const(gkv.shape), const(wkv.shape),
            table, table, table, table,
        ],
        out_specs=[
            pl.BlockSpec((1, tm // FFT_RADIX, FFT_RADIX * FOURIER_WIDTH), lambda i, j: (i, j, 0)),
            rows(QK_WIDTH), rows(QK_WIDTH), rows(QK_WIDTH)],
        out_shape=[
            jax.ShapeDtypeStruct((b, s // FFT_RADIX, FFT_RADIX * FOURIER_WIDTH), BF16),
            jax.ShapeDtypeStruct((b, s, QK_WIDTH), BF16),
            jax.ShapeDtypeStruct((b, s, QK_WIDTH), BF16),
            jax.ShapeDtypeStruct((b, s, QK_WIDTH), BF16),
        ],
        scratch_shapes=[pltpu.VMEM((FOURIER_GROUPS, tm, FOURIER_GROUP_DIM), F32)],
        compiler_params=pltpu.CompilerParams(vmem_limit_bytes=VMEM_LIMIT_BYTES),
        name="premix",
    )(x, mod3, gpre, win, gq, wq, gkv, wkv, cosq, sinq, cosk, sink)


def _ctxkv_kernel(x_ref, mod_ref, gpre_ref, win_ref, gkv_ref, wkv_ref, k_ref, v_ref):
    x = x_ref[0]
    shift = mod_ref[0, 0:1, :]
    scale = mod_ref[0, 1:2, :]
    h = _rms(x, gpre_ref[...]) * (1.0 + scale) + shift
    p = _dot(h.astype(BF16), win_ref[...])
    kvn = _rms(p[:, :KV_LORA_RANK], gkv_ref[...]).astype(BF16)
    kv = _dot(kvn, wkv_ref[...])
    kr = p[:, KV_LORA_RANK:]
    for hd in range(MLA_HEADS):
        lo = hd * HEAD_SLOT
        k_ref[0, :, lo:lo + HEAD_SLOT] = (kv[:, lo:lo + HEAD_SLOT] + kr).astype(BF16)
    v_ref[0] = (kv[:, QK_WIDTH:] + _denominator_lanes()).astype(BF16)


def _ctxkv(ctx, mod3, ctx_row, gpre, win_c, gkv, wkv):
    b, c, d = ctx.shape
    const = lambda shape: pl.BlockSpec(shape, lambda i: (0,) * len(shape))
    rows = lambda w: pl.BlockSpec((1, c, w), lambda i: (i, 0, 0))
    return pl.pallas_call(
        _ctxkv_kernel,
        grid=(b,),
        in_specs=[
            rows(d),
            pl.BlockSpec((1, 6, d), lambda i: (ctx_row, 0, 0)),
            const(gpre.shape), const(win_c.shape), const(gkv.shape), const(wkv.shape),
        ],
        out_specs=[rows(QK_WIDTH), rows(QK_WIDTH)],
        out_shape=[
            jax.ShapeDtypeStruct((b, c, QK_WIDTH), BF16),
            jax.ShapeDtypeStruct((b, c, QK_WIDTH), BF16),
        ],
        name="ctxkv",
    )(ctx, mod3, gpre, win_c, gkv, wkv)


def _attn_kernel(q_ref, kc_ref, kl_ref, vc_ref, vl_ref, vcp_ref, vlp_ref, omain_ref, olast_ref,
                 p_ref, oprev_ref):
    t = pl.program_id(0)
    n_blocks = pl.num_programs(0) - 1
    n_ctx = kc_ref.shape[1]
    n_lat = kl_ref.shape[1]
    tq = q_ref.shape[1]
    pair_w = 2 * V_HEAD_DIM
    first_half = lax.broadcasted_iota(jnp.int32, (tq, pair_w), 1) < V_HEAD_DIM
    last = MLA_HEADS - 1

    @pl.when(t == 0)
    def _():
        p_ref[...] = jnp.ones_like(p_ref)
        oprev_ref[...] = jnp.zeros_like(oprev_ref)

    def scores(hd):
        lo = hd * HEAD_SLOT
        qh = q_ref[0, :, lo:lo + HEAD_SLOT]
        s_c = _dot_nt(qh, kc_ref[0, :, lo:lo + HEAD_SLOT])
        s_l = _dot_nt(qh, kl_ref[0, :, lo:lo + HEAD_SLOT])
        m = jnp.maximum(jnp.max(s_c, axis=-1, keepdims=True), jnp.max(s_l, axis=-1, keepdims=True))
        return s_c, s_l, m

    def normalize(o):
        return o / o[:, V_HEAD_DIM:V_HEAD_DIM + 1]

    def drain():
        o = _dot(p_ref[:, :n_ctx], vcp_ref[0]) + _dot(p_ref[:, n_ctx:], vlp_ref[0])
        o = pltpu.roll(normalize(o), V_HEAD_DIM, axis=1)
        olast_ref[0] = jnp.where(first_half, oprev_ref[...], o).astype(BF16)

    @pl.when(t < n_blocks)
    def _():
        drain()
        for pair in range(MLA_HEADS // 2):
            outs = []
            for hd in (2 * pair, 2 * pair + 1):
                lo = hd * HEAD_SLOT
                s_c, s_l, m = scores(hd)
                if hd == last:
                    p_ref[:, :n_ctx] = jnp.exp2(s_c - m).astype(BF16)
                    p_ref[:, n_ctx:] = jnp.exp2(s_l - m).astype(BF16)
                else:
                    o = _dot(jnp.exp2(s_c - m).astype(BF16), vc_ref[0, :, lo:lo + HEAD_SLOT])
                    for k0 in range(0, n_lat, PV_KEY_TILE):
                        p = jnp.exp2(s_l[:, k0:k0 + PV_KEY_TILE] - m).astype(BF16)
                        o = o + _dot(p, vl_ref[0, k0:k0 + PV_KEY_TILE, lo:lo + HEAD_SLOT])
                    outs.append(normalize(o))
            if len(outs) == 2:
                vlo = pair * pair_w
                omain_ref[0, :, vlo:vlo + pair_w] = jnp.where(
                    first_half, outs[0], pltpu.roll(outs[1], V_HEAD_DIM, axis=1)).astype(BF16)
            else:
                oprev_ref[...] = outs[0]

    @pl.when(t == n_blocks)
    def _():
        drain()


def _attention(q, kc, kl, vc, vl, tq=512):
    b, s, _ = q.shape
    c = kc.shape[1]
    nq = s // tq
    n_blocks = b * nq
    pair_w = 2 * V_HEAD_DIM
    main_w = ATTN_WIDTH - pair_w
    last = MLA_HEADS - 1
    cur = lambda t: jnp.minimum(t, n_blocks - 1)
    prev = lambda t: jnp.maximum(t - 1, 0)
    return pl.pallas_call(
        _attn_kernel,
        grid=(n_blocks + 1,),
        in_specs=[
            pl.BlockSpec((1, tq, QK_WIDTH), lambda t: (cur(t) // nq, cur(t) % nq, 0)),
            pl.BlockSpec((1, c, QK_WIDTH), lambda t: (cur(t) // nq, 0, 0)),
            pl.BlockSpec((1, s, QK_WIDTH), lambda t: (cur(t) // nq, 0, 0)),
            pl.BlockSpec((1, c, QK_WIDTH), lambda t: (cur(t) // nq, 0, 0)),
            pl.BlockSpec((1, s, QK_WIDTH), lambda t: (cur(t) // nq, 0, 0)),
            pl.BlockSpec((1, c, HEAD_SLOT), lambda t: (prev(t) // nq, 0, last)),
            pl.BlockSpec((1, s, HEAD_SLOT), lambda t: (prev(t) // nq, 0, last)),
        ],
        out_specs=[
            pl.BlockSpec((1, tq, main_w), lambda t: (cur(t) // nq, cur(t) % nq, 0)),
            pl.BlockSpec((1, tq, pair_w), lambda t: (prev(t) // nq, prev(t) % nq, 0)),
        ],
        out_shape=[
            jax.ShapeDtypeStruct((b, s, main_w), BF16),
            jax.ShapeDtypeStruct((b, s, pair_w), BF16),
        ],
        scratch_shapes=[
            pltpu.VMEM((tq, c + s), BF16),
            pltpu.VMEM((tq, pair_w), F32),
        ],
        compiler_params=pltpu.CompilerParams(
            dimension_semantics=("arbitrary",), vmem_limit_bytes=VMEM_LIMIT_BYTES),
        name="attn",
    )(q, kc, kl, vc, vl, vc, vl)


def _fourier_kernel(t_ref, tw_ref, u_ref, cc_ref, wf_ref, o_ref, tb_ref, cw_ref):
    @pl.when(pl.program_id(0) == 0)
    def _():
        tb_ref[...] = t_ref[...].astype(BF16)
        cc = cc_ref[...].astype(BF16)
        for g in range(FOURIER_GROUPS):
            cw_ref[g] = _dot(cc, wf_ref[g]).astype(BF16)

    n_sub = tb_ref.shape[1]
    f = _dot(tb_ref[...], u_ref[0])
    gd = FOURIER_GROUP_DIM
    for g in range(FOURIER_GROUPS):
        gr, gi = [], []
        for r in range(FFT_RADIX):
            lo = r * FOURIER_WIDTH + g * gd
            a = f[:n_sub, lo:lo + gd]
            b = f[n_sub:, lo:lo + gd]
            if r == 0:
                gr.append(a)
                gi.append(b)
            else:
                c = tw_ref[r - 1, 0]
                s = tw_ref[r - 1, 1]
                gr.append(a * c - b * s)
                gi.append(a * s + b * c)
        ar, ai = gr[0] + gr[2], gi[0] + gi[2]
        br, bi = gr[0] - gr[2], gi[0] - gi[2]
        cr, ci = gr[1] + gr[3], gi[1] + gi[3]
        dr, di = gr[1] - gr[3], gi[1] - gi[3]
        xr = jnp.concatenate([ar + cr, br - di, ar - cr, br + di], axis=0)
        xi = jnp.concatenate([ai + ci, bi + dr, ai - ci, bi - dr], axis=0)
        lhs = jnp.concatenate([xr, xi], axis=1).astype(BF16)
        o_ref[0, :, g * gd:(g + 1) * gd] = _dot(lhs, cw_ref[g]).astype(BF16)


def _fourier(u4, tmat, tw, cmat, wf):
    b, n_sub, _ = u4.shape
    s = n_sub * FFT_RADIX
    full = lambda a: pl.BlockSpec(a.shape, lambda i: (0,) * a.ndim)
    return pl.pallas_call(
        _fourier_kernel,
        grid=(b,),
        in_specs=[
            full(tmat), full(tw),
            pl.BlockSpec((1, n_sub, FFT_RADIX * FOURIER_WIDTH), lambda i: (i, 0, 0)),
            full(cmat), full(wf),
        ],
        out_specs=pl.BlockSpec((1, s, FOURIER_WIDTH), lambda i: (i, 0, 0)),
        out_shape=jax.ShapeDtypeStruct((b, s, FOURIER_WIDTH), BF16),
        scratch_shapes=[
            pltpu.VMEM(tmat.shape, BF16),
            pltpu.VMEM((FOURIER_GROUPS, 2 * FOURIER_GROUP_DIM, FOURIER_GROUP_DIM), BF16),
        ],
        compiler_params=pltpu.CompilerParams(
            dimension_semantics=("arbitrary",), vmem_limit_bytes=VMEM_LIMIT_BYTES),
        name="fourier",
    )(tmat, tw, u4, cmat, wf)


def _post_kernel(x_ref, four_ref, am_ref, al_ref, mod_ref, gpm_ref, gpf_ref, gqf_ref,
                 wof_ref, woa_ref, wg_ref, wu_ref, wd_ref, o_ref):
    x = x_ref[0]
    gt_m = mod_ref[0, 2:3, :]
    sh_f = mod_ref[0, 3:4, :]
    sc_f = mod_ref[0, 4:5, :]
    gt_f = mod_ref[0, 5:6, :]
    attn = jnp.concatenate([am_ref[0], al_ref[0]], axis=1)
    y = _dot(four_ref[0], wof_ref[...]) + _dot(attn, woa_ref[...])
    x1 = x + gt_m * _rms(y, gpm_ref[...])
    h2 = (_rms(x1, gpf_ref[...]) * (1.0 + sc_f) + sh_f).astype(BF16)
    g = _dot(h2, wg_ref[...])
    up = _dot(h2, wu_ref[...])
    act = (g / (1.0 + jnp.exp(-g)) * up).astype(BF16)
    ffn = _dot(act, wd_ref[...])
    o_ref[0] = x1 + gt_f * _rms(ffn, gqf_ref[...])


def _post(x, four, attn_main, attn_last, mod3, gpm, gpf, gqf, wof, woa, wg, wu, wd, tm=512):
    b, s, d = x.shape
    const = lambda shape: pl.BlockSpec(shape, lambda i, j: (0,) * len(shape),
                                       pipeline_mode=pl.Buffered(1))
    rows = lambda w: pl.BlockSpec((1, tm, w), lambda i, j: (i, j, 0))
    return pl.pallas_call(
        _post_kernel,
        grid=(b, s // tm),
        in_specs=[
            rows(d), rows(FOURIER_WIDTH), rows(attn_main.shape[2]), rows(attn_last.shape[2]),
            pl.BlockSpec((1, 6, d), lambda i, j: (i, 0, 0)),
            const(gpm.shape), const(gpf.shape), const(gqf.shape),
            const(wof.shape), const(woa.shape), const(wg.shape), const(wu.shape), const(wd.shape),
        ],
        out_specs=rows(d),
        out_shape=jax.ShapeDtypeStruct((b, s, d), F32),
        compiler_params=pltpu.CompilerParams(vmem_limit_bytes=VMEM_LIMIT_BYTES),
        name="post",
    )(x, four, attn_main, attn_last, mod3, gpm, gpf, gqf, wof, woa, wg, wu, wd)


def _rope_rotate_cols(w):
    a = QK_ROPE_DIM // 2
    hf = a // 2
    blocks = []
    for s0 in (0, a):
        blocks += [-w[..., s0 + hf:s0 + a], w[..., s0:s0 + hf]]
    return jnp.concatenate(blocks, axis=-1)


def _head_slot(nope, rope):
    pad = HEAD_SLOT - QK_NOPE_DIM - QK_ROPE_DIM
    return jnp.concatenate([nope, rope, jnp.zeros(rope.shape[:-1] + (pad,), rope.dtype)], axis=-1)


def _rope_tables(n_lat, q_scale):
    t = np.arange(n_lat)
    hf = QK_ROPE_DIM // 4
    inv_freq = ROPE_BASE ** (-np.arange(hf, dtype=np.float64) / hf)
    ar = (t // GRID_W)[:, None] * inv_freq[None, :]
    ac = (t % GRID_W)[:, None] * inv_freq[None, :]
    cos32 = np.concatenate([np.cos(ar), np.cos(ar), np.cos(ac), np.cos(ac)], axis=-1)
    sin32 = np.concatenate([np.sin(ar), np.sin(ar), np.sin(ac), np.sin(ac)], axis=-1)
    pad = np.zeros((n_lat, HEAD_SLOT - QK_NOPE_DIM - QK_ROPE_DIM))
    ones = np.ones((n_lat, QK_NOPE_DIM))
    zeros = np.zeros((n_lat, QK_NOPE_DIM))
    slot = lambda nope, rope: jnp.asarray(np.concatenate([nope, rope, pad], axis=-1).astype(np.float32))
    return (slot(ones * q_scale, cos32 * q_scale), slot(zeros, sin32 * q_scale),
            slot(zeros, cos32), slot(zeros, sin32))


def _dft_tables(n_pos, n_ch):
    n_sub = n_pos // FFT_RADIX
    m = np.arange(n_sub, dtype=np.int64)
    ang = 2.0 * np.pi * ((m[:, None] * m[None, :]) % n_sub) / n_sub
    tmat = np.concatenate([np.cos(ang), np.sin(ang)], axis=0).astype(np.float32)
    tw = np.zeros((FFT_RADIX - 1, 2, n_sub, FOURIER_GROUP_DIM), np.float32)
    for r in range(1, FFT_RADIX):
        a = 2.0 * np.pi * r * m / n_pos
        tw[r - 1, 0] = np.cos(a)[:, None]
        tw[r - 1, 1] = np.sin(a)[:, None]
    c = np.arange(n_ch, dtype=np.int64)
    angc = 2.0 * np.pi * ((c[:, None] * c[None, :]) % n_ch) / n_ch
    norm = 1.0 / np.sqrt(float(n_pos * n_ch))
    cmat = np.concatenate([np.cos(angc) * norm, -np.sin(angc) * norm], axis=0).astype(np.float32)
    return jnp.asarray(tmat), jnp.asarray(tw), jnp.asarray(cmat)


def kernel(x, c, ctx, c_ctx, w_ada, b_ada, g_pre_mix, g_post_mix, g_pre_ffn, g_post_ffn, w_in, g_q_a,
           w_q_b, g_kv_a, w_kv_b, w_fourier, w_out, w_gate, w_up, w_down):
    assert w_ada.shape[0] == 1, "single-layer block"
    batch, n_lat, d = x.shape

    mod_rows = -(-(batch + 1) // 8) * 8
    cc = jnp.concatenate([c, c_ctx[None, :], jnp.zeros((mod_rows - batch - 1, d), F32)], axis=0)
    mod = _adaln(cc, w_ada[0], b_ada[0][None, :])
    mod3 = mod.reshape(mod_rows, 6, d)

    w_in0 = w_in[0]
    w_kr = w_in0[:, ROPE_COL:]
    zeros_d = jnp.zeros((d, QK_NOPE_DIM), F32)
    kr_slot = _head_slot(zeros_d, w_kr)
    kr_rot_slot = _head_slot(zeros_d, _rope_rotate_cols(w_kr))
    win = jnp.concatenate([w_in0[:, :ROPE_COL], kr_slot, kr_rot_slot], axis=1).astype(BF16)
    win_c = jnp.concatenate([w_in0[:, KV_COL:ROPE_COL], kr_slot], axis=1).astype(BF16)

    wq3 = w_q_b[0].reshape(Q_LORA_RANK, MLA_HEADS, QK_NOPE_DIM + QK_ROPE_DIM)
    wq_nope, wq_rope = wq3[..., :QK_NOPE_DIM], wq3[..., QK_NOPE_DIM:]
    wq_a = _head_slot(wq_nope, wq_rope).reshape(Q_LORA_RANK, QK_WIDTH)
    wq_b = _head_slot(jnp.zeros_like(wq_nope), _rope_rotate_cols(wq_rope)).reshape(Q_LORA_RANK, QK_WIDTH)
    wq = jnp.concatenate([wq_a, wq_b], axis=1).astype(BF16)

    wkv3 = w_kv_b[0].reshape(KV_LORA_RANK, MLA_HEADS, QK_NOPE_DIM + V_HEAD_DIM)
    wk_nope, wv = wkv3[..., :QK_NOPE_DIM], wkv3[..., QK_NOPE_DIM:]
    wk_slots = _head_slot(wk_nope, jnp.zeros(wk_nope.shape[:-1] + (QK_ROPE_DIM,), F32))
    wv_slots = jnp.concatenate([wv, jnp.zeros(wv.shape[:-1] + (HEAD_SLOT - V_HEAD_DIM,), F32)], axis=-1)
    wkv = jnp.concatenate([wk_slots.reshape(KV_LORA_RANK, QK_WIDTH),
                           wv_slots.reshape(KV_LORA_RANK, QK_WIDTH)], axis=1).astype(BF16)

    q_scale = float((QK_NOPE_DIM + QK_ROPE_DIM) ** -0.5 * np.log2(np.e))
    cosq, sinq, cosk, sink = _rope_tables(n_lat, q_scale)
    tmat, tw, cmat = _dft_tables(n_lat, FOURIER_GROUP_DIM)

    row2 = lambda g: g[0][None, :]
    u_f, q, k_lat, v_lat = _premix(x, mod3, row2(g_pre_mix), win, row2(g_q_a), wq, row2(g_kv_a), wkv,
                                   cosq, sinq, cosk, sink)
    k_ctx, v_ctx = _ctxkv(ctx, mod3, batch, row2(g_pre_mix), win_c, row2(g_kv_a), wkv)
    attn_main, attn_last = _attention(q, k_ctx, k_lat, v_ctx, v_lat)
    four = _fourier(u_f, tmat, tw, cmat, w_fourier[0].astype(BF16))
    w_out0 = w_out[0].astype(BF16)
    return _post(x, four, attn_main, attn_last, mod3, row2(g_post_mix), row2(g_pre_ffn), row2(g_post_ffn),
                 w_out0[:FOURIER_WIDTH], w_out0[FOURIER_WIDTH:],
                 w_gate[0].astype(BF16), w_up[0].astype(BF16), w_down[0].astype(BF16))
```

```python
import functools

import numpy as np
import jax
import jax.numpy as jnp
from jax import lax
from jax.experimental import pallas as pl
from jax.experimental.pallas import tpu as pltpu

F32 = jnp.float32
BF16 = jnp.bfloat16

D_MODEL = 1024
GRID_W = 64
FOURIER_GROUPS = 4
FOURIER_GROUP_DIM = 128
FOURIER_WIDTH = FOURIER_GROUPS * FOURIER_GROUP_DIM
MLA_HEADS = 8
QK_NOPE_DIM = 64
QK_ROPE_DIM = 32
V_HEAD_DIM = 64
Q_LORA_RANK = 256
KV_LORA_RANK = 128
KV_COL = FOURIER_WIDTH + Q_LORA_RANK
ROPE_COL = KV_COL + KV_LORA_RANK
ROPE_BASE = 10000.0
NORM_EPS = 1e-6
FFT_RADIX = 4
HEAD_SLOT = 128
SHIFT_DEN_MIN = 2.0 ** -40
SHIFT_DEN_MAX = 2.0 ** 40
PV_KEY_TILE = 256
ATTN_WIDTH = MLA_HEADS * V_HEAD_DIM
QK_WIDTH = MLA_HEADS * HEAD_SLOT

VMEM_LIMIT_BYTES = 56 * 1024 * 1024


def _rms(x, g):
    return x * lax.rsqrt(jnp.mean(x * x, axis=-1, keepdims=True) + NORM_EPS) * g


def _dot(a, b):
    return jnp.dot(a, b, preferred_element_type=F32)


def _dot_nt(a, b):
    return lax.dot_general(a, b, (((1,), (1,)), ((), ())), preferred_element_type=F32)


def _denominator_lanes():
    lane = lax.broadcasted_iota(jnp.int32, (1, QK_WIDTH), 1)
    return (lane % HEAD_SLOT == V_HEAD_DIM).astype(F32)


def _adaln_kernel(c_ref, w_ref, b_ref, o_ref):
    c = c_ref[...]
    a = c / (1.0 + jnp.exp(-c))
    o_ref[...] = _dot(a.astype(BF16), w_ref[...].astype(BF16)) + b_ref[...]


def _adaln(cc, w_ada, b_ada, tn=512):
    rows, d = cc.shape
    n = w_ada.shape[1]
    return pl.pallas_call(
        _adaln_kernel,
        grid=(n // tn,),
        in_specs=[
            pl.BlockSpec((rows, d), lambda j: (0, 0)),
            pl.BlockSpec((d, tn), lambda j: (0, j)),
            pl.BlockSpec((1, tn), lambda j: (0, j)),
        ],
        out_specs=pl.BlockSpec((rows, tn), lambda j: (0, j)),
        out_shape=jax.ShapeDtypeStruct((rows, n), F32),
        name="adaln",
    )(cc, w_ada, b_ada)


def _premix_kernel(x_ref, mod_ref, gpre_ref, win_ref, gq_ref, wq_ref, gkv_ref, wkv_ref,
                   cosq_ref, sinq_ref, cosk_ref, sink_ref, u_ref, q_ref, k_ref, v_ref, us_ref):
    x = x_ref[0]
    shift = mod_ref[0, 0:1, :]
    scale = mod_ref[0, 1:2, :]
    h = _rms(x, gpre_ref[...]) * (1.0 + scale) + shift
    p = _dot(h.astype(BF16), win_ref[...])
    sub = us_ref.shape[1] // FFT_RADIX
    for g in range(FOURIER_GROUPS):
        glo = g * FOURIER_GROUP_DIM
        us_ref[g] = p[:, glo:glo + FOURIER_GROUP_DIM]
        for r in range(FFT_RADIX):
            lo = r * FOURIER_WIDTH + glo
            u_ref[0, :, lo:lo + FOURIER_GROUP_DIM] = (
                us_ref[g, pl.ds(r, sub, stride=FFT_RADIX), :].astype(BF16))

    qn = _rms(p[:, FOURIER_WIDTH:KV_COL], gq_ref[...]).astype(BF16)
    qq = _dot(qn, wq_ref[...])
    cosq = cosq_ref[...]
    sinq = sinq_ref[...]
    for hd in range(MLA_HEADS):
        lo = hd * HEAD_SLOT
        q_ref[0, :, lo:lo + HEAD_SLOT] = (
            qq[:, lo:lo + HEAD_SLOT] * cosq + qq[:, QK_WIDTH + lo:QK_WIDTH + lo + HEAD_SLOT] * sinq
        ).astype(BF16)

    kvn = _rms(p[:, KV_COL:ROPE_COL], gkv_ref[...]).astype(BF16)
    kv = _dot(kvn, wkv_ref[...])
    kr = (p[:, ROPE_COL:ROPE_COL + HEAD_SLOT] * cosk_ref[...]
          + p[:, ROPE_COL + HEAD_SLOT:ROPE_COL + 2 * HEAD_SLOT] * sink_ref[...])
    for hd in range(MLA_HEADS):
        lo = hd * HEAD_SLOT
        k_ref[0, :, lo:lo + HEAD_SLOT] = (kv[:, lo:lo + HEAD_SLOT] + kr).astype(BF16)
    v_ref[0] = (kv[:, QK_WIDTH:] + _denominator_lanes()).astype(BF16)


def _premix(x, mod3, gpre, win, gq, wq, gkv, wkv, cosq, sinq, cosk, sink, tm=512):
    b, s, d = x.shape
    const = lambda shape: pl.BlockSpec(shape, lambda i, j: (0,) * len(shape))
    rows = lambda w: pl.BlockSpec((1, tm, w), lambda i, j: (i, j, 0))
    table = pl.BlockSpec((tm, HEAD_SLOT), lambda i, j: (j, 0))
    return pl.pallas_call(
        _premix_kernel,
        grid=(b, s // tm),
        in_specs=[
            rows(d),
            pl.BlockSpec((1, 6, d), lambda i, j: (i, 0, 0)),
            const(gpre.shape), const(win.shape), const(gq.shape), const(wq.shape),
            const(gkv.shape), const(wkv.shape),
            table, table, table, table,
        ],
        out_specs=[
            pl.BlockSpec((1, tm // FFT_RADIX, FFT_RADIX * FOURIER_WIDTH), lambda i, j: (i, j, 0)),
            rows(QK_WIDTH), rows(QK_WIDTH), rows(QK_WIDTH)],
        out_shape=[
            jax.ShapeDtypeStruct((b, s // FFT_RADIX, FFT_RADIX * FOURIER_WIDTH), BF16),
            jax.ShapeDtypeStruct((b, s, QK_WIDTH), BF16),
            jax.ShapeDtypeStruct((b, s, QK_WIDTH), BF16),
            jax.ShapeDtypeStruct((b, s, QK_WIDTH), BF16),
        ],
        scratch_shapes=[pltpu.VMEM((FOURIER_GROUPS, tm, FOURIER_GROUP_DIM), F32)],
        compiler_params=pltpu.CompilerParams(vmem_limit_bytes=VMEM_LIMIT_BYTES),
        name="premix",
    )(x, mod3, gpre, win, gq, wq, gkv, wkv, cosq, sinq, cosk, sink)


def _ctxkv_kernel(x_ref, mod_ref, gpre_ref, win_ref, gkv_ref, wkv_ref, k_ref, v_ref):
    x = x_ref[0]
    shift = mod_ref[0, 0:1, :]
    scale = mod_ref[0, 1:2, :]
    h = _rms(x, gpre_ref[...]) * (1.0 + scale) + shift
    p = _dot(h.astype(BF16), win_ref[...])
    kvn = _rms(p[:, :KV_LORA_RANK], gkv_ref[...]).astype(BF16)
    kv = _dot(kvn, wkv_ref[...])
    kr = p[:, KV_LORA_RANK:]
    for hd in range(MLA_HEADS):
        lo = hd * HEAD_SLOT
        k_ref[0, :, lo:lo + HEAD_SLOT] = (kv[:, lo:lo + HEAD_SLOT] + kr).astype(BF16)
    v_ref[0] = (kv[:, QK_WIDTH:] + _denominator_lanes()).astype(BF16)


def _ctxkv(ctx, mod3, ctx_row, gpre, win_c, gkv, wkv):
    b, c, d = ctx.shape
    const = lambda shape: pl.BlockSpec(shape, lambda i: (0,) * len(shape))
    rows = lambda w: pl.BlockSpec((1, c, w), lambda i: (i, 0, 0))
    return pl.pallas_call(
        _ctxkv_kernel,
        grid=(b,),
        in_specs=[
            rows(d),
            pl.BlockSpec((1, 6, d), lambda i: (ctx_row, 0, 0)),
            const(gpre.shape), const(win_c.shape), const(gkv.shape), const(wkv.shape),
        ],
        out_specs=[rows(QK_WIDTH), rows(QK_WIDTH)],
        out_shape=[
            jax.ShapeDtypeStruct((b, c, QK_WIDTH), BF16),
            jax.ShapeDtypeStruct((b, c, QK_WIDTH), BF16),
        ],
        name="ctxkv",
    )(ctx, mod3, gpre, win_c, gkv, wkv)


def _attn_kernel(q_ref, kc_ref, kl_ref, vc_ref, vl_ref, vcp_ref, vlp_ref, omain_ref, olast_ref,
                 p_ref, oprev_ref):
    t = pl.program_id(0)
    n_blocks = pl.num_programs(0) - 1
    n_ctx = kc_ref.shape[1]
    n_lat = kl_ref.shape[1]
    tq = q_ref.shape[1]
    pair_w = 2 * V_HEAD_DIM
    first_half = lax.broadcasted_iota(jnp.int32, (tq, pair_w), 1) < V_HEAD_DIM
    last = MLA_HEADS - 1

    @pl.when(t == 0)
    def _():
        p_ref[...] = jnp.ones_like(p_ref)
        oprev_ref[...] = jnp.zeros_like(oprev_ref)

    def drain():
        o = _dot(p_ref[:, :n_ctx], vcp_ref[0]) + _dot(p_ref[:, n_ctx:], vlp_ref[0])
        o = pltpu.roll(o / o[:, V_HEAD_DIM:V_HEAD_DIM + 1], V_HEAD_DIM, axis=1)
        olast_ref[0] = jnp.where(first_half, oprev_ref[...], o).astype(BF16)

    def head(hd, exact):
        lo = hd * HEAD_SLOT
        qh = q_ref[0, :, lo:lo + HEAD_SLOT]
        s_c = _dot_nt(qh, kc_ref[0, :, lo:lo + HEAD_SLOT])
        s_l = _dot_nt(qh, kl_ref[0, :, lo:lo + HEAD_SLOT])
        shift = jnp.max(s_c, axis=-1, keepdims=True)
        if exact:
            shift = jnp.maximum(shift, jnp.max(s_l, axis=-1, keepdims=True))
        if hd == last:
            p_c = jnp.exp2(s_c - shift)
            p_l = jnp.exp2(s_l - shift)
            p_ref[:, :n_ctx] = p_c.astype(BF16)
            p_ref[:, n_ctx:] = p_l.astype(BF16)
            return None, jnp.sum(p_c, axis=-1, keepdims=True) + jnp.sum(p_l, axis=-1, keepdims=True)
        o = _dot(jnp.exp2(s_c - shift).astype(BF16), vc_ref[0, :, lo:lo + HEAD_SLOT])
        for k0 in range(0, n_lat, PV_KEY_TILE):
            p = jnp.exp2(s_l[:, k0:k0 + PV_KEY_TILE] - shift).astype(BF16)
            o = o + _dot(p, vl_ref[0, k0:k0 + PV_KEY_TILE, lo:lo + HEAD_SLOT])
        den = o[:, V_HEAD_DIM:V_HEAD_DIM + 1]
        return o / den, den

    def block(exact):
        dens = []
        for pair in range(MLA_HEADS // 2):
            outs = []
            for hd in (2 * pair, 2 * pair + 1):
                o, den = head(hd, exact)
                dens.append(den)
                if o is not None:
                    outs.append(o)
            if len(outs) == 2:
                vlo = pair * pair_w
                omain_ref[0, :, vlo:vlo + pair_w] = jnp.where(
                    first_half, outs[0], pltpu.roll(outs[1], V_HEAD_DIM, axis=1)).astype(BF16)
            else:
                oprev_ref[...] = outs[0]
        return dens

    @pl.when(t < n_blocks)
    def _():
        drain()
        dens = block(exact=False)
        lo_den, hi_den = dens[0], dens[0]
        for den in dens[1:]:
            lo_den = jnp.minimum(lo_den, den)
            hi_den = jnp.maximum(hi_den, den)
        trusted = jnp.logical_and(jnp.min(lo_den) >= SHIFT_DEN_MIN, jnp.max(hi_den) <= SHIFT_DEN_MAX)

        @pl.when(jnp.logical_not(trusted))
        def _():
            block(exact=True)

    @pl.when(t == n_blocks)
    def _():
        drain()


def _attention(q, kc, kl, vc, vl, tq=512):
    b, s, _ = q.shape
    c = kc.shape[1]
    nq = s // tq
    n_blocks = b * nq
    pair_w = 2 * V_HEAD_DIM
    main_w = ATTN_WIDTH - pair_w
    last = MLA_HEADS - 1
    cur = lambda t: jnp.minimum(t, n_blocks - 1)
    prev = lambda t: jnp.maximum(t - 1, 0)
    return pl.pallas_call(
        _attn_kernel,
        grid=(n_blocks + 1,),
        in_specs=[
            pl.BlockSpec((1, tq, QK_WIDTH), lambda t: (cur(t) // nq, cur(t) % nq, 0)),
            pl.BlockSpec((1, c, QK_WIDTH), lambda t: (cur(t) // nq, 0, 0)),
            pl.BlockSpec((1, s, QK_WIDTH), lambda t: (cur(t) // nq, 0, 0)),
            pl.BlockSpec((1, c, QK_WIDTH), lambda t: (cur(t) // nq, 0, 0)),
            pl.BlockSpec((1, s, QK_WIDTH), lambda t: (cur(t) // nq, 0, 0)),
            pl.BlockSpec((1, c, HEAD_SLOT), lambda t: (prev(t) // nq, 0, last)),
            pl.BlockSpec((1, s, HEAD_SLOT), lambda t: (prev(t) // nq, 0, last)),
        ],
        out_specs=[
            pl.BlockSpec((1, tq, main_w), lambda t: (cur(t) // nq, cur(t) % nq, 0)),
            pl.BlockSpec((1, tq, pair_w), lambda t: (prev(t) // nq, prev(t) % nq, 0)),
        ],
        out_shape=[
            jax.ShapeDtypeStruct((b, s, main_w), BF16),
            jax.ShapeDtypeStruct((b, s, pair_w), BF16),
        ],
        scratch_shapes=[
            pltpu.VMEM((tq, c + s), BF16),
            pltpu.VMEM((tq, pair_w), F32),
        ],
        compiler_params=pltpu.CompilerParams(
            dimension_semantics=("arbitrary",), vmem_limit_bytes=VMEM_LIMIT_BYTES),
        name="attn",
    )(q, kc, kl, vc, vl, vc, vl)


def _fourier_kernel(t_ref, tw_ref, u_ref, cc_ref, wf_ref, o_ref, tb_ref, cw_ref):
    @pl.when(pl.program_id(0) == 0)
    def _():
        tb_ref[...] = t_ref[...].astype(BF16)
        cc = cc_ref[...].astype(BF16)
        for g in range(FOURIER_GROUPS):
            cw_ref[g] = _dot(cc, wf_ref[g]).astype(BF16)

    n_sub = tb_ref.shape[1]
    f = _dot(tb_ref[...], u_ref[0])
    gd = FOURIER_GROUP_DIM
    for g in range(FOURIER_GROUPS):
        gr, gi = [], []
        for r in range(FFT_RADIX):
            lo = r * FOURIER_WIDTH + g * gd
            a = f[:n_sub, lo:lo + gd]
            b = f[n_sub:, lo:lo + gd]
            if r == 0:
                gr.append(a)
                gi.append(b)
            else:
                c = tw_ref[r - 1, 0]
                s = tw_ref[r - 1, 1]
                gr.append(a * c - b * s)
                gi.append(a * s + b * c)
        ar, ai = gr[0] + gr[2], gi[0] + gi[2]
        br, bi = gr[0] - gr[2], gi[0] - gi[2]
        cr, ci = gr[1] + gr[3], gi[1] + gi[3]
        dr, di = gr[1] - gr[3], gi[1] - gi[3]
        xr = jnp.concatenate([ar + cr, br - di, ar - cr, br + di], axis=0)
        xi = jnp.concatenate([ai + ci, bi + dr, ai - ci, bi - dr], axis=0)
        lhs = jnp.concatenate([xr, xi], axis=1).astype(BF16)
        o_ref[0, :, g * gd:(g + 1) * gd] = _dot(lhs, cw_ref[g]).astype(BF16)


def _fourier(u4, tmat, tw, cmat, wf):
    b, n_sub, _ = u4.shape
    s = n_sub * FFT_RADIX
    full = lambda a: pl.BlockSpec(a.shape, lambda i: (0,) * a.ndim)
    return pl.pallas_call(
        _fourier_kernel,
        grid=(b,),
        in_specs=[
            full(tmat), full(tw),
            pl.BlockSpec((1, n_sub, FFT_RADIX * FOURIER_WIDTH), lambda i: (i, 0, 0)),
            full(cmat), full(wf),
        ],
        out_specs=pl.BlockSpec((1, s, FOURIER_WIDTH), lambda i: (i, 0, 0)),
        out_shape=jax.ShapeDtypeStruct((b, s, FOURIER_WIDTH), BF16),
        scratch_shapes=[
            pltpu.VMEM(tmat.shape, BF16),
            pltpu.VMEM((FOURIER_GROUPS, 2 * FOURIER_GROUP_DIM, FOURIER_GROUP_DIM), BF16),
        ],
        compiler_params=pltpu.CompilerParams(
            dimension_semantics=("arbitrary",), vmem_limit_bytes=VMEM_LIMIT_BYTES),
        name="fourier",
    )(tmat, tw, u4, cmat, wf)


def _post_kernel(x_ref, four_ref, am_ref, al_ref, mod_ref, gpm_ref, gpf_ref, gqf_ref,
                 wof_ref, woa_ref, wg_ref, wu_ref, wd_ref, o_ref):
    x = x_ref[0]
    gt_m = mod_ref[0, 2:3, :]
    sh_f = mod_ref[0, 3:4, :]
    sc_f = mod_ref[0, 4:5, :]
    gt_f = mod_ref[0, 5:6, :]
    attn = jnp.concatenate([am_ref[0], al_ref[0]], axis=1)
    y = _dot(four_ref[0], wof_ref[...]) + _dot(attn, woa_ref[...])
    x1 = x + gt_m * _rms(y, gpm_ref[...])
    h2 = (_rms(x1, gpf_ref[...]) * (1.0 + sc_f) + sh_f).astype(BF16)
    g = _dot(h2, wg_ref[...])
    up = _dot(h2, wu_ref[...])
    act = (g / (1.0 + jnp.exp(-g)) * up).astype(BF16)
    ffn = _dot(act, wd_ref[...])
    o_ref[0] = x1 + gt_f * _rms(ffn, gqf_ref[...])


def _post(x, four, attn_main, attn_last, mod3, gpm, gpf, gqf, wof, woa, wg, wu, wd, tm=512):
    b, s, d = x.shape
    const = lambda shape: pl.BlockSpec(shape, lambda i, j: (0,) * len(shape),
                                       pipeline_mode=pl.Buffered(1))
    rows = lambda w: pl.BlockSpec((1, tm, w), lambda i, j: (i, j, 0))
    return pl.pallas_call(
        _post_kernel,
        grid=(b, s // tm),
        in_specs=[
            rows(d), rows(FOURIER_WIDTH), rows(attn_main.shape[2]), rows(attn_last.shape[2]),
            pl.BlockSpec((1, 6, d), lambda i, j: (i, 0, 0)),
            const(gpm.shape), const(gpf.shape), const(gqf.shape),
            const(wof.shape), const(woa.shape), const(wg.shape), const(wu.shape), const(wd.shape),
        ],
        out_specs=rows(d),
        out_shape=jax.ShapeDtypeStruct((b, s, d), F32),
        compiler_params=pltpu.CompilerParams(vmem_limit_bytes=VMEM_LIMIT_BYTES),
        name="post",
    )(x, four, attn_main, attn_last, mod3, gpm, gpf, gqf, wof, woa, wg, wu, wd)


def _rope_rotate_cols(w):
    a = QK_ROPE_DIM // 2
    hf = a // 2
    blocks = []
    for s0 in (0, a):
        blocks += [-w[..., s0 + hf:s0 + a], w[..., s0:s0 + hf]]
    return jnp.concatenate(blocks, axis=-1)


def _head_slot(nope, rope):
    pad = HEAD_SLOT - QK_NOPE_DIM - QK_ROPE_DIM
    return jnp.concatenate([nope, rope, jnp.zeros(rope.shape[:-1] + (pad,), rope.dtype)], axis=-1)


def _rope_tables(n_lat, q_scale):
    t = np.arange(n_lat)
    hf = QK_ROPE_DIM // 4
    inv_freq = ROPE_BASE ** (-np.arange(hf, dtype=np.float64) / hf)
    ar = (t // GRID_W)[:, None] * inv_freq[None, :]
    ac = (t % GRID_W)[:, None] * inv_freq[None, :]
    cos32 = np.concatenate([np.cos(ar), np.cos(ar), np.cos(ac), np.cos(ac)], axis=-1)
    sin32 = np.concatenate([np.sin(ar), np.sin(ar), np.sin(ac), np.sin(ac)], axis=-1)
    pad = np.zeros((n_lat, HEAD_SLOT - QK_NOPE_DIM - QK_ROPE_DIM))
    ones = np.ones((n_lat, QK_NOPE_DIM))
    zeros = np.zeros((n_lat, QK_NOPE_DIM))
    slot = lambda nope, rope: jnp.asarray(np.concatenate([nope, rope, pad], axis=-1).astype(np.float32))
    return (slot(ones * q_scale, cos32 * q_scale), slot(zeros, sin32 * q_scale),
            slot(zeros, cos32), slot(zeros, sin32))


def _dft_tables(n_pos, n_ch):
    n_sub = n_pos // FFT_RADIX
    m = np.arange(n_sub, dtype=np.int64)
    ang = 2.0 * np.pi * ((m[:, None] * m[None, :]) % n_sub) / n_sub
    tmat = np.concatenate([np.cos(ang), np.sin(ang)], axis=0).astype(np.float32)
    tw = np.zeros((FFT_RADIX - 1, 2, n_sub, FOURIER_GROUP_DIM), np.float32)
    for r in range(1, FFT_RADIX):
        a = 2.0 * np.pi * r * m / n_pos
        tw[r - 1, 0] = np.cos(a)[:, None]
        tw[r - 1, 1] = np.sin(a)[:, None]
    c = np.arange(n_ch, dtype=np.int64)
    angc = 2.0 * np.pi * ((c[:, None] * c[None, :]) % n_ch) / n_ch
    norm = 1.0 / np.sqrt(float(n_pos * n_ch))
    cmat = np.concatenate([np.cos(angc) * norm, -np.sin(angc) * norm], axis=0).astype(np.float32)
    return jnp.asarray(tmat), jnp.asarray(tw), jnp.asarray(cmat)


def kernel(x, c, ctx, c_ctx, w_ada, b_ada, g_pre_mix, g_post_mix, g_pre_ffn, g_post_ffn, w_in, g_q_a,
           w_q_b, g_kv_a, w_kv_b, w_fourier, w_out, w_gate, w_up, w_down):
    assert w_ada.shape[0] == 1, "single-layer block"
    batch, n_lat, d = x.shape

    mod_rows = -(-(batch + 1) // 8) * 8
    cc = jnp.concatenate([c, c_ctx[None, :], jnp.zeros((mod_rows - batch - 1, d), F32)], axis=0)
    mod = _adaln(cc, w_ada[0], b_ada[0][None, :])
    mod3 = mod.reshape(mod_rows, 6, d)

    w_in0 = w_in[0]
    w_kr = w_in0[:, ROPE_COL:]
    zeros_d = jnp.zeros((d, QK_NOPE_DIM), F32)
    kr_slot = _head_slot(zeros_d, w_kr)
    kr_rot_slot = _head_slot(zeros_d, _rope_rotate_cols(w_kr))
    win = jnp.concatenate([w_in0[:, :ROPE_COL], kr_slot, kr_rot_slot], axis=1).astype(BF16)
    win_c = jnp.concatenate([w_in0[:, KV_COL:ROPE_COL], kr_slot], axis=1).astype(BF16)

    wq3 = w_q_b[0].reshape(Q_LORA_RANK, MLA_HEADS, QK_NOPE_DIM + QK_ROPE_DIM)
    wq_nope, wq_rope = wq3[..., :QK_NOPE_DIM], wq3[..., QK_NOPE_DIM:]
    wq_a = _head_slot(wq_nope, wq_rope).reshape(Q_LORA_RANK, QK_WIDTH)
    wq_b = _head_slot(jnp.zeros_like(wq_nope), _rope_rotate_cols(wq_rope)).reshape(Q_LORA_RANK, QK_WIDTH)
    wq = jnp.concatenate([wq_a, wq_b], axis=1).astype(BF16)

    wkv3 = w_kv_b[0].reshape(KV_LORA_RANK, MLA_HEADS, QK_NOPE_DIM + V_HEAD_DIM)
    wk_nope, wv = wkv3[..., :QK_NOPE_DIM], wkv3[..., QK_NOPE_DIM:]
    wk_slots = _head_slot(wk_nope, jnp.zeros(wk_nope.shape[:-1] + (QK_ROPE_DIM,), F32))
    wv_slots = jnp.concatenate([wv, jnp.zeros(wv.shape[:-1] + (HEAD_SLOT - V_HEAD_DIM,), F32)], axis=-1)
    wkv = jnp.concatenate([wk_slots.reshape(KV_LORA_RANK, QK_WIDTH),
                           wv_slots.reshape(KV_LORA_RANK, QK_WIDTH)], axis=1).astype(BF16)

    q_scale = float((QK_NOPE_DIM + QK_ROPE_DIM) ** -0.5 * np.log2(np.e))
    cosq, sinq, cosk, sink = _rope_tables(n_lat, q_scale)
    tmat, tw, cmat = _dft_tables(n_lat, FOURIER_GROUP_DIM)

    row2 = lambda g: g[0][None, :]
    u_f, q, k_lat, v_lat = _premix(x, mod3, row2(g_pre_mix), win, row2(g_q_a), wq, row2(g_kv_a), wkv,
                                   cosq, sinq, cosk, sink)
    k_ctx, v_ctx = _ctxkv(ctx, mod3, batch, row2(g_pre_mix), win_c, row2(g_kv_a), wkv)
    attn_main, attn_last = _attention(q, k_ctx, k_lat, v_ctx, v_lat)
    four = _fourier(u_f, tmat, tw, cmat, w_fourier[0].astype(BF16))
    w_out0 = w_out[0].astype(BF16)
    return _post(x, four, attn_main, attn_last, mod3, row2(g_post_mix), row2(g_pre_ffn), row2(g_post_ffn),
                 w_out0[:FOURIER_WIDTH], w_out0[FOURIER_WIDTH:],
                 w_gate[0].astype(BF16), w_up[0].astype(BF16), w_down[0].astype(BF16))
```

```python
import functools

import numpy as np
import jax
import jax.numpy as jnp
from jax import lax
from jax.experimental import pallas as pl
from jax.experimental.pallas import tpu as pltpu

F32 = jnp.float32
BF16 = jnp.bfloat16

D_MODEL = 1024
GRID_W = 64
FOURIER_GROUPS = 4
FOURIER_GROUP_DIM = 128
FOURIER_WIDTH = FOURIER_GROUPS * FOURIER_GROUP_DIM
MLA_HEADS = 8
QK_NOPE_DIM = 64
QK_ROPE_DIM = 32
V_HEAD_DIM = 64
Q_LORA_RANK = 256
KV_LORA_RANK = 128
KV_COL = FOURIER_WIDTH + Q_LORA_RANK
ROPE_COL = KV_COL + KV_LORA_RANK
ROPE_BASE = 10000.0
NORM_EPS = 1e-6
FFT_RADIX = 4
HEAD_SLOT = 128
SHIFT_DEN_MIN = 2.0 ** -40
SHIFT_DEN_MAX = 2.0 ** 40
PV_KEY_TILE = 256
ATTN_WIDTH = MLA_HEADS * V_HEAD_DIM
QK_WIDTH = MLA_HEADS * HEAD_SLOT

VMEM_LIMIT_BYTES = 56 * 1024 * 1024


def _rms(x, g):
    return x * lax.rsqrt(jnp.mean(x * x, axis=-1, keepdims=True) + NORM_EPS) * g


def _dot(a, b):
    return jnp.dot(a, b, preferred_element_type=F32)


def _dot_nt(a, b):
    return lax.dot_general(a, b, (((1,), (1,)), ((), ())), preferred_element_type=F32)


def _denominator_lanes():
    lane = lax.broadcasted_iota(jnp.int32, (1, QK_WIDTH), 1)
    return (lane % HEAD_SLOT == V_HEAD_DIM).astype(F32)


def _adaln_kernel(c_ref, w_ref, b_ref, o_ref):
    c = c_ref[...]
    a = c / (1.0 + jnp.exp(-c))
    o_ref[...] = _dot(a.astype(BF16), w_ref[...].astype(BF16)) + b_ref[...]


def _adaln(cc, w_ada, b_ada, tn=512):
    rows, d = cc.shape
    n = w_ada.shape[1]
    return pl.pallas_call(
        _adaln_kernel,
        grid=(n // tn,),
        in_specs=[
            pl.BlockSpec((rows, d), lambda j: (0, 0)),
            pl.BlockSpec((d, tn), lambda j: (0, j)),
            pl.BlockSpec((1, tn), lambda j: (0, j)),
        ],
        out_specs=pl.BlockSpec((rows, tn), lambda j: (0, j)),
        out_shape=jax.ShapeDtypeStruct((rows, n), F32),
        name="adaln",
    )(cc, w_ada, b_ada)


def _premix_kernel(x_ref, mod_ref, gpre_ref, win_ref, gq_ref, wq_ref, gkv_ref, wkv_ref,
                   cosq_ref, sinq_ref, cosk_ref, sink_ref, u_ref, q_ref, k_ref, v_ref, us_ref):
    x = x_ref[0]
    shift = mod_ref[0, 0:1, :]
    scale = mod_ref[0, 1:2, :]
    h = _rms(x, gpre_ref[...]) * (1.0 + scale) + shift
    p = _dot(h.astype(BF16), win_ref[...])
    sub = us_ref.shape[1] // FFT_RADIX
    for g in range(FOURIER_GROUPS):
        glo = g * FOURIER_GROUP_DIM
        us_ref[g] = p[:, glo:glo + FOURIER_GROUP_DIM]
        for r in range(FFT_RADIX):
            lo = r * FOURIER_WIDTH + glo
            u_ref[0, :, lo:lo + FOURIER_GROUP_DIM] = (
                us_ref[g, pl.ds(r, sub, stride=FFT_RADIX), :].astype(BF16))

    qn = _rms(p[:, FOURIER_WIDTH:KV_COL], gq_ref[...]).astype(BF16)
    qq = _dot(qn, wq_ref[...])
    cosq = cosq_ref[...]
    sinq = sinq_ref[...]
    for hd in range(MLA_HEADS):
        lo = hd * HEAD_SLOT
        q_ref[0, :, lo:lo + HEAD_SLOT] = (
            qq[:, lo:lo + HEAD_SLOT] * cosq + qq[:, QK_WIDTH + lo:QK_WIDTH + lo + HEAD_SLOT] * sinq
        ).astype(BF16)

    kvn = _rms(p[:, KV_COL:ROPE_COL], gkv_ref[...]).astype(BF16)
    kv = _dot(kvn, wkv_ref[...])
    kr = (p[:, ROPE_COL:ROPE_COL + HEAD_SLOT] * cosk_ref[...]
          + p[:, ROPE_COL + HEAD_SLOT:ROPE_COL + 2 * HEAD_SLOT] * sink_ref[...])
    for hd in range(MLA_HEADS):
        lo = hd * HEAD_SLOT
        k_ref[0, :, lo:lo + HEAD_SLOT] = (kv[:, lo:lo + HEAD_SLOT] + kr).astype(BF16)
    v_ref[0] = (kv[:, QK_WIDTH:] + _denominator_lanes()).astype(BF16)


def _premix(x, mod3, gpre, win, gq, wq, gkv, wkv, cosq, sinq, cosk, sink, tm=512):
    b, s, d = x.shape
    const = lambda shape: pl.BlockSpec(shape, lambda i, j: (0,) * len(shape))
    rows = lambda w: pl.BlockSpec((1, tm, w), lambda i, j: (i, j, 0))
    table = pl.BlockSpec((tm, HEAD_SLOT), lambda i, j: (j, 0))
    return pl.pallas_call(
        _premix_kernel,
        grid=(b, s // tm),
        in_specs=[
            rows(d),
            pl.BlockSpec((1, 6, d), lambda i, j: (i, 0, 0)),
            const(gpre.shape), const(win.shape), const(gq.shape), const(wq.shape),
            const(gkv.shape), const(wkv.shape),
            table, table, table, table,
        ],
        out_specs=[
            pl.BlockSpec((1, tm // FFT_RADIX, FFT_RADIX * FOURIER_WIDTH), lambda i, j: (i, j, 0)),
            rows(QK_WIDTH), rows(QK_WIDTH), rows(QK_WIDTH)],
        out_shape=[
            jax.ShapeDtypeStruct((b, s // FFT_RADIX, FFT_RADIX * FOURIER_WIDTH), BF16),
            jax.ShapeDtypeStruct((b, s, QK_WIDTH), BF16),
            jax.ShapeDtypeStruct((b, s, QK_WIDTH), BF16),
            jax.ShapeDtypeStruct((b, s, QK_WIDTH), BF16),
        ],
        scratch_shapes=[pltpu.VMEM((FOURIER_GROUPS, tm, FOURIER_GROUP_DIM), F32)],
        compiler_params=pltpu.CompilerParams(vmem_limit_bytes=VMEM_LIMIT_BYTES),
        name="premix",
    )(x, mod3, gpre, win, gq, wq, gkv, wkv, cosq, sinq, cosk, sink)


def _ctxkv_kernel(x_ref, mod_ref, gpre_ref, win_ref, gkv_ref, wkv_ref, k_ref, v_ref):
    x = x_ref[0]
    shift = mod_ref[0, 0:1, :]
    scale = mod_ref[0, 1:2, :]
    h = _rms(x, gpre_ref[...]) * (1.0 + scale) + shift
    p = _dot(h.astype(BF16), win_ref[...])
    kvn = _rms(p[:, :KV_LORA_RANK], gkv_ref[...]).astype(BF16)
    kv = _dot(kvn, wkv_ref[...])
    kr = p[:, KV_LORA_RANK:]
    for hd in range(MLA_HEADS):
        lo = hd * HEAD_SLOT
        k_ref[0, :, lo:lo + HEAD_SLOT] = (kv[:, lo:lo + HEAD_SLOT] + kr).astype(BF16)
    v_ref[0] = (kv[:, QK_WIDTH:] + _denominator_lanes()).astype(BF16)


def _ctxkv(ctx, mod3, ctx_row, gpre, win_c, gkv, wkv):
    b, c, d = ctx.shape
    const = lambda shape: pl.BlockSpec(shape, lambda i: (0,) * len(shape))
    rows = lambda w: pl.BlockSpec((1, c, w), lambda i: (i, 0, 0))
    return pl.pallas_call(
        _ctxkv_kernel,
        grid=(b,),
        in_specs=[
            rows(d),
            pl.BlockSpec((1, 6, d), lambda i: (ctx_row, 0, 0)),
            const(gpre.shape), const(win_c.shape), const(gkv.shape), const(wkv.shape),
        ],
        out_specs=[rows(QK_WIDTH), rows(QK_WIDTH)],
        out_shape=[
            jax.ShapeDtypeStruct((b, c, QK_WIDTH), BF16),
            jax.ShapeDtypeStruct((b, c, QK_WIDTH), BF16),
        ],
        name="ctxkv",
    )(ctx, mod3, gpre, win_c, gkv, wkv)


def _attn_kernel(q_ref, kc_ref, kl_ref, vc_ref, vl_ref, vcp_ref, vlp_ref, omain_ref, olag_ref,
                 p_ref, oprev_ref):
    t = pl.program_id(0)
    n_blocks = pl.num_programs(0) - 1
    n_ctx = kc_ref.shape[1]
    n_lat = kl_ref.shape[1]
    tq = q_ref.shape[1]
    pair_w = 2 * V_HEAD_DIM
    first_half = lax.broadcasted_iota(jnp.int32, (tq, pair_w), 1) < V_HEAD_DIM
    deferred = 0

    @pl.when(t == 0)
    def _():
        p_ref[...] = jnp.ones_like(p_ref)
        oprev_ref[...] = jnp.zeros_like(oprev_ref)

    def drain():
        o = _dot(p_ref[:, :n_ctx], vcp_ref[0]) + _dot(p_ref[:, n_ctx:], vlp_ref[0])
        o = o / o[:, V_HEAD_DIM:V_HEAD_DIM + 1]
        olag_ref[0] = jnp.where(first_half, o, pltpu.roll(oprev_ref[...], V_HEAD_DIM, axis=1)).astype(BF16)

    def head(hd, exact):
        lo = hd * HEAD_SLOT
        qh = q_ref[0, :, lo:lo + HEAD_SLOT]
        s_c = _dot_nt(qh, kc_ref[0, :, lo:lo + HEAD_SLOT])
        s_l = _dot_nt(qh, kl_ref[0, :, lo:lo + HEAD_SLOT])
        shift = jnp.max(s_c, axis=-1, keepdims=True)
        if exact:
            shift = jnp.maximum(shift, jnp.max(s_l, axis=-1, keepdims=True))
        if hd == deferred:
            p_c = jnp.exp2(s_c - shift)
            p_l = jnp.exp2(s_l - shift)
            p_ref[:, :n_ctx] = p_c.astype(BF16)
            p_ref[:, n_ctx:] = p_l.astype(BF16)
            return None, jnp.sum(p_c, axis=-1, keepdims=True) + jnp.sum(p_l, axis=-1, keepdims=True)
        o = _dot(jnp.exp2(s_c - shift).astype(BF16), vc_ref[0, :, lo:lo + HEAD_SLOT])
        for k0 in range(0, n_lat, PV_KEY_TILE):
            p = jnp.exp2(s_l[:, k0:k0 + PV_KEY_TILE] - shift).astype(BF16)
            o = o + _dot(p, vl_ref[0, k0:k0 + PV_KEY_TILE, lo:lo + HEAD_SLOT])
        den = o[:, V_HEAD_DIM:V_HEAD_DIM + 1]
        return o / den, den

    def block(exact):
        dens = []
        for pair in range(MLA_HEADS // 2):
            outs = []
            for hd in (2 * pair, 2 * pair + 1):
                o, den = head(hd, exact)
                dens.append(den)
                if o is not None:
                    outs.append(o)
            if len(outs) == 2:
                vlo = (pair - 1) * pair_w
                omain_ref[0, :, vlo:vlo + pair_w] = jnp.where(
                    first_half, outs[0], pltpu.roll(outs[1], V_HEAD_DIM, axis=1)).astype(BF16)
            else:
                oprev_ref[...] = outs[0]
        return dens

    @pl.when(t < n_blocks)
    def _():
        drain()
        dens = block(exact=False)
        lo_den, hi_den = dens[0], dens[0]
        for den in dens[1:]:
            lo_den = jnp.minimum(lo_den, den)
            hi_den = jnp.maximum(hi_den, den)
        trusted = jnp.logical_and(jnp.min(lo_den) >= SHIFT_DEN_MIN, jnp.max(hi_den) <= SHIFT_DEN_MAX)

        @pl.when(jnp.logical_not(trusted))
        def _():
            block(exact=True)

    @pl.when(t == n_blocks)
    def _():
        drain()


def _attention(q, kc, kl, vc, vl, tq=512):
    b, s, _ = q.shape
    c = kc.shape[1]
    nq = s // tq
    n_blocks = b * nq
    pair_w = 2 * V_HEAD_DIM
    main_w = ATTN_WIDTH - pair_w
    cur = lambda t: jnp.minimum(t, n_blocks - 1)
    prev = lambda t: jnp.maximum(t - 1, 0)
    return pl.pallas_call(
        _attn_kernel,
        grid=(n_blocks + 1,),
        in_specs=[
            pl.BlockSpec((1, tq, QK_WIDTH), lambda t: (cur(t) // nq, cur(t) % nq, 0)),
            pl.BlockSpec((1, c, QK_WIDTH), lambda t: (cur(t) // nq, 0, 0)),
            pl.BlockSpec((1, s, QK_WIDTH), lambda t: (cur(t) // nq, 0, 0)),
            pl.BlockSpec((1, c, QK_WIDTH), lambda t: (cur(t) // nq, 0, 0)),
            pl.BlockSpec((1, s, QK_WIDTH), lambda t: (cur(t) // nq, 0, 0)),
            pl.BlockSpec((1, c, HEAD_SLOT), lambda t: (prev(t) // nq, 0, 0)),
            pl.BlockSpec((1, s, HEAD_SLOT), lambda t: (prev(t) // nq, 0, 0)),
        ],
        out_specs=[
            pl.BlockSpec((1, tq, main_w), lambda t: (cur(t) // nq, cur(t) % nq, 0)),
            pl.BlockSpec((1, tq, pair_w), lambda t: (prev(t) // nq, prev(t) % nq, 0)),
        ],
        out_shape=[
            jax.ShapeDtypeStruct((b, s, main_w), BF16),
            jax.ShapeDtypeStruct((b, s, pair_w), BF16),
        ],
        scratch_shapes=[
            pltpu.VMEM((tq, c + s), BF16),
            pltpu.VMEM((tq, pair_w), F32),
        ],
        compiler_params=pltpu.CompilerParams(
            dimension_semantics=("arbitrary",), vmem_limit_bytes=VMEM_LIMIT_BYTES),
        name="attn",
    )(q, kc, kl, vc, vl, vc, vl)


def _fourier_kernel(t_ref, tw_ref, u_ref, cc_ref, wf_ref, o_ref, tb_ref, cw_ref):
    @pl.when(pl.program_id(0) == 0)
    def _():
        tb_ref[...] = t_ref[...].astype(BF16)
        cc = cc_ref[...].astype(BF16)
        for g in range(FOURIER_GROUPS):
            cw_ref[g] = _dot(cc, wf_ref[g]).astype(BF16)

    n_sub = tb_ref.shape[1]
    f = _dot(tb_ref[...], u_ref[0])
    gd = FOURIER_GROUP_DIM
    for g in range(FOURIER_GROUPS):
        gr, gi = [], []
        for r in range(FFT_RADIX):
            lo = r * FOURIER_WIDTH + g * gd
            a = f[:n_sub, lo:lo + gd]
            b = f[n_sub:, lo:lo + gd]
            if r == 0:
                gr.append(a)
                gi.append(b)
            else:
                c = tw_ref[r - 1, 0]
                s = tw_ref[r - 1, 1]
                gr.append(a * c - b * s)
                gi.append(a * s + b * c)
        ar, ai = gr[0] + gr[2], gi[0] + gi[2]
        br, bi = gr[0] - gr[2], gi[0] - gi[2]
        cr, ci = gr[1] + gr[3], gi[1] + gi[3]
        dr, di = gr[1] - gr[3], gi[1] - gi[3]
        xr = jnp.concatenate([ar + cr, br - di, ar - cr, br + di], axis=0)
        xi = jnp.concatenate([ai + ci, bi + dr, ai - ci, bi - dr], axis=0)
        lhs = jnp.concatenate([xr, xi], axis=1).astype(BF16)
        o_ref[0, :, g * gd:(g + 1) * gd] = _dot(lhs, cw_ref[g]).astype(BF16)


def _fourier(u4, tmat, tw, cmat, wf):
    b, n_sub, _ = u4.shape
    s = n_sub * FFT_RADIX
    full = lambda a: pl.BlockSpec(a.shape, lambda i: (0,) * a.ndim)
    return pl.pallas_call(
        _fourier_kernel,
        grid=(b,),
        in_specs=[
            full(tmat), full(tw),
            pl.BlockSpec((1, n_sub, FFT_RADIX * FOURIER_WIDTH), lambda i: (i, 0, 0)),
            full(cmat), full(wf),
        ],
        out_specs=pl.BlockSpec((1, s, FOURIER_WIDTH), lambda i: (i, 0, 0)),
        out_shape=jax.ShapeDtypeStruct((b, s, FOURIER_WIDTH), BF16),
        scratch_shapes=[
            pltpu.VMEM(tmat.shape, BF16),
            pltpu.VMEM((FOURIER_GROUPS, 2 * FOURIER_GROUP_DIM, FOURIER_GROUP_DIM), BF16),
        ],
        compiler_params=pltpu.CompilerParams(
            dimension_semantics=("arbitrary",), vmem_limit_bytes=VMEM_LIMIT_BYTES),
        name="fourier",
    )(tmat, tw, u4, cmat, wf)


def _post_kernel(x_ref, four_ref, am_ref, al_ref, mod_ref, gpm_ref, gpf_ref, gqf_ref,
                 wof_ref, woa_ref, wg_ref, wu_ref, wd_ref, o_ref):
    x = x_ref[0]
    gt_m = mod_ref[0, 2:3, :]
    sh_f = mod_ref[0, 3:4, :]
    sc_f = mod_ref[0, 4:5, :]
    gt_f = mod_ref[0, 5:6, :]
    attn = jnp.concatenate([al_ref[0], am_ref[0]], axis=1)
    y = _dot(four_ref[0], wof_ref[...]) + _dot(attn, woa_ref[...])
    x1 = x + gt_m * _rms(y, gpm_ref[...])
    h2 = (_rms(x1, gpf_ref[...]) * (1.0 + sc_f) + sh_f).astype(BF16)
    g = _dot(h2, wg_ref[...])
    up = _dot(h2, wu_ref[...])
    act = (g / (1.0 + jnp.exp(-g)) * up).astype(BF16)
    ffn = _dot(act, wd_ref[...])
    o_ref[0] = x1 + gt_f * _rms(ffn, gqf_ref[...])


def _post(x, four, attn_main, attn_last, mod3, gpm, gpf, gqf, wof, woa, wg, wu, wd, tm=512):
    b, s, d = x.shape
    const = lambda shape: pl.BlockSpec(shape, lambda i, j: (0,) * len(shape),
                                       pipeline_mode=pl.Buffered(1))
    rows = lambda w: pl.BlockSpec((1, tm, w), lambda i, j: (i, j, 0))
    return pl.pallas_call(
        _post_kernel,
        grid=(b, s // tm),
        in_specs=[
            rows(d), rows(FOURIER_WIDTH), rows(attn_main.shape[2]), rows(attn_last.shape[2]),
            pl.BlockSpec((1, 6, d), lambda i, j: (i, 0, 0)),
            const(gpm.shape), const(gpf.shape), const(gqf.shape),
            const(wof.shape), const(woa.shape), const(wg.shape), const(wu.shape), const(wd.shape),
        ],
        out_specs=rows(d),
        out_shape=jax.ShapeDtypeStruct((b, s, d), F32),
        compiler_params=pltpu.CompilerParams(vmem_limit_bytes=VMEM_LIMIT_BYTES),
        name="post",
    )(x, four, attn_main, attn_last, mod3, gpm, gpf, gqf, wof, woa, wg, wu, wd)


def _rope_rotate_cols(w):
    a = QK_ROPE_DIM // 2
    hf = a // 2
    blocks = []
    for s0 in (0, a):
        blocks += [-w[..., s0 + hf:s0 + a], w[..., s0:s0 + hf]]
    return jnp.concatenate(blocks, axis=-1)


def _head_slot(nope, rope):
    pad = HEAD_SLOT - QK_NOPE_DIM - QK_ROPE_DIM
    return jnp.concatenate([nope, rope, jnp.zeros(rope.shape[:-1] + (pad,), rope.dtype)], axis=-1)


def _rope_tables(n_lat, q_scale):
    t = np.arange(n_lat)
    hf = QK_ROPE_DIM // 4
    inv_freq = ROPE_BASE ** (-np.arange(hf, dtype=np.float64) / hf)
    ar = (t // GRID_W)[:, None] * inv_freq[None, :]
    ac = (t % GRID_W)[:, None] * inv_freq[None, :]
    cos32 = np.concatenate([np.cos(ar), np.cos(ar), np.cos(ac), np.cos(ac)], axis=-1)
    sin32 = np.concatenate([np.sin(ar), np.sin(ar), np.sin(ac), np.sin(ac)], axis=-1)
    pad = np.zeros((n_lat, HEAD_SLOT - QK_NOPE_DIM - QK_ROPE_DIM))
    ones = np.ones((n_lat, QK_NOPE_DIM))
    zeros = np.zeros((n_lat, QK_NOPE_DIM))
    slot = lambda nope, rope: jnp.asarray(np.concatenate([nope, rope, pad], axis=-1).astype(np.float32))
    return (slot(ones * q_scale, cos32 * q_scale), slot(zeros, sin32 * q_scale),
            slot(zeros, cos32), slot(zeros, sin32))


def _dft_tables(n_pos, n_ch):
    n_sub = n_pos // FFT_RADIX
    m = np.arange(n_sub, dtype=np.int64)
    ang = 2.0 * np.pi * ((m[:, None] * m[None, :]) % n_sub) / n_sub
    tmat = np.concatenate([np.cos(ang), np.sin(ang)], axis=0).astype(np.float32)
    tw = np.zeros((FFT_RADIX - 1, 2, n_sub, FOURIER_GROUP_DIM), np.float32)
    for r in range(1, FFT_RADIX):
        a = 2.0 * np.pi * r * m / n_pos
        tw[r - 1, 0] = np.cos(a)[:, None]
        tw[r - 1, 1] = np.sin(a)[:, None]
    c = np.arange(n_ch, dtype=np.int64)
    angc = 2.0 * np.pi * ((c[:, None] * c[None, :]) % n_ch) / n_ch
    norm = 1.0 / np.sqrt(float(n_pos * n_ch))
    cmat = np.concatenate([np.cos(angc) * norm, -np.sin(angc) * norm], axis=0).astype(np.float32)
    return jnp.asarray(tmat), jnp.asarray(tw), jnp.asarray(cmat)


def kernel(x, c, ctx, c_ctx, w_ada, b_ada, g_pre_mix, g_post_mix, g_pre_ffn, g_post_ffn, w_in, g_q_a,
           w_q_b, g_kv_a, w_kv_b, w_fourier, w_out, w_gate, w_up, w_down):
    assert w_ada.shape[0] == 1, "single-layer block"
    batch, n_lat, d = x.shape

    mod_rows = -(-(batch + 1) // 8) * 8
    cc = jnp.concatenate([c, c_ctx[None, :], jnp.zeros((mod_rows - batch - 1, d), F32)], axis=0)
    mod = _adaln(cc, w_ada[0], b_ada[0][None, :])
    mod3 = mod.reshape(mod_rows, 6, d)

    w_in0 = w_in[0]
    w_kr = w_in0[:, ROPE_COL:]
    zeros_d = jnp.zeros((d, QK_NOPE_DIM), F32)
    kr_slot = _head_slot(zeros_d, w_kr)
    kr_rot_slot = _head_slot(zeros_d, _rope_rotate_cols(w_kr))
    win = jnp.concatenate([w_in0[:, :ROPE_COL], kr_slot, kr_rot_slot], axis=1).astype(BF16)
    win_c = jnp.concatenate([w_in0[:, KV_COL:ROPE_COL], kr_slot], axis=1).astype(BF16)

    wq3 = w_q_b[0].reshape(Q_LORA_RANK, MLA_HEADS, QK_NOPE_DIM + QK_ROPE_DIM)
    wq_nope, wq_rope = wq3[..., :QK_NOPE_DIM], wq3[..., QK_NOPE_DIM:]
    wq_a = _head_slot(wq_nope, wq_rope).reshape(Q_LORA_RANK, QK_WIDTH)
    wq_b = _head_slot(jnp.zeros_like(wq_nope), _rope_rotate_cols(wq_rope)).reshape(Q_LORA_RANK, QK_WIDTH)
    wq = jnp.concatenate([wq_a, wq_b], axis=1).astype(BF16)

    wkv3 = w_kv_b[0].reshape(KV_LORA_RANK, MLA_HEADS, QK_NOPE_DIM + V_HEAD_DIM)
    wk_nope, wv = wkv3[..., :QK_NOPE_DIM], wkv3[..., QK_NOPE_DIM:]
    wk_slots = _head_slot(wk_nope, jnp.zeros(wk_nope.shape[:-1] + (QK_ROPE_DIM,), F32))
    wv_slots = jnp.concatenate([wv, jnp.zeros(wv.shape[:-1] + (HEAD_SLOT - V_HEAD_DIM,), F32)], axis=-1)
    wkv = jnp.concatenate([wk_slots.reshape(KV_LORA_RANK, QK_WIDTH),
                           wv_slots.reshape(KV_LORA_RANK, QK_WIDTH)], axis=1).astype(BF16)

    q_scale = float((QK_NOPE_DIM + QK_ROPE_DIM) ** -0.5 * np.log2(np.e))
    cosq, sinq, cosk, sink = _rope_tables(n_lat, q_scale)
    tmat, tw, cmat = _dft_tables(n_lat, FOURIER_GROUP_DIM)

    row2 = lambda g: g[0][None, :]
    u_f, q, k_lat, v_lat = _premix(x, mod3, row2(g_pre_mix), win, row2(g_q_a), wq, row2(g_kv_a), wkv,
                                   cosq, sinq, cosk, sink)
    k_ctx, v_ctx = _ctxkv(ctx, mod3, batch, row2(g_pre_mix), win_c, row2(g_kv_a), wkv)
    attn_main, attn_last = _attention(q, k_ctx, k_lat, v_ctx, v_lat)
    four = _fourier(u_f, tmat, tw, cmat, w_fourier[0].astype(BF16))
    w_out0 = w_out[0].astype(BF16)
    return _post(x, four, attn_main, attn_last, mod3, row2(g_post_mix), row2(g_pre_ffn), row2(g_post_ffn),
                 w_out0[:FOURIER_WIDTH], w_out0[FOURIER_WIDTH:],
                 w_gate[0].astype(BF16), w_up[0].astype(BF16), w_down[0].astype(BF16))
```

```python
import functools

import numpy as np
import jax
import jax.numpy as jnp
from jax import lax
from jax.experimental import pallas as pl
from jax.experimental.pallas import tpu as pltpu

F32 = jnp.float32
BF16 = jnp.bfloat16

D_MODEL = 1024
GRID_W = 64
FOURIER_GROUPS = 4
FOURIER_GROUP_DIM = 128
FOURIER_WIDTH = FOURIER_GROUPS * FOURIER_GROUP_DIM
MLA_HEADS = 8
QK_NOPE_DIM = 64
QK_ROPE_DIM = 32
V_HEAD_DIM = 64
Q_LORA_RANK = 256
KV_LORA_RANK = 128
KV_COL = FOURIER_WIDTH + Q_LORA_RANK
ROPE_COL = KV_COL + KV_LORA_RANK
ROPE_BASE = 10000.0
NORM_EPS = 1e-6
FFT_RADIX = 4
HEAD_SLOT = 128
SHIFT_DEN_MIN = 2.0 ** -40
SHIFT_DEN_MAX = 2.0 ** 40
PV_KEY_TILE = 256
ATTN_WIDTH = MLA_HEADS * V_HEAD_DIM
QK_WIDTH = MLA_HEADS * HEAD_SLOT

VMEM_LIMIT_BYTES = 56 * 1024 * 1024


def _rms(x, g):
    return x * lax.rsqrt(jnp.mean(x * x, axis=-1, keepdims=True) + NORM_EPS) * g


def _dot(a, b):
    return jnp.dot(a, b, preferred_element_type=F32)


def _dot_nt(a, b):
    return lax.dot_general(a, b, (((1,), (1,)), ((), ())), preferred_element_type=F32)


def _denominator_lanes():
    lane = lax.broadcasted_iota(jnp.int32, (1, QK_WIDTH), 1)
    return (lane % HEAD_SLOT == V_HEAD_DIM).astype(F32)


def _adaln_kernel(c_ref, w_ref, b_ref, o_ref):
    c = c_ref[...]
    a = c / (1.0 + jnp.exp(-c))
    o_ref[...] = _dot(a.astype(BF16), w_ref[...].astype(BF16)) + b_ref[...]


def _adaln(cc, w_ada, b_ada, tn=512):
    rows, d = cc.shape
    n = w_ada.shape[1]
    return pl.pallas_call(
        _adaln_kernel,
        grid=(n // tn,),
        in_specs=[
            pl.BlockSpec((rows, d), lambda j: (0, 0)),
            pl.BlockSpec((d, tn), lambda j: (0, j)),
            pl.BlockSpec((1, tn), lambda j: (0, j)),
        ],
        out_specs=pl.BlockSpec((rows, tn), lambda j: (0, j)),
        out_shape=jax.ShapeDtypeStruct((rows, n), F32),
        name="adaln",
    )(cc, w_ada, b_ada)


def _premix_kernel(x_ref, mod_ref, gpre_ref, win_ref, gq_ref, wq_ref, gkv_ref, wkv_ref,
                   cosq_ref, sinq_ref, cosk_ref, sink_ref, u_ref, q_ref, k_ref, v_ref, us_ref):
    x = x_ref[0]
    shift = mod_ref[0, 0:1, :]
    scale = mod_ref[0, 1:2, :]
    h = _rms(x, gpre_ref[...]) * (1.0 + scale) + shift
    p = _dot(h.astype(BF16), win_ref[...])
    sub = us_ref.shape[1] // FFT_RADIX
    for g in range(FOURIER_GROUPS):
        glo = g * FOURIER_GROUP_DIM
        us_ref[g] = p[:, glo:glo + FOURIER_GROUP_DIM]
        for r in range(FFT_RADIX):
            lo = r * FOURIER_WIDTH + glo
            u_ref[0, :, lo:lo + FOURIER_GROUP_DIM] = (
                us_ref[g, pl.ds(r, sub, stride=FFT_RADIX), :].astype(BF16))

    qn = _rms(p[:, FOURIER_WIDTH:KV_COL], gq_ref[...]).astype(BF16)
    qq = _dot(qn, wq_ref[...])
    cosq = cosq_ref[...]
    sinq = sinq_ref[...]
    for hd in range(MLA_HEADS):
        lo = hd * HEAD_SLOT
        q_ref[0, :, lo:lo + HEAD_SLOT] = (
            qq[:, lo:lo + HEAD_SLOT] * cosq + qq[:, QK_WIDTH + lo:QK_WIDTH + lo + HEAD_SLOT] * sinq
        ).astype(BF16)

    kvn = _rms(p[:, KV_COL:ROPE_COL], gkv_ref[...]).astype(BF16)
    kv = _dot(kvn, wkv_ref[...])
    kr = (p[:, ROPE_COL:ROPE_COL + HEAD_SLOT] * cosk_ref[...]
          + p[:, ROPE_COL + HEAD_SLOT:ROPE_COL + 2 * HEAD_SLOT] * sink_ref[...])
    for hd in range(MLA_HEADS):
        lo = hd * HEAD_SLOT
        k_ref[0, :, lo:lo + HEAD_SLOT] = (kv[:, lo:lo + HEAD_SLOT] + kr).astype(BF16)
    v_ref[0] = (kv[:, QK_WIDTH:] + _denominator_lanes()).astype(BF16)


def _premix(x, mod3, gpre, win, gq, wq, gkv, wkv, cosq, sinq, cosk, sink, tm=512):
    b, s, d = x.shape
    const = lambda shape: pl.BlockSpec(shape, lambda i, j: (0,) * len(shape))
    rows = lambda w: pl.BlockSpec((1, tm, w), lambda i, j: (i, j, 0))
    table = pl.BlockSpec((tm, HEAD_SLOT), lambda i, j: (j, 0))
    return pl.pallas_call(
        _premix_kernel,
        grid=(b, s // tm),
        in_specs=[
            rows(d),
            pl.BlockSpec((1, 6, d), lambda i, j: (i, 0, 0)),
            const(gpre.shape), const(win.shape), const(gq.shape), const(wq.shape),
            const(gkv.shape), const(wkv.shape),
            table, table, table, table,
        ],
        out_specs=[
            pl.BlockSpec((1, tm // FFT_RADIX, FFT_RADIX * FOURIER_WIDTH), lambda i, j: (i, j, 0)),
            rows(QK_WIDTH), rows(QK_WIDTH), rows(QK_WIDTH)],
        out_shape=[
            jax.ShapeDtypeStruct((b, s // FFT_RADIX, FFT_RADIX * FOURIER_WIDTH), BF16),
            jax.ShapeDtypeStruct((b, s, QK_WIDTH), BF16),
            jax.ShapeDtypeStruct((b, s, QK_WIDTH), BF16),
            jax.ShapeDtypeStruct((b, s, QK_WIDTH), BF16),
        ],
        scratch_shapes=[pltpu.VMEM((FOURIER_GROUPS, tm, FOURIER_GROUP_DIM), F32)],
        compiler_params=pltpu.CompilerParams(vmem_limit_bytes=VMEM_LIMIT_BYTES),
        name="premix",
    )(x, mod3, gpre, win, gq, wq, gkv, wkv, cosq, sinq, cosk, sink)


def _ctxkv_kernel(x_ref, mod_ref, gpre_ref, win_ref, gkv_ref, wkv_ref, k_ref, v_ref):
    x = x_ref[0]
    shift = mod_ref[0, 0:1, :]
    scale = mod_ref[0, 1:2, :]
    h = _rms(x, gpre_ref[...]) * (1.0 + scale) + shift
    p = _dot(h.astype(BF16), win_ref[...])
    kvn = _rms(p[:, :KV_LORA_RANK], gkv_ref[...]).astype(BF16)
    kv = _dot(kvn, wkv_ref[...])
    kr = p[:, KV_LORA_RANK:]
    for hd in range(MLA_HEADS):
        lo = hd * HEAD_SLOT
        k_ref[0, :, lo:lo + HEAD_SLOT] = (kv[:, lo:lo + HEAD_SLOT] + kr).astype(BF16)
    v_ref[0] = (kv[:, QK_WIDTH:] + _denominator_lanes()).astype(BF16)


def _ctxkv(ctx, mod3, ctx_row, gpre, win_c, gkv, wkv):
    b, c, d = ctx.shape
    const = lambda shape: pl.BlockSpec(shape, lambda i: (0,) * len(shape))
    rows = lambda w: pl.BlockSpec((1, c, w), lambda i: (i, 0, 0))
    return pl.pallas_call(
        _ctxkv_kernel,
        grid=(b,),
        in_specs=[
            rows(d),
            pl.BlockSpec((1, 6, d), lambda i: (ctx_row, 0, 0)),
            const(gpre.shape), const(win_c.shape), const(gkv.shape), const(wkv.shape),
        ],
        out_specs=[rows(QK_WIDTH), rows(QK_WIDTH)],
        out_shape=[
            jax.ShapeDtypeStruct((b, c, QK_WIDTH), BF16),
            jax.ShapeDtypeStruct((b, c, QK_WIDTH), BF16),
        ],
        name="ctxkv",
    )(ctx, mod3, gpre, win_c, gkv, wkv)


def _attn_kernel(q_ref, kc_ref, kl_ref, vc_ref, vl_ref, vcp_ref, vlp_ref, omain_ref, olast_ref,
                 p_ref, oprev_ref):
    t = pl.program_id(0)
    n_blocks = pl.num_programs(0) - 1
    n_ctx = kc_ref.shape[1]
    n_lat = kl_ref.shape[1]
    tq = q_ref.shape[1]
    pair_w = 2 * V_HEAD_DIM
    first_half = lax.broadcasted_iota(jnp.int32, (tq, pair_w), 1) < V_HEAD_DIM
    last = MLA_HEADS - 1

    @pl.when(t == 0)
    def _():
        p_ref[...] = jnp.ones_like(p_ref)
        oprev_ref[...] = jnp.zeros_like(oprev_ref)

    def drain():
        o = _dot(p_ref[:, :n_ctx], vcp_ref[0]) + _dot(p_ref[:, n_ctx:], vlp_ref[0])
        o = pltpu.roll(o / o[:, V_HEAD_DIM:V_HEAD_DIM + 1], V_HEAD_DIM, axis=1)
        olast_ref[0] = jnp.where(first_half, oprev_ref[...], o).astype(BF16)

    def head(hd, exact):
        lo = hd * HEAD_SLOT
        qh = q_ref[0, :, lo:lo + HEAD_SLOT]
        s_c = _dot_nt(qh, kc_ref[0, :, lo:lo + HEAD_SLOT])
        s_l = _dot_nt(qh, kl_ref[0, :, lo:lo + HEAD_SLOT])
        shift = jnp.max(s_c, axis=-1, keepdims=True)
        if exact:
            shift = jnp.maximum(shift, jnp.max(s_l, axis=-1, keepdims=True))
        if hd == last:
            p_c = jnp.exp2(s_c - shift)
            p_l = jnp.exp2(s_l - shift)
            p_ref[:, :n_ctx] = p_c.astype(BF16)
            p_ref[:, n_ctx:] = p_l.astype(BF16)
            return None, jnp.sum(p_c, axis=-1, keepdims=True) + jnp.sum(p_l, axis=-1, keepdims=True)
        o = _dot(jnp.exp2(s_c - shift).astype(BF16), vc_ref[0, :, lo:lo + HEAD_SLOT])
        for k0 in range(0, n_lat, PV_KEY_TILE):
            p = jnp.exp2(s_l[:, k0:k0 + PV_KEY_TILE] - shift).astype(BF16)
            o = o + _dot(p, vl_ref[0, k0:k0 + PV_KEY_TILE, lo:lo + HEAD_SLOT])
        den = o[:, V_HEAD_DIM:V_HEAD_DIM + 1]
        return o / den, den

    def block(exact):
        dens = []
        for pair in range(MLA_HEADS // 2):
            outs = []
            for hd in (2 * pair, 2 * pair + 1):
                o, den = head(hd, exact)
                dens.append(den)
                if o is not None:
                    outs.append(o)
            if len(outs) == 2:
                vlo = pair * pair_w
                omain_ref[0, :, vlo:vlo + pair_w] = jnp.where(
                    first_half, outs[0], pltpu.roll(outs[1], V_HEAD_DIM, axis=1)).astype(BF16)
            else:
                oprev_ref[...] = outs[0]
        return dens

    @pl.when(t < n_blocks)
    def _():
        drain()
        dens = block(exact=False)
        lo_den, hi_den = dens[0], dens[0]
        for den in dens[1:]:
            lo_den = jnp.minimum(lo_den, den)
            hi_den = jnp.maximum(hi_den, den)
        trusted = jnp.logical_and(jnp.min(lo_den) >= SHIFT_DEN_MIN, jnp.max(hi_den) <= SHIFT_DEN_MAX)

        @pl.when(jnp.logical_not(trusted))
        def _():
            block(exact=True)

    @pl.when(t == n_blocks)
    def _():
        drain()


def _attention(q, kc, kl, vc, vl, tq=512):
    b, s, _ = q.shape
    c = kc.shape[1]
    nq = s // tq
    n_blocks = b * nq
    pair_w = 2 * V_HEAD_DIM
    main_w = ATTN_WIDTH - pair_w
    last = MLA_HEADS - 1
    cur = lambda t: jnp.minimum(t, n_blocks - 1)
    prev = lambda t: jnp.maximum(t - 1, 0)
    return pl.pallas_call(
        _attn_kernel,
        grid=(n_blocks + 1,),
        in_specs=[
            pl.BlockSpec((1, tq, QK_WIDTH), lambda t: (cur(t) // nq, cur(t) % nq, 0)),
            pl.BlockSpec((1, c, QK_WIDTH), lambda t: (cur(t) // nq, 0, 0)),
            pl.BlockSpec((1, s, QK_WIDTH), lambda t: (cur(t) // nq, 0, 0)),
            pl.BlockSpec((1, c, QK_WIDTH), lambda t: (cur(t) // nq, 0, 0)),
            pl.BlockSpec((1, s, QK_WIDTH), lambda t: (cur(t) // nq, 0, 0)),
            pl.BlockSpec((1, c, HEAD_SLOT), lambda t: (prev(t) // nq, 0, last)),
            pl.BlockSpec((1, s, HEAD_SLOT), lambda t: (prev(t) // nq, 0, last)),
        ],
        out_specs=[
            pl.BlockSpec((1, tq, main_w), lambda t: (cur(t) // nq, cur(t) % nq, 0)),
            pl.BlockSpec((1, tq, pair_w), lambda t: (prev(t) // nq, prev(t) % nq, 0)),
        ],
        out_shape=[
            jax.ShapeDtypeStruct((b, s, main_w), BF16),
            jax.ShapeDtypeStruct((b, s, pair_w), BF16),
        ],
        scratch_shapes=[
            pltpu.VMEM((tq, c + s), BF16),
            pltpu.VMEM((tq, pair_w), F32),
        ],
        compiler_params=pltpu.CompilerParams(
            dimension_semantics=("arbitrary",), vmem_limit_bytes=VMEM_LIMIT_BYTES),
        name="attn",
    )(q, kc, kl, vc, vl, vc, vl)


def _fourier_kernel(t_ref, tw_ref, u_ref, cc_ref, wf_ref, o_ref, tb_ref, cw_ref):
    @pl.when(pl.program_id(0) == 0)
    def _():
        tb_ref[...] = t_ref[...].astype(BF16)
        cc = cc_ref[...].astype(BF16)
        for g in range(FOURIER_GROUPS):
            cw_ref[g] = _dot(cc, wf_ref[g]).astype(BF16)

    n_sub = tb_ref.shape[1]
    f = _dot(tb_ref[...], u_ref[0])
    gd = FOURIER_GROUP_DIM
    for g in range(FOURIER_GROUPS):
        gr, gi = [], []
        for r in range(FFT_RADIX):
            lo = r * FOURIER_WIDTH + g * gd
            a = f[:n_sub, lo:lo + gd]
            b = f[n_sub:, lo:lo + gd]
            if r == 0:
                gr.append(a)
                gi.append(b)
            else:
                c = tw_ref[r - 1, 0]
                s = tw_ref[r - 1, 1]
                gr.append(a * c - b * s)
                gi.append(a * s + b * c)
        ar, ai = gr[0] + gr[2], gi[0] + gi[2]
        br, bi = gr[0] - gr[2], gi[0] - gi[2]
        cr, ci = gr[1] + gr[3], gi[1] + gi[3]
        dr, di = gr[1] - gr[3], gi[1] - gi[3]
        xr = jnp.concatenate([ar + cr, br - di, ar - cr, br + di], axis=0)
        xi = jnp.concatenate([ai + ci, bi + dr, ai - ci, bi - dr], axis=0)
        lhs = jnp.concatenate([xr, xi], axis=1).astype(BF16)
        o_ref[0, :, g * gd:(g + 1) * gd] = _dot(lhs, cw_ref[g]).astype(BF16)


def _fourier(u4, tmat, tw, cmat, wf):
    b, n_sub, _ = u4.shape
    s = n_sub * FFT_RADIX
    full = lambda a: pl.BlockSpec(a.shape, lambda i: (0,) * a.ndim)
    return pl.pallas_call(
        _fourier_kernel,
        grid=(b,),
        in_specs=[
            full(tmat), full(tw),
            pl.BlockSpec((1, n_sub, FFT_RADIX * FOURIER_WIDTH), lambda i: (i, 0, 0)),
            full(cmat), full(wf),
        ],
        out_specs=pl.BlockSpec((1, s, FOURIER_WIDTH), lambda i: (i, 0, 0)),
        out_shape=jax.ShapeDtypeStruct((b, s, FOURIER_WIDTH), BF16),
        scratch_shapes=[
            pltpu.VMEM(tmat.shape, BF16),
            pltpu.VMEM((FOURIER_GROUPS, 2 * FOURIER_GROUP_DIM, FOURIER_GROUP_DIM), BF16),
        ],
        compiler_params=pltpu.CompilerParams(
            dimension_semantics=("arbitrary",), vmem_limit_bytes=VMEM_LIMIT_BYTES),
        name="fourier",
    )(tmat, tw, u4, cmat, wf)


def _post_kernel(x_ref, four_ref, am_ref, al_ref, modc_ref, modp_ref, gpm_ref, gpf_ref, gqf_ref,
                 wof_ref, woa_ref, wg_ref, wu_ref, wd_ref, o_ref, x1_ref, h2_ref):
    t = pl.program_id(0)
    n_blocks = pl.num_programs(0) - 1

    def mix():
        gt_m = modc_ref[0, 2:3, :]
        sh_f = modc_ref[0, 3:4, :]
        sc_f = modc_ref[0, 4:5, :]
        attn = jnp.concatenate([am_ref[0], al_ref[0]], axis=1)
        y = _dot(four_ref[0], wof_ref[...]) + _dot(attn, woa_ref[...])
        x1 = x_ref[0] + gt_m * _rms(y, gpm_ref[...])
        x1_ref[...] = x1
        h2_ref[...] = (_rms(x1, gpf_ref[...]) * (1.0 + sc_f) + sh_f).astype(BF16)

    def ffn():
        gt_f = modp_ref[0, 5:6, :]
        h2 = h2_ref[...]
        g = _dot(h2, wg_ref[...])
        up = _dot(h2, wu_ref[...])
        act = (g / (1.0 + jnp.exp(-g)) * up).astype(BF16)
        o_ref[0] = x1_ref[...] + gt_f * _rms(_dot(act, wd_ref[...]), gqf_ref[...])

    @pl.when(t == 0)
    def _():
        mix()

    @pl.when(jnp.logical_and(t > 0, t < n_blocks))
    def _():
        ffn()
        mix()

    @pl.when(t == n_blocks)
    def _():
        ffn()


def _post(x, four, attn_main, attn_last, mod3, gpm, gpf, gqf, wof, woa, wg, wu, wd, tm=512):
    b, s, d = x.shape
    nj = s // tm
    n_blocks = b * nj
    cur = lambda t: jnp.minimum(t, n_blocks - 1)
    prev = lambda t: jnp.maximum(t - 1, 0)
    const = lambda shape: pl.BlockSpec(shape, lambda t: (0,) * len(shape), pipeline_mode=pl.Buffered(1))
    rows = lambda w: pl.BlockSpec((1, tm, w), lambda t: (cur(t) // nj, cur(t) % nj, 0))
    return pl.pallas_call(
        _post_kernel,
        grid=(n_blocks + 1,),
        in_specs=[
            rows(d), rows(FOURIER_WIDTH), rows(attn_main.shape[2]), rows(attn_last.shape[2]),
            pl.BlockSpec((1, 6, d), lambda t: (cur(t) // nj, 0, 0)),
            pl.BlockSpec((1, 6, d), lambda t: (prev(t) // nj, 0, 0)),
            const(gpm.shape), const(gpf.shape), const(gqf.shape),
            const(wof.shape), const(woa.shape), const(wg.shape), const(wu.shape), const(wd.shape),
        ],
        out_specs=pl.BlockSpec((1, tm, d), lambda t: (prev(t) // nj, prev(t) % nj, 0)),
        out_shape=jax.ShapeDtypeStruct((b, s, d), F32),
        scratch_shapes=[pltpu.VMEM((tm, d), F32), pltpu.VMEM((tm, d), BF16)],
        compiler_params=pltpu.CompilerParams(
            dimension_semantics=("arbitrary",), vmem_limit_bytes=VMEM_LIMIT_BYTES),
        name="post",
    )(x, four, attn_main, attn_last, mod3, mod3, gpm, gpf, gqf, wof, woa, wg, wu, wd)


def _rope_rotate_cols(w):
    a = QK_ROPE_DIM // 2
    hf = a // 2
    blocks = []
    for s0 in (0, a):
        blocks += [-w[..., s0 + hf:s0 + a], w[..., s0:s0 + hf]]
    return jnp.concatenate(blocks, axis=-1)


def _head_slot(nope, rope):
    pad = HEAD_SLOT - QK_NOPE_DIM - QK_ROPE_DIM
    return jnp.concatenate([nope, rope, jnp.zeros(rope.shape[:-1] + (pad,), rope.dtype)], axis=-1)


def _rope_tables(n_lat, q_scale):
    t = np.arange(n_lat)
    hf = QK_ROPE_DIM // 4
    inv_freq = ROPE_BASE ** (-np.arange(hf, dtype=np.float64) / hf)
    ar = (t // GRID_W)[:, None] * inv_freq[None, :]
    ac = (t % GRID_W)[:, None] * inv_freq[None, :]
    cos32 = np.concatenate([np.cos(ar), np.cos(ar), np.cos(ac), np.cos(ac)], axis=-1)
    sin32 = np.concatenate([np.sin(ar), np.sin(ar), np.sin(ac), np.sin(ac)], axis=-1)
    pad = np.zeros((n_lat, HEAD_SLOT - QK_NOPE_DIM - QK_ROPE_DIM))
    ones = np.ones((n_lat, QK_NOPE_DIM))
    zeros = np.zeros((n_lat, QK_NOPE_DIM))
    slot = lambda nope, rope: jnp.asarray(np.concatenate([nope, rope, pad], axis=-1).astype(np.float32))
    return (slot(ones * q_scale, cos32 * q_scale), slot(zeros, sin32 * q_scale),
            slot(zeros, cos32), slot(zeros, sin32))


def _dft_tables(n_pos, n_ch):
    n_sub = n_pos // FFT_RADIX
    m = np.arange(n_sub, dtype=np.int64)
    ang = 2.0 * np.pi * ((m[:, None] * m[None, :]) % n_sub) / n_sub
    tmat = np.concatenate([np.cos(ang), np.sin(ang)], axis=0).astype(np.float32)
    tw = np.zeros((FFT_RADIX - 1, 2, n_sub, FOURIER_GROUP_DIM), np.float32)
    for r in range(1, FFT_RADIX):
        a = 2.0 * np.pi * r * m / n_pos
        tw[r - 1, 0] = np.cos(a)[:, None]
        tw[r - 1, 1] = np.sin(a)[:, None]
    c = np.arange(n_ch, dtype=np.int64)
    angc = 2.0 * np.pi * ((c[:, None] * c[None, :]) % n_ch) / n_ch
    norm = 1.0 / np.sqrt(float(n_pos * n_ch))
    cmat = np.concatenate([np.cos(angc) * norm, -np.sin(angc) * norm], axis=0).astype(np.float32)
    return jnp.asarray(tmat), jnp.asarray(tw), jnp.asarray(cmat)


def kernel(x, c, ctx, c_ctx, w_ada, b_ada, g_pre_mix, g_post_mix, g_pre_ffn, g_post_ffn, w_in, g_q_a,
           w_q_b, g_kv_a, w_kv_b, w_fourier, w_out, w_gate, w_up, w_down):
    assert w_ada.shape[0] == 1, "single-layer block"
    batch, n_lat, d = x.shape

    mod_rows = -(-(batch + 1) // 8) * 8
    cc = jnp.concatenate([c, c_ctx[None, :], jnp.zeros((mod_rows - batch - 1, d), F32)], axis=0)
    mod = _adaln(cc, w_ada[0], b_ada[0][None, :])
    mod3 = mod.reshape(mod_rows, 6, d)

    w_in0 = w_in[0]
    w_kr = w_in0[:, ROPE_COL:]
    zeros_d = jnp.zeros((d, QK_NOPE_DIM), F32)
    kr_slot = _head_slot(zeros_d, w_kr)
    kr_rot_slot = _head_slot(zeros_d, _rope_rotate_cols(w_kr))
    win = jnp.concatenate([w_in0[:, :ROPE_COL], kr_slot, kr_rot_slot], axis=1).astype(BF16)
    win_c = jnp.concatenate([w_in0[:, KV_COL:ROPE_COL], kr_slot], axis=1).astype(BF16)

    wq3 = w_q_b[0].reshape(Q_LORA_RANK, MLA_HEADS, QK_NOPE_DIM + QK_ROPE_DIM)
    wq_nope, wq_rope = wq3[..., :QK_NOPE_DIM], wq3[..., QK_NOPE_DIM:]
    wq_a = _head_slot(wq_nope, wq_rope).reshape(Q_LORA_RANK, QK_WIDTH)
    wq_b = _head_slot(jnp.zeros_like(wq_nope), _rope_rotate_cols(wq_rope)).reshape(Q_LORA_RANK, QK_WIDTH)
    wq = jnp.concatenate([wq_a, wq_b], axis=1).astype(BF16)

    wkv3 = w_kv_b[0].reshape(KV_LORA_RANK, MLA_HEADS, QK_NOPE_DIM + V_HEAD_DIM)
    wk_nope, wv = wkv3[..., :QK_NOPE_DIM], wkv3[..., QK_NOPE_DIM:]
    wk_slots = _head_slot(wk_nope, jnp.zeros(wk_nope.shape[:-1] + (QK_ROPE_DIM,), F32))
    wv_slots = jnp.concatenate([wv, jnp.zeros(wv.shape[:-1] + (HEAD_SLOT - V_HEAD_DIM,), F32)], axis=-1)
    wkv = jnp.concatenate([wk_slots.reshape(KV_LORA_RANK, QK_WIDTH),
                           wv_slots.reshape(KV_LORA_RANK, QK_WIDTH)], axis=1).astype(BF16)

    q_scale = float((QK_NOPE_DIM + QK_ROPE_DIM) ** -0.5 * np.log2(np.e))
    cosq, sinq, cosk, sink = _rope_tables(n_lat, q_scale)
    tmat, tw, cmat = _dft_tables(n_lat, FOURIER_GROUP_DIM)

    row2 = lambda g: g[0][None, :]
    u_f, q, k_lat, v_lat = _premix(x, mod3, row2(g_pre_mix), win, row2(g_q_a), wq, row2(g_kv_a), wkv,
                                   cosq, sinq, cosk, sink)
    k_ctx, v_ctx = _ctxkv(ctx, mod3, batch, row2(g_pre_mix), win_c, row2(g_kv_a), wkv)
    attn_main, attn_last = _attention(q, k_ctx, k_lat, v_ctx, v_lat)
    four = _fourier(u_f, tmat, tw, cmat, w_fourier[0].astype(BF16))
    w_out0 = w_out[0].astype(BF16)
    return _post(x, four, attn_main, attn_last, mod3, row2(g_post_mix), row2(g_pre_ffn), row2(g_post_ffn),
                 w_out0[:FOURIER_WIDTH], w_out0[FOURIER_WIDTH:],
                 w_gate[0].astype(BF16), w_up[0].astype(BF16), w_down[0].astype(BF16))
```

```python
import functools

import numpy as np
import jax
import jax.numpy as jnp
from jax import lax
from jax.experimental import pallas as pl
from jax.experimental.pallas import tpu as pltpu

F32 = jnp.float32
BF16 = jnp.bfloat16

D_MODEL = 1024
GRID_W = 64
FOURIER_GROUPS = 4
FOURIER_GROUP_DIM = 128
FOURIER_WIDTH = FOURIER_GROUPS * FOURIER_GROUP_DIM
MLA_HEADS = 8
QK_NOPE_DIM = 64
QK_ROPE_DIM = 32
V_HEAD_DIM = 64
Q_LORA_RANK = 256
KV_LORA_RANK = 128
KV_COL = FOURIER_WIDTH + Q_LORA_RANK
ROPE_COL = KV_COL + KV_LORA_RANK
ROPE_BASE = 10000.0
NORM_EPS = 1e-6
FFT_RADIX = 4
HEAD_SLOT = 128
SHIFT_DEN_MIN = 2.0 ** -40
SHIFT_DEN_MAX = 2.0 ** 40
PV_KEY_TILE = 256
ATTN_WIDTH = MLA_HEADS * V_HEAD_DIM
QK_WIDTH = MLA_HEADS * HEAD_SLOT

VMEM_LIMIT_BYTES = 56 * 1024 * 1024


def _rms(x, g):
    return x * lax.rsqrt(jnp.mean(x * x, axis=-1, keepdims=True) + NORM_EPS) * g


def _dot(a, b):
    return jnp.dot(a, b, preferred_element_type=F32)


def _dot_nt(a, b):
    return lax.dot_general(a, b, (((1,), (1,)), ((), ())), preferred_element_type=F32)


def _rotary(a, rope_ref, first):
    half = QK_ROPE_DIM // 4
    below = pltpu.roll(a, half, axis=1)
    above = pltpu.roll(a, HEAD_SLOT - half, axis=1)
    return a * rope_ref[first] + below * rope_ref[first + 1] + above * rope_ref[first + 2]


def _denominator_lanes():
    lane = lax.broadcasted_iota(jnp.int32, (1, QK_WIDTH), 1)
    return (lane % HEAD_SLOT == V_HEAD_DIM).astype(F32)


def _adaln_kernel(c_ref, w_ref, b_ref, o_ref):
    c = c_ref[...]
    a = c / (1.0 + jnp.exp(-c))
    o_ref[...] = _dot(a.astype(BF16), w_ref[...].astype(BF16)) + b_ref[...]


def _adaln(cc, w_ada, b_ada, tn=512):
    rows, d = cc.shape
    n = w_ada.shape[1]
    return pl.pallas_call(
        _adaln_kernel,
        grid=(n // tn,),
        in_specs=[
            pl.BlockSpec((rows, d), lambda j: (0, 0)),
            pl.BlockSpec((d, tn), lambda j: (0, j)),
            pl.BlockSpec((1, tn), lambda j: (0, j)),
        ],
        out_specs=pl.BlockSpec((rows, tn), lambda j: (0, j)),
        out_shape=jax.ShapeDtypeStruct((rows, n), F32),
        name="adaln",
    )(cc, w_ada, b_ada)


def _premix_kernel(x_ref, mod_ref, gpre_ref, win_ref, gq_ref, wq_ref, gkv_ref, wkv_ref,
                   rope_ref, u_ref, q_ref, k_ref, v_ref, us_ref):
    x = x_ref[0]
    shift = mod_ref[0, 0:1, :]
    scale = mod_ref[0, 1:2, :]
    h = _rms(x, gpre_ref[...]) * (1.0 + scale) + shift
    p = _dot(h.astype(BF16), win_ref[...])
    sub = us_ref.shape[1] // FFT_RADIX
    for g in range(FOURIER_GROUPS):
        glo = g * FOURIER_GROUP_DIM
        us_ref[g] = p[:, glo:glo + FOURIER_GROUP_DIM]
        for r in range(FFT_RADIX):
            lo = r * FOURIER_WIDTH + glo
            u_ref[0, :, lo:lo + FOURIER_GROUP_DIM] = (
                us_ref[g, pl.ds(r, sub, stride=FFT_RADIX), :].astype(BF16))

    qn = _rms(p[:, FOURIER_WIDTH:KV_COL], gq_ref[...]).astype(BF16)
    qq = _dot(qn, wq_ref[...])
    for hd in range(MLA_HEADS):
        lo = hd * HEAD_SLOT
        q_ref[0, :, lo:lo + HEAD_SLOT] = _rotary(qq[:, lo:lo + HEAD_SLOT], rope_ref, 0).astype(BF16)

    kvn = _rms(p[:, KV_COL:ROPE_COL], gkv_ref[...]).astype(BF16)
    kv = _dot(kvn, wkv_ref[...])
    kr = _rotary(p[:, ROPE_COL:ROPE_COL + HEAD_SLOT], rope_ref, 3)
    for hd in range(MLA_HEADS):
        lo = hd * HEAD_SLOT
        k_ref[0, :, lo:lo + HEAD_SLOT] = (kv[:, lo:lo + HEAD_SLOT] + kr).astype(BF16)
    v_ref[0] = (kv[:, QK_WIDTH:] + _denominator_lanes()).astype(BF16)


def _premix(x, mod3, gpre, win, gq, wq, gkv, wkv, rope, tm=512):
    b, s, d = x.shape
    const = lambda shape: pl.BlockSpec(shape, lambda i, j: (0,) * len(shape))
    rows = lambda w: pl.BlockSpec((1, tm, w), lambda i, j: (i, j, 0))
    return pl.pallas_call(
        _premix_kernel,
        grid=(b, s // tm),
        in_specs=[
            rows(d),
            pl.BlockSpec((1, 6, d), lambda i, j: (i, 0, 0)),
            const(gpre.shape), const(win.shape), const(gq.shape), const(wq.shape),
            const(gkv.shape), const(wkv.shape),
            pl.BlockSpec((rope.shape[0], tm, HEAD_SLOT), lambda i, j: (0, j, 0)),
        ],
        out_specs=[
            pl.BlockSpec((1, tm // FFT_RADIX, FFT_RADIX * FOURIER_WIDTH), lambda i, j: (i, j, 0)),
            rows(QK_WIDTH), rows(QK_WIDTH), rows(QK_WIDTH)],
        out_shape=[
            jax.ShapeDtypeStruct((b, s // FFT_RADIX, FFT_RADIX * FOURIER_WIDTH), BF16),
            jax.ShapeDtypeStruct((b, s, QK_WIDTH), BF16),
            jax.ShapeDtypeStruct((b, s, QK_WIDTH), BF16),
            jax.ShapeDtypeStruct((b, s, QK_WIDTH), BF16),
        ],
        scratch_shapes=[pltpu.VMEM((FOURIER_GROUPS, tm, FOURIER_GROUP_DIM), F32)],
        compiler_params=pltpu.CompilerParams(vmem_limit_bytes=VMEM_LIMIT_BYTES),
        name="premix",
    )(x, mod3, gpre, win, gq, wq, gkv, wkv, rope)


def _ctxkv_kernel(x_ref, mod_ref, gpre_ref, win_ref, gkv_ref, wkv_ref, k_ref, v_ref):
    x = x_ref[0]
    shift = mod_ref[0, 0:1, :]
    scale = mod_ref[0, 1:2, :]
    h = _rms(x, gpre_ref[...]) * (1.0 + scale) + shift
    p = _dot(h.astype(BF16), win_ref[...])
    kvn = _rms(p[:, :KV_LORA_RANK], gkv_ref[...]).astype(BF16)
    kv = _dot(kvn, wkv_ref[...])
    kr = p[:, KV_LORA_RANK:]
    for hd in range(MLA_HEADS):
        lo = hd * HEAD_SLOT
        k_ref[0, :, lo:lo + HEAD_SLOT] = (kv[:, lo:lo + HEAD_SLOT] + kr).astype(BF16)
    v_ref[0] = (kv[:, QK_WIDTH:] + _denominator_lanes()).astype(BF16)


def _ctxkv(ctx, mod3, ctx_row, gpre, win_c, gkv, wkv):
    b, c, d = ctx.shape
    const = lambda shape: pl.BlockSpec(shape, lambda i: (0,) * len(shape))
    rows = lambda w: pl.BlockSpec((1, c, w), lambda i: (i, 0, 0))
    return pl.pallas_call(
        _ctxkv_kernel,
        grid=(b,),
        in_specs=[
            rows(d),
            pl.BlockSpec((1, 6, d), lambda i: (ctx_row, 0, 0)),
            const(gpre.shape), const(win_c.shape), const(gkv.shape), const(wkv.shape),
        ],
        out_specs=[rows(QK_WIDTH), rows(QK_WIDTH)],
        out_shape=[
            jax.ShapeDtypeStruct((b, c, QK_WIDTH), BF16),
            jax.ShapeDtypeStruct((b, c, QK_WIDTH), BF16),
        ],
        name="ctxkv",
    )(ctx, mod3, gpre, win_c, gkv, wkv)


def _attn_kernel(q_ref, kc_ref, kl_ref, vc_ref, vl_ref, vcp_ref, vlp_ref, omain_ref, olast_ref,
                 p_ref, oprev_ref):
    t = pl.program_id(0)
    n_blocks = pl.num_programs(0) - 1
    n_ctx = kc_ref.shape[1]
    n_lat = kl_ref.shape[1]
    tq = q_ref.shape[1]
    pair_w = 2 * V_HEAD_DIM
    first_half = lax.broadcasted_iota(jnp.int32, (tq, pair_w), 1) < V_HEAD_DIM
    last = MLA_HEADS - 1

    @pl.when(t == 0)
    def _():
        p_ref[...] = jnp.ones_like(p_ref)
        oprev_ref[...] = jnp.zeros_like(oprev_ref)

    def drain():
        o = _dot(p_ref[:, :n_ctx], vcp_ref[0]) + _dot(p_ref[:, n_ctx:], vlp_ref[0])
        o = pltpu.roll(o / o[:, V_HEAD_DIM:V_HEAD_DIM + 1], V_HEAD_DIM, axis=1)
        olast_ref[0] = jnp.where(first_half, oprev_ref[...], o).astype(BF16)

    def head(hd, exact):
        lo = hd * HEAD_SLOT
        qh = q_ref[0, :, lo:lo + HEAD_SLOT]
        s_c = _dot_nt(qh, kc_ref[0, :, lo:lo + HEAD_SLOT])
        s_l = _dot_nt(qh, kl_ref[0, :, lo:lo + HEAD_SLOT])
        shift = jnp.max(s_c, axis=-1, keepdims=True)
        if exact:
            shift = jnp.maximum(shift, jnp.max(s_l, axis=-1, keepdims=True))
        if hd == last:
            p_c = jnp.exp2(s_c - shift)
            p_l = jnp.exp2(s_l - shift)
            p_ref[:, :n_ctx] = p_c.astype(BF16)
            p_ref[:, n_ctx:] = p_l.astype(BF16)
            return None, jnp.sum(p_c, axis=-1, keepdims=True) + jnp.sum(p_l, axis=-1, keepdims=True)
        o = _dot(jnp.exp2(s_c - shift).astype(BF16), vc_ref[0, :, lo:lo + HEAD_SLOT])
        for k0 in range(0, n_lat, PV_KEY_TILE):
            p = jnp.exp2(s_l[:, k0:k0 + PV_KEY_TILE] - shift).astype(BF16)
            o = o + _dot(p, vl_ref[0, k0:k0 + PV_KEY_TILE, lo:lo + HEAD_SLOT])
        den = o[:, V_HEAD_DIM:V_HEAD_DIM + 1]
        return o / den, den

    def block(exact):
        dens = []
        for pair in range(MLA_HEADS // 2):
            outs = []
            for hd in (2 * pair, 2 * pair + 1):
                o, den = head(hd, exact)
                dens.append(den)
                if o is not None:
                    outs.append(o)
            if len(outs) == 2:
                vlo = pair * pair_w
                omain_ref[0, :, vlo:vlo + pair_w] = jnp.where(
                    first_half, outs[0], pltpu.roll(outs[1], V_HEAD_DIM, axis=1)).astype(BF16)
            else:
                oprev_ref[...] = outs[0]
        return dens

    @pl.when(t < n_blocks)
    def _():
        drain()
        dens = block(exact=False)
        lo_den, hi_den = dens[0], dens[0]
        for den in dens[1:]:
            lo_den = jnp.minimum(lo_den, den)
            hi_den = jnp.maximum(hi_den, den)
        trusted = jnp.logical_and(jnp.min(lo_den) >= SHIFT_DEN_MIN, jnp.max(hi_den) <= SHIFT_DEN_MAX)

        @pl.when(jnp.logical_not(trusted))
        def _():
            block(exact=True)

    @pl.when(t == n_blocks)
    def _():
        drain()


def _attention(q, kc, kl, vc, vl, tq=512):
    b, s, _ = q.shape
    c = kc.shape[1]
    nq = s // tq
    n_blocks = b * nq
    pair_w = 2 * V_HEAD_DIM
    main_w = ATTN_WIDTH - pair_w
    last = MLA_HEADS - 1
    cur = lambda t: jnp.minimum(t, n_blocks - 1)
    prev = lambda t: jnp.maximum(t - 1, 0)
    return pl.pallas_call(
        _attn_kernel,
        grid=(n_blocks + 1,),
        in_specs=[
            pl.BlockSpec((1, tq, QK_WIDTH), lambda t: (cur(t) // nq, cur(t) % nq, 0)),
            pl.BlockSpec((1, c, QK_WIDTH), lambda t: (cur(t) // nq, 0, 0)),
            pl.BlockSpec((1, s, QK_WIDTH), lambda t: (cur(t) // nq, 0, 0)),
            pl.BlockSpec((1, c, QK_WIDTH), lambda t: (cur(t) // nq, 0, 0)),
            pl.BlockSpec((1, s, QK_WIDTH), lambda t: (cur(t) // nq, 0, 0)),
            pl.BlockSpec((1, c, HEAD_SLOT), lambda t: (prev(t) // nq, 0, last)),
            pl.BlockSpec((1, s, HEAD_SLOT), lambda t: (prev(t) // nq, 0, last)),
        ],
        out_specs=[
            pl.BlockSpec((1, tq, main_w), lambda t: (cur(t) // nq, cur(t) % nq, 0)),
            pl.BlockSpec((1, tq, pair_w), lambda t: (prev(t) // nq, prev(t) % nq, 0)),
        ],
        out_shape=[
            jax.ShapeDtypeStruct((b, s, main_w), BF16),
            jax.ShapeDtypeStruct((b, s, pair_w), BF16),
        ],
        scratch_shapes=[
            pltpu.VMEM((tq, c + s), BF16),
            pltpu.VMEM((tq, pair_w), F32),
        ],
        compiler_params=pltpu.CompilerParams(
            dimension_semantics=("arbitrary",), vmem_limit_bytes=VMEM_LIMIT_BYTES),
        name="attn",
    )(q, kc, kl, vc, vl, vc, vl)


def _fourier_kernel(t_ref, tw_ref, u_ref, cc_ref, wf_ref, o_ref, tb_ref, cw_ref):
    @pl.when(pl.program_id(0) == 0)
    def _():
        tb_ref[...] = t_ref[...].astype(BF16)
        cc = cc_ref[...].astype(BF16)
        for g in range(FOURIER_GROUPS):
            cw_ref[g] = _dot(cc, wf_ref[g]).astype(BF16)

    n_sub = tb_ref.shape[1]
    f = _dot(tb_ref[...], u_ref[0])
    gd = FOURIER_GROUP_DIM
    for g in range(FOURIER_GROUPS):
        gr, gi = [], []
        for r in range(FFT_RADIX):
            lo = r * FOURIER_WIDTH + g * gd
            a = f[:n_sub, lo:lo + gd]
            b = f[n_sub:, lo:lo + gd]
            if r == 0:
                gr.append(a)
                gi.append(b)
            else:
                c = tw_ref[r - 1, 0]
                s = tw_ref[r - 1, 1]
                gr.append(a * c - b * s)
                gi.append(a * s + b * c)
        ar, ai = gr[0] + gr[2], gi[0] + gi[2]
        br, bi = gr[0] - gr[2], gi[0] - gi[2]
        cr, ci = gr[1] + gr[3], gi[1] + gi[3]
        dr, di = gr[1] - gr[3], gi[1] - gi[3]
        xr = jnp.concatenate([ar + cr, br - di, ar - cr, br + di], axis=0)
        xi = jnp.concatenate([ai + ci, bi + dr, ai - ci, bi - dr], axis=0)
        lhs = jnp.concatenate([xr, xi], axis=1).astype(BF16)
        o_ref[0, :, g * gd:(g + 1) * gd] = _dot(lhs, cw_ref[g]).astype(BF16)


def _fourier(u4, tmat, tw, cmat, wf):
    b, n_sub, _ = u4.shape
    s = n_sub * FFT_RADIX
    full = lambda a: pl.BlockSpec(a.shape, lambda i: (0,) * a.ndim)
    return pl.pallas_call(
        _fourier_kernel,
        grid=(b,),
        in_specs=[
            full(tmat), full(tw),
            pl.BlockSpec((1, n_sub, FFT_RADIX * FOURIER_WIDTH), lambda i: (i, 0, 0)),
            full(cmat), full(wf),
        ],
        out_specs=pl.BlockSpec((1, s, FOURIER_WIDTH), lambda i: (i, 0, 0)),
        out_shape=jax.ShapeDtypeStruct((b, s, FOURIER_WIDTH), BF16),
        scratch_shapes=[
            pltpu.VMEM(tmat.shape, BF16),
            pltpu.VMEM((FOURIER_GROUPS, 2 * FOURIER_GROUP_DIM, FOURIER_GROUP_DIM), BF16),
        ],
        compiler_params=pltpu.CompilerParams(
            dimension_semantics=("arbitrary",), vmem_limit_bytes=VMEM_LIMIT_BYTES),
        name="fourier",
    )(tmat, tw, u4, cmat, wf)


def _post_kernel(x_ref, four_ref, am_ref, al_ref, modc_ref, modp_ref, gpm_ref, gpf_ref, gqf_ref,
                 wof_ref, woa_ref, wg_ref, wu_ref, wd_ref, o_ref, x1_ref, h2_ref):
    t = pl.program_id(0)
    n_blocks = pl.num_programs(0) - 1

    def mix():
        gt_m = modc_ref[0, 2:3, :]
        sh_f = modc_ref[0, 3:4, :]
        sc_f = modc_ref[0, 4:5, :]
        attn = jnp.concatenate([am_ref[0], al_ref[0]], axis=1)
        y = _dot(four_ref[0], wof_ref[...]) + _dot(attn, woa_ref[...])
        x1 = x_ref[0] + gt_m * _rms(y, gpm_ref[...])
        x1_ref[...] = x1
        h2_ref[...] = (_rms(x1, gpf_ref[...]) * (1.0 + sc_f) + sh_f).astype(BF16)

    def ffn():
        gt_f = modp_ref[0, 5:6, :]
        h2 = h2_ref[...]
        g = _dot(h2, wg_ref[...])
        up = _dot(h2, wu_ref[...])
        act = (g / (1.0 + jnp.exp(-g)) * up).astype(BF16)
        o_ref[0] = x1_ref[...] + gt_f * _rms(_dot(act, wd_ref[...]), gqf_ref[...])

    @pl.when(t == 0)
    def _():
        mix()

    @pl.when(jnp.logical_and(t > 0, t < n_blocks))
    def _():
        ffn()
        mix()

    @pl.when(t == n_blocks)
    def _():
        ffn()


def _post(x, four, attn_main, attn_last, mod3, gpm, gpf, gqf, wof, woa, wg, wu, wd, tm=512):
    b, s, d = x.shape
    nj = s // tm
    n_blocks = b * nj
    cur = lambda t: jnp.minimum(t, n_blocks - 1)
    prev = lambda t: jnp.maximum(t - 1, 0)
    const = lambda shape: pl.BlockSpec(shape, lambda t: (0,) * len(shape), pipeline_mode=pl.Buffered(1))
    rows = lambda w: pl.BlockSpec((1, tm, w), lambda t: (cur(t) // nj, cur(t) % nj, 0))
    return pl.pallas_call(
        _post_kernel,
        grid=(n_blocks + 1,),
        in_specs=[
            rows(d), rows(FOURIER_WIDTH), rows(attn_main.shape[2]), rows(attn_last.shape[2]),
            pl.BlockSpec((1, 6, d), lambda t: (cur(t) // nj, 0, 0)),
            pl.BlockSpec((1, 6, d), lambda t: (prev(t) // nj, 0, 0)),
            const(gpm.shape), const(gpf.shape), const(gqf.shape),
            const(wof.shape), const(woa.shape), const(wg.shape), const(wu.shape), const(wd.shape),
        ],
        out_specs=pl.BlockSpec((1, tm, d), lambda t: (prev(t) // nj, prev(t) % nj, 0)),
        out_shape=jax.ShapeDtypeStruct((b, s, d), F32),
        scratch_shapes=[pltpu.VMEM((tm, d), F32), pltpu.VMEM((tm, d), BF16)],
        compiler_params=pltpu.CompilerParams(
            dimension_semantics=("arbitrary",), vmem_limit_bytes=VMEM_LIMIT_BYTES),
        name="post",
    )(x, four, attn_main, attn_last, mod3, mod3, gpm, gpf, gqf, wof, woa, wg, wu, wd)


def _head_slot(nope, rope):
    pad = HEAD_SLOT - QK_NOPE_DIM - QK_ROPE_DIM
    return jnp.concatenate([nope, rope, jnp.zeros(rope.shape[:-1] + (pad,), rope.dtype)], axis=-1)


def _rope_tables(n_lat, q_scale):
    t = np.arange(n_lat)
    hf = QK_ROPE_DIM // 4
    inv_freq = ROPE_BASE ** (-np.arange(hf, dtype=np.float64) / hf)
    ar = (t // GRID_W)[:, None] * inv_freq[None, :]
    ac = (t % GRID_W)[:, None] * inv_freq[None, :]
    z = np.zeros_like(ar)
    cos32 = np.concatenate([np.cos(ar), np.cos(ar), np.cos(ac), np.cos(ac)], axis=-1)
    below32 = np.concatenate([z, np.sin(ar), z, np.sin(ac)], axis=-1)
    above32 = np.concatenate([-np.sin(ar), z, -np.sin(ac), z], axis=-1)
    pad = np.zeros((n_lat, HEAD_SLOT - QK_NOPE_DIM - QK_ROPE_DIM))
    ones = np.ones((n_lat, QK_NOPE_DIM))
    zeros = np.zeros((n_lat, QK_NOPE_DIM))
    slot = lambda nope, rope: np.concatenate([nope, rope, pad], axis=-1)
    tables = [slot(ones * q_scale, cos32 * q_scale), slot(zeros, below32 * q_scale), slot(zeros, above32 * q_scale),
              slot(zeros, cos32), slot(zeros, below32), slot(zeros, above32)]
    return jnp.asarray(np.stack(tables).astype(np.float32))


def _dft_tables(n_pos, n_ch):
    n_sub = n_pos // FFT_RADIX
    m = np.arange(n_sub, dtype=np.int64)
    ang = 2.0 * np.pi * ((m[:, None] * m[None, :]) % n_sub) / n_sub
    tmat = np.concatenate([np.cos(ang), np.sin(ang)], axis=0).astype(np.float32)
    tw = np.zeros((FFT_RADIX - 1, 2, n_sub, FOURIER_GROUP_DIM), np.float32)
    for r in range(1, FFT_RADIX):
        a = 2.0 * np.pi * r * m / n_pos
        tw[r - 1, 0] = np.cos(a)[:, None]
        tw[r - 1, 1] = np.sin(a)[:, None]
    c = np.arange(n_ch, dtype=np.int64)
    angc = 2.0 * np.pi * ((c[:, None] * c[None, :]) % n_ch) / n_ch
    norm = 1.0 / np.sqrt(float(n_pos * n_ch))
    cmat = np.concatenate([np.cos(angc) * norm, -np.sin(angc) * norm], axis=0).astype(np.float32)
    return jnp.asarray(tmat), jnp.asarray(tw), jnp.asarray(cmat)


def kernel(x, c, ctx, c_ctx, w_ada, b_ada, g_pre_mix, g_post_mix, g_pre_ffn, g_post_ffn, w_in, g_q_a,
           w_q_b, g_kv_a, w_kv_b, w_fourier, w_out, w_gate, w_up, w_down):
    assert w_ada.shape[0] == 1, "single-layer block"
    batch, n_lat, d = x.shape

    mod_rows = -(-(batch + 1) // 8) * 8
    cc = jnp.concatenate([c, c_ctx[None, :], jnp.zeros((mod_rows - batch - 1, d), F32)], axis=0)
    mod = _adaln(cc, w_ada[0], b_ada[0][None, :])
    mod3 = mod.reshape(mod_rows, 6, d)

    w_in0 = w_in[0]
    w_kr = w_in0[:, ROPE_COL:]
    zeros_d = jnp.zeros((d, QK_NOPE_DIM), F32)
    kr_slot = _head_slot(zeros_d, w_kr)
    win = jnp.concatenate([w_in0[:, :ROPE_COL], kr_slot], axis=1).astype(BF16)
    win_c = jnp.concatenate([w_in0[:, KV_COL:ROPE_COL], kr_slot], axis=1).astype(BF16)

    wq3 = w_q_b[0].reshape(Q_LORA_RANK, MLA_HEADS, QK_NOPE_DIM + QK_ROPE_DIM)
    wq_nope, wq_rope = wq3[..., :QK_NOPE_DIM], wq3[..., QK_NOPE_DIM:]
    wq = _head_slot(wq_nope, wq_rope).reshape(Q_LORA_RANK, QK_WIDTH).astype(BF16)

    wkv3 = w_kv_b[0].reshape(KV_LORA_RANK, MLA_HEADS, QK_NOPE_DIM + V_HEAD_DIM)
    wk_nope, wv = wkv3[..., :QK_NOPE_DIM], wkv3[..., QK_NOPE_DIM:]
    wk_slots = _head_slot(wk_nope, jnp.zeros(wk_nope.shape[:-1] + (QK_ROPE_DIM,), F32))
    wv_slots = jnp.concatenate([wv, jnp.zeros(wv.shape[:-1] + (HEAD_SLOT - V_HEAD_DIM,), F32)], axis=-1)
    wkv = jnp.concatenate([wk_slots.reshape(KV_LORA_RANK, QK_WIDTH),
                           wv_slots.reshape(KV_LORA_RANK, QK_WIDTH)], axis=1).astype(BF16)

    q_scale = float((QK_NOPE_DIM + QK_ROPE_DIM) ** -0.5 * np.log2(np.e))
    rope = _rope_tables(n_lat, q_scale)
    tmat, tw, cmat = _dft_tables(n_lat, FOURIER_GROUP_DIM)

    row2 = lambda g: g[0][None, :]
    u_f, q, k_lat, v_lat = _premix(x, mod3, row2(g_pre_mix), win, row2(g_q_a), wq, row2(g_kv_a), wkv,
                                   rope)
    k_ctx, v_ctx = _ctxkv(ctx, mod3, batch, row2(g_pre_mix), win_c, row2(g_kv_a), wkv)
    attn_main, attn_last = _attention(q, k_ctx, k_lat, v_ctx, v_lat)
    four = _fourier(u_f, tmat, tw, cmat, w_fourier[0].astype(BF16))
    w_out0 = w_out[0].astype(BF16)
    return _post(x, four, attn_main, attn_last, mod3, row2(g_post_mix), row2(g_pre_ffn), row2(g_post_ffn),
                 w_out0[:FOURIER_WIDTH], w_out0[FOURIER_WIDTH:],
                 w_gate[0].astype(BF16), w_up[0].astype(BF16), w_down[0].astype(BF16))
```

```python
import functools

import numpy as np
import jax
import jax.numpy as jnp
from jax import lax
from jax.experimental import pallas as pl
from jax.experimental.pallas import tpu as pltpu

F32 = jnp.float32
BF16 = jnp.bfloat16

D_MODEL = 1024
GRID_W = 64
FOURIER_GROUPS = 4
FOURIER_GROUP_DIM = 128
FOURIER_WIDTH = FOURIER_GROUPS * FOURIER_GROUP_DIM
MLA_HEADS = 8
QK_NOPE_DIM = 64
QK_ROPE_DIM = 32
V_HEAD_DIM = 64
Q_LORA_RANK = 256
KV_LORA_RANK = 128
KV_COL = FOURIER_WIDTH + Q_LORA_RANK
ROPE_COL = KV_COL + KV_LORA_RANK
ROPE_BASE = 10000.0
NORM_EPS = 1e-6
FFT_RADIX = 4
HEAD_SLOT = 128
SHIFT_DEN_MIN = 2.0 ** -40
SHIFT_DEN_MAX = 2.0 ** 40
PV_KEY_TILE = 256
ATTN_WIDTH = MLA_HEADS * V_HEAD_DIM
QK_WIDTH = MLA_HEADS * HEAD_SLOT

VMEM_LIMIT_BYTES = 56 * 1024 * 1024


def _rms(x, g):
    return x * lax.rsqrt(jnp.mean(x * x, axis=-1, keepdims=True) + NORM_EPS) * g


def _dot(a, b):
    return jnp.dot(a, b, preferred_element_type=F32)


def _dot_nt(a, b):
    return lax.dot_general(a, b, (((1,), (1,)), ((), ())), preferred_element_type=F32)


def _rotary(a, rope_ref, first):
    half = QK_ROPE_DIM // 4
    below = pltpu.roll(a, half, axis=1)
    above = pltpu.roll(a, HEAD_SLOT - half, axis=1)
    return a * rope_ref[first] + below * rope_ref[first + 1] + above * rope_ref[first + 2]


def _denominator_lanes():
    lane = lax.broadcasted_iota(jnp.int32, (1, QK_WIDTH), 1)
    return (lane % HEAD_SLOT == V_HEAD_DIM).astype(F32)


def _adaln_kernel(c_ref, w_ref, b_ref, o_ref):
    c = c_ref[...]
    a = c / (1.0 + jnp.exp(-c))
    o_ref[...] = _dot(a.astype(BF16), w_ref[...].astype(BF16)) + b_ref[...]


def _adaln(cc, w_ada, b_ada, tn=512):
    rows, d = cc.shape
    n = w_ada.shape[1]
    return pl.pallas_call(
        _adaln_kernel,
        grid=(n // tn,),
        in_specs=[
            pl.BlockSpec((rows, d), lambda j: (0, 0)),
            pl.BlockSpec((d, tn), lambda j: (0, j)),
            pl.BlockSpec((1, tn), lambda j: (0, j)),
        ],
        out_specs=pl.BlockSpec((rows, tn), lambda j: (0, j)),
        out_shape=jax.ShapeDtypeStruct((rows, n), F32),
        name="adaln",
    )(cc, w_ada, b_ada)


def _premix_kernel(x_ref, mod_ref, gpre_ref, win_ref, gq_ref, wq_ref, gkv_ref, wkv_ref,
                   rope_ref, u_ref, q_ref, k_ref, v_ref, us_ref):
    x = x_ref[0]
    shift = mod_ref[0, 0:1, :]
    scale = mod_ref[0, 1:2, :]
    h = _rms(x, gpre_ref[...]) * (1.0 + scale) + shift
    p = _dot(h.astype(BF16), win_ref[...])
    sub = us_ref.shape[1] // FFT_RADIX
    for g in range(FOURIER_GROUPS):
        glo = g * FOURIER_GROUP_DIM
        us_ref[g] = p[:, glo:glo + FOURIER_GROUP_DIM]
        for r in range(FFT_RADIX):
            lo = r * FOURIER_WIDTH + glo
            u_ref[0, :, lo:lo + FOURIER_GROUP_DIM] = (
                us_ref[g, pl.ds(r, sub, stride=FFT_RADIX), :].astype(BF16))

    qn = _rms(p[:, FOURIER_WIDTH:KV_COL], gq_ref[...]).astype(BF16)
    qq = _dot(qn, wq_ref[...])
    cosq = rope_ref[0]
    sinq = rope_ref[1]
    for hd in range(MLA_HEADS):
        lo = hd * HEAD_SLOT
        q_ref[0, :, lo:lo + HEAD_SLOT] = (
            qq[:, lo:lo + HEAD_SLOT] * cosq + qq[:, QK_WIDTH + lo:QK_WIDTH + lo + HEAD_SLOT] * sinq
        ).astype(BF16)

    kvn = _rms(p[:, KV_COL:ROPE_COL], gkv_ref[...]).astype(BF16)
    kv = _dot(kvn, wkv_ref[...])
    kr = _rotary(p[:, ROPE_COL:ROPE_COL + HEAD_SLOT], rope_ref, 2)
    for hd in range(MLA_HEADS):
        lo = hd * HEAD_SLOT
        k_ref[0, :, lo:lo + HEAD_SLOT] = (kv[:, lo:lo + HEAD_SLOT] + kr).astype(BF16)
    v_ref[0] = (kv[:, QK_WIDTH:] + _denominator_lanes()).astype(BF16)


def _premix(x, mod3, gpre, win, gq, wq, gkv, wkv, rope, tm=512):
    b, s, d = x.shape
    const = lambda shape: pl.BlockSpec(shape, lambda i, j: (0,) * len(shape))
    rows = lambda w: pl.BlockSpec((1, tm, w), lambda i, j: (i, j, 0))
    return pl.pallas_call(
        _premix_kernel,
        grid=(b, s // tm),
        in_specs=[
            rows(d),
            pl.BlockSpec((1, 6, d), lambda i, j: (i, 0, 0)),
            const(gpre.shape), const(win.shape), const(gq.shape), const(wq.shape),
            const(gkv.shape), const(wkv.shape),
            pl.BlockSpec((rope.shape[0], tm, HEAD_SLOT), lambda i, j: (0, j, 0)),
        ],
        out_specs=[
            pl.BlockSpec((1, tm // FFT_RADIX, FFT_RADIX * FOURIER_WIDTH), lambda i, j: (i, j, 0)),
            rows(QK_WIDTH), rows(QK_WIDTH), rows(QK_WIDTH)],
        out_shape=[
            jax.ShapeDtypeStruct((b, s // FFT_RADIX, FFT_RADIX * FOURIER_WIDTH), BF16),
            jax.ShapeDtypeStruct((b, s, QK_WIDTH), BF16),
            jax.ShapeDtypeStruct((b, s, QK_WIDTH), BF16),
            jax.ShapeDtypeStruct((b, s, QK_WIDTH), BF16),
        ],
        scratch_shapes=[pltpu.VMEM((FOURIER_GROUPS, tm, FOURIER_GROUP_DIM), F32)],
        compiler_params=pltpu.CompilerParams(vmem_limit_bytes=VMEM_LIMIT_BYTES),
        name="premix",
    )(x, mod3, gpre, win, gq, wq, gkv, wkv, rope)


def _ctxkv_kernel(x_ref, mod_ref, gpre_ref, win_ref, gkv_ref, wkv_ref, k_ref, v_ref):
    x = x_ref[0]
    shift = mod_ref[0, 0:1, :]
    scale = mod_ref[0, 1:2, :]
    h = _rms(x, gpre_ref[...]) * (1.0 + scale) + shift
    p = _dot(h.astype(BF16), win_ref[...])
    kvn = _rms(p[:, :KV_LORA_RANK], gkv_ref[...]).astype(BF16)
    kv = _dot(kvn, wkv_ref[...])
    kr = p[:, KV_LORA_RANK:]
    for hd in range(MLA_HEADS):
        lo = hd * HEAD_SLOT
        k_ref[0, :, lo:lo + HEAD_SLOT] = (kv[:, lo:lo + HEAD_SLOT] + kr).astype(BF16)
    v_ref[0] = (kv[:, QK_WIDTH:] + _denominator_lanes()).astype(BF16)


def _ctxkv(ctx, mod3, ctx_row, gpre, win_c, gkv, wkv):
    b, c, d = ctx.shape
    const = lambda shape: pl.BlockSpec(shape, lambda i: (0,) * len(shape))
    rows = lambda w: pl.BlockSpec((1, c, w), lambda i: (i, 0, 0))
    return pl.pallas_call(
        _ctxkv_kernel,
        grid=(b,),
        in_specs=[
            rows(d),
            pl.BlockSpec((1, 6, d), lambda i: (ctx_row, 0, 0)),
            const(gpre.shape), const(win_c.shape), const(gkv.shape), const(wkv.shape),
        ],
        out_specs=[rows(QK_WIDTH), rows(QK_WIDTH)],
        out_shape=[
            jax.ShapeDtypeStruct((b, c, QK_WIDTH), BF16),
            jax.ShapeDtypeStruct((b, c, QK_WIDTH), BF16),
        ],
        name="ctxkv",
    )(ctx, mod3, gpre, win_c, gkv, wkv)


def _attn_kernel(q_ref, kc_ref, kl_ref, vc_ref, vl_ref, vcp_ref, vlp_ref, omain_ref, olast_ref,
                 p_ref, oprev_ref):
    t = pl.program_id(0)
    n_blocks = pl.num_programs(0) - 1
    n_ctx = kc_ref.shape[1]
    n_lat = kl_ref.shape[1]
    tq = q_ref.shape[1]
    pair_w = 2 * V_HEAD_DIM
    first_half = lax.broadcasted_iota(jnp.int32, (tq, pair_w), 1) < V_HEAD_DIM
    last = MLA_HEADS - 1

    @pl.when(t == 0)
    def _():
        p_ref[...] = jnp.ones_like(p_ref)
        oprev_ref[...] = jnp.zeros_like(oprev_ref)

    def drain():
        o = _dot(p_ref[:, :n_ctx], vcp_ref[0]) + _dot(p_ref[:, n_ctx:], vlp_ref[0])
        o = pltpu.roll(o / o[:, V_HEAD_DIM:V_HEAD_DIM + 1], V_HEAD_DIM, axis=1)
        olast_ref[0] = jnp.where(first_half, oprev_ref[...], o).astype(BF16)

    def head(hd, exact):
        lo = hd * HEAD_SLOT
        qh = q_ref[0, :, lo:lo + HEAD_SLOT]
        s_c = _dot_nt(qh, kc_ref[0, :, lo:lo + HEAD_SLOT])
        s_l = _dot_nt(qh, kl_ref[0, :, lo:lo + HEAD_SLOT])
        shift = jnp.max(s_c, axis=-1, keepdims=True)
        if exact:
            shift = jnp.maximum(shift, jnp.max(s_l, axis=-1, keepdims=True))
        if hd == last:
            p_c = jnp.exp2(s_c - shift)
            p_l = jnp.exp2(s_l - shift)
            p_ref[:, :n_ctx] = p_c.astype(BF16)
            p_ref[:, n_ctx:] = p_l.astype(BF16)
            return None, jnp.sum(p_c, axis=-1, keepdims=True) + jnp.sum(p_l, axis=-1, keepdims=True)
        o = _dot(jnp.exp2(s_c - shift).astype(BF16), vc_ref[0, :, lo:lo + HEAD_SLOT])
        for k0 in range(0, n_lat, PV_KEY_TILE):
            p = jnp.exp2(s_l[:, k0:k0 + PV_KEY_TILE] - shift).astype(BF16)
            o = o + _dot(p, vl_ref[0, k0:k0 + PV_KEY_TILE, lo:lo + HEAD_SLOT])
        den = o[:, V_HEAD_DIM:V_HEAD_DIM + 1]
        return o / den, den

    def block(exact):
        dens = []
        for pair in range(MLA_HEADS // 2):
            outs = []
            for hd in (2 * pair, 2 * pair + 1):
                o, den = head(hd, exact)
                dens.append(den)
                if o is not None:
                    outs.append(o)
            if len(outs) == 2:
                vlo = pair * pair_w
                omain_ref[0, :, vlo:vlo + pair_w] = jnp.where(
                    first_half, outs[0], pltpu.roll(outs[1], V_HEAD_DIM, axis=1)).astype(BF16)
            else:
                oprev_ref[...] = outs[0]
        return dens

    @pl.when(t < n_blocks)
    def _():
        drain()
        dens = block(exact=False)
        lo_den, hi_den = dens[0], dens[0]
        for den in dens[1:]:
            lo_den = jnp.minimum(lo_den, den)
            hi_den = jnp.maximum(hi_den, den)
        trusted = jnp.logical_and(jnp.min(lo_den) >= SHIFT_DEN_MIN, jnp.max(hi_den) <= SHIFT_DEN_MAX)

        @pl.when(jnp.logical_not(trusted))
        def _():
            block(exact=True)

    @pl.when(t == n_blocks)
    def _():
        drain()


def _attention(q, kc, kl, vc, vl, tq=512):
    b, s, _ = q.shape
    c = kc.shape[1]
    nq = s // tq
    n_blocks = b * nq
    pair_w = 2 * V_HEAD_DIM
    main_w = ATTN_WIDTH - pair_w
    last = MLA_HEADS - 1
    cur = lambda t: jnp.minimum(t, n_blocks - 1)
    prev = lambda t: jnp.maximum(t - 1, 0)
    return pl.pallas_call(
        _attn_kernel,
        grid=(n_blocks + 1,),
        in_specs=[
            pl.BlockSpec((1, tq, QK_WIDTH), lambda t: (cur(t) // nq, cur(t) % nq, 0)),
            pl.BlockSpec((1, c, QK_WIDTH), lambda t: (cur(t) // nq, 0, 0)),
            pl.BlockSpec((1, s, QK_WIDTH), lambda t: (cur(t) // nq, 0, 0)),
            pl.BlockSpec((1, c, QK_WIDTH), lambda t: (cur(t) // nq, 0, 0)),
            pl.BlockSpec((1, s, QK_WIDTH), lambda t: (cur(t) // nq, 0, 0)),
            pl.BlockSpec((1, c, HEAD_SLOT), lambda t: (prev(t) // nq, 0, last)),
            pl.BlockSpec((1, s, HEAD_SLOT), lambda t: (prev(t) // nq, 0, last)),
        ],
        out_specs=[
            pl.BlockSpec((1, tq, main_w), lambda t: (cur(t) // nq, cur(t) % nq, 0)),
            pl.BlockSpec((1, tq, pair_w), lambda t: (prev(t) // nq, prev(t) % nq, 0)),
        ],
        out_shape=[
            jax.ShapeDtypeStruct((b, s, main_w), BF16),
            jax.ShapeDtypeStruct((b, s, pair_w), BF16),
        ],
        scratch_shapes=[
            pltpu.VMEM((tq, c + s), BF16),
            pltpu.VMEM((tq, pair_w), F32),
        ],
        compiler_params=pltpu.CompilerParams(
            dimension_semantics=("arbitrary",), vmem_limit_bytes=VMEM_LIMIT_BYTES),
        name="attn",
    )(q, kc, kl, vc, vl, vc, vl)


def _fourier_kernel(t_ref, tw_ref, u_ref, cc_ref, wf_ref, o_ref, tb_ref, cw_ref):
    @pl.when(pl.program_id(0) == 0)
    def _():
        tb_ref[...] = t_ref[...].astype(BF16)
        cc = cc_ref[...].astype(BF16)
        for g in range(FOURIER_GROUPS):
            cw_ref[g] = _dot(cc, wf_ref[g]).astype(BF16)

    n_sub = tb_ref.shape[1]
    f = _dot(tb_ref[...], u_ref[0])
    gd = FOURIER_GROUP_DIM
    for g in range(FOURIER_GROUPS):
        gr, gi = [], []
        for r in range(FFT_RADIX):
            lo = r * FOURIER_WIDTH + g * gd
            a = f[:n_sub, lo:lo + gd]
            b = f[n_sub:, lo:lo + gd]
            if r == 0:
                gr.append(a)
                gi.append(b)
            else:
                c = tw_ref[r - 1, 0]
                s = tw_ref[r - 1, 1]
                gr.append(a * c - b * s)
                gi.append(a * s + b * c)
        ar, ai = gr[0] + gr[2], gi[0] + gi[2]
        br, bi = gr[0] - gr[2], gi[0] - gi[2]
        cr, ci = gr[1] + gr[3], gi[1] + gi[3]
        dr, di = gr[1] - gr[3], gi[1] - gi[3]
        xr = jnp.concatenate([ar + cr, br - di, ar - cr, br + di], axis=0)
        xi = jnp.concatenate([ai + ci, bi + dr, ai - ci, bi - dr], axis=0)
        lhs = jnp.concatenate([xr, xi], axis=1).astype(BF16)
        o_ref[0, :, g * gd:(g + 1) * gd] = _dot(lhs, cw_ref[g]).astype(BF16)


def _fourier(u4, tmat, tw, cmat, wf):
    b, n_sub, _ = u4.shape
    s = n_sub * FFT_RADIX
    full = lambda a: pl.BlockSpec(a.shape, lambda i: (0,) * a.ndim)
    return pl.pallas_call(
        _fourier_kernel,
        grid=(b,),
        in_specs=[
            full(tmat), full(tw),
            pl.BlockSpec((1, n_sub, FFT_RADIX * FOURIER_WIDTH), lambda i: (i, 0, 0)),
            full(cmat), full(wf),
        ],
        out_specs=pl.BlockSpec((1, s, FOURIER_WIDTH), lambda i: (i, 0, 0)),
        out_shape=jax.ShapeDtypeStruct((b, s, FOURIER_WIDTH), BF16),
        scratch_shapes=[
            pltpu.VMEM(tmat.shape, BF16),
            pltpu.VMEM((FOURIER_GROUPS, 2 * FOURIER_GROUP_DIM, FOURIER_GROUP_DIM), BF16),
        ],
        compiler_params=pltpu.CompilerParams(
            dimension_semantics=("arbitrary",), vmem_limit_bytes=VMEM_LIMIT_BYTES),
        name="fourier",
    )(tmat, tw, u4, cmat, wf)


def _post_kernel(x_ref, four_ref, am_ref, al_ref, modc_ref, modp_ref, gpm_ref, gpf_ref, gqf_ref,
                 wof_ref, woa_ref, wg_ref, wu_ref, wd_ref, o_ref, x1_ref, h2_ref):
    t = pl.program_id(0)
    n_blocks = pl.num_programs(0) - 1

    def mix():
        gt_m = modc_ref[0, 2:3, :]
        sh_f = modc_ref[0, 3:4, :]
        sc_f = modc_ref[0, 4:5, :]
        attn = jnp.concatenate([am_ref[0], al_ref[0]], axis=1)
        y = _dot(four_ref[0], wof_ref[...]) + _dot(attn, woa_ref[...])
        x1 = x_ref[0] + gt_m * _rms(y, gpm_ref[...])
        x1_ref[...] = x1
        h2_ref[...] = (_rms(x1, gpf_ref[...]) * (1.0 + sc_f) + sh_f).astype(BF16)

    def ffn():
        gt_f = modp_ref[0, 5:6, :]
        h2 = h2_ref[...]
        g = _dot(h2, wg_ref[...])
        up = _dot(h2, wu_ref[...])
        act = (g / (1.0 + jnp.exp(-g)) * up).astype(BF16)
        o_ref[0] = x1_ref[...] + gt_f * _rms(_dot(act, wd_ref[...]), gqf_ref[...])

    @pl.when(t == 0)
    def _():
        mix()

    @pl.when(jnp.logical_and(t > 0, t < n_blocks))
    def _():
        ffn()
        mix()

    @pl.when(t == n_blocks)
    def _():
        ffn()


def _post(x, four, attn_main, attn_last, mod3, gpm, gpf, gqf, wof, woa, wg, wu, wd, tm=512):
    b, s, d = x.shape
    nj = s // tm
    n_blocks = b * nj
    cur = lambda t: jnp.minimum(t, n_blocks - 1)
    prev = lambda t: jnp.maximum(t - 1, 0)
    const = lambda shape: pl.BlockSpec(shape, lambda t: (0,) * len(shape), pipeline_mode=pl.Buffered(1))
    rows = lambda w: pl.BlockSpec((1, tm, w), lambda t: (cur(t) // nj, cur(t) % nj, 0))
    return pl.pallas_call(
        _post_kernel,
        grid=(n_blocks + 1,),
        in_specs=[
            rows(d), rows(FOURIER_WIDTH), rows(attn_main.shape[2]), rows(attn_last.shape[2]),
            pl.BlockSpec((1, 6, d), lambda t: (cur(t) // nj, 0, 0)),
            pl.BlockSpec((1, 6, d), lambda t: (prev(t) // nj, 0, 0)),
            const(gpm.shape), const(gpf.shape), const(gqf.shape),
            const(wof.shape), const(woa.shape), const(wg.shape), const(wu.shape), const(wd.shape),
        ],
        out_specs=pl.BlockSpec((1, tm, d), lambda t: (prev(t) // nj, prev(t) % nj, 0)),
        out_shape=jax.ShapeDtypeStruct((b, s, d), F32),
        scratch_shapes=[pltpu.VMEM((tm, d), F32), pltpu.VMEM((tm, d), BF16)],
        compiler_params=pltpu.CompilerParams(
            dimension_semantics=("arbitrary",), vmem_limit_bytes=VMEM_LIMIT_BYTES),
        name="post",
    )(x, four, attn_main, attn_last, mod3, mod3, gpm, gpf, gqf, wof, woa, wg, wu, wd)


def _rope_rotate_cols(w):
    a = QK_ROPE_DIM // 2
    hf = a // 2
    blocks = []
    for s0 in (0, a):
        blocks += [-w[..., s0 + hf:s0 + a], w[..., s0:s0 + hf]]
    return jnp.concatenate(blocks, axis=-1)


def _head_slot(nope, rope):
    pad = HEAD_SLOT - QK_NOPE_DIM - QK_ROPE_DIM
    return jnp.concatenate([nope, rope, jnp.zeros(rope.shape[:-1] + (pad,), rope.dtype)], axis=-1)


def _rope_tables(n_lat, q_scale):
    t = np.arange(n_lat)
    hf = QK_ROPE_DIM // 4
    inv_freq = ROPE_BASE ** (-np.arange(hf, dtype=np.float64) / hf)
    ar = (t // GRID_W)[:, None] * inv_freq[None, :]
    ac = (t % GRID_W)[:, None] * inv_freq[None, :]
    z = np.zeros_like(ar)
    cos32 = np.concatenate([np.cos(ar), np.cos(ar), np.cos(ac), np.cos(ac)], axis=-1)
    sin32 = np.concatenate([np.sin(ar), np.sin(ar), np.sin(ac), np.sin(ac)], axis=-1)
    below32 = np.concatenate([z, np.sin(ar), z, np.sin(ac)], axis=-1)
    above32 = np.concatenate([-np.sin(ar), z, -np.sin(ac), z], axis=-1)
    pad = np.zeros((n_lat, HEAD_SLOT - QK_NOPE_DIM - QK_ROPE_DIM))
    ones = np.ones((n_lat, QK_NOPE_DIM))
    zeros = np.zeros((n_lat, QK_NOPE_DIM))
    slot = lambda nope, rope: np.concatenate([nope, rope, pad], axis=-1)
    tables = [slot(ones * q_scale, cos32 * q_scale), slot(zeros, sin32 * q_scale),
              slot(zeros, cos32), slot(zeros, below32), slot(zeros, above32)]
    return jnp.asarray(np.stack(tables).astype(np.float32))


def _dft_tables(n_pos, n_ch):
    n_sub = n_pos // FFT_RADIX
    m = np.arange(n_sub, dtype=np.int64)
    ang = 2.0 * np.pi * ((m[:, None] * m[None, :]) % n_sub) / n_sub
    tmat = np.concatenate([np.cos(ang), np.sin(ang)], axis=0).astype(np.float32)
    tw = np.zeros((FFT_RADIX - 1, 2, n_sub, FOURIER_GROUP_DIM), np.float32)
    for r in range(1, FFT_RADIX):
        a = 2.0 * np.pi * r * m / n_pos
        tw[r - 1, 0] = np.cos(a)[:, None]
        tw[r - 1, 1] = np.sin(a)[:, None]
    c = np.arange(n_ch, dtype=np.int64)
    angc = 2.0 * np.pi * ((c[:, None] * c[None, :]) % n_ch) / n_ch
    norm = 1.0 / np.sqrt(float(n_pos * n_ch))
    cmat = np.concatenate([np.cos(angc) * norm, -np.sin(angc) * norm], axis=0).astype(np.float32)
    return jnp.asarray(tmat), jnp.asarray(tw), jnp.asarray(cmat)


def kernel(x, c, ctx, c_ctx, w_ada, b_ada, g_pre_mix, g_post_mix, g_pre_ffn, g_post_ffn, w_in, g_q_a,
           w_q_b, g_kv_a, w_kv_b, w_fourier, w_out, w_gate, w_up, w_down):
    assert w_ada.shape[0] == 1, "single-layer block"
    batch, n_lat, d = x.shape

    mod_rows = -(-(batch + 1) // 8) * 8
    cc = jnp.concatenate([c, c_ctx[None, :], jnp.zeros((mod_rows - batch - 1, d), F32)], axis=0)
    mod = _adaln(cc, w_ada[0], b_ada[0][None, :])
    mod3 = mod.reshape(mod_rows, 6, d)

    w_in0 = w_in[0]
    w_kr = w_in0[:, ROPE_COL:]
    zeros_d = jnp.zeros((d, QK_NOPE_DIM), F32)
    kr_slot = _head_slot(zeros_d, w_kr)
    win = jnp.concatenate([w_in0[:, :ROPE_COL], kr_slot], axis=1).astype(BF16)
    win_c = jnp.concatenate([w_in0[:, KV_COL:ROPE_COL], kr_slot], axis=1).astype(BF16)

    wq3 = w_q_b[0].reshape(Q_LORA_RANK, MLA_HEADS, QK_NOPE_DIM + QK_ROPE_DIM)
    wq_nope, wq_rope = wq3[..., :QK_NOPE_DIM], wq3[..., QK_NOPE_DIM:]
    wq_a = _head_slot(wq_nope, wq_rope).reshape(Q_LORA_RANK, QK_WIDTH)
    wq_b = _head_slot(jnp.zeros_like(wq_nope), _rope_rotate_cols(wq_rope)).reshape(Q_LORA_RANK, QK_WIDTH)
    wq = jnp.concatenate([wq_a, wq_b], axis=1).astype(BF16)

    wkv3 = w_kv_b[0].reshape(KV_LORA_RANK, MLA_HEADS, QK_NOPE_DIM + V_HEAD_DIM)
    wk_nope, wv = wkv3[..., :QK_NOPE_DIM], wkv3[..., QK_NOPE_DIM:]
    wk_slots = _head_slot(wk_nope, jnp.zeros(wk_nope.shape[:-1] + (QK_ROPE_DIM,), F32))
    wv_slots = jnp.concatenate([wv, jnp.zeros(wv.shape[:-1] + (HEAD_SLOT - V_HEAD_DIM,), F32)], axis=-1)
    wkv = jnp.concatenate([wk_slots.reshape(KV_LORA_RANK, QK_WIDTH),
                           wv_slots.reshape(KV_LORA_RANK, QK_WIDTH)], axis=1).astype(BF16)

    q_scale = float((QK_NOPE_DIM + QK_ROPE_DIM) ** -0.5 * np.log2(np.e))
    rope = _rope_tables(n_lat, q_scale)
    tmat, tw, cmat = _dft_tables(n_lat, FOURIER_GROUP_DIM)

    row2 = lambda g: g[0][None, :]
    u_f, q, k_lat, v_lat = _premix(x, mod3, row2(g_pre_mix), win, row2(g_q_a), wq, row2(g_kv_a), wkv,
                                   rope)
    k_ctx, v_ctx = _ctxkv(ctx, mod3, batch, row2(g_pre_mix), win_c, row2(g_kv_a), wkv)
    attn_main, attn_last = _attention(q, k_ctx, k_lat, v_ctx, v_lat)
    four = _fourier(u_f, tmat, tw, cmat, w_fourier[0].astype(BF16))
    w_out0 = w_out[0].astype(BF16)
    return _post(x, four, attn_main, attn_last, mod3, row2(g_post_mix), row2(g_pre_ffn), row2(g_post_ffn),
                 w_out0[:FOURIER_WIDTH], w_out0[FOURIER_WIDTH:],
                 w_gate[0].astype(BF16), w_up[0].astype(BF16), w_down[0].astype(BF16))
```

```python
import functools

import numpy as np
import jax
import jax.numpy as jnp
from jax import lax
from jax.experimental import pallas as pl
from jax.experimental.pallas import tpu as pltpu

F32 = jnp.float32
BF16 = jnp.bfloat16

D_MODEL = 1024
GRID_W = 64
FOURIER_GROUPS = 4
FOURIER_GROUP_DIM = 128
FOURIER_WIDTH = FOURIER_GROUPS * FOURIER_GROUP_DIM
MLA_HEADS = 8
QK_NOPE_DIM = 64
QK_ROPE_DIM = 32
V_HEAD_DIM = 64
Q_LORA_RANK = 256
KV_LORA_RANK = 128
KV_COL = FOURIER_WIDTH + Q_LORA_RANK
ROPE_COL = KV_COL + KV_LORA_RANK
ROPE_BASE = 10000.0
NORM_EPS = 1e-6
FFT_RADIX = 4
HEAD_SLOT = 128
STAGE_ROWS_WIDE = 128
STAGE_ROWS_TALL = 352
STAGE_ROWS_OUT = 256
SHIFT_DEN_MIN = 2.0 ** -40
SHIFT_DEN_MAX = 2.0 ** 40
PV_KEY_TILE = 256
ATTN_WIDTH = MLA_HEADS * V_HEAD_DIM
QK_WIDTH = MLA_HEADS * HEAD_SLOT

VMEM_LIMIT_BYTES = 56 * 1024 * 1024


def _rms(x, g):
    return x * lax.rsqrt(jnp.mean(x * x, axis=-1, keepdims=True) + NORM_EPS) * g


def _dot(a, b):
    return jnp.dot(a, b, preferred_element_type=F32)


def _dot_nt(a, b):
    return lax.dot_general(a, b, (((1,), (1,)), ((), ())), preferred_element_type=F32)


def _rotary(a, rope_ref, first):
    half = QK_ROPE_DIM // 4
    below = pltpu.roll(a, half, axis=1)
    above = pltpu.roll(a, HEAD_SLOT - half, axis=1)
    return a * rope_ref[first] + below * rope_ref[first + 1] + above * rope_ref[first + 2]


def _denominator_lanes():
    lane = lax.broadcasted_iota(jnp.int32, (1, QK_WIDTH), 1)
    return (lane % HEAD_SLOT == V_HEAD_DIM).astype(F32)


def _adaln_kernel(c_ref, w_ref, b_ref, o_ref):
    c = c_ref[...]
    a = c / (1.0 + jnp.exp(-c))
    o_ref[...] = _dot(a.astype(BF16), w_ref[...].astype(BF16)) + b_ref[...]


def _adaln(cc, w_ada, b_ada, tn=512):
    rows, d = cc.shape
    n = w_ada.shape[1]
    return pl.pallas_call(
        _adaln_kernel,
        grid=(n // tn,),
        in_specs=[
            pl.BlockSpec((rows, d), lambda j: (0, 0)),
            pl.BlockSpec((d, tn), lambda j: (0, j)),
            pl.BlockSpec((1, tn), lambda j: (0, j)),
        ],
        out_specs=pl.BlockSpec((rows, tn), lambda j: (0, j)),
        out_shape=jax.ShapeDtypeStruct((rows, n), F32),
        name="adaln",
    )(cc, w_ada, b_ada)


def _premix_kernel(x_ref, mod_ref, gpre_ref, win_ref, gq_ref, wq_ref, gkv_ref, wkv_ref,
                   rope_ref, u_ref, q_ref, k_ref, v_ref, us_ref):
    x = x_ref[0]
    shift = mod_ref[0, 0:1, :]
    scale = mod_ref[0, 1:2, :]
    h = _rms(x, gpre_ref[...]) * (1.0 + scale) + shift
    p = _dot(h.astype(BF16), win_ref[...])
    sub = us_ref.shape[1] // FFT_RADIX
    for g in range(FOURIER_GROUPS):
        glo = g * FOURIER_GROUP_DIM
        us_ref[g] = p[:, glo:glo + FOURIER_GROUP_DIM]
        for r in range(FFT_RADIX):
            lo = r * FOURIER_WIDTH + glo
            u_ref[0, :, lo:lo + FOURIER_GROUP_DIM] = (
                us_ref[g, pl.ds(r, sub, stride=FFT_RADIX), :].astype(BF16))

    qn = _rms(p[:, FOURIER_WIDTH:KV_COL], gq_ref[...]).astype(BF16)
    qq = _dot(qn, wq_ref[...])
    cosq = rope_ref[0]
    sinq = rope_ref[1]
    for hd in range(MLA_HEADS):
        lo = hd * HEAD_SLOT
        q_ref[0, :, lo:lo + HEAD_SLOT] = (
            qq[:, lo:lo + HEAD_SLOT] * cosq + qq[:, QK_WIDTH + lo:QK_WIDTH + lo + HEAD_SLOT] * sinq
        ).astype(BF16)

    kvn = _rms(p[:, KV_COL:ROPE_COL], gkv_ref[...]).astype(BF16)
    kv = _dot(kvn, wkv_ref[...])
    kr = _rotary(p[:, ROPE_COL:ROPE_COL + HEAD_SLOT], rope_ref, 2)
    for hd in range(MLA_HEADS):
        lo = hd * HEAD_SLOT
        k_ref[0, :, lo:lo + HEAD_SLOT] = (kv[:, lo:lo + HEAD_SLOT] + kr).astype(BF16)
    v_ref[0] = (kv[:, QK_WIDTH:] + _denominator_lanes()).astype(BF16)


def _premix(x, mod3, gpre, win, gq, wq, gkv, wkv, rope, tm=512):
    b, s, d = x.shape
    const = lambda shape: pl.BlockSpec(shape, lambda i, j: (0,) * len(shape))
    rows = lambda w: pl.BlockSpec((1, tm, w), lambda i, j: (i, j, 0))
    return pl.pallas_call(
        _premix_kernel,
        grid=(b, s // tm),
        in_specs=[
            rows(d),
            pl.BlockSpec((1, 6, d), lambda i, j: (i, 0, 0)),
            const(gpre.shape), const(win.shape), const(gq.shape), const(wq.shape),
            const(gkv.shape), const(wkv.shape),
            pl.BlockSpec((rope.shape[0], tm, HEAD_SLOT), lambda i, j: (0, j, 0)),
        ],
        out_specs=[
            pl.BlockSpec((1, tm // FFT_RADIX, FFT_RADIX * FOURIER_WIDTH), lambda i, j: (i, j, 0)),
            rows(QK_WIDTH), rows(QK_WIDTH), rows(QK_WIDTH)],
        out_shape=[
            jax.ShapeDtypeStruct((b, s // FFT_RADIX, FFT_RADIX * FOURIER_WIDTH), BF16),
            jax.ShapeDtypeStruct((b, s, QK_WIDTH), BF16),
            jax.ShapeDtypeStruct((b, s, QK_WIDTH), BF16),
            jax.ShapeDtypeStruct((b, s, QK_WIDTH), BF16),
        ],
        scratch_shapes=[pltpu.VMEM((FOURIER_GROUPS, tm, FOURIER_GROUP_DIM), F32)],
        compiler_params=pltpu.CompilerParams(vmem_limit_bytes=VMEM_LIMIT_BYTES),
        name="premix",
    )(x, mod3, gpre, win, gq, wq, gkv, wkv, rope)


def _ctxkv_kernel(x_ref, mod_ref, gpre_ref, win_ref, gkv_ref, wkv_ref, k_ref, v_ref):
    x = x_ref[0]
    shift = mod_ref[0, 0:1, :]
    scale = mod_ref[0, 1:2, :]
    h = _rms(x, gpre_ref[...]) * (1.0 + scale) + shift
    p = _dot(h.astype(BF16), win_ref[...])
    kvn = _rms(p[:, :KV_LORA_RANK], gkv_ref[...]).astype(BF16)
    kv = _dot(kvn, wkv_ref[...])
    kr = p[:, KV_LORA_RANK:]
    for hd in range(MLA_HEADS):
        lo = hd * HEAD_SLOT
        k_ref[0, :, lo:lo + HEAD_SLOT] = (kv[:, lo:lo + HEAD_SLOT] + kr).astype(BF16)
    v_ref[0] = (kv[:, QK_WIDTH:] + _denominator_lanes()).astype(BF16)


def _ctxkv(ctx, mod3, ctx_row, gpre, win_c, gkv, wkv):
    b, c, d = ctx.shape
    const = lambda shape: pl.BlockSpec(shape, lambda i: (0,) * len(shape))
    rows = lambda w: pl.BlockSpec((1, c, w), lambda i: (i, 0, 0))
    return pl.pallas_call(
        _ctxkv_kernel,
        grid=(b,),
        in_specs=[
            rows(d),
            pl.BlockSpec((1, 6, d), lambda i: (ctx_row, 0, 0)),
            const(gpre.shape), const(win_c.shape), const(gkv.shape), const(wkv.shape),
        ],
        out_specs=[rows(QK_WIDTH), rows(QK_WIDTH)],
        out_shape=[
            jax.ShapeDtypeStruct((b, c, QK_WIDTH), BF16),
            jax.ShapeDtypeStruct((b, c, QK_WIDTH), BF16),
        ],
        name="ctxkv",
    )(ctx, mod3, gpre, win_c, gkv, wkv)


def _attn_kernel(q_ref, kc_ref, kl_ref, vc_ref, vl_ref, vcp_ref, vlp_ref, omain_ref, olast_ref,
                 p_ref, oprev_ref):
    t = pl.program_id(0)
    n_blocks = pl.num_programs(0) - 1
    n_ctx = kc_ref.shape[1]
    n_lat = kl_ref.shape[1]
    tq = q_ref.shape[1]
    pair_w = 2 * V_HEAD_DIM
    first_half = lax.broadcasted_iota(jnp.int32, (tq, pair_w), 1) < V_HEAD_DIM
    last = MLA_HEADS - 1

    @pl.when(t == 0)
    def _():
        p_ref[...] = jnp.ones_like(p_ref)
        oprev_ref[...] = jnp.zeros_like(oprev_ref)

    def drain():
        o = _dot(p_ref[:, :n_ctx], vcp_ref[0]) + _dot(p_ref[:, n_ctx:], vlp_ref[0])
        o = pltpu.roll(o / o[:, V_HEAD_DIM:V_HEAD_DIM + 1], V_HEAD_DIM, axis=1)
        olast_ref[0] = jnp.where(first_half, oprev_ref[...], o).astype(BF16)

    def head(hd, exact):
        lo = hd * HEAD_SLOT
        qh = q_ref[0, :, lo:lo + HEAD_SLOT]
        s_c = _dot_nt(qh, kc_ref[0, :, lo:lo + HEAD_SLOT])
        s_l = _dot_nt(qh, kl_ref[0, :, lo:lo + HEAD_SLOT])
        shift = jnp.max(s_c, axis=-1, keepdims=True)
        if exact:
            shift = jnp.maximum(shift, jnp.max(s_l, axis=-1, keepdims=True))
        if hd == last:
            p_c = jnp.exp2(s_c - shift)
            p_l = jnp.exp2(s_l - shift)
            p_ref[:, :n_ctx] = p_c.astype(BF16)
            p_ref[:, n_ctx:] = p_l.astype(BF16)
            return None, jnp.sum(p_c, axis=-1, keepdims=True) + jnp.sum(p_l, axis=-1, keepdims=True)
        o = _dot(jnp.exp2(s_c - shift).astype(BF16), vc_ref[0, :, lo:lo + HEAD_SLOT])
        for k0 in range(0, n_lat, PV_KEY_TILE):
            p = jnp.exp2(s_l[:, k0:k0 + PV_KEY_TILE] - shift).astype(BF16)
            o = o + _dot(p, vl_ref[0, k0:k0 + PV_KEY_TILE, lo:lo + HEAD_SLOT])
        den = o[:, V_HEAD_DIM:V_HEAD_DIM + 1]
        return o / den, den

    def block(exact):
        dens = []
        for pair in range(MLA_HEADS // 2):
            outs = []
            for hd in (2 * pair, 2 * pair + 1):
                o, den = head(hd, exact)
                dens.append(den)
                if o is not None:
                    outs.append(o)
            if len(outs) == 2:
                vlo = pair * pair_w
                omain_ref[0, :, vlo:vlo + pair_w] = jnp.where(
                    first_half, outs[0], pltpu.roll(outs[1], V_HEAD_DIM, axis=1)).astype(BF16)
            else:
                oprev_ref[...] = outs[0]
        return dens

    @pl.when(t < n_blocks)
    def _():
        drain()
        dens = block(exact=False)
        lo_den, hi_den = dens[0], dens[0]
        for den in dens[1:]:
            lo_den = jnp.minimum(lo_den, den)
            hi_den = jnp.maximum(hi_den, den)
        trusted = jnp.logical_and(jnp.min(lo_den) >= SHIFT_DEN_MIN, jnp.max(hi_den) <= SHIFT_DEN_MAX)

        @pl.when(jnp.logical_not(trusted))
        def _():
            block(exact=True)

    @pl.when(t == n_blocks)
    def _():
        drain()


def _attention(q, kc, kl, vc, vl, tq=512):
    b, s, _ = q.shape
    c = kc.shape[1]
    nq = s // tq
    n_blocks = b * nq
    pair_w = 2 * V_HEAD_DIM
    main_w = ATTN_WIDTH - pair_w
    last = MLA_HEADS - 1
    cur = lambda t: jnp.minimum(t, n_blocks - 1)
    prev = lambda t: jnp.maximum(t - 1, 0)
    return pl.pallas_call(
        _attn_kernel,
        grid=(n_blocks + 1,),
        in_specs=[
            pl.BlockSpec((1, tq, QK_WIDTH), lambda t: (cur(t) // nq, cur(t) % nq, 0)),
            pl.BlockSpec((1, c, QK_WIDTH), lambda t: (cur(t) // nq, 0, 0)),
            pl.BlockSpec((1, s, QK_WIDTH), lambda t: (cur(t) // nq, 0, 0)),
            pl.BlockSpec((1, c, QK_WIDTH), lambda t: (cur(t) // nq, 0, 0)),
            pl.BlockSpec((1, s, QK_WIDTH), lambda t: (cur(t) // nq, 0, 0)),
            pl.BlockSpec((1, c, HEAD_SLOT), lambda t: (prev(t) // nq, 0, last)),
            pl.BlockSpec((1, s, HEAD_SLOT), lambda t: (prev(t) // nq, 0, last)),
        ],
        out_specs=[
            pl.BlockSpec((1, tq, main_w), lambda t: (cur(t) // nq, cur(t) % nq, 0)),
            pl.BlockSpec((1, tq, pair_w), lambda t: (prev(t) // nq, prev(t) % nq, 0)),
        ],
        out_shape=[
            jax.ShapeDtypeStruct((b, s, main_w), BF16),
            jax.ShapeDtypeStruct((b, s, pair_w), BF16),
        ],
        scratch_shapes=[
            pltpu.VMEM((tq, c + s), BF16),
            pltpu.VMEM((tq, pair_w), F32),
        ],
        compiler_params=pltpu.CompilerParams(
            dimension_semantics=("arbitrary",), vmem_limit_bytes=VMEM_LIMIT_BYTES),
        name="attn",
    )(q, kc, kl, vc, vl, vc, vl)


def _fourier_kernel(t_ref, tw_ref, u_ref, cc_ref, wf_ref, o_ref, tb_ref, cw_ref):
    @pl.when(pl.program_id(0) == 0)
    def _():
        tb_ref[...] = t_ref[...].astype(BF16)
        cc = cc_ref[...].astype(BF16)
        for g in range(FOURIER_GROUPS):
            cw_ref[g] = _dot(cc, wf_ref[g]).astype(BF16)

    n_sub = tb_ref.shape[1]
    f = _dot(tb_ref[...], u_ref[0])
    gd = FOURIER_GROUP_DIM
    for g in range(FOURIER_GROUPS):
        gr, gi = [], []
        for r in range(FFT_RADIX):
            lo = r * FOURIER_WIDTH + g * gd
            a = f[:n_sub, lo:lo + gd]
            b = f[n_sub:, lo:lo + gd]
            if r == 0:
                gr.append(a)
                gi.append(b)
            else:
                c = tw_ref[r - 1, 0]
                s = tw_ref[r - 1, 1]
                gr.append(a * c - b * s)
                gi.append(a * s + b * c)
        ar, ai = gr[0] + gr[2], gi[0] + gi[2]
        br, bi = gr[0] - gr[2], gi[0] - gi[2]
        cr, ci = gr[1] + gr[3], gi[1] + gi[3]
        dr, di = gr[1] - gr[3], gi[1] - gi[3]
        xr = jnp.concatenate([ar + cr, br - di, ar - cr, br + di], axis=0)
        xi = jnp.concatenate([ai + ci, bi + dr, ai - ci, bi - dr], axis=0)
        lhs = jnp.concatenate([xr, xi], axis=1).astype(BF16)
        o_ref[0, :, g * gd:(g + 1) * gd] = _dot(lhs, cw_ref[g]).astype(BF16)


def _fourier(u4, tmat, tw, cmat, wf):
    b, n_sub, _ = u4.shape
    s = n_sub * FFT_RADIX
    full = lambda a: pl.BlockSpec(a.shape, lambda i: (0,) * a.ndim)
    return pl.pallas_call(
        _fourier_kernel,
        grid=(b,),
        in_specs=[
            full(tmat), full(tw),
            pl.BlockSpec((1, n_sub, FFT_RADIX * FOURIER_WIDTH), lambda i: (i, 0, 0)),
            full(cmat), full(wf),
        ],
        out_specs=pl.BlockSpec((1, s, FOURIER_WIDTH), lambda i: (i, 0, 0)),
        out_shape=jax.ShapeDtypeStruct((b, s, FOURIER_WIDTH), BF16),
        scratch_shapes=[
            pltpu.VMEM(tmat.shape, BF16),
            pltpu.VMEM((FOURIER_GROUPS, 2 * FOURIER_GROUP_DIM, FOURIER_GROUP_DIM), BF16),
        ],
        compiler_params=pltpu.CompilerParams(
            dimension_semantics=("arbitrary",), vmem_limit_bytes=VMEM_LIMIT_BYTES),
        name="fourier",
    )(tmat, tw, u4, cmat, wf)


def _stage_cast(src_hbm, dst_ref, stage_ref, sem_ref, chunk):
    n_chunks = src_hbm.shape[0] // chunk

    def copy(i):
        slot = i % 2
        return pltpu.make_async_copy(
            src_hbm.at[pl.ds(i * chunk, chunk)], stage_ref.at[slot, pl.ds(0, chunk)], sem_ref.at[slot])

    copy(0).start()
    for i in range(n_chunks):
        if i + 1 < n_chunks:
            copy(i + 1).start()
        copy(i).wait()
        dst_ref[pl.ds(i * chunk, chunk), :] = stage_ref[i % 2, pl.ds(0, chunk), :].astype(BF16)


def _post_kernel(x_ref, four_ref, am_ref, al_ref, modc_ref, modp_ref, gpm_ref, gpf_ref, gqf_ref,
                 wo_hbm, wg_hbm, wu_hbm, wd_hbm, o_ref,
                 x1_ref, h2_ref, wo_ref, wg_ref, wu_ref, wd_ref, wide_stage, tall_stage, sem_ref):
    t = pl.program_id(0)
    n_blocks = pl.num_programs(0) - 1
    n_four = four_ref.shape[2]

    def mix():
        gt_m = modc_ref[0, 2:3, :]
        sh_f = modc_ref[0, 3:4, :]
        sc_f = modc_ref[0, 4:5, :]
        attn = jnp.concatenate([am_ref[0], al_ref[0]], axis=1)
        y = _dot(four_ref[0], wo_ref[:n_four, :]) + _dot(attn, wo_ref[n_four:, :])
        x1 = x_ref[0] + gt_m * _rms(y, gpm_ref[...])
        x1_ref[...] = x1
        h2_ref[...] = (_rms(x1, gpf_ref[...]) * (1.0 + sc_f) + sh_f).astype(BF16)

    def ffn():
        gt_f = modp_ref[0, 5:6, :]
        h2 = h2_ref[...]
        g = _dot(h2, wg_ref[...])
        up = _dot(h2, wu_ref[...])
        act = (g / (1.0 + jnp.exp(-g)) * up).astype(BF16)
        o_ref[0] = x1_ref[...] + gt_f * _rms(_dot(act, wd_ref[...]), gqf_ref[...])

    @pl.when(t == 0)
    def _():
        _stage_cast(wo_hbm, wo_ref, tall_stage, sem_ref, STAGE_ROWS_OUT)
        mix()
        _stage_cast(wg_hbm, wg_ref, wide_stage, sem_ref, STAGE_ROWS_WIDE)
        _stage_cast(wu_hbm, wu_ref, wide_stage, sem_ref, STAGE_ROWS_WIDE)
        _stage_cast(wd_hbm, wd_ref, tall_stage, sem_ref, STAGE_ROWS_TALL)

    @pl.when(jnp.logical_and(t > 0, t < n_blocks))
    def _():
        ffn()
        mix()

    @pl.when(t == n_blocks)
    def _():
        ffn()


def _post(x, four, attn_main, attn_last, mod3, gpm, gpf, gqf, wo, wg, wu, wd, tm=512):
    b, s, d = x.shape
    d_ff = wg.shape[1]
    assert wo.shape[0] % STAGE_ROWS_OUT == 0 and STAGE_ROWS_OUT <= STAGE_ROWS_TALL
    assert d % STAGE_ROWS_WIDE == 0 and d_ff % STAGE_ROWS_TALL == 0
    nj = s // tm
    n_blocks = b * nj
    cur = lambda t: jnp.minimum(t, n_blocks - 1)
    prev = lambda t: jnp.maximum(t - 1, 0)
    const = lambda shape: pl.BlockSpec(shape, lambda t: (0,) * len(shape), pipeline_mode=pl.Buffered(1))
    rows = lambda w: pl.BlockSpec((1, tm, w), lambda t: (cur(t) // nj, cur(t) % nj, 0))
    hbm = pl.BlockSpec(memory_space=pl.ANY)
    return pl.pallas_call(
        _post_kernel,
        grid=(n_blocks + 1,),
        in_specs=[
            rows(d), rows(FOURIER_WIDTH), rows(attn_main.shape[2]), rows(attn_last.shape[2]),
            pl.BlockSpec((1, 6, d), lambda t: (cur(t) // nj, 0, 0)),
            pl.BlockSpec((1, 6, d), lambda t: (prev(t) // nj, 0, 0)),
            const(gpm.shape), const(gpf.shape), const(gqf.shape),
            hbm, hbm, hbm, hbm,
        ],
        out_specs=pl.BlockSpec((1, tm, d), lambda t: (prev(t) // nj, prev(t) % nj, 0)),
        out_shape=jax.ShapeDtypeStruct((b, s, d), F32),
        scratch_shapes=[
            pltpu.VMEM((tm, d), F32), pltpu.VMEM((tm, d), BF16),
            pltpu.VMEM(wo.shape, BF16), pltpu.VMEM(wg.shape, BF16), pltpu.VMEM(wu.shape, BF16),
            pltpu.VMEM(wd.shape, BF16),
            pltpu.VMEM((2, STAGE_ROWS_WIDE, d_ff), F32), pltpu.VMEM((2, STAGE_ROWS_TALL, d), F32),
            pltpu.SemaphoreType.DMA((2,)),
        ],
        compiler_params=pltpu.CompilerParams(
            dimension_semantics=("arbitrary",), vmem_limit_bytes=VMEM_LIMIT_BYTES),
        name="post",
    )(x, four, attn_main, attn_last, mod3, mod3, gpm, gpf, gqf, wo, wg, wu, wd)


def _rope_rotate_cols(w):
    a = QK_ROPE_DIM // 2
    hf = a // 2
    blocks = []
    for s0 in (0, a):
        blocks += [-w[..., s0 + hf:s0 + a], w[..., s0:s0 + hf]]
    return jnp.concatenate(blocks, axis=-1)


def _head_slot(nope, rope):
    pad = HEAD_SLOT - QK_NOPE_DIM - QK_ROPE_DIM
    return jnp.concatenate([nope, rope, jnp.zeros(rope.shape[:-1] + (pad,), rope.dtype)], axis=-1)


def _rope_tables(n_lat, q_scale):
    t = np.arange(n_lat)
    hf = QK_ROPE_DIM // 4
    inv_freq = ROPE_BASE ** (-np.arange(hf, dtype=np.float64) / hf)
    ar = (t // GRID_W)[:, None] * inv_freq[None, :]
    ac = (t % GRID_W)[:, None] * inv_freq[None, :]
    z = np.zeros_like(ar)
    cos32 = np.concatenate([np.cos(ar), np.cos(ar), np.cos(ac), np.cos(ac)], axis=-1)
    sin32 = np.concatenate([np.sin(ar), np.sin(ar), np.sin(ac), np.sin(ac)], axis=-1)
    below32 = np.concatenate([z, np.sin(ar), z, np.sin(ac)], axis=-1)
    above32 = np.concatenate([-np.sin(ar), z, -np.sin(ac), z], axis=-1)
    pad = np.zeros((n_lat, HEAD_SLOT - QK_NOPE_DIM - QK_ROPE_DIM))
    ones = np.ones((n_lat, QK_NOPE_DIM))
    zeros = np.zeros((n_lat, QK_NOPE_DIM))
    slot = lambda nope, rope: np.concatenate([nope, rope, pad], axis=-1)
    tables = [slot(ones * q_scale, cos32 * q_scale), slot(zeros, sin32 * q_scale),
              slot(zeros, cos32), slot(zeros, below32), slot(zeros, above32)]
    return jnp.asarray(np.stack(tables).astype(np.float32))


def _dft_tables(n_pos, n_ch):
    n_sub = n_pos // FFT_RADIX
    m = np.arange(n_sub, dtype=np.int64)
    ang = 2.0 * np.pi * ((m[:, None] * m[None, :]) % n_sub) / n_sub
    tmat = np.concatenate([np.cos(ang), np.sin(ang)], axis=0).astype(np.float32)
    tw = np.zeros((FFT_RADIX - 1, 2, n_sub, FOURIER_GROUP_DIM), np.float32)
    for r in range(1, FFT_RADIX):
        a = 2.0 * np.pi * r * m / n_pos
        tw[r - 1, 0] = np.cos(a)[:, None]
        tw[r - 1, 1] = np.sin(a)[:, None]
    c = np.arange(n_ch, dtype=np.int64)
    angc = 2.0 * np.pi * ((c[:, None] * c[None, :]) % n_ch) / n_ch
    norm = 1.0 / np.sqrt(float(n_pos * n_ch))
    cmat = np.concatenate([np.cos(angc) * norm, -np.sin(angc) * norm], axis=0).astype(np.float32)
    return jnp.asarray(tmat), jnp.asarray(tw), jnp.asarray(cmat)


def kernel(x, c, ctx, c_ctx, w_ada, b_ada, g_pre_mix, g_post_mix, g_pre_ffn, g_post_ffn, w_in, g_q_a,
           w_q_b, g_kv_a, w_kv_b, w_fourier, w_out, w_gate, w_up, w_down):
    assert w_ada.shape[0] == 1, "single-layer block"
    batch, n_lat, d = x.shape

    mod_rows = -(-(batch + 1) // 8) * 8
    cc = jnp.concatenate([c, c_ctx[None, :], jnp.zeros((mod_rows - batch - 1, d), F32)], axis=0)
    mod = _adaln(cc, w_ada[0], b_ada[0][None, :])
    mod3 = mod.reshape(mod_rows, 6, d)

    w_in0 = w_in[0]
    w_kr = w_in0[:, ROPE_COL:]
    zeros_d = jnp.zeros((d, QK_NOPE_DIM), F32)
    kr_slot = _head_slot(zeros_d, w_kr)
    win = jnp.concatenate([w_in0[:, :ROPE_COL], kr_slot], axis=1).astype(BF16)
    win_c = jnp.concatenate([w_in0[:, KV_COL:ROPE_COL], kr_slot], axis=1).astype(BF16)

    wq3 = w_q_b[0].reshape(Q_LORA_RANK, MLA_HEADS, QK_NOPE_DIM + QK_ROPE_DIM)
    wq_nope, wq_rope = wq3[..., :QK_NOPE_DIM], wq3[..., QK_NOPE_DIM:]
    wq_a = _head_slot(wq_nope, wq_rope).reshape(Q_LORA_RANK, QK_WIDTH)
    wq_b = _head_slot(jnp.zeros_like(wq_nope), _rope_rotate_cols(wq_rope)).reshape(Q_LORA_RANK, QK_WIDTH)
    wq = jnp.concatenate([wq_a, wq_b], axis=1).astype(BF16)

    wkv3 = w_kv_b[0].reshape(KV_LORA_RANK, MLA_HEADS, QK_NOPE_DIM + V_HEAD_DIM)
    wk_nope, wv = wkv3[..., :QK_NOPE_DIM], wkv3[..., QK_NOPE_DIM:]
    wk_slots = _head_slot(wk_nope, jnp.zeros(wk_nope.shape[:-1] + (QK_ROPE_DIM,), F32))
    wv_slots = jnp.concatenate([wv, jnp.zeros(wv.shape[:-1] + (HEAD_SLOT - V_HEAD_DIM,), F32)], axis=-1)
    wkv = jnp.concatenate([wk_slots.reshape(KV_LORA_RANK, QK_WIDTH),
                           wv_slots.reshape(KV_LORA_RANK, QK_WIDTH)], axis=1).astype(BF16)

    q_scale = float((QK_NOPE_DIM + QK_ROPE_DIM) ** -0.5 * np.log2(np.e))
    rope = _rope_tables(n_lat, q_scale)
    tmat, tw, cmat = _dft_tables(n_lat, FOURIER_GROUP_DIM)

    row2 = lambda g: g[0][None, :]
    u_f, q, k_lat, v_lat = _premix(x, mod3, row2(g_pre_mix), win, row2(g_q_a), wq, row2(g_kv_a), wkv,
                                   rope)
    k_ctx, v_ctx = _ctxkv(ctx, mod3, batch, row2(g_pre_mix), win_c, row2(g_kv_a), wkv)
    attn_main, attn_last = _attention(q, k_ctx, k_lat, v_ctx, v_lat)
    four = _fourier(u_f, tmat, tw, cmat, w_fourier[0].astype(BF16))
    return _post(x, four, attn_main, attn_last, mod3, row2(g_post_mix), row2(g_pre_ffn), row2(g_post_ffn),
                 w_out[0], w_gate[0], w_up[0], w_down[0])
```

```python
import functools

import numpy as np
import jax
import jax.numpy as jnp
from jax import lax
from jax.experimental import pallas as pl
from jax.experimental.pallas import tpu as pltpu

F32 = jnp.float32
BF16 = jnp.bfloat16

D_MODEL = 1024
GRID_W = 64
FOURIER_GROUPS = 4
FOURIER_GROUP_DIM = 128
FOURIER_WIDTH = FOURIER_GROUPS * FOURIER_GROUP_DIM
MLA_HEADS = 8
QK_NOPE_DIM = 64
QK_ROPE_DIM = 32
V_HEAD_DIM = 64
Q_LORA_RANK = 256
KV_LORA_RANK = 128
KV_COL = FOURIER_WIDTH + Q_LORA_RANK
ROPE_COL = KV_COL + KV_LORA_RANK
ROPE_BASE = 10000.0
NORM_EPS = 1e-6
FFT_RADIX = 4
HEAD_SLOT = 128
STAGE_ROWS_WIDE = 128
STAGE_ROWS_TALL = 352
STAGE_ROWS_OUT = 256
SHIFT_DEN_MIN = 2.0 ** -40
SHIFT_DEN_MAX = 2.0 ** 40
PV_KEY_TILE = 256
ATTN_WIDTH = MLA_HEADS * V_HEAD_DIM
QK_WIDTH = MLA_HEADS * HEAD_SLOT

VMEM_LIMIT_BYTES = 56 * 1024 * 1024


def _rms(x, g):
    return x * lax.rsqrt(jnp.mean(x * x, axis=-1, keepdims=True) + NORM_EPS) * g


def _dot(a, b):
    return jnp.dot(a, b, preferred_element_type=F32)


def _dot_nt(a, b):
    return lax.dot_general(a, b, (((1,), (1,)), ((), ())), preferred_element_type=F32)


def _rotary(a, rope_ref, first):
    half = QK_ROPE_DIM // 4
    below = pltpu.roll(a, half, axis=1)
    above = pltpu.roll(a, HEAD_SLOT - half, axis=1)
    return a * rope_ref[first] + below * rope_ref[first + 1] + above * rope_ref[first + 2]


def _denominator_lanes():
    lane = lax.broadcasted_iota(jnp.int32, (1, QK_WIDTH), 1)
    return (lane % HEAD_SLOT == V_HEAD_DIM).astype(F32)


def _adaln_kernel(c_ref, w_ref, b_ref, o_ref):
    c = c_ref[...]
    a = c / (1.0 + jnp.exp(-c))
    o_ref[...] = _dot(a.astype(BF16), w_ref[...].astype(BF16)) + b_ref[...]


def _adaln(cc, w_ada, b_ada, tn=512):
    rows, d = cc.shape
    n = w_ada.shape[1]
    return pl.pallas_call(
        _adaln_kernel,
        grid=(n // tn,),
        in_specs=[
            pl.BlockSpec((rows, d), lambda j: (0, 0)),
            pl.BlockSpec((d, tn), lambda j: (0, j)),
            pl.BlockSpec((1, tn), lambda j: (0, j)),
        ],
        out_specs=pl.BlockSpec((rows, tn), lambda j: (0, j)),
        out_shape=jax.ShapeDtypeStruct((rows, n), F32),
        name="adaln",
    )(cc, w_ada, b_ada)


def _premix_kernel(x_ref, mod_ref, gpre_ref, win_ref, gq_ref, wq_ref, gkv_ref, wkv_ref,
                   rope_ref, u_ref, q_ref, k_ref, v_ref, us_ref):
    x = x_ref[0]
    shift = mod_ref[0, 0:1, :]
    scale = mod_ref[0, 1:2, :]
    h = _rms(x, gpre_ref[...]) * (1.0 + scale) + shift
    p = _dot(h.astype(BF16), win_ref[...])
    sub = us_ref.shape[1] // FFT_RADIX
    for g in range(FOURIER_GROUPS):
        glo = g * FOURIER_GROUP_DIM
        us_ref[g] = p[:, glo:glo + FOURIER_GROUP_DIM]
        for r in range(FFT_RADIX):
            lo = r * FOURIER_WIDTH + glo
            u_ref[0, :, lo:lo + FOURIER_GROUP_DIM] = (
                us_ref[g, pl.ds(r, sub, stride=FFT_RADIX), :].astype(BF16))

    qn = _rms(p[:, FOURIER_WIDTH:KV_COL], gq_ref[...]).astype(BF16)
    qq = _dot(qn, wq_ref[...])
    cosq = rope_ref[0]
    sinq = rope_ref[1]
    for hd in range(MLA_HEADS):
        lo = hd * HEAD_SLOT
        q_ref[0, :, lo:lo + HEAD_SLOT] = (
            qq[:, lo:lo + HEAD_SLOT] * cosq + qq[:, QK_WIDTH + lo:QK_WIDTH + lo + HEAD_SLOT] * sinq
        ).astype(BF16)

    kvn = _rms(p[:, KV_COL:ROPE_COL], gkv_ref[...]).astype(BF16)
    kv = _dot(kvn, wkv_ref[...])
    kr = _rotary(p[:, ROPE_COL:ROPE_COL + HEAD_SLOT], rope_ref, 2)
    for hd in range(MLA_HEADS):
        lo = hd * HEAD_SLOT
        k_ref[0, :, lo:lo + HEAD_SLOT] = (kv[:, lo:lo + HEAD_SLOT] + kr).astype(BF16)
    v_ref[0] = (kv[:, QK_WIDTH:] + _denominator_lanes()).astype(BF16)


def _premix(x, mod3, gpre, win, gq, wq, gkv, wkv, rope, tm=512):
    b, s, d = x.shape
    const = lambda shape: pl.BlockSpec(shape, lambda i, j: (0,) * len(shape))
    rows = lambda w: pl.BlockSpec((1, tm, w), lambda i, j: (i, j, 0))
    return pl.pallas_call(
        _premix_kernel,
        grid=(b, s // tm),
        in_specs=[
            rows(d),
            pl.BlockSpec((1, 6, d), lambda i, j: (i, 0, 0)),
            const(gpre.shape), const(win.shape), const(gq.shape), const(wq.shape),
            const(gkv.shape), const(wkv.shape),
            pl.BlockSpec((rope.shape[0], tm, HEAD_SLOT), lambda i, j: (0, j, 0)),
        ],
        out_specs=[
            pl.BlockSpec((1, tm // FFT_RADIX, FFT_RADIX * FOURIER_WIDTH), lambda i, j: (i, j, 0)),
            rows(QK_WIDTH), rows(QK_WIDTH), rows(QK_WIDTH)],
        out_shape=[
            jax.ShapeDtypeStruct((b, s // FFT_RADIX, FFT_RADIX * FOURIER_WIDTH), BF16),
            jax.ShapeDtypeStruct((b, s, QK_WIDTH), BF16),
            jax.ShapeDtypeStruct((b, s, QK_WIDTH), BF16),
            jax.ShapeDtypeStruct((b, s, QK_WIDTH), BF16),
        ],
        scratch_shapes=[pltpu.VMEM((FOURIER_GROUPS, tm, FOURIER_GROUP_DIM), F32)],
        compiler_params=pltpu.CompilerParams(vmem_limit_bytes=VMEM_LIMIT_BYTES),
        name="premix",
    )(x, mod3, gpre, win, gq, wq, gkv, wkv, rope)


def _ctxkv_kernel(x_ref, mod_ref, gpre_ref, win_ref, gkv_ref, wkv_ref, k_ref, v_ref):
    x = x_ref[0]
    shift = mod_ref[0, 0:1, :]
    scale = mod_ref[0, 1:2, :]
    h = _rms(x, gpre_ref[...]) * (1.0 + scale) + shift
    p = _dot(h.astype(BF16), win_ref[...])
    kvn = _rms(p[:, :KV_LORA_RANK], gkv_ref[...]).astype(BF16)
    kv = _dot(kvn, wkv_ref[...])
    kr = p[:, KV_LORA_RANK:]
    for hd in range(MLA_HEADS):
        lo = hd * HEAD_SLOT
        k_ref[0, :, lo:lo + HEAD_SLOT] = (kv[:, lo:lo + HEAD_SLOT] + kr).astype(BF16)
    v_ref[0] = (kv[:, QK_WIDTH:] + _denominator_lanes()).astype(BF16)


def _ctxkv(ctx, mod3, ctx_row, gpre, win_c, gkv, wkv):
    b, c, d = ctx.shape
    const = lambda shape: pl.BlockSpec(shape, lambda i: (0,) * len(shape))
    rows = lambda w: pl.BlockSpec((1, c, w), lambda i: (i, 0, 0))
    return pl.pallas_call(
        _ctxkv_kernel,
        grid=(b,),
        in_specs=[
            rows(d),
            pl.BlockSpec((1, 6, d), lambda i: (ctx_row, 0, 0)),
            const(gpre.shape), const(win_c.shape), const(gkv.shape), const(wkv.shape),
        ],
        out_specs=[rows(QK_WIDTH), rows(QK_WIDTH)],
        out_shape=[
            jax.ShapeDtypeStruct((b, c, QK_WIDTH), BF16),
            jax.ShapeDtypeStruct((b, c, QK_WIDTH), BF16),
        ],
        name="ctxkv",
    )(ctx, mod3, gpre, win_c, gkv, wkv)


def _attn_kernel(q_ref, kc_ref, kl_ref, vc_ref, vl_ref, vcp_ref, vlp_ref, omain_ref, olast_ref,
                 p_ref, oprev_ref, ohead_ref):
    t = pl.program_id(0)
    n_blocks = pl.num_programs(0) - 1
    n_ctx = kc_ref.shape[1]
    n_lat = kl_ref.shape[1]
    tq = q_ref.shape[1]
    pair_w = 2 * V_HEAD_DIM
    first_half = lax.broadcasted_iota(jnp.int32, (tq, pair_w), 1) < V_HEAD_DIM
    last = MLA_HEADS - 1

    @pl.when(t == 0)
    def _():
        p_ref[...] = jnp.ones_like(p_ref)
        oprev_ref[...] = jnp.zeros_like(oprev_ref)

    def drain():
        o = _dot(p_ref[:, :n_ctx], vcp_ref[0]) + _dot(p_ref[:, n_ctx:], vlp_ref[0])
        o = pltpu.roll(o / o[:, V_HEAD_DIM:V_HEAD_DIM + 1], V_HEAD_DIM, axis=1)
        olast_ref[0] = jnp.where(first_half, oprev_ref[...], o).astype(BF16)

    def head(hd, exact):
        lo = hd * HEAD_SLOT
        qh = q_ref[0, :, lo:lo + HEAD_SLOT]
        s_c = _dot_nt(qh, kc_ref[0, :, lo:lo + HEAD_SLOT])
        s_l = _dot_nt(qh, kl_ref[0, :, lo:lo + HEAD_SLOT])
        shift = jnp.max(s_c, axis=-1, keepdims=True)
        if exact:
            shift = jnp.maximum(shift, jnp.max(s_l, axis=-1, keepdims=True))
        if hd == last:
            p_c = jnp.exp2(s_c - shift)
            p_l = jnp.exp2(s_l - shift)
            p_ref[:, :n_ctx] = p_c.astype(BF16)
            p_ref[:, n_ctx:] = p_l.astype(BF16)
            return None, jnp.sum(p_c, axis=-1, keepdims=True) + jnp.sum(p_l, axis=-1, keepdims=True)
        o = _dot(jnp.exp2(s_c - shift).astype(BF16), vc_ref[0, :, lo:lo + HEAD_SLOT])
        for k0 in range(0, n_lat, PV_KEY_TILE):
            p = jnp.exp2(s_l[:, k0:k0 + PV_KEY_TILE] - shift).astype(BF16)
            o = o + _dot(p, vl_ref[0, k0:k0 + PV_KEY_TILE, lo:lo + HEAD_SLOT])
        den = o[:, V_HEAD_DIM:V_HEAD_DIM + 1]
        return o / den, den

    def block():
        dens = []
        for pair in range(MLA_HEADS // 2):
            outs = []
            for hd in (2 * pair, 2 * pair + 1):
                o, den = head(hd, False)
                dens.append(den)
                if o is not None:
                    outs.append(o)
            if len(outs) == 2:
                vlo = pair * pair_w
                omain_ref[0, :, vlo:vlo + pair_w] = jnp.where(
                    first_half, outs[0], pltpu.roll(outs[1], V_HEAD_DIM, axis=1)).astype(BF16)
            else:
                oprev_ref[...] = outs[0]
        return dens

    def redo_block_exact():
        def one_head(hd, carry):
            lo = pl.multiple_of(hd * HEAD_SLOT, HEAD_SLOT)
            qh = q_ref[0, :, pl.ds(lo, HEAD_SLOT)]
            s_c = _dot_nt(qh, kc_ref[0, :, pl.ds(lo, HEAD_SLOT)])
            s_l = _dot_nt(qh, kl_ref[0, :, pl.ds(lo, HEAD_SLOT)])
            shift = jnp.maximum(jnp.max(s_c, axis=-1, keepdims=True), jnp.max(s_l, axis=-1, keepdims=True))
            o = (_dot(jnp.exp2(s_c - shift).astype(BF16), vc_ref[0, :, pl.ds(lo, HEAD_SLOT)])
                 + _dot(jnp.exp2(s_l - shift).astype(BF16), vl_ref[0, :, pl.ds(lo, HEAD_SLOT)]))
            ohead_ref[hd] = o / o[:, V_HEAD_DIM:V_HEAD_DIM + 1]
            return carry

        lax.fori_loop(0, last, one_head, 0)
        for pair in range(MLA_HEADS // 2 - 1):
            vlo = pair * pair_w
            omain_ref[0, :, vlo:vlo + pair_w] = jnp.where(
                first_half, ohead_ref[2 * pair], pltpu.roll(ohead_ref[2 * pair + 1], V_HEAD_DIM, axis=1)
            ).astype(BF16)
        oprev_ref[...] = ohead_ref[last - 1]
        head(last, True)

    @pl.when(t < n_blocks)
    def _():
        drain()
        dens = block()
        lo_den, hi_den = dens[0], dens[0]
        for den in dens[1:]:
            lo_den = jnp.minimum(lo_den, den)
            hi_den = jnp.maximum(hi_den, den)
        trusted = jnp.logical_and(jnp.min(lo_den) >= SHIFT_DEN_MIN, jnp.max(hi_den) <= SHIFT_DEN_MAX)

        @pl.when(jnp.logical_not(trusted))
        def _():
            redo_block_exact()

    @pl.when(t == n_blocks)
    def _():
        drain()


def _attention(q, kc, kl, vc, vl, tq=512):
    b, s, _ = q.shape
    c = kc.shape[1]
    nq = s // tq
    n_blocks = b * nq
    pair_w = 2 * V_HEAD_DIM
    main_w = ATTN_WIDTH - pair_w
    last = MLA_HEADS - 1
    cur = lambda t: jnp.minimum(t, n_blocks - 1)
    prev = lambda t: jnp.maximum(t - 1, 0)
    return pl.pallas_call(
        _attn_kernel,
        grid=(n_blocks + 1,),
        in_specs=[
            pl.BlockSpec((1, tq, QK_WIDTH), lambda t: (cur(t) // nq, cur(t) % nq, 0)),
            pl.BlockSpec((1, c, QK_WIDTH), lambda t: (cur(t) // nq, 0, 0)),
            pl.BlockSpec((1, s, QK_WIDTH), lambda t: (cur(t) // nq, 0, 0)),
            pl.BlockSpec((1, c, QK_WIDTH), lambda t: (cur(t) // nq, 0, 0)),
            pl.BlockSpec((1, s, QK_WIDTH), lambda t: (cur(t) // nq, 0, 0)),
            pl.BlockSpec((1, c, HEAD_SLOT), lambda t: (prev(t) // nq, 0, last)),
            pl.BlockSpec((1, s, HEAD_SLOT), lambda t: (prev(t) // nq, 0, last)),
        ],
        out_specs=[
            pl.BlockSpec((1, tq, main_w), lambda t: (cur(t) // nq, cur(t) % nq, 0)),
            pl.BlockSpec((1, tq, pair_w), lambda t: (prev(t) // nq, prev(t) % nq, 0)),
        ],
        out_shape=[
            jax.ShapeDtypeStruct((b, s, main_w), BF16),
            jax.ShapeDtypeStruct((b, s, pair_w), BF16),
        ],
        scratch_shapes=[
            pltpu.VMEM((tq, c + s), BF16),
            pltpu.VMEM((tq, pair_w), F32),
            pltpu.VMEM((MLA_HEADS - 1, tq, HEAD_SLOT), F32),
        ],
        compiler_params=pltpu.CompilerParams(
            dimension_semantics=("arbitrary",), vmem_limit_bytes=VMEM_LIMIT_BYTES),
        name="attn",
    )(q, kc, kl, vc, vl, vc, vl)


def _fourier_kernel(t_ref, tw_ref, u_ref, cc_ref, wf_ref, o_ref, tb_ref, cw_ref):
    @pl.when(pl.program_id(0) == 0)
    def _():
        tb_ref[...] = t_ref[...].astype(BF16)
        cc = cc_ref[...].astype(BF16)
        for g in range(FOURIER_GROUPS):
            cw_ref[g] = _dot(cc, wf_ref[g]).astype(BF16)

    n_sub = tb_ref.shape[1]
    f = _dot(tb_ref[...], u_ref[0])
    gd = FOURIER_GROUP_DIM
    for g in range(FOURIER_GROUPS):
        gr, gi = [], []
        for r in range(FFT_RADIX):
            lo = r * FOURIER_WIDTH + g * gd
            a = f[:n_sub, lo:lo + gd]
            b = f[n_sub:, lo:lo + gd]
            if r == 0:
                gr.append(a)
                gi.append(b)
            else:
                c = tw_ref[r - 1, 0]
                s = tw_ref[r - 1, 1]
                gr.append(a * c - b * s)
                gi.append(a * s + b * c)
        ar, ai = gr[0] + gr[2], gi[0] + gi[2]
        br, bi = gr[0] - gr[2], gi[0] - gi[2]
        cr, ci = gr[1] + gr[3], gi[1] + gi[3]
        dr, di = gr[1] - gr[3], gi[1] - gi[3]
        xr = jnp.concatenate([ar + cr, br - di, ar - cr, br + di], axis=0)
        xi = jnp.concatenate([ai + ci, bi + dr, ai - ci, bi - dr], axis=0)
        lhs = jnp.concatenate([xr, xi], axis=1).astype(BF16)
        o_ref[0, :, g * gd:(g + 1) * gd] = _dot(lhs, cw_ref[g]).astype(BF16)


def _fourier(u4, tmat, tw, cmat, wf):
    b, n_sub, _ = u4.shape
    s = n_sub * FFT_RADIX
    full = lambda a: pl.BlockSpec(a.shape, lambda i: (0,) * a.ndim)
    return pl.pallas_call(
        _fourier_kernel,
        grid=(b,),
        in_specs=[
            full(tmat), full(tw),
            pl.BlockSpec((1, n_sub, FFT_RADIX * FOURIER_WIDTH), lambda i: (i, 0, 0)),
            full(cmat), full(wf),
        ],
        out_specs=pl.BlockSpec((1, s, FOURIER_WIDTH), lambda i: (i, 0, 0)),
        out_shape=jax.ShapeDtypeStruct((b, s, FOURIER_WIDTH), BF16),
        scratch_shapes=[
            pltpu.VMEM(tmat.shape, BF16),
            pltpu.VMEM((FOURIER_GROUPS, 2 * FOURIER_GROUP_DIM, FOURIER_GROUP_DIM), BF16),
        ],
        compiler_params=pltpu.CompilerParams(
            dimension_semantics=("arbitrary",), vmem_limit_bytes=VMEM_LIMIT_BYTES),
        name="fourier",
    )(tmat, tw, u4, cmat, wf)


def _stage_cast(src_hbm, dst_ref, stage_ref, sem_ref, chunk):
    n_chunks = src_hbm.shape[0] // chunk

    def copy(i):
        slot = i % 2
        return pltpu.make_async_copy(
            src_hbm.at[pl.ds(i * chunk, chunk)], stage_ref.at[slot, pl.ds(0, chunk)], sem_ref.at[slot])

    copy(0).start()
    for i in range(n_chunks):
        if i + 1 < n_chunks:
            copy(i + 1).start()
        copy(i).wait()
        dst_ref[pl.ds(i * chunk, chunk), :] = stage_ref[i % 2, pl.ds(0, chunk), :].astype(BF16)


def _post_kernel(x_ref, four_ref, am_ref, al_ref, modc_ref, modp_ref, gpm_ref, gpf_ref, gqf_ref,
                 wo_hbm, wg_hbm, wu_hbm, wd_hbm, o_ref,
                 x1_ref, h2_ref, wo_ref, wg_ref, wu_ref, wd_ref, wide_stage, tall_stage, sem_ref):
    t = pl.program_id(0)
    n_four = four_ref.shape[2]

    def mix():
        gt_m = modc_ref[0, 2:3, :]
        sh_f = modc_ref[0, 3:4, :]
        sc_f = modc_ref[0, 4:5, :]
        attn = jnp.concatenate([am_ref[0], al_ref[0]], axis=1)
        y = _dot(four_ref[0], wo_ref[:n_four, :]) + _dot(attn, wo_ref[n_four:, :])
        x1 = x_ref[0] + gt_m * _rms(y, gpm_ref[...])
        x1_ref[...] = x1
        h2_ref[...] = (_rms(x1, gpf_ref[...]) * (1.0 + sc_f) + sh_f).astype(BF16)

    def ffn():
        gt_f = modp_ref[0, 5:6, :]
        h2 = h2_ref[...]
        g = _dot(h2, wg_ref[...])
        up = _dot(h2, wu_ref[...])
        act = (g / (1.0 + jnp.exp(-g)) * up).astype(BF16)
        o_ref[0] = x1_ref[...] + gt_f * _rms(_dot(act, wd_ref[...]), gqf_ref[...])

    @pl.when(t == 0)
    def _():
        _stage_cast(wo_hbm, wo_ref, tall_stage, sem_ref, STAGE_ROWS_OUT)
        mix()
        _stage_cast(wg_hbm, wg_ref, wide_stage, sem_ref, STAGE_ROWS_WIDE)
        _stage_cast(wu_hbm, wu_ref, wide_stage, sem_ref, STAGE_ROWS_WIDE)
        _stage_cast(wd_hbm, wd_ref, tall_stage, sem_ref, STAGE_ROWS_TALL)

    @pl.when(t > 0)
    def _():
        ffn()
        mix()


def _post(x, four, attn_main, attn_last, mod3, gpm, gpf, gqf, wo, wg, wu, wd, tm=512):
    b, s, d = x.shape
    d_ff = wg.shape[1]
    assert wo.shape[0] % STAGE_ROWS_OUT == 0 and STAGE_ROWS_OUT <= STAGE_ROWS_TALL
    assert d % STAGE_ROWS_WIDE == 0 and d_ff % STAGE_ROWS_TALL == 0
    nj = s // tm
    n_blocks = b * nj
    cur = lambda t: jnp.minimum(t, n_blocks - 1)
    prev = lambda t: jnp.maximum(t - 1, 0)
    const = lambda shape: pl.BlockSpec(shape, lambda t: (0,) * len(shape), pipeline_mode=pl.Buffered(1))
    rows = lambda w: pl.BlockSpec((1, tm, w), lambda t: (cur(t) // nj, cur(t) % nj, 0))
    hbm = pl.BlockSpec(memory_space=pl.ANY)
    return pl.pallas_call(
        _post_kernel,
        grid=(n_blocks + 1,),
        in_specs=[
            rows(d), rows(FOURIER_WIDTH), rows(attn_main.shape[2]), rows(attn_last.shape[2]),
            pl.BlockSpec((1, 6, d), lambda t: (cur(t) // nj, 0, 0)),
            pl.BlockSpec((1, 6, d), lambda t: (prev(t) // nj, 0, 0)),
            const(gpm.shape), const(gpf.shape), const(gqf.shape),
            hbm, hbm, hbm, hbm,
        ],
        out_specs=pl.BlockSpec((1, tm, d), lambda t: (prev(t) // nj, prev(t) % nj, 0)),
        out_shape=jax.ShapeDtypeStruct((b, s, d), F32),
        scratch_shapes=[
            pltpu.VMEM((tm, d), F32), pltpu.VMEM((tm, d), BF16),
            pltpu.VMEM(wo.shape, BF16), pltpu.VMEM(wg.shape, BF16), pltpu.VMEM(wu.shape, BF16),
            pltpu.VMEM(wd.shape, BF16),
            pltpu.VMEM((2, STAGE_ROWS_WIDE, d_ff), F32), pltpu.VMEM((2, STAGE_ROWS_TALL, d), F32),
            pltpu.SemaphoreType.DMA((2,)),
        ],
        compiler_params=pltpu.CompilerParams(
            dimension_semantics=("arbitrary",), vmem_limit_bytes=VMEM_LIMIT_BYTES),
        name="post",
    )(x, four, attn_main, attn_last, mod3, mod3, gpm, gpf, gqf, wo, wg, wu, wd)


def _rope_rotate_cols(w):
    a = QK_ROPE_DIM // 2
    hf = a // 2
    blocks = []
    for s0 in (0, a):
        blocks += [-w[..., s0 + hf:s0 + a], w[..., s0:s0 + hf]]
    return jnp.concatenate(blocks, axis=-1)


def _head_slot(nope, rope):
    pad = HEAD_SLOT - QK_NOPE_DIM - QK_ROPE_DIM
    return jnp.concatenate([nope, rope, jnp.zeros(rope.shape[:-1] + (pad,), rope.dtype)], axis=-1)


def _rope_tables(n_lat, q_scale):
    t = np.arange(n_lat)
    hf = QK_ROPE_DIM // 4
    inv_freq = ROPE_BASE ** (-np.arange(hf, dtype=np.float64) / hf)
    ar = (t // GRID_W)[:, None] * inv_freq[None, :]
    ac = (t % GRID_W)[:, None] * inv_freq[None, :]
    z = np.zeros_like(ar)
    cos32 = np.concatenate([np.cos(ar), np.cos(ar), np.cos(ac), np.cos(ac)], axis=-1)
    sin32 = np.concatenate([np.sin(ar), np.sin(ar), np.sin(ac), np.sin(ac)], axis=-1)
    below32 = np.concatenate([z, np.sin(ar), z, np.sin(ac)], axis=-1)
    above32 = np.concatenate([-np.sin(ar), z, -np.sin(ac), z], axis=-1)
    pad = np.zeros((n_lat, HEAD_SLOT - QK_NOPE_DIM - QK_ROPE_DIM))
    ones = np.ones((n_lat, QK_NOPE_DIM))
    zeros = np.zeros((n_lat, QK_NOPE_DIM))
    slot = lambda nope, rope: np.concatenate([nope, rope, pad], axis=-1)
    tables = [slot(ones * q_scale, cos32 * q_scale), slot(zeros, sin32 * q_scale),
              slot(zeros, cos32), slot(zeros, below32), slot(zeros, above32)]
    return jnp.asarray(np.stack(tables).astype(np.float32))


def _dft_tables(n_pos, n_ch):
    n_sub = n_pos // FFT_RADIX
    m = np.arange(n_sub, dtype=np.int64)
    ang = 2.0 * np.pi * ((m[:, None] * m[None, :]) % n_sub) / n_sub
    tmat = np.concatenate([np.cos(ang), np.sin(ang)], axis=0).astype(np.float32)
    tw = np.zeros((FFT_RADIX - 1, 2, n_sub, FOURIER_GROUP_DIM), np.float32)
    for r in range(1, FFT_RADIX):
        a = 2.0 * np.pi * r * m / n_pos
        tw[r - 1, 0] = np.cos(a)[:, None]
        tw[r - 1, 1] = np.sin(a)[:, None]
    c = np.arange(n_ch, dtype=np.int64)
    angc = 2.0 * np.pi * ((c[:, None] * c[None, :]) % n_ch) / n_ch
    norm = 1.0 / np.sqrt(float(n_pos * n_ch))
    cmat = np.concatenate([np.cos(angc) * norm, -np.sin(angc) * norm], axis=0).astype(np.float32)
    return jnp.asarray(tmat), jnp.asarray(tw), jnp.asarray(cmat)


def kernel(x, c, ctx, c_ctx, w_ada, b_ada, g_pre_mix, g_post_mix, g_pre_ffn, g_post_ffn, w_in, g_q_a,
           w_q_b, g_kv_a, w_kv_b, w_fourier, w_out, w_gate, w_up, w_down):
    assert w_ada.shape[0] == 1, "single-layer block"
    batch, n_lat, d = x.shape

    mod_rows = -(-(batch + 1) // 8) * 8
    cc = jnp.concatenate([c, c_ctx[None, :], jnp.zeros((mod_rows - batch - 1, d), F32)], axis=0)
    mod = _adaln(cc, w_ada[0], b_ada[0][None, :])
    mod3 = mod.reshape(mod_rows, 6, d)

    w_in0 = w_in[0]
    w_kr = w_in0[:, ROPE_COL:]
    zeros_d = jnp.zeros((d, QK_NOPE_DIM), F32)
    kr_slot = _head_slot(zeros_d, w_kr)
    win = jnp.concatenate([w_in0[:, :ROPE_COL], kr_slot], axis=1).astype(BF16)
    win_c = jnp.concatenate([w_in0[:, KV_COL:ROPE_COL], kr_slot], axis=1).astype(BF16)

    wq3 = w_q_b[0].reshape(Q_LORA_RANK, MLA_HEADS, QK_NOPE_DIM + QK_ROPE_DIM)
    wq_nope, wq_rope = wq3[..., :QK_NOPE_DIM], wq3[..., QK_NOPE_DIM:]
    wq_a = _head_slot(wq_nope, wq_rope).reshape(Q_LORA_RANK, QK_WIDTH)
    wq_b = _head_slot(jnp.zeros_like(wq_nope), _rope_rotate_cols(wq_rope)).reshape(Q_LORA_RANK, QK_WIDTH)
    wq = jnp.concatenate([wq_a, wq_b], axis=1).astype(BF16)

    wkv3 = w_kv_b[0].reshape(KV_LORA_RANK, MLA_HEADS, QK_NOPE_DIM + V_HEAD_DIM)
    wk_nope, wv = wkv3[..., :QK_NOPE_DIM], wkv3[..., QK_NOPE_DIM:]
    wk_slots = _head_slot(wk_nope, jnp.zeros(wk_nope.shape[:-1] + (QK_ROPE_DIM,), F32))
    wv_slots = jnp.concatenate([wv, jnp.zeros(wv.shape[:-1] + (HEAD_SLOT - V_HEAD_DIM,), F32)], axis=-1)
    wkv = jnp.concatenate([wk_slots.reshape(KV_LORA_RANK, QK_WIDTH),
                           wv_slots.reshape(KV_LORA_RANK, QK_WIDTH)], axis=1).astype(BF16)

    q_scale = float((QK_NOPE_DIM + QK_ROPE_DIM) ** -0.5 * np.log2(np.e))
    rope = _rope_tables(n_lat, q_scale)
    tmat, tw, cmat = _dft_tables(n_lat, FOURIER_GROUP_DIM)

    row2 = lambda g: g[0][None, :]
    u_f, q, k_lat, v_lat = _premix(x, mod3, row2(g_pre_mix), win, row2(g_q_a), wq, row2(g_kv_a), wkv,
                                   rope)
    k_ctx, v_ctx = _ctxkv(ctx, mod3, batch, row2(g_pre_mix), win_c, row2(g_kv_a), wkv)
    attn_main, attn_last = _attention(q, k_ctx, k_lat, v_ctx, v_lat)
    four = _fourier(u_f, tmat, tw, cmat, w_fourier[0].astype(BF16))
    return _post(x, four, attn_main, attn_last, mod3, row2(g_post_mix), row2(g_pre_ffn), row2(g_post_ffn),
                 w_out[0], w_gate[0], w_up[0], w_down[0])
```

```python
import functools

import numpy as np
import jax
import jax.numpy as jnp
from jax import lax
from jax.experimental import pallas as pl
from jax.experimental.pallas import tpu as pltpu

F32 = jnp.float32
BF16 = jnp.bfloat16

D_MODEL = 1024
GRID_W = 64
FOURIER_GROUPS = 4
FOURIER_GROUP_DIM = 128
FOURIER_WIDTH = FOURIER_GROUPS * FOURIER_GROUP_DIM
MLA_HEADS = 8
QK_NOPE_DIM = 64
QK_ROPE_DIM = 32
V_HEAD_DIM = 64
Q_LORA_RANK = 256
KV_LORA_RANK = 128
KV_COL = FOURIER_WIDTH + Q_LORA_RANK
ROPE_COL = KV_COL + KV_LORA_RANK
ROPE_BASE = 10000.0
NORM_EPS = 1e-6
FFT_RADIX = 4
HEAD_SLOT = 128
STAGE_ROWS_WIDE = 128
STAGE_ROWS_TALL = 352
STAGE_ROWS_OUT = 256
SHIFT_DEN_MIN = 2.0 ** -40
SHIFT_DEN_MAX = 2.0 ** 40
PV_KEY_TILE = 256
ATTN_WIDTH = MLA_HEADS * V_HEAD_DIM
QK_WIDTH = MLA_HEADS * HEAD_SLOT

VMEM_LIMIT_BYTES = 56 * 1024 * 1024


def _rms(x, g):
    return x * lax.rsqrt(jnp.mean(x * x, axis=-1, keepdims=True) + NORM_EPS) * g


def _dot(a, b):
    return jnp.dot(a, b, preferred_element_type=F32)


def _dot_nt(a, b):
    return lax.dot_general(a, b, (((1,), (1,)), ((), ())), preferred_element_type=F32)


def _rotary(a, rope_ref, first):
    half = QK_ROPE_DIM // 4
    below = pltpu.roll(a, half, axis=1)
    above = pltpu.roll(a, HEAD_SLOT - half, axis=1)
    return a * rope_ref[first] + below * rope_ref[first + 1] + above * rope_ref[first + 2]


def _denominator_lanes():
    lane = lax.broadcasted_iota(jnp.int32, (1, QK_WIDTH), 1)
    return (lane % HEAD_SLOT == V_HEAD_DIM).astype(F32)


def _adaln_kernel(c_ref, w_ref, b_ref, o_ref):
    c = c_ref[...]
    a = c / (1.0 + jnp.exp(-c))
    o_ref[...] = _dot(a.astype(BF16), w_ref[...].astype(BF16)) + b_ref[...]


def _adaln(cc, w_ada, b_ada, tn=1024):
    rows, d = cc.shape
    n = w_ada.shape[1]
    return pl.pallas_call(
        _adaln_kernel,
        grid=(n // tn,),
        in_specs=[
            pl.BlockSpec((rows, d), lambda j: (0, 0)),
            pl.BlockSpec((d, tn), lambda j: (0, j)),
            pl.BlockSpec((1, tn), lambda j: (0, j)),
        ],
        out_specs=pl.BlockSpec((rows, tn), lambda j: (0, j)),
        out_shape=jax.ShapeDtypeStruct((rows, n), F32),
        name="adaln",
    )(cc, w_ada, b_ada)


def _premix_kernel(x_ref, mod_ref, gpre_ref, win_ref, gq_ref, wq_ref, gkv_ref, wkv_ref,
                   rope_ref, u_ref, q_ref, k_ref, v_ref, us_ref):
    x = x_ref[0]
    shift = mod_ref[0, 0:1, :]
    scale = mod_ref[0, 1:2, :]
    h = _rms(x, gpre_ref[...]) * (1.0 + scale) + shift
    p = _dot(h.astype(BF16), win_ref[...])
    sub = us_ref.shape[1] // FFT_RADIX
    for g in range(FOURIER_GROUPS):
        glo = g * FOURIER_GROUP_DIM
        us_ref[g] = p[:, glo:glo + FOURIER_GROUP_DIM]
        for r in range(FFT_RADIX):
            lo = r * FOURIER_WIDTH + glo
            u_ref[0, :, lo:lo + FOURIER_GROUP_DIM] = (
                us_ref[g, pl.ds(r, sub, stride=FFT_RADIX), :].astype(BF16))

    qn = _rms(p[:, FOURIER_WIDTH:KV_COL], gq_ref[...]).astype(BF16)
    qq = _dot(qn, wq_ref[...])
    cosq = rope_ref[0]
    sinq = rope_ref[1]
    for hd in range(MLA_HEADS):
        lo = hd * HEAD_SLOT
        q_ref[0, :, lo:lo + HEAD_SLOT] = (
            qq[:, lo:lo + HEAD_SLOT] * cosq + qq[:, QK_WIDTH + lo:QK_WIDTH + lo + HEAD_SLOT] * sinq
        ).astype(BF16)

    kvn = _rms(p[:, KV_COL:ROPE_COL], gkv_ref[...]).astype(BF16)
    kv = _dot(kvn, wkv_ref[...])
    kr = _rotary(p[:, ROPE_COL:ROPE_COL + HEAD_SLOT], rope_ref, 2)
    for hd in range(MLA_HEADS):
        lo = hd * HEAD_SLOT
        k_ref[0, :, lo:lo + HEAD_SLOT] = (kv[:, lo:lo + HEAD_SLOT] + kr).astype(BF16)
    v_ref[0] = (kv[:, QK_WIDTH:] + _denominator_lanes()).astype(BF16)


def _premix(x, mod3, gpre, win, gq, wq, gkv, wkv, rope, tm=512):
    b, s, d = x.shape
    const = lambda shape: pl.BlockSpec(shape, lambda i, j: (0,) * len(shape))
    rows = lambda w: pl.BlockSpec((1, tm, w), lambda i, j: (i, j, 0))
    return pl.pallas_call(
        _premix_kernel,
        grid=(b, s // tm),
        in_specs=[
            rows(d),
            pl.BlockSpec((1, 6, d), lambda i, j: (i, 0, 0)),
            const(gpre.shape), const(win.shape), const(gq.shape), const(wq.shape),
            const(gkv.shape), const(wkv.shape),
            pl.BlockSpec((rope.shape[0], tm, HEAD_SLOT), lambda i, j: (0, j, 0)),
        ],
        out_specs=[
            pl.BlockSpec((1, tm // FFT_RADIX, FFT_RADIX * FOURIER_WIDTH), lambda i, j: (i, j, 0)),
            rows(QK_WIDTH), rows(QK_WIDTH), rows(QK_WIDTH)],
        out_shape=[
            jax.ShapeDtypeStruct((b, s // FFT_RADIX, FFT_RADIX * FOURIER_WIDTH), BF16),
            jax.ShapeDtypeStruct((b, s, QK_WIDTH), BF16),
            jax.ShapeDtypeStruct((b, s, QK_WIDTH), BF16),
            jax.ShapeDtypeStruct((b, s, QK_WIDTH), BF16),
        ],
        scratch_shapes=[pltpu.VMEM((FOURIER_GROUPS, tm, FOURIER_GROUP_DIM), F32)],
        compiler_params=pltpu.CompilerParams(vmem_limit_bytes=VMEM_LIMIT_BYTES),
        name="premix",
    )(x, mod3, gpre, win, gq, wq, gkv, wkv, rope)


def _ctxkv_kernel(x_ref, mod_ref, gpre_ref, win_ref, gkv_ref, wkv_ref, k_ref, v_ref):
    x = x_ref[0]
    shift = mod_ref[0, 0:1, :]
    scale = mod_ref[0, 1:2, :]
    h = _rms(x, gpre_ref[...]) * (1.0 + scale) + shift
    p = _dot(h.astype(BF16), win_ref[...])
    kvn = _rms(p[:, :KV_LORA_RANK], gkv_ref[...]).astype(BF16)
    kv = _dot(kvn, wkv_ref[...])
    kr = p[:, KV_LORA_RANK:]
    for hd in range(MLA_HEADS):
        lo = hd * HEAD_SLOT
        k_ref[0, :, lo:lo + HEAD_SLOT] = (kv[:, lo:lo + HEAD_SLOT] + kr).astype(BF16)
    v_ref[0] = (kv[:, QK_WIDTH:] + _denominator_lanes()).astype(BF16)


def _ctxkv(ctx, mod3, ctx_row, gpre, win_c, gkv, wkv):
    b, c, d = ctx.shape
    const = lambda shape: pl.BlockSpec(shape, lambda i: (0,) * len(shape))
    rows = lambda w: pl.BlockSpec((1, c, w), lambda i: (i, 0, 0))
    return pl.pallas_call(
        _ctxkv_kernel,
        grid=(b,),
        in_specs=[
            rows(d),
            pl.BlockSpec((1, 6, d), lambda i: (ctx_row, 0, 0)),
            const(gpre.shape), const(win_c.shape), const(gkv.shape), const(wkv.shape),
        ],
        out_specs=[rows(QK_WIDTH), rows(QK_WIDTH)],
        out_shape=[
            jax.ShapeDtypeStruct((b, c, QK_WIDTH), BF16),
            jax.ShapeDtypeStruct((b, c, QK_WIDTH), BF16),
        ],
        name="ctxkv",
    )(ctx, mod3, gpre, win_c, gkv, wkv)


def _attn_kernel(q_ref, kc_ref, kl_ref, vc_ref, vl_ref, vcp_ref, vlp_ref, omain_ref, olast_ref,
                 p_ref, oprev_ref, ohead_ref):
    t = pl.program_id(0)
    n_blocks = pl.num_programs(0) - 1
    n_ctx = kc_ref.shape[1]
    n_lat = kl_ref.shape[1]
    tq = q_ref.shape[1]
    pair_w = 2 * V_HEAD_DIM
    first_half = lax.broadcasted_iota(jnp.int32, (tq, pair_w), 1) < V_HEAD_DIM
    last = MLA_HEADS - 1

    @pl.when(t == 0)
    def _():
        p_ref[...] = jnp.ones_like(p_ref)
        oprev_ref[...] = jnp.zeros_like(oprev_ref)

    def drain():
        o = _dot(p_ref[:, :n_ctx], vcp_ref[0]) + _dot(p_ref[:, n_ctx:], vlp_ref[0])
        o = pltpu.roll(o / o[:, V_HEAD_DIM:V_HEAD_DIM + 1], V_HEAD_DIM, axis=1)
        olast_ref[0] = jnp.where(first_half, oprev_ref[...], o).astype(BF16)

    def head(hd, exact):
        lo = hd * HEAD_SLOT
        qh = q_ref[0, :, lo:lo + HEAD_SLOT]
        s_c = _dot_nt(qh, kc_ref[0, :, lo:lo + HEAD_SLOT])
        s_l = _dot_nt(qh, kl_ref[0, :, lo:lo + HEAD_SLOT])
        shift = jnp.max(s_c, axis=-1, keepdims=True)
        if exact:
            shift = jnp.maximum(shift, jnp.max(s_l, axis=-1, keepdims=True))
        if hd == last:
            p_c = jnp.exp2(s_c - shift)
            p_l = jnp.exp2(s_l - shift)
            p_ref[:, :n_ctx] = p_c.astype(BF16)
            p_ref[:, n_ctx:] = p_l.astype(BF16)
            return None, jnp.sum(p_c, axis=-1, keepdims=True) + jnp.sum(p_l, axis=-1, keepdims=True)
        o = _dot(jnp.exp2(s_c - shift).astype(BF16), vc_ref[0, :, lo:lo + HEAD_SLOT])
        for k0 in range(0, n_lat, PV_KEY_TILE):
            p = jnp.exp2(s_l[:, k0:k0 + PV_KEY_TILE] - shift).astype(BF16)
            o = o + _dot(p, vl_ref[0, k0:k0 + PV_KEY_TILE, lo:lo + HEAD_SLOT])
        den = o[:, V_HEAD_DIM:V_HEAD_DIM + 1]
        return o / den, den

    def block():
        dens = []
        for pair in range(MLA_HEADS // 2):
            outs = []
            for hd in (2 * pair, 2 * pair + 1):
                o, den = head(hd, False)
                dens.append(den)
                if o is not None:
                    outs.append(o)
            if len(outs) == 2:
                vlo = pair * pair_w
                omain_ref[0, :, vlo:vlo + pair_w] = jnp.where(
                    first_half, outs[0], pltpu.roll(outs[1], V_HEAD_DIM, axis=1)).astype(BF16)
            else:
                oprev_ref[...] = outs[0]
        return dens

    def redo_block_exact():
        def one_head(hd, carry):
            lo = pl.multiple_of(hd * HEAD_SLOT, HEAD_SLOT)
            qh = q_ref[0, :, pl.ds(lo, HEAD_SLOT)]
            s_c = _dot_nt(qh, kc_ref[0, :, pl.ds(lo, HEAD_SLOT)])
            s_l = _dot_nt(qh, kl_ref[0, :, pl.ds(lo, HEAD_SLOT)])
            shift = jnp.maximum(jnp.max(s_c, axis=-1, keepdims=True), jnp.max(s_l, axis=-1, keepdims=True))
            o = (_dot(jnp.exp2(s_c - shift).astype(BF16), vc_ref[0, :, pl.ds(lo, HEAD_SLOT)])
                 + _dot(jnp.exp2(s_l - shift).astype(BF16), vl_ref[0, :, pl.ds(lo, HEAD_SLOT)]))
            ohead_ref[hd] = o / o[:, V_HEAD_DIM:V_HEAD_DIM + 1]
            return carry

        lax.fori_loop(0, last, one_head, 0)
        for pair in range(MLA_HEADS // 2 - 1):
            vlo = pair * pair_w
            omain_ref[0, :, vlo:vlo + pair_w] = jnp.where(
                first_half, ohead_ref[2 * pair], pltpu.roll(ohead_ref[2 * pair + 1], V_HEAD_DIM, axis=1)
            ).astype(BF16)
        oprev_ref[...] = ohead_ref[last - 1]
        head(last, True)

    @pl.when(t < n_blocks)
    def _():
        drain()
        dens = block()
        lo_den, hi_den = dens[0], dens[0]
        for den in dens[1:]:
            lo_den = jnp.minimum(lo_den, den)
            hi_den = jnp.maximum(hi_den, den)
        trusted = jnp.logical_and(jnp.min(lo_den) >= SHIFT_DEN_MIN, jnp.max(hi_den) <= SHIFT_DEN_MAX)

        @pl.when(jnp.logical_not(trusted))
        def _():
            redo_block_exact()

    @pl.when(t == n_blocks)
    def _():
        drain()


def _attention(q, kc, kl, vc, vl, tq=512):
    b, s, _ = q.shape
    c = kc.shape[1]
    nq = s // tq
    n_blocks = b * nq
    pair_w = 2 * V_HEAD_DIM
    main_w = ATTN_WIDTH - pair_w
    last = MLA_HEADS - 1
    cur = lambda t: jnp.minimum(t, n_blocks - 1)
    prev = lambda t: jnp.maximum(t - 1, 0)
    return pl.pallas_call(
        _attn_kernel,
        grid=(n_blocks + 1,),
        in_specs=[
            pl.BlockSpec((1, tq, QK_WIDTH), lambda t: (cur(t) // nq, cur(t) % nq, 0)),
            pl.BlockSpec((1, c, QK_WIDTH), lambda t: (cur(t) // nq, 0, 0)),
            pl.BlockSpec((1, s, QK_WIDTH), lambda t: (cur(t) // nq, 0, 0)),
            pl.BlockSpec((1, c, QK_WIDTH), lambda t: (cur(t) // nq, 0, 0)),
            pl.BlockSpec((1, s, QK_WIDTH), lambda t: (cur(t) // nq, 0, 0)),
            pl.BlockSpec((1, c, HEAD_SLOT), lambda t: (prev(t) // nq, 0, last)),
            pl.BlockSpec((1, s, HEAD_SLOT), lambda t: (prev(t) // nq, 0, last)),
        ],
        out_specs=[
            pl.BlockSpec((1, tq, main_w), lambda t: (cur(t) // nq, cur(t) % nq, 0)),
            pl.BlockSpec((1, tq, pair_w), lambda t: (prev(t) // nq, prev(t) % nq, 0)),
        ],
        out_shape=[
            jax.ShapeDtypeStruct((b, s, main_w), BF16),
            jax.ShapeDtypeStruct((b, s, pair_w), BF16),
        ],
        scratch_shapes=[
            pltpu.VMEM((tq, c + s), BF16),
            pltpu.VMEM((tq, pair_w), F32),
            pltpu.VMEM((MLA_HEADS - 1, tq, HEAD_SLOT), F32),
        ],
        compiler_params=pltpu.CompilerParams(
            dimension_semantics=("arbitrary",), vmem_limit_bytes=VMEM_LIMIT_BYTES),
        name="attn",
    )(q, kc, kl, vc, vl, vc, vl)


def _fourier_kernel(t_ref, tw_ref, u_ref, cc_ref, wf_ref, o_ref, tb_ref, cw_ref):
    @pl.when(pl.program_id(0) == 0)
    def _():
        tb_ref[...] = t_ref[...].astype(BF16)
        cc = cc_ref[...].astype(BF16)
        for g in range(FOURIER_GROUPS):
            cw_ref[g] = _dot(cc, wf_ref[g]).astype(BF16)

    n_sub = tb_ref.shape[1]
    f = _dot(tb_ref[...], u_ref[0])
    gd = FOURIER_GROUP_DIM
    for g in range(FOURIER_GROUPS):
        gr, gi = [], []
        for r in range(FFT_RADIX):
            lo = r * FOURIER_WIDTH + g * gd
            a = f[:n_sub, lo:lo + gd]
            b = f[n_sub:, lo:lo + gd]
            if r == 0:
                gr.append(a)
                gi.append(b)
            else:
                c = tw_ref[r - 1, 0]
                s = tw_ref[r - 1, 1]
                gr.append(a * c - b * s)
                gi.append(a * s + b * c)
        ar, ai = gr[0] + gr[2], gi[0] + gi[2]
        br, bi = gr[0] - gr[2], gi[0] - gi[2]
        cr, ci = gr[1] + gr[3], gi[1] + gi[3]
        dr, di = gr[1] - gr[3], gi[1] - gi[3]
        xr = jnp.concatenate([ar + cr, br - di, ar - cr, br + di], axis=0)
        xi = jnp.concatenate([ai + ci, bi + dr, ai - ci, bi - dr], axis=0)
        lhs = jnp.concatenate([xr, xi], axis=1).astype(BF16)
        o_ref[0, :, g * gd:(g + 1) * gd] = _dot(lhs, cw_ref[g]).astype(BF16)


def _fourier(u4, tmat, tw, cmat, wf):
    b, n_sub, _ = u4.shape
    s = n_sub * FFT_RADIX
    full = lambda a: pl.BlockSpec(a.shape, lambda i: (0,) * a.ndim)
    return pl.pallas_call(
        _fourier_kernel,
        grid=(b,),
        in_specs=[
            full(tmat), full(tw),
            pl.BlockSpec((1, n_sub, FFT_RADIX * FOURIER_WIDTH), lambda i: (i, 0, 0)),
            full(cmat), full(wf),
        ],
        out_specs=pl.BlockSpec((1, s, FOURIER_WIDTH), lambda i: (i, 0, 0)),
        out_shape=jax.ShapeDtypeStruct((b, s, FOURIER_WIDTH), BF16),
        scratch_shapes=[
            pltpu.VMEM(tmat.shape, BF16),
            pltpu.VMEM((FOURIER_GROUPS, 2 * FOURIER_GROUP_DIM, FOURIER_GROUP_DIM), BF16),
        ],
        compiler_params=pltpu.CompilerParams(
            dimension_semantics=("arbitrary",), vmem_limit_bytes=VMEM_LIMIT_BYTES),
        name="fourier",
    )(tmat, tw, u4, cmat, wf)


def _stage_cast(src_hbm, dst_ref, stage_ref, sem_ref, chunk):
    n_chunks = src_hbm.shape[0] // chunk

    def copy(i):
        slot = i % 2
        return pltpu.make_async_copy(
            src_hbm.at[pl.ds(i * chunk, chunk)], stage_ref.at[slot, pl.ds(0, chunk)], sem_ref.at[slot])

    def body(i, carry):
        @pl.when(i + 1 < n_chunks)
        def _():
            copy(i + 1).start()

        copy(i).wait()
        start = pl.multiple_of(i * chunk, chunk)
        dst_ref[pl.ds(start, chunk), :] = stage_ref[i % 2, pl.ds(0, chunk), :].astype(BF16)
        return carry

    copy(0).start()
    lax.fori_loop(0, n_chunks, body, 0)


def _post_kernel(x_ref, four_ref, am_ref, al_ref, modc_ref, modp_ref, gpm_ref, gpf_ref, gqf_ref,
                 wo_hbm, wg_hbm, wu_hbm, wd_hbm, o_ref,
                 x1_ref, h2_ref, wo_ref, wg_ref, wu_ref, wd_ref, wide_stage, tall_stage, sem_ref):
    t = pl.program_id(0)
    n_four = four_ref.shape[2]

    def mix():
        gt_m = modc_ref[0, 2:3, :]
        sh_f = modc_ref[0, 3:4, :]
        sc_f = modc_ref[0, 4:5, :]
        attn = jnp.concatenate([am_ref[0], al_ref[0]], axis=1)
        y = _dot(four_ref[0], wo_ref[:n_four, :]) + _dot(attn, wo_ref[n_four:, :])
        x1 = x_ref[0] + gt_m * _rms(y, gpm_ref[...])
        x1_ref[...] = x1
        h2_ref[...] = (_rms(x1, gpf_ref[...]) * (1.0 + sc_f) + sh_f).astype(BF16)

    def ffn():
        gt_f = modp_ref[0, 5:6, :]
        h2 = h2_ref[...]
        g = _dot(h2, wg_ref[...])
        up = _dot(h2, wu_ref[...])
        act = (g / (1.0 + jnp.exp(-g)) * up).astype(BF16)
        o_ref[0] = x1_ref[...] + gt_f * _rms(_dot(act, wd_ref[...]), gqf_ref[...])

    @pl.when(t == 0)
    def _():
        _stage_cast(wo_hbm, wo_ref, tall_stage, sem_ref, STAGE_ROWS_OUT)
        mix()
        _stage_cast(wg_hbm, wg_ref, wide_stage, sem_ref, STAGE_ROWS_WIDE)
        _stage_cast(wu_hbm, wu_ref, wide_stage, sem_ref, STAGE_ROWS_WIDE)
        _stage_cast(wd_hbm, wd_ref, tall_stage, sem_ref, STAGE_ROWS_TALL)

    @pl.when(t > 0)
    def _():
        ffn()
        mix()


def _post(x, four, attn_main, attn_last, mod3, gpm, gpf, gqf, wo, wg, wu, wd, tm=512):
    b, s, d = x.shape
    d_ff = wg.shape[1]
    assert wo.shape[0] % STAGE_ROWS_OUT == 0 and STAGE_ROWS_OUT <= STAGE_ROWS_TALL
    assert d % STAGE_ROWS_WIDE == 0 and d_ff % STAGE_ROWS_TALL == 0
    nj = s // tm
    n_blocks = b * nj
    cur = lambda t: jnp.minimum(t, n_blocks - 1)
    prev = lambda t: jnp.maximum(t - 1, 0)
    const = lambda shape: pl.BlockSpec(shape, lambda t: (0,) * len(shape), pipeline_mode=pl.Buffered(1))
    rows = lambda w: pl.BlockSpec((1, tm, w), lambda t: (cur(t) // nj, cur(t) % nj, 0))
    hbm = pl.BlockSpec(memory_space=pl.ANY)
    return pl.pallas_call(
        _post_kernel,
        grid=(n_blocks + 1,),
        in_specs=[
            rows(d), rows(FOURIER_WIDTH), rows(attn_main.shape[2]), rows(attn_last.shape[2]),
            pl.BlockSpec((1, 6, d), lambda t: (cur(t) // nj, 0, 0)),
            pl.BlockSpec((1, 6, d), lambda t: (prev(t) // nj, 0, 0)),
            const(gpm.shape), const(gpf.shape), const(gqf.shape),
            hbm, hbm, hbm, hbm,
        ],
        out_specs=pl.BlockSpec((1, tm, d), lambda t: (prev(t) // nj, prev(t) % nj, 0)),
        out_shape=jax.ShapeDtypeStruct((b, s, d), F32),
        scratch_shapes=[
            pltpu.VMEM((tm, d), F32), pltpu.VMEM((tm, d), BF16),
            pltpu.VMEM(wo.shape, BF16), pltpu.VMEM(wg.shape, BF16), pltpu.VMEM(wu.shape, BF16),
            pltpu.VMEM(wd.shape, BF16),
            pltpu.VMEM((2, STAGE_ROWS_WIDE, d_ff), F32), pltpu.VMEM((2, STAGE_ROWS_TALL, d), F32),
            pltpu.SemaphoreType.DMA((2,)),
        ],
        compiler_params=pltpu.CompilerParams(
            dimension_semantics=("arbitrary",), vmem_limit_bytes=VMEM_LIMIT_BYTES),
        name="post",
    )(x, four, attn_main, attn_last, mod3, mod3, gpm, gpf, gqf, wo, wg, wu, wd)


def _rope_rotate_cols(w):
    a = QK_ROPE_DIM // 2
    hf = a // 2
    blocks = []
    for s0 in (0, a):
        blocks += [-w[..., s0 + hf:s0 + a], w[..., s0:s0 + hf]]
    return jnp.concatenate(blocks, axis=-1)


def _head_slot(nope, rope):
    pad = HEAD_SLOT - QK_NOPE_DIM - QK_ROPE_DIM
    return jnp.concatenate([nope, rope, jnp.zeros(rope.shape[:-1] + (pad,), rope.dtype)], axis=-1)


def _rope_tables(n_lat, q_scale):
    t = np.arange(n_lat)
    hf = QK_ROPE_DIM // 4
    inv_freq = ROPE_BASE ** (-np.arange(hf, dtype=np.float64) / hf)
    ar = (t // GRID_W)[:, None] * inv_freq[None, :]
    ac = (t % GRID_W)[:, None] * inv_freq[None, :]
    z = np.zeros_like(ar)
    cos32 = np.concatenate([np.cos(ar), np.cos(ar), np.cos(ac), np.cos(ac)], axis=-1)
    sin32 = np.concatenate([np.sin(ar), np.sin(ar), np.sin(ac), np.sin(ac)], axis=-1)
    below32 = np.concatenate([z, np.sin(ar), z, np.sin(ac)], axis=-1)
    above32 = np.concatenate([-np.sin(ar), z, -np.sin(ac), z], axis=-1)
    pad = np.zeros((n_lat, HEAD_SLOT - QK_NOPE_DIM - QK_ROPE_DIM))
    ones = np.ones((n_lat, QK_NOPE_DIM))
    zeros = np.zeros((n_lat, QK_NOPE_DIM))
    slot = lambda nope, rope: np.concatenate([nope, rope, pad], axis=-1)
    tables = [slot(ones * q_scale, cos32 * q_scale), slot(zeros, sin32 * q_scale),
              slot(zeros, cos32), slot(zeros, below32), slot(zeros, above32)]
    return jnp.asarray(np.stack(tables).astype(np.float32))


def _dft_tables(n_pos, n_ch):
    n_sub = n_pos // FFT_RADIX
    m = np.arange(n_sub, dtype=np.int64)
    ang = 2.0 * np.pi * ((m[:, None] * m[None, :]) % n_sub) / n_sub
    tmat = np.concatenate([np.cos(ang), np.sin(ang)], axis=0).astype(np.float32)
    tw = np.zeros((FFT_RADIX - 1, 2, n_sub, FOURIER_GROUP_DIM), np.float32)
    for r in range(1, FFT_RADIX):
        a = 2.0 * np.pi * r * m / n_pos
        tw[r - 1, 0] = np.cos(a)[:, None]
        tw[r - 1, 1] = np.sin(a)[:, None]
    c = np.arange(n_ch, dtype=np.int64)
    angc = 2.0 * np.pi * ((c[:, None] * c[None, :]) % n_ch) / n_ch
    norm = 1.0 / np.sqrt(float(n_pos * n_ch))
    cmat = np.concatenate([np.cos(angc) * norm, -np.sin(angc) * norm], axis=0).astype(np.float32)
    return jnp.asarray(tmat), jnp.asarray(tw), jnp.asarray(cmat)


def kernel(x, c, ctx, c_ctx, w_ada, b_ada, g_pre_mix, g_post_mix, g_pre_ffn, g_post_ffn, w_in, g_q_a,
           w_q_b, g_kv_a, w_kv_b, w_fourier, w_out, w_gate, w_up, w_down):
    assert w_ada.shape[0] == 1, "single-layer block"
    batch, n_lat, d = x.shape

    mod_rows = -(-(batch + 1) // 8) * 8
    cc = jnp.concatenate([c, c_ctx[None, :], jnp.zeros((mod_rows - batch - 1, d), F32)], axis=0)
    mod = _adaln(cc, w_ada[0], b_ada[0][None, :])
    mod3 = mod.reshape(mod_rows, 6, d)

    w_in0 = w_in[0]
    w_kr = w_in0[:, ROPE_COL:]
    zeros_d = jnp.zeros((d, QK_NOPE_DIM), F32)
    kr_slot = _head_slot(zeros_d, w_kr)
    win = jnp.concatenate([w_in0[:, :ROPE_COL], kr_slot], axis=1).astype(BF16)
    win_c = jnp.concatenate([w_in0[:, KV_COL:ROPE_COL], kr_slot], axis=1).astype(BF16)

    wq3 = w_q_b[0].reshape(Q_LORA_RANK, MLA_HEADS, QK_NOPE_DIM + QK_ROPE_DIM)
    wq_nope, wq_rope = wq3[..., :QK_NOPE_DIM], wq3[..., QK_NOPE_DIM:]
    wq_a = _head_slot(wq_nope, wq_rope).reshape(Q_LORA_RANK, QK_WIDTH)
    wq_b = _head_slot(jnp.zeros_like(wq_nope), _rope_rotate_cols(wq_rope)).reshape(Q_LORA_RANK, QK_WIDTH)
    wq = jnp.concatenate([wq_a, wq_b], axis=1).astype(BF16)

    wkv3 = w_kv_b[0].reshape(KV_LORA_RANK, MLA_HEADS, QK_NOPE_DIM + V_HEAD_DIM)
    wk_nope, wv = wkv3[..., :QK_NOPE_DIM], wkv3[..., QK_NOPE_DIM:]
    wk_slots = _head_slot(wk_nope, jnp.zeros(wk_nope.shape[:-1] + (QK_ROPE_DIM,), F32))
    wv_slots = jnp.concatenate([wv, jnp.zeros(wv.shape[:-1] + (HEAD_SLOT - V_HEAD_DIM,), F32)], axis=-1)
    wkv = jnp.concatenate([wk_slots.reshape(KV_LORA_RANK, QK_WIDTH),
                           wv_slots.reshape(KV_LORA_RANK, QK_WIDTH)], axis=1).astype(BF16)

    q_scale = float((QK_NOPE_DIM + QK_ROPE_DIM) ** -0.5 * np.log2(np.e))
    rope = _rope_tables(n_lat, q_scale)
    tmat, tw, cmat = _dft_tables(n_lat, FOURIER_GROUP_DIM)

    row2 = lambda g: g[0][None, :]
    u_f, q, k_lat, v_lat = _premix(x, mod3, row2(g_pre_mix), win, row2(g_q_a), wq, row2(g_kv_a), wkv,
                                   rope)
    k_ctx, v_ctx = _ctxkv(ctx, mod3, batch, row2(g_pre_mix), win_c, row2(g_kv_a), wkv)
    attn_main, attn_last = _attention(q, k_ctx, k_lat, v_ctx, v_lat)
    four = _fourier(u_f, tmat, tw, cmat, w_fourier[0].astype(BF16))
    return _post(x, four, attn_main, attn_last, mod3, row2(g_post_mix), row2(g_pre_ffn), row2(g_post_ffn),
                 w_out[0], w_gate[0], w_up[0], w_down[0])
```

```python
import functools

import numpy as np
import jax
import jax.numpy as jnp
from jax import lax
from jax.experimental import pallas as pl
from jax.experimental.pallas import tpu as pltpu

F32 = jnp.float32
BF16 = jnp.bfloat16

D_MODEL = 1024
GRID_W = 64
FOURIER_GROUPS = 4
FOURIER_GROUP_DIM = 128
FOURIER_WIDTH = FOURIER_GROUPS * FOURIER_GROUP_DIM
MLA_HEADS = 8
QK_NOPE_DIM = 64
QK_ROPE_DIM = 32
V_HEAD_DIM = 64
Q_LORA_RANK = 256
KV_LORA_RANK = 128
KV_COL = FOURIER_WIDTH + Q_LORA_RANK
ROPE_COL = KV_COL + KV_LORA_RANK
ROPE_BASE = 10000.0
NORM_EPS = 1e-6
FFT_RADIX = 4
HEAD_SLOT = 128
STAGE_ROWS_WIDE = 128
STAGE_ROWS_TALL = 352
STAGE_ROWS_OUT = 256
SHIFT_DEN_MIN = 2.0 ** -40
SHIFT_DEN_MAX = 2.0 ** 40
PV_KEY_TILE = 256
ATTN_WIDTH = MLA_HEADS * V_HEAD_DIM
QK_WIDTH = MLA_HEADS * HEAD_SLOT

VMEM_LIMIT_BYTES = 56 * 1024 * 1024


def _rms(x, g):
    return x * lax.rsqrt(jnp.mean(x * x, axis=-1, keepdims=True) + NORM_EPS) * g


def _dot(a, b):
    return jnp.dot(a, b, preferred_element_type=F32)


def _dot_nt(a, b):
    return lax.dot_general(a, b, (((1,), (1,)), ((), ())), preferred_element_type=F32)


def _rotary(a, rope_ref, first):
    half = QK_ROPE_DIM // 4
    below = pltpu.roll(a, half, axis=1)
    above = pltpu.roll(a, HEAD_SLOT - half, axis=1)
    return a * rope_ref[first] + below * rope_ref[first + 1] + above * rope_ref[first + 2]


def _denominator_lanes():
    half = lax.broadcasted_iota(jnp.int32, (1, QK_WIDTH), 1) // V_HEAD_DIM
    return jnp.logical_or(half % 4 == 1, half % 4 == 2).astype(F32)


def _adaln_kernel(c_ref, w_ref, b_ref, o_ref):
    c = c_ref[...]
    a = c / (1.0 + jnp.exp(-c))
    o_ref[...] = _dot(a.astype(BF16), w_ref[...].astype(BF16)) + b_ref[...]


def _adaln(cc, w_ada, b_ada, tn=1024):
    rows, d = cc.shape
    n = w_ada.shape[1]
    return pl.pallas_call(
        _adaln_kernel,
        grid=(n // tn,),
        in_specs=[
            pl.BlockSpec((rows, d), lambda j: (0, 0)),
            pl.BlockSpec((d, tn), lambda j: (0, j)),
            pl.BlockSpec((1, tn), lambda j: (0, j)),
        ],
        out_specs=pl.BlockSpec((rows, tn), lambda j: (0, j)),
        out_shape=jax.ShapeDtypeStruct((rows, n), F32),
        name="adaln",
    )(cc, w_ada, b_ada)


def _premix_kernel(x_ref, mod_ref, gpre_ref, win_ref, gq_ref, wq_ref, gkv_ref, wkv_ref,
                   rope_ref, u_ref, q_ref, k_ref, v_ref, us_ref):
    x = x_ref[0]
    shift = mod_ref[0, 0:1, :]
    scale = mod_ref[0, 1:2, :]
    h = _rms(x, gpre_ref[...]) * (1.0 + scale) + shift
    p = _dot(h.astype(BF16), win_ref[...])
    sub = us_ref.shape[1] // FFT_RADIX
    for g in range(FOURIER_GROUPS):
        glo = g * FOURIER_GROUP_DIM
        us_ref[g] = p[:, glo:glo + FOURIER_GROUP_DIM]
        for r in range(FFT_RADIX):
            lo = r * FOURIER_WIDTH + glo
            u_ref[0, :, lo:lo + FOURIER_GROUP_DIM] = (
                us_ref[g, pl.ds(r, sub, stride=FFT_RADIX), :].astype(BF16))

    qn = _rms(p[:, FOURIER_WIDTH:KV_COL], gq_ref[...]).astype(BF16)
    qq = _dot(qn, wq_ref[...])
    cosq = rope_ref[0]
    sinq = rope_ref[1]
    for hd in range(MLA_HEADS):
        lo = hd * HEAD_SLOT
        q_ref[0, :, lo:lo + HEAD_SLOT] = (
            qq[:, lo:lo + HEAD_SLOT] * cosq + qq[:, QK_WIDTH + lo:QK_WIDTH + lo + HEAD_SLOT] * sinq
        ).astype(BF16)

    kvn = _rms(p[:, KV_COL:ROPE_COL], gkv_ref[...]).astype(BF16)
    kv = _dot(kvn, wkv_ref[...])
    kr = _rotary(p[:, ROPE_COL:ROPE_COL + HEAD_SLOT], rope_ref, 2)
    for hd in range(MLA_HEADS):
        lo = hd * HEAD_SLOT
        k_ref[0, :, lo:lo + HEAD_SLOT] = (kv[:, lo:lo + HEAD_SLOT] + kr).astype(BF16)
    v_ref[0] = (kv[:, QK_WIDTH:] + _denominator_lanes()).astype(BF16)


def _premix(x, mod3, gpre, win, gq, wq, gkv, wkv, rope, tm=512):
    b, s, d = x.shape
    const = lambda shape: pl.BlockSpec(shape, lambda i, j: (0,) * len(shape))
    rows = lambda w: pl.BlockSpec((1, tm, w), lambda i, j: (i, j, 0))
    return pl.pallas_call(
        _premix_kernel,
        grid=(b, s // tm),
        in_specs=[
            rows(d),
            pl.BlockSpec((1, 6, d), lambda i, j: (i, 0, 0)),
            const(gpre.shape), const(win.shape), const(gq.shape), const(wq.shape),
            const(gkv.shape), const(wkv.shape),
            pl.BlockSpec((rope.shape[0], tm, HEAD_SLOT), lambda i, j: (0, j, 0)),
        ],
        out_specs=[
            pl.BlockSpec((1, tm // FFT_RADIX, FFT_RADIX * FOURIER_WIDTH), lambda i, j: (i, j, 0)),
            rows(QK_WIDTH), rows(QK_WIDTH), rows(QK_WIDTH)],
        out_shape=[
            jax.ShapeDtypeStruct((b, s // FFT_RADIX, FFT_RADIX * FOURIER_WIDTH), BF16),
            jax.ShapeDtypeStruct((b, s, QK_WIDTH), BF16),
            jax.ShapeDtypeStruct((b, s, QK_WIDTH), BF16),
            jax.ShapeDtypeStruct((b, s, QK_WIDTH), BF16),
        ],
        scratch_shapes=[pltpu.VMEM((FOURIER_GROUPS, tm, FOURIER_GROUP_DIM), F32)],
        compiler_params=pltpu.CompilerParams(vmem_limit_bytes=VMEM_LIMIT_BYTES),
        name="premix",
    )(x, mod3, gpre, win, gq, wq, gkv, wkv, rope)


def _ctxkv_kernel(x_ref, mod_ref, gpre_ref, win_ref, gkv_ref, wkv_ref, k_ref, v_ref):
    x = x_ref[0]
    shift = mod_ref[0, 0:1, :]
    scale = mod_ref[0, 1:2, :]
    h = _rms(x, gpre_ref[...]) * (1.0 + scale) + shift
    p = _dot(h.astype(BF16), win_ref[...])
    kvn = _rms(p[:, :KV_LORA_RANK], gkv_ref[...]).astype(BF16)
    kv = _dot(kvn, wkv_ref[...])
    kr = p[:, KV_LORA_RANK:]
    for hd in range(MLA_HEADS):
        lo = hd * HEAD_SLOT
        k_ref[0, :, lo:lo + HEAD_SLOT] = (kv[:, lo:lo + HEAD_SLOT] + kr).astype(BF16)
    v_ref[0] = (kv[:, QK_WIDTH:] + _denominator_lanes()).astype(BF16)


def _ctxkv(ctx, mod3, ctx_row, gpre, win_c, gkv, wkv):
    b, c, d = ctx.shape
    const = lambda shape: pl.BlockSpec(shape, lambda i: (0,) * len(shape))
    rows = lambda w: pl.BlockSpec((1, c, w), lambda i: (i, 0, 0))
    return pl.pallas_call(
        _ctxkv_kernel,
        grid=(b,),
        in_specs=[
            rows(d),
            pl.BlockSpec((1, 6, d), lambda i: (ctx_row, 0, 0)),
            const(gpre.shape), const(win_c.shape), const(gkv.shape), const(wkv.shape),
        ],
        out_specs=[rows(QK_WIDTH), rows(QK_WIDTH)],
        out_shape=[
            jax.ShapeDtypeStruct((b, c, QK_WIDTH), BF16),
            jax.ShapeDtypeStruct((b, c, QK_WIDTH), BF16),
        ],
        name="ctxkv",
    )(ctx, mod3, gpre, win_c, gkv, wkv)


def _attn_kernel(q_ref, kc_ref, kl_ref, vc_ref, vl_ref, vcp_ref, vlp_ref, omain_ref, olast_ref,
                 p_ref, oprev_ref, ohead_ref):
    t = pl.program_id(0)
    n_blocks = pl.num_programs(0) - 1
    n_ctx = kc_ref.shape[1]
    n_lat = kl_ref.shape[1]
    tq = q_ref.shape[1]
    pair_w = 2 * V_HEAD_DIM
    first_half = lax.broadcasted_iota(jnp.int32, (tq, pair_w), 1) < V_HEAD_DIM
    last = MLA_HEADS - 1

    @pl.when(t == 0)
    def _():
        p_ref[...] = jnp.ones_like(p_ref)
        oprev_ref[...] = jnp.ones_like(oprev_ref)

    def join_pair(o_even, o_odd):
        num = jnp.where(first_half, o_even, o_odd)
        den = pltpu.roll(jnp.where(first_half, o_odd, o_even), V_HEAD_DIM, axis=1)
        return num / den

    def den_of(o, hd):
        lane = V_HEAD_DIM if hd % 2 == 0 else 0
        return o[:, lane:lane + 1]

    def drain():
        o = _dot(p_ref[:, :n_ctx], vcp_ref[0]) + _dot(p_ref[:, n_ctx:], vlp_ref[0])
        olast_ref[0] = join_pair(oprev_ref[...], o).astype(BF16)

    def head(hd, exact):
        lo = hd * HEAD_SLOT
        qh = q_ref[0, :, lo:lo + HEAD_SLOT]
        s_c = _dot_nt(qh, kc_ref[0, :, lo:lo + HEAD_SLOT])
        s_l = _dot_nt(qh, kl_ref[0, :, lo:lo + HEAD_SLOT])
        shift = jnp.max(s_c, axis=-1, keepdims=True)
        if exact:
            shift = jnp.maximum(shift, jnp.max(s_l, axis=-1, keepdims=True))
        if hd == last:
            p_c = jnp.exp2(s_c - shift)
            p_l = jnp.exp2(s_l - shift)
            p_ref[:, :n_ctx] = p_c.astype(BF16)
            p_ref[:, n_ctx:] = p_l.astype(BF16)
            return None, jnp.sum(p_c, axis=-1, keepdims=True) + jnp.sum(p_l, axis=-1, keepdims=True)
        o = _dot(jnp.exp2(s_c - shift).astype(BF16), vc_ref[0, :, lo:lo + HEAD_SLOT])
        for k0 in range(0, n_lat, PV_KEY_TILE):
            p = jnp.exp2(s_l[:, k0:k0 + PV_KEY_TILE] - shift).astype(BF16)
            o = o + _dot(p, vl_ref[0, k0:k0 + PV_KEY_TILE, lo:lo + HEAD_SLOT])
        return o, den_of(o, hd)

    def block():
        dens = []
        for pair in range(MLA_HEADS // 2):
            outs = []
            for hd in (2 * pair, 2 * pair + 1):
                o, den = head(hd, False)
                dens.append(den)
                if o is not None:
                    outs.append(o)
            if len(outs) == 2:
                vlo = pair * pair_w
                omain_ref[0, :, vlo:vlo + pair_w] = join_pair(outs[0], outs[1]).astype(BF16)
            else:
                oprev_ref[...] = outs[0]
        return dens

    def redo_block_exact():
        def one_head(hd, carry):
            lo = pl.multiple_of(hd * HEAD_SLOT, HEAD_SLOT)
            qh = q_ref[0, :, pl.ds(lo, HEAD_SLOT)]
            s_c = _dot_nt(qh, kc_ref[0, :, pl.ds(lo, HEAD_SLOT)])
            s_l = _dot_nt(qh, kl_ref[0, :, pl.ds(lo, HEAD_SLOT)])
            shift = jnp.maximum(jnp.max(s_c, axis=-1, keepdims=True), jnp.max(s_l, axis=-1, keepdims=True))
            ohead_ref[hd] = (_dot(jnp.exp2(s_c - shift).astype(BF16), vc_ref[0, :, pl.ds(lo, HEAD_SLOT)])
                             + _dot(jnp.exp2(s_l - shift).astype(BF16), vl_ref[0, :, pl.ds(lo, HEAD_SLOT)]))
            return carry

        lax.fori_loop(0, last, one_head, 0)
        for pair in range(MLA_HEADS // 2 - 1):
            vlo = pair * pair_w
            omain_ref[0, :, vlo:vlo + pair_w] = join_pair(ohead_ref[2 * pair], ohead_ref[2 * pair + 1]).astype(BF16)
        oprev_ref[...] = ohead_ref[last - 1]
        head(last, True)

    @pl.when(t < n_blocks)
    def _():
        drain()
        dens = block()
        lo_den, hi_den = dens[0], dens[0]
        for den in dens[1:]:
            lo_den = jnp.minimum(lo_den, den)
            hi_den = jnp.maximum(hi_den, den)
        trusted = jnp.logical_and(jnp.min(lo_den) >= SHIFT_DEN_MIN, jnp.max(hi_den) <= SHIFT_DEN_MAX)

        @pl.when(jnp.logical_not(trusted))
        def _():
            redo_block_exact()

    @pl.when(t == n_blocks)
    def _():
        drain()


def _attention(q, kc, kl, vc, vl, tq=512):
    b, s, _ = q.shape
    c = kc.shape[1]
    nq = s // tq
    n_blocks = b * nq
    pair_w = 2 * V_HEAD_DIM
    main_w = ATTN_WIDTH - pair_w
    last = MLA_HEADS - 1
    cur = lambda t: jnp.minimum(t, n_blocks - 1)
    prev = lambda t: jnp.maximum(t - 1, 0)
    return pl.pallas_call(
        _attn_kernel,
        grid=(n_blocks + 1,),
        in_specs=[
            pl.BlockSpec((1, tq, QK_WIDTH), lambda t: (cur(t) // nq, cur(t) % nq, 0)),
            pl.BlockSpec((1, c, QK_WIDTH), lambda t: (cur(t) // nq, 0, 0)),
            pl.BlockSpec((1, s, QK_WIDTH), lambda t: (cur(t) // nq, 0, 0)),
            pl.BlockSpec((1, c, QK_WIDTH), lambda t: (cur(t) // nq, 0, 0)),
            pl.BlockSpec((1, s, QK_WIDTH), lambda t: (cur(t) // nq, 0, 0)),
            pl.BlockSpec((1, c, HEAD_SLOT), lambda t: (prev(t) // nq, 0, last)),
            pl.BlockSpec((1, s, HEAD_SLOT), lambda t: (prev(t) // nq, 0, last)),
        ],
        out_specs=[
            pl.BlockSpec((1, tq, main_w), lambda t: (cur(t) // nq, cur(t) % nq, 0)),
            pl.BlockSpec((1, tq, pair_w), lambda t: (prev(t) // nq, prev(t) % nq, 0)),
        ],
        out_shape=[
            jax.ShapeDtypeStruct((b, s, main_w), BF16),
            jax.ShapeDtypeStruct((b, s, pair_w), BF16),
        ],
        scratch_shapes=[
            pltpu.VMEM((tq, c + s), BF16),
            pltpu.VMEM((tq, pair_w), F32),
            pltpu.VMEM((MLA_HEADS - 1, tq, HEAD_SLOT), F32),
        ],
        compiler_params=pltpu.CompilerParams(
            dimension_semantics=("arbitrary",), vmem_limit_bytes=VMEM_LIMIT_BYTES),
        name="attn",
    )(q, kc, kl, vc, vl, vc, vl)


def _fourier_kernel(t_ref, tw_ref, u_ref, cc_ref, wf_ref, o_ref, tb_ref, cw_ref):
    @pl.when(pl.program_id(0) == 0)
    def _():
        tb_ref[...] = t_ref[...].astype(BF16)
        cc = cc_ref[...].astype(BF16)
        for g in range(FOURIER_GROUPS):
            cw_ref[g] = _dot(cc, wf_ref[g]).astype(BF16)

    n_sub = tb_ref.shape[1]
    f = _dot(tb_ref[...], u_ref[0])
    gd = FOURIER_GROUP_DIM
    for g in range(FOURIER_GROUPS):
        gr, gi = [], []
        for r in range(FFT_RADIX):
            lo = r * FOURIER_WIDTH + g * gd
            a = f[:n_sub, lo:lo + gd]
            b = f[n_sub:, lo:lo + gd]
            if r == 0:
                gr.append(a)
                gi.append(b)
            else:
                c = tw_ref[r - 1, 0]
                s = tw_ref[r - 1, 1]
                gr.append(a * c - b * s)
                gi.append(a * s + b * c)
        ar, ai = gr[0] + gr[2], gi[0] + gi[2]
        br, bi = gr[0] - gr[2], gi[0] - gi[2]
        cr, ci = gr[1] + gr[3], gi[1] + gi[3]
        dr, di = gr[1] - gr[3], gi[1] - gi[3]
        xr = jnp.concatenate([ar + cr, br - di, ar - cr, br + di], axis=0)
        xi = jnp.concatenate([ai + ci, bi + dr, ai - ci, bi - dr], axis=0)
        lhs = jnp.concatenate([xr, xi], axis=1).astype(BF16)
        o_ref[0, :, g * gd:(g + 1) * gd] = _dot(lhs, cw_ref[g]).astype(BF16)


def _fourier(u4, tmat, tw, cmat, wf):
    b, n_sub, _ = u4.shape
    s = n_sub * FFT_RADIX
    full = lambda a: pl.BlockSpec(a.shape, lambda i: (0,) * a.ndim)
    return pl.pallas_call(
        _fourier_kernel,
        grid=(b,),
        in_specs=[
            full(tmat), full(tw),
            pl.BlockSpec((1, n_sub, FFT_RADIX * FOURIER_WIDTH), lambda i: (i, 0, 0)),
            full(cmat), full(wf),
        ],
        out_specs=pl.BlockSpec((1, s, FOURIER_WIDTH), lambda i: (i, 0, 0)),
        out_shape=jax.ShapeDtypeStruct((b, s, FOURIER_WIDTH), BF16),
        scratch_shapes=[
            pltpu.VMEM(tmat.shape, BF16),
            pltpu.VMEM((FOURIER_GROUPS, 2 * FOURIER_GROUP_DIM, FOURIER_GROUP_DIM), BF16),
        ],
        compiler_params=pltpu.CompilerParams(
            dimension_semantics=("arbitrary",), vmem_limit_bytes=VMEM_LIMIT_BYTES),
        name="fourier",
    )(tmat, tw, u4, cmat, wf)


def _stage_cast(src_hbm, dst_ref, stage_ref, sem_ref, chunk):
    n_chunks = src_hbm.shape[0] // chunk

    def copy(i):
        slot = i % 2
        return pltpu.make_async_copy(
            src_hbm.at[pl.ds(i * chunk, chunk)], stage_ref.at[slot, pl.ds(0, chunk)], sem_ref.at[slot])

    def body(i, carry):
        @pl.when(i + 1 < n_chunks)
        def _():
            copy(i + 1).start()

        copy(i).wait()
        start = pl.multiple_of(i * chunk, chunk)
        dst_ref[pl.ds(start, chunk), :] = stage_ref[i % 2, pl.ds(0, chunk), :].astype(BF16)
        return carry

    copy(0).start()
    lax.fori_loop(0, n_chunks, body, 0)


def _post_kernel(x_ref, four_ref, am_ref, al_ref, modc_ref, modp_ref, gpm_ref, gpf_ref, gqf_ref,
                 wo_hbm, wg_hbm, wu_hbm, wd_hbm, o_ref,
                 x1_ref, h2_ref, wo_ref, wg_ref, wu_ref, wd_ref, wide_stage, tall_stage, sem_ref):
    t = pl.program_id(0)
    n_four = four_ref.shape[2]

    def mix():
        gt_m = modc_ref[0, 2:3, :]
        sh_f = modc_ref[0, 3:4, :]
        sc_f = modc_ref[0, 4:5, :]
        attn = jnp.concatenate([am_ref[0], al_ref[0]], axis=1)
        y = _dot(four_ref[0], wo_ref[:n_four, :]) + _dot(attn, wo_ref[n_four:, :])
        x1 = x_ref[0] + gt_m * _rms(y, gpm_ref[...])
        x1_ref[...] = x1
        h2_ref[...] = (_rms(x1, gpf_ref[...]) * (1.0 + sc_f) + sh_f).astype(BF16)

    def ffn():
        gt_f = modp_ref[0, 5:6, :]
        h2 = h2_ref[...]
        g = _dot(h2, wg_ref[...])
        up = _dot(h2, wu_ref[...])
        act = (g / (1.0 + jnp.exp(-g)) * up).astype(BF16)
        o_ref[0] = x1_ref[...] + gt_f * _rms(_dot(act, wd_ref[...]), gqf_ref[...])

    @pl.when(t == 0)
    def _():
        _stage_cast(wo_hbm, wo_ref, tall_stage, sem_ref, STAGE_ROWS_OUT)
        mix()
        _stage_cast(wg_hbm, wg_ref, wide_stage, sem_ref, STAGE_ROWS_WIDE)
        _stage_cast(wu_hbm, wu_ref, wide_stage, sem_ref, STAGE_ROWS_WIDE)
        _stage_cast(wd_hbm, wd_ref, tall_stage, sem_ref, STAGE_ROWS_TALL)

    @pl.when(t > 0)
    def _():
        ffn()
        mix()


def _post(x, four, attn_main, attn_last, mod3, gpm, gpf, gqf, wo, wg, wu, wd, tm=512):
    b, s, d = x.shape
    d_ff = wg.shape[1]
    assert wo.shape[0] % STAGE_ROWS_OUT == 0 and STAGE_ROWS_OUT <= STAGE_ROWS_TALL
    assert d % STAGE_ROWS_WIDE == 0 and d_ff % STAGE_ROWS_TALL == 0
    nj = s // tm
    n_blocks = b * nj
    cur = lambda t: jnp.minimum(t, n_blocks - 1)
    prev = lambda t: jnp.maximum(t - 1, 0)
    const = lambda shape: pl.BlockSpec(shape, lambda t: (0,) * len(shape), pipeline_mode=pl.Buffered(1))
    rows = lambda w: pl.BlockSpec((1, tm, w), lambda t: (cur(t) // nj, cur(t) % nj, 0))
    hbm = pl.BlockSpec(memory_space=pl.ANY)
    return pl.pallas_call(
        _post_kernel,
        grid=(n_blocks + 1,),
        in_specs=[
            rows(d), rows(FOURIER_WIDTH), rows(attn_main.shape[2]), rows(attn_last.shape[2]),
            pl.BlockSpec((1, 6, d), lambda t: (cur(t) // nj, 0, 0)),
            pl.BlockSpec((1, 6, d), lambda t: (prev(t) // nj, 0, 0)),
            const(gpm.shape), const(gpf.shape), const(gqf.shape),
            hbm, hbm, hbm, hbm,
        ],
        out_specs=pl.BlockSpec((1, tm, d), lambda t: (prev(t) // nj, prev(t) % nj, 0)),
        out_shape=jax.ShapeDtypeStruct((b, s, d), F32),
        scratch_shapes=[
            pltpu.VMEM((tm, d), F32), pltpu.VMEM((tm, d), BF16),
            pltpu.VMEM(wo.shape, BF16), pltpu.VMEM(wg.shape, BF16), pltpu.VMEM(wu.shape, BF16),
            pltpu.VMEM(wd.shape, BF16),
            pltpu.VMEM((2, STAGE_ROWS_WIDE, d_ff), F32), pltpu.VMEM((2, STAGE_ROWS_TALL, d), F32),
            pltpu.SemaphoreType.DMA((2,)),
        ],
        compiler_params=pltpu.CompilerParams(
            dimension_semantics=("arbitrary",), vmem_limit_bytes=VMEM_LIMIT_BYTES),
        name="post",
    )(x, four, attn_main, attn_last, mod3, mod3, gpm, gpf, gqf, wo, wg, wu, wd)


def _rope_rotate_cols(w):
    a = QK_ROPE_DIM // 2
    hf = a // 2
    blocks = []
    for s0 in (0, a):
        blocks += [-w[..., s0 + hf:s0 + a], w[..., s0:s0 + hf]]
    return jnp.concatenate(blocks, axis=-1)


def _head_slot(nope, rope):
    pad = HEAD_SLOT - QK_NOPE_DIM - QK_ROPE_DIM
    return jnp.concatenate([nope, rope, jnp.zeros(rope.shape[:-1] + (pad,), rope.dtype)], axis=-1)


def _rope_tables(n_lat, q_scale):
    t = np.arange(n_lat)
    hf = QK_ROPE_DIM // 4
    inv_freq = ROPE_BASE ** (-np.arange(hf, dtype=np.float64) / hf)
    ar = (t // GRID_W)[:, None] * inv_freq[None, :]
    ac = (t % GRID_W)[:, None] * inv_freq[None, :]
    z = np.zeros_like(ar)
    cos32 = np.concatenate([np.cos(ar), np.cos(ar), np.cos(ac), np.cos(ac)], axis=-1)
    sin32 = np.concatenate([np.sin(ar), np.sin(ar), np.sin(ac), np.sin(ac)], axis=-1)
    below32 = np.concatenate([z, np.sin(ar), z, np.sin(ac)], axis=-1)
    above32 = np.concatenate([-np.sin(ar), z, -np.sin(ac), z], axis=-1)
    pad = np.zeros((n_lat, HEAD_SLOT - QK_NOPE_DIM - QK_ROPE_DIM))
    ones = np.ones((n_lat, QK_NOPE_DIM))
    zeros = np.zeros((n_lat, QK_NOPE_DIM))
    slot = lambda nope, rope: np.concatenate([nope, rope, pad], axis=-1)
    tables = [slot(ones * q_scale, cos32 * q_scale), slot(zeros, sin32 * q_scale),
              slot(zeros, cos32), slot(zeros, below32), slot(zeros, above32)]
    return jnp.asarray(np.stack(tables).astype(np.float32))


def _dft_tables(n_pos, n_ch):
    n_sub = n_pos // FFT_RADIX
    m = np.arange(n_sub, dtype=np.int64)
    ang = 2.0 * np.pi * ((m[:, None] * m[None, :]) % n_sub) / n_sub
    tmat = np.concatenate([np.cos(ang), np.sin(ang)], axis=0).astype(np.float32)
    tw = np.zeros((FFT_RADIX - 1, 2, n_sub, FOURIER_GROUP_DIM), np.float32)
    for r in range(1, FFT_RADIX):
        a = 2.0 * np.pi * r * m / n_pos
        tw[r - 1, 0] = np.cos(a)[:, None]
        tw[r - 1, 1] = np.sin(a)[:, None]
    c = np.arange(n_ch, dtype=np.int64)
    angc = 2.0 * np.pi * ((c[:, None] * c[None, :]) % n_ch) / n_ch
    norm = 1.0 / np.sqrt(float(n_pos * n_ch))
    cmat = np.concatenate([np.cos(angc) * norm, -np.sin(angc) * norm], axis=0).astype(np.float32)
    return jnp.asarray(tmat), jnp.asarray(tw), jnp.asarray(cmat)


def kernel(x, c, ctx, c_ctx, w_ada, b_ada, g_pre_mix, g_post_mix, g_pre_ffn, g_post_ffn, w_in, g_q_a,
           w_q_b, g_kv_a, w_kv_b, w_fourier, w_out, w_gate, w_up, w_down):
    assert w_ada.shape[0] == 1, "single-layer block"
    batch, n_lat, d = x.shape

    mod_rows = -(-(batch + 1) // 8) * 8
    cc = jnp.concatenate([c, c_ctx[None, :], jnp.zeros((mod_rows - batch - 1, d), F32)], axis=0)
    mod = _adaln(cc, w_ada[0], b_ada[0][None, :])
    mod3 = mod.reshape(mod_rows, 6, d)

    w_in0 = w_in[0]
    w_kr = w_in0[:, ROPE_COL:]
    zeros_d = jnp.zeros((d, QK_NOPE_DIM), F32)
    kr_slot = _head_slot(zeros_d, w_kr)
    win = jnp.concatenate([w_in0[:, :ROPE_COL], kr_slot], axis=1).astype(BF16)
    win_c = jnp.concatenate([w_in0[:, KV_COL:ROPE_COL], kr_slot], axis=1).astype(BF16)

    wq3 = w_q_b[0].reshape(Q_LORA_RANK, MLA_HEADS, QK_NOPE_DIM + QK_ROPE_DIM)
    wq_nope, wq_rope = wq3[..., :QK_NOPE_DIM], wq3[..., QK_NOPE_DIM:]
    wq_a = _head_slot(wq_nope, wq_rope).reshape(Q_LORA_RANK, QK_WIDTH)
    wq_b = _head_slot(jnp.zeros_like(wq_nope), _rope_rotate_cols(wq_rope)).reshape(Q_LORA_RANK, QK_WIDTH)
    wq = jnp.concatenate([wq_a, wq_b], axis=1).astype(BF16)

    wkv3 = w_kv_b[0].reshape(KV_LORA_RANK, MLA_HEADS, QK_NOPE_DIM + V_HEAD_DIM)
    wk_nope, wv = wkv3[..., :QK_NOPE_DIM], wkv3[..., QK_NOPE_DIM:]
    wk_slots = _head_slot(wk_nope, jnp.zeros(wk_nope.shape[:-1] + (QK_ROPE_DIM,), F32))
    wv_pairs = wv.reshape(KV_LORA_RANK, MLA_HEADS // 2, 2 * V_HEAD_DIM)
    zero_pairs = jnp.zeros_like(wv_pairs)
    wv_slots = jnp.concatenate(
        [wv_pairs[..., :V_HEAD_DIM], zero_pairs, wv_pairs[..., V_HEAD_DIM:]], axis=-1)
    wkv = jnp.concatenate([wk_slots.reshape(KV_LORA_RANK, QK_WIDTH),
                           wv_slots.reshape(KV_LORA_RANK, QK_WIDTH)], axis=1).astype(BF16)

    q_scale = float((QK_NOPE_DIM + QK_ROPE_DIM) ** -0.5 * np.log2(np.e))
    rope = _rope_tables(n_lat, q_scale)
    tmat, tw, cmat = _dft_tables(n_lat, FOURIER_GROUP_DIM)

    row2 = lambda g: g[0][None, :]
    u_f, q, k_lat, v_lat = _premix(x, mod3, row2(g_pre_mix), win, row2(g_q_a), wq, row2(g_kv_a), wkv,
                                   rope)
    k_ctx, v_ctx = _ctxkv(ctx, mod3, batch, row2(g_pre_mix), win_c, row2(g_kv_a), wkv)
    attn_main, attn_last = _attention(q, k_ctx, k_lat, v_ctx, v_lat)
    four = _fourier(u_f, tmat, tw, cmat, w_fourier[0].astype(BF16))
    return _post(x, four, attn_main, attn_last, mod3, row2(g_post_mix), row2(g_pre_ffn), row2(g_post_ffn),
                 w_out[0], w_gate[0], w_up[0], w_down[0])
```

```python
import functools

import numpy as np
import jax
import jax.numpy as jnp
from jax import lax
from jax.experimental import pallas as pl
from jax.experimental.pallas import tpu as pltpu

F32 = jnp.float32
BF16 = jnp.bfloat16

D_MODEL = 1024
GRID_W = 64
FOURIER_GROUPS = 4
FOURIER_GROUP_DIM = 128
FOURIER_WIDTH = FOURIER_GROUPS * FOURIER_GROUP_DIM
MLA_HEADS = 8
QK_NOPE_DIM = 64
QK_ROPE_DIM = 32
V_HEAD_DIM = 64
Q_LORA_RANK = 256
KV_LORA_RANK = 128
KV_COL = FOURIER_WIDTH + Q_LORA_RANK
ROPE_COL = KV_COL + KV_LORA_RANK
ROPE_BASE = 10000.0
NORM_EPS = 1e-6
FFT_RADIX = 4
HEAD_SLOT = 128
STAGE_ROWS_WIDE = 128
STAGE_ROWS_TALL = 352
STAGE_ROWS_OUT = 256
SHIFT_DEN_MIN = 2.0 ** -40
SHIFT_DEN_MAX = 2.0 ** 40
PV_KEY_TILE = 256
ATTN_WIDTH = MLA_HEADS * V_HEAD_DIM
QK_WIDTH = MLA_HEADS * HEAD_SLOT

VMEM_LIMIT_BYTES = 56 * 1024 * 1024


def _rms(x, g):
    return x * lax.rsqrt(jnp.mean(x * x, axis=-1, keepdims=True) + NORM_EPS) * g


def _dot(a, b):
    return jnp.dot(a, b, preferred_element_type=F32)


def _dot_nt(a, b):
    return lax.dot_general(a, b, (((1,), (1,)), ((), ())), preferred_element_type=F32)


def _rotary(a, rope_ref, first):
    half = QK_ROPE_DIM // 4
    below = pltpu.roll(a, half, axis=1)
    above = pltpu.roll(a, HEAD_SLOT - half, axis=1)
    return a * rope_ref[first] + below * rope_ref[first + 1] + above * rope_ref[first + 2]


def _store_value_slots(v_ref, v):
    pair_w = 2 * V_HEAD_DIM
    lower = lax.broadcasted_iota(jnp.int32, (v.shape[0], pair_w), 1) < V_HEAD_DIM
    for pair in range(MLA_HEADS // 2):
        vp = v[:, pair * pair_w:(pair + 1) * pair_w]
        lo = 2 * pair * HEAD_SLOT
        v_ref[0, :, lo:lo + HEAD_SLOT] = jnp.where(lower, vp, 1.0).astype(BF16)
        v_ref[0, :, lo + HEAD_SLOT:lo + 2 * HEAD_SLOT] = jnp.where(lower, 1.0, vp).astype(BF16)


def _adaln_kernel(c_ref, w_ref, b_ref, o_ref):
    c = c_ref[...]
    a = c / (1.0 + jnp.exp(-c))
    o_ref[...] = _dot(a.astype(BF16), w_ref[...].astype(BF16)) + b_ref[...]


def _adaln(cc, w_ada, b_ada, tn=1024):
    rows, d = cc.shape
    n = w_ada.shape[1]
    return pl.pallas_call(
        _adaln_kernel,
        grid=(n // tn,),
        in_specs=[
            pl.BlockSpec((rows, d), lambda j: (0, 0)),
            pl.BlockSpec((d, tn), lambda j: (0, j)),
            pl.BlockSpec((1, tn), lambda j: (0, j)),
        ],
        out_specs=pl.BlockSpec((rows, tn), lambda j: (0, j)),
        out_shape=jax.ShapeDtypeStruct((rows, n), F32),
        name="adaln",
    )(cc, w_ada, b_ada)


def _premix_kernel(x_ref, mod_ref, gpre_ref, win_ref, gq_ref, wq_ref, gkv_ref, wkv_ref,
                   rope_ref, u_ref, q_ref, k_ref, v_ref, us_ref):
    x = x_ref[0]
    shift = mod_ref[0, 0:1, :]
    scale = mod_ref[0, 1:2, :]
    h = _rms(x, gpre_ref[...] * (1.0 + scale)) + shift
    p = _dot(h.astype(BF16), win_ref[...])
    sub = us_ref.shape[1] // FFT_RADIX
    for g in range(FOURIER_GROUPS):
        glo = g * FOURIER_GROUP_DIM
        us_ref[g] = p[:, glo:glo + FOURIER_GROUP_DIM]
        for r in range(FFT_RADIX):
            lo = r * FOURIER_WIDTH + glo
            u_ref[0, :, lo:lo + FOURIER_GROUP_DIM] = (
                us_ref[g, pl.ds(r, sub, stride=FFT_RADIX), :].astype(BF16))

    qn = _rms(p[:, FOURIER_WIDTH:KV_COL], gq_ref[...]).astype(BF16)
    qq = _dot(qn, wq_ref[...])
    cosq = rope_ref[0]
    sinq = rope_ref[1]
    for hd in range(MLA_HEADS):
        lo = hd * HEAD_SLOT
        q_ref[0, :, lo:lo + HEAD_SLOT] = (
            qq[:, lo:lo + HEAD_SLOT] * cosq + qq[:, QK_WIDTH + lo:QK_WIDTH + lo + HEAD_SLOT] * sinq
        ).astype(BF16)

    kvn = _rms(p[:, KV_COL:ROPE_COL], gkv_ref[...]).astype(BF16)
    kv = _dot(kvn, wkv_ref[...])
    kr = _rotary(p[:, ROPE_COL:ROPE_COL + HEAD_SLOT], rope_ref, 2)
    for hd in range(MLA_HEADS):
        lo = hd * HEAD_SLOT
        k_ref[0, :, lo:lo + HEAD_SLOT] = (kv[:, lo:lo + HEAD_SLOT] + kr).astype(BF16)
    _store_value_slots(v_ref, kv[:, QK_WIDTH:])


def _premix(x, mod3, gpre, win, gq, wq, gkv, wkv, rope, tm=512):
    b, s, d = x.shape
    const = lambda shape: pl.BlockSpec(shape, lambda i, j: (0,) * len(shape))
    rows = lambda w: pl.BlockSpec((1, tm, w), lambda i, j: (i, j, 0))
    return pl.pallas_call(
        _premix_kernel,
        grid=(b, s // tm),
        in_specs=[
            rows(d),
            pl.BlockSpec((1, 6, d), lambda i, j: (i, 0, 0)),
            const(gpre.shape), const(win.shape), const(gq.shape), const(wq.shape),
            const(gkv.shape), const(wkv.shape),
            pl.BlockSpec((rope.shape[0], tm, HEAD_SLOT), lambda i, j: (0, j, 0)),
        ],
        out_specs=[
            pl.BlockSpec((1, tm // FFT_RADIX, FFT_RADIX * FOURIER_WIDTH), lambda i, j: (i, j, 0)),
            rows(QK_WIDTH), rows(QK_WIDTH), rows(QK_WIDTH)],
        out_shape=[
            jax.ShapeDtypeStruct((b, s // FFT_RADIX, FFT_RADIX * FOURIER_WIDTH), BF16),
            jax.ShapeDtypeStruct((b, s, QK_WIDTH), BF16),
            jax.ShapeDtypeStruct((b, s, QK_WIDTH), BF16),
            jax.ShapeDtypeStruct((b, s, QK_WIDTH), BF16),
        ],
        scratch_shapes=[pltpu.VMEM((FOURIER_GROUPS, tm, FOURIER_GROUP_DIM), F32)],
        compiler_params=pltpu.CompilerParams(vmem_limit_bytes=VMEM_LIMIT_BYTES),
        name="premix",
    )(x, mod3, gpre, win, gq, wq, gkv, wkv, rope)


def _ctxkv_kernel(x_ref, mod_ref, gpre_ref, win_ref, gkv_ref, wkv_ref, k_ref, v_ref):
    x = x_ref[0]
    shift = mod_ref[0, 0:1, :]
    scale = mod_ref[0, 1:2, :]
    h = _rms(x, gpre_ref[...] * (1.0 + scale)) + shift
    p = _dot(h.astype(BF16), win_ref[...])
    kvn = _rms(p[:, :KV_LORA_RANK], gkv_ref[...]).astype(BF16)
    kv = _dot(kvn, wkv_ref[...])
    kr = p[:, KV_LORA_RANK:]
    for hd in range(MLA_HEADS):
        lo = hd * HEAD_SLOT
        k_ref[0, :, lo:lo + HEAD_SLOT] = (kv[:, lo:lo + HEAD_SLOT] + kr).astype(BF16)
    _store_value_slots(v_ref, kv[:, QK_WIDTH:])


def _ctxkv(ctx, mod3, ctx_row, gpre, win_c, gkv, wkv):
    b, c, d = ctx.shape
    const = lambda shape: pl.BlockSpec(shape, lambda i: (0,) * len(shape))
    rows = lambda w: pl.BlockSpec((1, c, w), lambda i: (i, 0, 0))
    return pl.pallas_call(
        _ctxkv_kernel,
        grid=(b,),
        in_specs=[
            rows(d),
            pl.BlockSpec((1, 6, d), lambda i: (ctx_row, 0, 0)),
            const(gpre.shape), const(win_c.shape), const(gkv.shape), const(wkv.shape),
        ],
        out_specs=[rows(QK_WIDTH), rows(QK_WIDTH)],
        out_shape=[
            jax.ShapeDtypeStruct((b, c, QK_WIDTH), BF16),
            jax.ShapeDtypeStruct((b, c, QK_WIDTH), BF16),
        ],
        name="ctxkv",
    )(ctx, mod3, gpre, win_c, gkv, wkv)


def _attn_kernel(q_ref, kc_ref, kl_ref, vc_ref, vl_ref, vcp_ref, vlp_ref, omain_ref, olast_ref,
                 p_ref, oprev_ref, ohead_ref):
    t = pl.program_id(0)
    n_blocks = pl.num_programs(0) - 1
    n_ctx = kc_ref.shape[1]
    n_lat = kl_ref.shape[1]
    tq = q_ref.shape[1]
    pair_w = 2 * V_HEAD_DIM
    first_half = lax.broadcasted_iota(jnp.int32, (tq, pair_w), 1) < V_HEAD_DIM
    last = MLA_HEADS - 1

    @pl.when(t == 0)
    def _():
        p_ref[...] = jnp.ones_like(p_ref)
        oprev_ref[...] = jnp.ones_like(oprev_ref)

    def join_pair(o_even, o_odd):
        num = jnp.where(first_half, o_even, o_odd)
        den = pltpu.roll(jnp.where(first_half, o_odd, o_even), V_HEAD_DIM, axis=1)
        return num / den

    def den_of(o, hd):
        lane = V_HEAD_DIM if hd % 2 == 0 else 0
        return o[:, lane:lane + 1]

    def drain():
        o = _dot(p_ref[:, :n_ctx], vcp_ref[0]) + _dot(p_ref[:, n_ctx:], vlp_ref[0])
        olast_ref[0] = join_pair(oprev_ref[...], o).astype(BF16)

    def head(hd, exact):
        lo = hd * HEAD_SLOT
        qh = q_ref[0, :, lo:lo + HEAD_SLOT]
        s_c = _dot_nt(qh, kc_ref[0, :, lo:lo + HEAD_SLOT])
        s_l = _dot_nt(qh, kl_ref[0, :, lo:lo + HEAD_SLOT])
        shift = jnp.max(s_c, axis=-1, keepdims=True)
        if exact:
            shift = jnp.maximum(shift, jnp.max(s_l, axis=-1, keepdims=True))
        if hd == last:
            p_c = jnp.exp2(s_c - shift)
            p_l = jnp.exp2(s_l - shift)
            p_ref[:, :n_ctx] = p_c.astype(BF16)
            p_ref[:, n_ctx:] = p_l.astype(BF16)
            return None, jnp.sum(p_c, axis=-1, keepdims=True) + jnp.sum(p_l, axis=-1, keepdims=True)
        o = _dot(jnp.exp2(s_c - shift).astype(BF16), vc_ref[0, :, lo:lo + HEAD_SLOT])
        for k0 in range(0, n_lat, PV_KEY_TILE):
            p = jnp.exp2(s_l[:, k0:k0 + PV_KEY_TILE] - shift).astype(BF16)
            o = o + _dot(p, vl_ref[0, k0:k0 + PV_KEY_TILE, lo:lo + HEAD_SLOT])
        return o, den_of(o, hd)

    def block():
        dens = []
        for pair in range(MLA_HEADS // 2):
            outs = []
            for hd in (2 * pair, 2 * pair + 1):
                o, den = head(hd, False)
                dens.append(den)
                if o is not None:
                    outs.append(o)
            if len(outs) == 2:
                vlo = pair * pair_w
                omain_ref[0, :, vlo:vlo + pair_w] = join_pair(outs[0], outs[1]).astype(BF16)
            else:
                oprev_ref[...] = outs[0]
        return dens

    def redo_block_exact():
        def one_head(hd, carry):
            lo = pl.multiple_of(hd * HEAD_SLOT, HEAD_SLOT)
            qh = q_ref[0, :, pl.ds(lo, HEAD_SLOT)]
            s_c = _dot_nt(qh, kc_ref[0, :, pl.ds(lo, HEAD_SLOT)])
            s_l = _dot_nt(qh, kl_ref[0, :, pl.ds(lo, HEAD_SLOT)])
            shift = jnp.maximum(jnp.max(s_c, axis=-1, keepdims=True), jnp.max(s_l, axis=-1, keepdims=True))
            ohead_ref[hd] = (_dot(jnp.exp2(s_c - shift).astype(BF16), vc_ref[0, :, pl.ds(lo, HEAD_SLOT)])
                             + _dot(jnp.exp2(s_l - shift).astype(BF16), vl_ref[0, :, pl.ds(lo, HEAD_SLOT)]))
            return carry

        lax.fori_loop(0, last, one_head, 0)
        for pair in range(MLA_HEADS // 2 - 1):
            vlo = pair * pair_w
            omain_ref[0, :, vlo:vlo + pair_w] = join_pair(ohead_ref[2 * pair], ohead_ref[2 * pair + 1]).astype(BF16)
        oprev_ref[...] = ohead_ref[last - 1]
        head(last, True)

    @pl.when(t < n_blocks)
    def _():
        drain()
        dens = block()
        lo_den, hi_den = dens[0], dens[0]
        for den in dens[1:]:
            lo_den = jnp.minimum(lo_den, den)
            hi_den = jnp.maximum(hi_den, den)
        trusted = jnp.logical_and(jnp.min(lo_den) >= SHIFT_DEN_MIN, jnp.max(hi_den) <= SHIFT_DEN_MAX)

        @pl.when(jnp.logical_not(trusted))
        def _():
            redo_block_exact()

    @pl.when(t == n_blocks)
    def _():
        drain()


def _attention(q, kc, kl, vc, vl, tq=512):
    b, s, _ = q.shape
    c = kc.shape[1]
    nq = s // tq
    n_blocks = b * nq
    pair_w = 2 * V_HEAD_DIM
    main_w = ATTN_WIDTH - pair_w
    last = MLA_HEADS - 1
    cur = lambda t: jnp.minimum(t, n_blocks - 1)
    prev = lambda t: jnp.maximum(t - 1, 0)
    return pl.pallas_call(
        _attn_kernel,
        grid=(n_blocks + 1,),
        in_specs=[
            pl.BlockSpec((1, tq, QK_WIDTH), lambda t: (cur(t) // nq, cur(t) % nq, 0)),
            pl.BlockSpec((1, c, QK_WIDTH), lambda t: (cur(t) // nq, 0, 0)),
            pl.BlockSpec((1, s, QK_WIDTH), lambda t: (cur(t) // nq, 0, 0)),
            pl.BlockSpec((1, c, QK_WIDTH), lambda t: (cur(t) // nq, 0, 0)),
            pl.BlockSpec((1, s, QK_WIDTH), lambda t: (cur(t) // nq, 0, 0)),
            pl.BlockSpec((1, c, HEAD_SLOT), lambda t: (prev(t) // nq, 0, last)),
            pl.BlockSpec((1, s, HEAD_SLOT), lambda t: (prev(t) // nq, 0, last)),
        ],
        out_specs=[
            pl.BlockSpec((1, tq, main_w), lambda t: (cur(t) // nq, cur(t) % nq, 0)),
            pl.BlockSpec((1, tq, pair_w), lambda t: (prev(t) // nq, prev(t) % nq, 0)),
        ],
        out_shape=[
            jax.ShapeDtypeStruct((b, s, main_w), BF16),
            jax.ShapeDtypeStruct((b, s, pair_w), BF16),
        ],
        scratch_shapes=[
            pltpu.VMEM((tq, c + s), BF16),
            pltpu.VMEM((tq, pair_w), F32),
            pltpu.VMEM((MLA_HEADS - 1, tq, HEAD_SLOT), F32),
        ],
        compiler_params=pltpu.CompilerParams(
            dimension_semantics=("arbitrary",), vmem_limit_bytes=VMEM_LIMIT_BYTES),
        name="attn",
    )(q, kc, kl, vc, vl, vc, vl)


def _fourier_kernel(t_ref, tw_ref, u_ref, cc_ref, wf_ref, o_ref, tb_ref, cw_ref):
    @pl.when(pl.program_id(0) == 0)
    def _():
        tb_ref[...] = t_ref[...].astype(BF16)
        cc = cc_ref[...].astype(BF16)
        for g in range(FOURIER_GROUPS):
            cw_ref[g] = _dot(cc, wf_ref[g]).astype(BF16)

    n_sub = tb_ref.shape[1]
    f = _dot(tb_ref[...], u_ref[0])
    gd = FOURIER_GROUP_DIM
    for g in range(FOURIER_GROUPS):
        gr, gi = [], []
        for r in range(FFT_RADIX):
            lo = r * FOURIER_WIDTH + g * gd
            a = f[:n_sub, lo:lo + gd]
            b = f[n_sub:, lo:lo + gd]
            if r == 0:
                gr.append(a)
                gi.append(b)
            else:
                c = tw_ref[r - 1, 0]
                s = tw_ref[r - 1, 1]
                gr.append(a * c - b * s)
                gi.append(a * s + b * c)
        ar, ai = gr[0] + gr[2], gi[0] + gi[2]
        br, bi = gr[0] - gr[2], gi[0] - gi[2]
        cr, ci = gr[1] + gr[3], gi[1] + gi[3]
        dr, di = gr[1] - gr[3], gi[1] - gi[3]
        xr = jnp.concatenate([ar + cr, br - di, ar - cr, br + di], axis=0)
        xi = jnp.concatenate([ai + ci, bi + dr, ai - ci, bi - dr], axis=0)
        lhs = jnp.concatenate([xr, xi], axis=1).astype(BF16)
        o_ref[0, :, g * gd:(g + 1) * gd] = _dot(lhs, cw_ref[g]).astype(BF16)


def _fourier(u4, tmat, tw, cmat, wf):
    b, n_sub, _ = u4.shape
    s = n_sub * FFT_RADIX
    full = lambda a: pl.BlockSpec(a.shape, lambda i: (0,) * a.ndim)
    return pl.pallas_call(
        _fourier_kernel,
        grid=(b,),
        in_specs=[
            full(tmat), full(tw),
            pl.BlockSpec((1, n_sub, FFT_RADIX * FOURIER_WIDTH), lambda i: (i, 0, 0)),
            full(cmat), full(wf),
        ],
        out_specs=pl.BlockSpec((1, s, FOURIER_WIDTH), lambda i: (i, 0, 0)),
        out_shape=jax.ShapeDtypeStruct((b, s, FOURIER_WIDTH), BF16),
        scratch_shapes=[
            pltpu.VMEM(tmat.shape, BF16),
            pltpu.VMEM((FOURIER_GROUPS, 2 * FOURIER_GROUP_DIM, FOURIER_GROUP_DIM), BF16),
        ],
        compiler_params=pltpu.CompilerParams(
            dimension_semantics=("arbitrary",), vmem_limit_bytes=VMEM_LIMIT_BYTES),
        name="fourier",
    )(tmat, tw, u4, cmat, wf)


def _stage_cast(src_hbm, dst_ref, stage_ref, sem_ref, chunk):
    n_chunks = src_hbm.shape[0] // chunk

    def copy(i):
        slot = i % 2
        return pltpu.make_async_copy(
            src_hbm.at[pl.ds(i * chunk, chunk)], stage_ref.at[slot, pl.ds(0, chunk)], sem_ref.at[slot])

    def body(i, carry):
        @pl.when(i + 1 < n_chunks)
        def _():
            copy(i + 1).start()

        copy(i).wait()
        start = pl.multiple_of(i * chunk, chunk)
        dst_ref[pl.ds(start, chunk), :] = stage_ref[i % 2, pl.ds(0, chunk), :].astype(BF16)
        return carry

    copy(0).start()
    lax.fori_loop(0, n_chunks, body, 0)


def _post_kernel(x_ref, four_ref, am_ref, al_ref, modc_ref, modp_ref, gpm_ref, gpf_ref, gqf_ref,
                 wo_hbm, wg_hbm, wu_hbm, wd_hbm, o_ref,
                 x1_ref, h2_ref, wo_ref, wg_ref, wu_ref, wd_ref, wide_stage, tall_stage, sem_ref):
    t = pl.program_id(0)
    n_four = four_ref.shape[2]

    def mix():
        gt_m = modc_ref[0, 2:3, :]
        sh_f = modc_ref[0, 3:4, :]
        sc_f = modc_ref[0, 4:5, :]
        attn = jnp.concatenate([am_ref[0], al_ref[0]], axis=1)
        y = _dot(four_ref[0], wo_ref[:n_four, :]) + _dot(attn, wo_ref[n_four:, :])
        x1 = x_ref[0] + _rms(y, gt_m * gpm_ref[...])
        x1_ref[...] = x1
        h2_ref[...] = (_rms(x1, gpf_ref[...] * (1.0 + sc_f)) + sh_f).astype(BF16)

    def ffn():
        gt_f = modp_ref[0, 5:6, :]
        h2 = h2_ref[...]
        g = _dot(h2, wg_ref[...])
        up = _dot(h2, wu_ref[...])
        act = (g / (1.0 + jnp.exp(-g)) * up).astype(BF16)
        o_ref[0] = x1_ref[...] + _rms(_dot(act, wd_ref[...]), gt_f * gqf_ref[...])

    @pl.when(t == 0)
    def _():
        _stage_cast(wo_hbm, wo_ref, tall_stage, sem_ref, STAGE_ROWS_OUT)
        mix()
        _stage_cast(wg_hbm, wg_ref, wide_stage, sem_ref, STAGE_ROWS_WIDE)
        _stage_cast(wu_hbm, wu_ref, wide_stage, sem_ref, STAGE_ROWS_WIDE)
        _stage_cast(wd_hbm, wd_ref, tall_stage, sem_ref, STAGE_ROWS_TALL)

    @pl.when(t > 0)
    def _():
        ffn()
        mix()


def _post(x, four, attn_main, attn_last, mod3, gpm, gpf, gqf, wo, wg, wu, wd, tm=512):
    b, s, d = x.shape
    d_ff = wg.shape[1]
    assert wo.shape[0] % STAGE_ROWS_OUT == 0 and STAGE_ROWS_OUT <= STAGE_ROWS_TALL
    assert d % STAGE_ROWS_WIDE == 0 and d_ff % STAGE_ROWS_TALL == 0
    nj = s // tm
    n_blocks = b * nj
    cur = lambda t: jnp.minimum(t, n_blocks - 1)
    prev = lambda t: jnp.maximum(t - 1, 0)
    const = lambda shape: pl.BlockSpec(shape, lambda t: (0,) * len(shape), pipeline_mode=pl.Buffered(1))
    rows = lambda w: pl.BlockSpec((1, tm, w), lambda t: (cur(t) // nj, cur(t) % nj, 0))
    hbm = pl.BlockSpec(memory_space=pl.ANY)
    return pl.pallas_call(
        _post_kernel,
        grid=(n_blocks + 1,),
        in_specs=[
            rows(d), rows(FOURIER_WIDTH), rows(attn_main.shape[2]), rows(attn_last.shape[2]),
            pl.BlockSpec((1, 6, d), lambda t: (cur(t) // nj, 0, 0)),
            pl.BlockSpec((1, 6, d), lambda t: (prev(t) // nj, 0, 0)),
            const(gpm.shape), const(gpf.shape), const(gqf.shape),
            hbm, hbm, hbm, hbm,
        ],
        out_specs=pl.BlockSpec((1, tm, d), lambda t: (prev(t) // nj, prev(t) % nj, 0)),
        out_shape=jax.ShapeDtypeStruct((b, s, d), F32),
        scratch_shapes=[
            pltpu.VMEM((tm, d), F32), pltpu.VMEM((tm, d), BF16),
            pltpu.VMEM(wo.shape, BF16), pltpu.VMEM(wg.shape, BF16), pltpu.VMEM(wu.shape, BF16),
            pltpu.VMEM(wd.shape, BF16),
            pltpu.VMEM((2, STAGE_ROWS_WIDE, d_ff), F32), pltpu.VMEM((2, STAGE_ROWS_TALL, d), F32),
            pltpu.SemaphoreType.DMA((2,)),
        ],
        compiler_params=pltpu.CompilerParams(
            dimension_semantics=("arbitrary",), vmem_limit_bytes=VMEM_LIMIT_BYTES),
        name="post",
    )(x, four, attn_main, attn_last, mod3, mod3, gpm, gpf, gqf, wo, wg, wu, wd)


def _rope_rotate_cols(w):
    a = QK_ROPE_DIM // 2
    hf = a // 2
    blocks = []
    for s0 in (0, a):
        blocks += [-w[..., s0 + hf:s0 + a], w[..., s0:s0 + hf]]
    return jnp.concatenate(blocks, axis=-1)


def _head_slot(nope, rope):
    pad = HEAD_SLOT - QK_NOPE_DIM - QK_ROPE_DIM
    return jnp.concatenate([nope, rope, jnp.zeros(rope.shape[:-1] + (pad,), rope.dtype)], axis=-1)


def _rope_tables(n_lat, q_scale):
    t = np.arange(n_lat)
    hf = QK_ROPE_DIM // 4
    inv_freq = ROPE_BASE ** (-np.arange(hf, dtype=np.float64) / hf)
    ar = (t // GRID_W)[:, None] * inv_freq[None, :]
    ac = (t % GRID_W)[:, None] * inv_freq[None, :]
    z = np.zeros_like(ar)
    cos32 = np.concatenate([np.cos(ar), np.cos(ar), np.cos(ac), np.cos(ac)], axis=-1)
    sin32 = np.concatenate([np.sin(ar), np.sin(ar), np.sin(ac), np.sin(ac)], axis=-1)
    below32 = np.concatenate([z, np.sin(ar), z, np.sin(ac)], axis=-1)
    above32 = np.concatenate([-np.sin(ar), z, -np.sin(ac), z], axis=-1)
    pad = np.zeros((n_lat, HEAD_SLOT - QK_NOPE_DIM - QK_ROPE_DIM))
    ones = np.ones((n_lat, QK_NOPE_DIM))
    zeros = np.zeros((n_lat, QK_NOPE_DIM))
    slot = lambda nope, rope: np.concatenate([nope, rope, pad], axis=-1)
    tables = [slot(ones * q_scale, cos32 * q_scale), slot(zeros, sin32 * q_scale),
              slot(zeros, cos32), slot(zeros, below32), slot(zeros, above32)]
    return jnp.asarray(np.stack(tables).astype(np.float32))


def _dft_tables(n_pos, n_ch):
    n_sub = n_pos // FFT_RADIX
    m = np.arange(n_sub, dtype=np.int64)
    ang = 2.0 * np.pi * ((m[:, None] * m[None, :]) % n_sub) / n_sub
    tmat = np.concatenate([np.cos(ang), np.sin(ang)], axis=0).astype(np.float32)
    tw = np.zeros((FFT_RADIX - 1, 2, n_sub, FOURIER_GROUP_DIM), np.float32)
    for r in range(1, FFT_RADIX):
        a = 2.0 * np.pi * r * m / n_pos
        tw[r - 1, 0] = np.cos(a)[:, None]
        tw[r - 1, 1] = np.sin(a)[:, None]
    c = np.arange(n_ch, dtype=np.int64)
    angc = 2.0 * np.pi * ((c[:, None] * c[None, :]) % n_ch) / n_ch
    norm = 1.0 / np.sqrt(float(n_pos * n_ch))
    cmat = np.concatenate([np.cos(angc) * norm, -np.sin(angc) * norm], axis=0).astype(np.float32)
    return jnp.asarray(tmat), jnp.asarray(tw), jnp.asarray(cmat)


def kernel(x, c, ctx, c_ctx, w_ada, b_ada, g_pre_mix, g_post_mix, g_pre_ffn, g_post_ffn, w_in, g_q_a,
           w_q_b, g_kv_a, w_kv_b, w_fourier, w_out, w_gate, w_up, w_down):
    assert w_ada.shape[0] == 1, "single-layer block"
    batch, n_lat, d = x.shape

    mod_rows = -(-(batch + 1) // 8) * 8
    cc = jnp.concatenate([c, c_ctx[None, :], jnp.zeros((mod_rows - batch - 1, d), F32)], axis=0)
    mod = _adaln(cc, w_ada[0], b_ada[0][None, :])
    mod3 = mod.reshape(mod_rows, 6, d)

    w_in0 = w_in[0]
    w_kr = w_in0[:, ROPE_COL:]
    zeros_d = jnp.zeros((d, QK_NOPE_DIM), F32)
    kr_slot = _head_slot(zeros_d, w_kr)
    win = jnp.concatenate([w_in0[:, :ROPE_COL], kr_slot], axis=1).astype(BF16)
    win_c = jnp.concatenate([w_in0[:, KV_COL:ROPE_COL], kr_slot], axis=1).astype(BF16)

    wq3 = w_q_b[0].reshape(Q_LORA_RANK, MLA_HEADS, QK_NOPE_DIM + QK_ROPE_DIM)
    wq_nope, wq_rope = wq3[..., :QK_NOPE_DIM], wq3[..., QK_NOPE_DIM:]
    wq_a = _head_slot(wq_nope, wq_rope).reshape(Q_LORA_RANK, QK_WIDTH)
    wq_b = _head_slot(jnp.zeros_like(wq_nope), _rope_rotate_cols(wq_rope)).reshape(Q_LORA_RANK, QK_WIDTH)
    wq = jnp.concatenate([wq_a, wq_b], axis=1).astype(BF16)

    wkv3 = w_kv_b[0].reshape(KV_LORA_RANK, MLA_HEADS, QK_NOPE_DIM + V_HEAD_DIM)
    wk_nope, wv = wkv3[..., :QK_NOPE_DIM], wkv3[..., QK_NOPE_DIM:]
    wk_slots = _head_slot(wk_nope, jnp.zeros(wk_nope.shape[:-1] + (QK_ROPE_DIM,), F32))
    wkv = jnp.concatenate([wk_slots.reshape(KV_LORA_RANK, QK_WIDTH),
                           wv.reshape(KV_LORA_RANK, ATTN_WIDTH)], axis=1).astype(BF16)

    q_scale = float((QK_NOPE_DIM + QK_ROPE_DIM) ** -0.5 * np.log2(np.e))
    rope = _rope_tables(n_lat, q_scale)
    tmat, tw, cmat = _dft_tables(n_lat, FOURIER_GROUP_DIM)

    row2 = lambda g: g[0][None, :]
    u_f, q, k_lat, v_lat = _premix(x, mod3, row2(g_pre_mix), win, row2(g_q_a), wq, row2(g_kv_a), wkv,
                                   rope)
    k_ctx, v_ctx = _ctxkv(ctx, mod3, batch, row2(g_pre_mix), win_c, row2(g_kv_a), wkv)
    attn_main, attn_last = _attention(q, k_ctx, k_lat, v_ctx, v_lat)
    four = _fourier(u_f, tmat, tw, cmat, w_fourier[0].astype(BF16))
    return _post(x, four, attn_main, attn_last, mod3, row2(g_post_mix), row2(g_pre_ffn), row2(g_post_ffn),
                 w_out[0], w_gate[0], w_up[0], w_down[0])
```

```python
import functools

import numpy as np
import jax
import jax.numpy as jnp
from jax import lax
from jax.experimental import pallas as pl
from jax.experimental.pallas import tpu as pltpu

F32 = jnp.float32
BF16 = jnp.bfloat16

D_MODEL = 1024
GRID_W = 64
FOURIER_GROUPS = 4
FOURIER_GROUP_DIM = 128
FOURIER_WIDTH = FOURIER_GROUPS * FOURIER_GROUP_DIM
MLA_HEADS = 8
QK_NOPE_DIM = 64
QK_ROPE_DIM = 32
V_HEAD_DIM = 64
Q_LORA_RANK = 256
KV_LORA_RANK = 128
KV_COL = FOURIER_WIDTH + Q_LORA_RANK
ROPE_COL = KV_COL + KV_LORA_RANK
ROPE_BASE = 10000.0
NORM_EPS = 1e-6
FFT_RADIX = 4
HEAD_SLOT = 128
STAGE_ROWS_WIDE = 128
STAGE_ROWS_TALL = 352
STAGE_ROWS_OUT = 256
SHIFT_DEN_MIN = 2.0 ** -40
SHIFT_DEN_MAX = 2.0 ** 40
PV_KEY_TILE = 256
ATTN_WIDTH = MLA_HEADS * V_HEAD_DIM
QK_WIDTH = MLA_HEADS * HEAD_SLOT

VMEM_LIMIT_BYTES = 56 * 1024 * 1024


def _rms(x, g):
    return x * lax.rsqrt(jnp.mean(x * x, axis=-1, keepdims=True) + NORM_EPS) * g


def _dot(a, b):
    return jnp.dot(a, b, preferred_element_type=F32)


def _dot_nt(a, b):
    return lax.dot_general(a, b, (((1,), (1,)), ((), ())), preferred_element_type=F32)


def _rotary(a, table, first):
    half = QK_ROPE_DIM // 4
    below = pltpu.roll(a, half, axis=1)
    above = pltpu.roll(a, HEAD_SLOT - half, axis=1)
    return a * table(first) + below * table(first + 1) + above * table(first + 2)


def _store_value_slots(v_ref, v):
    pair_w = 2 * V_HEAD_DIM
    lower = lax.broadcasted_iota(jnp.int32, (v.shape[0], pair_w), 1) < V_HEAD_DIM
    for pair in range(MLA_HEADS // 2):
        vp = v[:, pair * pair_w:(pair + 1) * pair_w]
        lo = 2 * pair * HEAD_SLOT
        v_ref[0, :, lo:lo + HEAD_SLOT] = jnp.where(lower, vp, 1.0).astype(BF16)
        v_ref[0, :, lo + HEAD_SLOT:lo + 2 * HEAD_SLOT] = jnp.where(lower, 1.0, vp).astype(BF16)


def _adaln_kernel(c_ref, w_ref, b_ref, o_ref):
    c = c_ref[...]
    a = c / (1.0 + jnp.exp(-c))
    o_ref[...] = _dot(a.astype(BF16), w_ref[...].astype(BF16)) + b_ref[...]


def _adaln(cc, w_ada, b_ada, tn=1024):
    rows, d = cc.shape
    n = w_ada.shape[1]
    return pl.pallas_call(
        _adaln_kernel,
        grid=(n // tn,),
        in_specs=[
            pl.BlockSpec((rows, d), lambda j: (0, 0)),
            pl.BlockSpec((d, tn), lambda j: (0, j)),
            pl.BlockSpec((1, tn), lambda j: (0, j)),
        ],
        out_specs=pl.BlockSpec((rows, tn), lambda j: (0, j)),
        out_shape=jax.ShapeDtypeStruct((rows, n), F32),
        name="adaln",
    )(cc, w_ada, b_ada)


def _premix_kernel(x_ref, mod_ref, gpre_ref, win_ref, gq_ref, wq_ref, gkv_ref, wkv_ref,
                   rope_ref, u_ref, q_ref, k_ref, v_ref, us_ref):
    x = x_ref[0]
    shift = mod_ref[0, 0:1, :]
    scale = mod_ref[0, 1:2, :]
    h = _rms(x, gpre_ref[...] * (1.0 + scale)) + shift
    p = _dot(h.astype(BF16), win_ref[...])
    sub = us_ref.shape[1] // FFT_RADIX
    for g in range(FOURIER_GROUPS):
        glo = g * FOURIER_GROUP_DIM
        us_ref[g] = p[:, glo:glo + FOURIER_GROUP_DIM]
        for r in range(FFT_RADIX):
            lo = r * FOURIER_WIDTH + glo
            u_ref[0, :, lo:lo + FOURIER_GROUP_DIM] = (
                us_ref[g, pl.ds(r, sub, stride=FFT_RADIX), :].astype(BF16))

    qn = _rms(p[:, FOURIER_WIDTH:KV_COL], gq_ref[...]).astype(BF16)
    qq = _dot(qn, wq_ref[...])
    rows = x_ref.shape[1]
    row0 = pl.multiple_of(pl.program_id(1) * rows, rows)
    table = lambda k: rope_ref[k, pl.ds(row0, rows), :]
    cosq = table(0)
    sinq = table(1)
    for hd in range(MLA_HEADS):
        lo = hd * HEAD_SLOT
        q_ref[0, :, lo:lo + HEAD_SLOT] = (
            qq[:, lo:lo + HEAD_SLOT] * cosq + qq[:, QK_WIDTH + lo:QK_WIDTH + lo + HEAD_SLOT] * sinq
        ).astype(BF16)

    kvn = _rms(p[:, KV_COL:ROPE_COL], gkv_ref[...]).astype(BF16)
    kv = _dot(kvn, wkv_ref[...])
    kr = _rotary(p[:, ROPE_COL:ROPE_COL + HEAD_SLOT], table, 2)
    for hd in range(MLA_HEADS):
        lo = hd * HEAD_SLOT
        k_ref[0, :, lo:lo + HEAD_SLOT] = (kv[:, lo:lo + HEAD_SLOT] + kr).astype(BF16)
    _store_value_slots(v_ref, kv[:, QK_WIDTH:])


def _premix(x, mod3, gpre, win, gq, wq, gkv, wkv, rope, tm=512):
    b, s, d = x.shape
    const = lambda shape: pl.BlockSpec(shape, lambda i, j: (0,) * len(shape))
    rows = lambda w: pl.BlockSpec((1, tm, w), lambda i, j: (i, j, 0))
    return pl.pallas_call(
        _premix_kernel,
        grid=(b, s // tm),
        in_specs=[
            rows(d),
            pl.BlockSpec((1, 6, d), lambda i, j: (i, 0, 0)),
            const(gpre.shape), const(win.shape), const(gq.shape), const(wq.shape),
            const(gkv.shape), const(wkv.shape),
            pl.BlockSpec(rope.shape, lambda i, j: (0, 0, 0), pipeline_mode=pl.Buffered(1)),
        ],
        out_specs=[
            pl.BlockSpec((1, tm // FFT_RADIX, FFT_RADIX * FOURIER_WIDTH), lambda i, j: (i, j, 0)),
            rows(QK_WIDTH), rows(QK_WIDTH), rows(QK_WIDTH)],
        out_shape=[
            jax.ShapeDtypeStruct((b, s // FFT_RADIX, FFT_RADIX * FOURIER_WIDTH), BF16),
            jax.ShapeDtypeStruct((b, s, QK_WIDTH), BF16),
            jax.ShapeDtypeStruct((b, s, QK_WIDTH), BF16),
            jax.ShapeDtypeStruct((b, s, QK_WIDTH), BF16),
        ],
        scratch_shapes=[pltpu.VMEM((FOURIER_GROUPS, tm, FOURIER_GROUP_DIM), F32)],
        compiler_params=pltpu.CompilerParams(vmem_limit_bytes=VMEM_LIMIT_BYTES),
        name="premix",
    )(x, mod3, gpre, win, gq, wq, gkv, wkv, rope)


def _ctxkv_kernel(x_ref, mod_ref, gpre_ref, win_ref, gkv_ref, wkv_ref, k_ref, v_ref):
    x = x_ref[0]
    shift = mod_ref[0, 0:1, :]
    scale = mod_ref[0, 1:2, :]
    h = _rms(x, gpre_ref[...] * (1.0 + scale)) + shift
    p = _dot(h.astype(BF16), win_ref[...])
    kvn = _rms(p[:, :KV_LORA_RANK], gkv_ref[...]).astype(BF16)
    kv = _dot(kvn, wkv_ref[...])
    kr = p[:, KV_LORA_RANK:]
    for hd in range(MLA_HEADS):
        lo = hd * HEAD_SLOT
        k_ref[0, :, lo:lo + HEAD_SLOT] = (kv[:, lo:lo + HEAD_SLOT] + kr).astype(BF16)
    _store_value_slots(v_ref, kv[:, QK_WIDTH:])


def _ctxkv(ctx, mod3, ctx_row, gpre, win_c, gkv, wkv):
    b, c, d = ctx.shape
    const = lambda shape: pl.BlockSpec(shape, lambda i: (0,) * len(shape))
    rows = lambda w: pl.BlockSpec((1, c, w), lambda i: (i, 0, 0))
    return pl.pallas_call(
        _ctxkv_kernel,
        grid=(b,),
        in_specs=[
            rows(d),
            pl.BlockSpec((1, 6, d), lambda i: (ctx_row, 0, 0)),
            const(gpre.shape), const(win_c.shape), const(gkv.shape), const(wkv.shape),
        ],
        out_specs=[rows(QK_WIDTH), rows(QK_WIDTH)],
        out_shape=[
            jax.ShapeDtypeStruct((b, c, QK_WIDTH), BF16),
            jax.ShapeDtypeStruct((b, c, QK_WIDTH), BF16),
        ],
        name="ctxkv",
    )(ctx, mod3, gpre, win_c, gkv, wkv)


def _attn_kernel(q_ref, kc_ref, kl_ref, vc_ref, vl_ref, vcp_ref, vlp_ref, omain_ref, olast_ref,
                 p_ref, oprev_ref, ohead_ref):
    t = pl.program_id(0)
    n_blocks = pl.num_programs(0) - 1
    n_ctx = kc_ref.shape[1]
    n_lat = kl_ref.shape[1]
    tq = q_ref.shape[1]
    pair_w = 2 * V_HEAD_DIM
    first_half = lax.broadcasted_iota(jnp.int32, (tq, pair_w), 1) < V_HEAD_DIM
    last = MLA_HEADS - 1

    @pl.when(t == 0)
    def _():
        p_ref[...] = jnp.ones_like(p_ref)
        oprev_ref[...] = jnp.ones_like(oprev_ref)

    def join_pair(o_even, o_odd):
        num = jnp.where(first_half, o_even, o_odd)
        den = pltpu.roll(jnp.where(first_half, o_odd, o_even), V_HEAD_DIM, axis=1)
        return num / den

    def den_of(o, hd):
        lane = V_HEAD_DIM if hd % 2 == 0 else 0
        return o[:, lane:lane + 1]

    def drain():
        o = _dot(p_ref[:, :n_ctx], vcp_ref[0]) + _dot(p_ref[:, n_ctx:], vlp_ref[0])
        olast_ref[0] = join_pair(oprev_ref[...], o).astype(BF16)

    def head(hd, exact):
        lo = hd * HEAD_SLOT
        qh = q_ref[0, :, lo:lo + HEAD_SLOT]
        s_c = _dot_nt(qh, kc_ref[0, :, lo:lo + HEAD_SLOT])
        s_l = _dot_nt(qh, kl_ref[0, :, lo:lo + HEAD_SLOT])
        shift = jnp.max(s_c, axis=-1, keepdims=True)
        if exact:
            shift = jnp.maximum(shift, jnp.max(s_l, axis=-1, keepdims=True))
        if hd == last:
            p_c = jnp.exp2(s_c - shift)
            p_l = jnp.exp2(s_l - shift)
            p_ref[:, :n_ctx] = p_c.astype(BF16)
            p_ref[:, n_ctx:] = p_l.astype(BF16)
            return None, jnp.sum(p_c, axis=-1, keepdims=True) + jnp.sum(p_l, axis=-1, keepdims=True)
        o = _dot(jnp.exp2(s_c - shift).astype(BF16), vc_ref[0, :, lo:lo + HEAD_SLOT])
        for k0 in range(0, n_lat, PV_KEY_TILE):
            p = jnp.exp2(s_l[:, k0:k0 + PV_KEY_TILE] - shift).astype(BF16)
            o = o + _dot(p, vl_ref[0, k0:k0 + PV_KEY_TILE, lo:lo + HEAD_SLOT])
        return o, den_of(o, hd)

    def block():
        dens = []
        for pair in range(MLA_HEADS // 2):
            outs = []
            for hd in (2 * pair, 2 * pair + 1):
                o, den = head(hd, False)
                dens.append(den)
                if o is not None:
                    outs.append(o)
            if len(outs) == 2:
                vlo = pair * pair_w
                omain_ref[0, :, vlo:vlo + pair_w] = join_pair(outs[0], outs[1]).astype(BF16)
            else:
                oprev_ref[...] = outs[0]
        return dens

    def redo_block_exact():
        def one_head(hd, carry):
            lo = pl.multiple_of(hd * HEAD_SLOT, HEAD_SLOT)
            qh = q_ref[0, :, pl.ds(lo, HEAD_SLOT)]
            s_c = _dot_nt(qh, kc_ref[0, :, pl.ds(lo, HEAD_SLOT)])
            s_l = _dot_nt(qh, kl_ref[0, :, pl.ds(lo, HEAD_SLOT)])
            shift = jnp.maximum(jnp.max(s_c, axis=-1, keepdims=True), jnp.max(s_l, axis=-1, keepdims=True))
            ohead_ref[hd] = (_dot(jnp.exp2(s_c - shift).astype(BF16), vc_ref[0, :, pl.ds(lo, HEAD_SLOT)])
                             + _dot(jnp.exp2(s_l - shift).astype(BF16), vl_ref[0, :, pl.ds(lo, HEAD_SLOT)]))
            return carry

        lax.fori_loop(0, last, one_head, 0)
        for pair in range(MLA_HEADS // 2 - 1):
            vlo = pair * pair_w
            omain_ref[0, :, vlo:vlo + pair_w] = join_pair(ohead_ref[2 * pair], ohead_ref[2 * pair + 1]).astype(BF16)
        oprev_ref[...] = ohead_ref[last - 1]
        head(last, True)

    @pl.when(t < n_blocks)
    def _():
        drain()
        dens = block()
        lo_den, hi_den = dens[0], dens[0]
        for den in dens[1:]:
            lo_den = jnp.minimum(lo_den, den)
            hi_den = jnp.maximum(hi_den, den)
        trusted = jnp.logical_and(jnp.min(lo_den) >= SHIFT_DEN_MIN, jnp.max(hi_den) <= SHIFT_DEN_MAX)

        @pl.when(jnp.logical_not(trusted))
        def _():
            redo_block_exact()

    @pl.when(t == n_blocks)
    def _():
        drain()


def _attention(q, kc, kl, vc, vl, tq=512):
    b, s, _ = q.shape
    c = kc.shape[1]
    nq = s // tq
    n_blocks = b * nq
    pair_w = 2 * V_HEAD_DIM
    main_w = ATTN_WIDTH - pair_w
    last = MLA_HEADS - 1
    cur = lambda t: jnp.minimum(t, n_blocks - 1)
    prev = lambda t: jnp.maximum(t - 1, 0)
    return pl.pallas_call(
        _attn_kernel,
        grid=(n_blocks + 1,),
        in_specs=[
            pl.BlockSpec((1, tq, QK_WIDTH), lambda t: (cur(t) // nq, cur(t) % nq, 0)),
            pl.BlockSpec((1, c, QK_WIDTH), lambda t: (cur(t) // nq, 0, 0)),
            pl.BlockSpec((1, s, QK_WIDTH), lambda t: (cur(t) // nq, 0, 0)),
            pl.BlockSpec((1, c, QK_WIDTH), lambda t: (cur(t) // nq, 0, 0)),
            pl.BlockSpec((1, s, QK_WIDTH), lambda t: (cur(t) // nq, 0, 0)),
            pl.BlockSpec((1, c, HEAD_SLOT), lambda t: (prev(t) // nq, 0, last)),
            pl.BlockSpec((1, s, HEAD_SLOT), lambda t: (prev(t) // nq, 0, last)),
        ],
        out_specs=[
            pl.BlockSpec((1, tq, main_w), lambda t: (cur(t) // nq, cur(t) % nq, 0)),
            pl.BlockSpec((1, tq, pair_w), lambda t: (prev(t) // nq, prev(t) % nq, 0)),
        ],
        out_shape=[
            jax.ShapeDtypeStruct((b, s, main_w), BF16),
            jax.ShapeDtypeStruct((b, s, pair_w), BF16),
        ],
        scratch_shapes=[
            pltpu.VMEM((tq, c + s), BF16),
            pltpu.VMEM((tq, pair_w), F32),
            pltpu.VMEM((MLA_HEADS - 1, tq, HEAD_SLOT), F32),
        ],
        compiler_params=pltpu.CompilerParams(
            dimension_semantics=("arbitrary",), vmem_limit_bytes=VMEM_LIMIT_BYTES),
        name="attn",
    )(q, kc, kl, vc, vl, vc, vl)


def _fourier_kernel(t_ref, tw_ref, u_ref, cc_ref, wf_ref, o_ref, tb_ref, cw_ref):
    @pl.when(pl.program_id(0) == 0)
    def _():
        tb_ref[...] = t_ref[...].astype(BF16)
        cc = cc_ref[...].astype(BF16)
        for g in range(FOURIER_GROUPS):
            cw_ref[g] = _dot(cc, wf_ref[g]).astype(BF16)

    n_sub = tb_ref.shape[1]
    f = _dot(tb_ref[...], u_ref[0])
    gd = FOURIER_GROUP_DIM
    for g in range(FOURIER_GROUPS):
        gr, gi = [], []
        for r in range(FFT_RADIX):
            lo = r * FOURIER_WIDTH + g * gd
            a = f[:n_sub, lo:lo + gd]
            b = f[n_sub:, lo:lo + gd]
            if r == 0:
                gr.append(a)
                gi.append(b)
            else:
                c = tw_ref[r - 1, 0]
                s = tw_ref[r - 1, 1]
                gr.append(a * c - b * s)
                gi.append(a * s + b * c)
        ar, ai = gr[0] + gr[2], gi[0] + gi[2]
        br, bi = gr[0] - gr[2], gi[0] - gi[2]
        cr, ci = gr[1] + gr[3], gi[1] + gi[3]
        dr, di = gr[1] - gr[3], gi[1] - gi[3]
        xr = jnp.concatenate([ar + cr, br - di, ar - cr, br + di], axis=0)
        xi = jnp.concatenate([ai + ci, bi + dr, ai - ci, bi - dr], axis=0)
        lhs = jnp.concatenate([xr, xi], axis=1).astype(BF16)
        o_ref[0, :, g * gd:(g + 1) * gd] = _dot(lhs, cw_ref[g]).astype(BF16)


def _fourier(u4, tmat, tw, cmat, wf):
    b, n_sub, _ = u4.shape
    s = n_sub * FFT_RADIX
    full = lambda a: pl.BlockSpec(a.shape, lambda i: (0,) * a.ndim)
    return pl.pallas_call(
        _fourier_kernel,
        grid=(b,),
        in_specs=[
            full(tmat), full(tw),
            pl.BlockSpec((1, n_sub, FFT_RADIX * FOURIER_WIDTH), lambda i: (i, 0, 0)),
            full(cmat), full(wf),
        ],
        out_specs=pl.BlockSpec((1, s, FOURIER_WIDTH), lambda i: (i, 0, 0)),
        out_shape=jax.ShapeDtypeStruct((b, s, FOURIER_WIDTH), BF16),
        scratch_shapes=[
            pltpu.VMEM(tmat.shape, BF16),
            pltpu.VMEM((FOURIER_GROUPS, 2 * FOURIER_GROUP_DIM, FOURIER_GROUP_DIM), BF16),
        ],
        compiler_params=pltpu.CompilerParams(
            dimension_semantics=("arbitrary",), vmem_limit_bytes=VMEM_LIMIT_BYTES),
        name="fourier",
    )(tmat, tw, u4, cmat, wf)


def _stage_cast(src_hbm, dst_ref, stage_ref, sem_ref, chunk):
    n_chunks = src_hbm.shape[0] // chunk

    def copy(i):
        slot = i % 2
        return pltpu.make_async_copy(
            src_hbm.at[pl.ds(i * chunk, chunk)], stage_ref.at[slot, pl.ds(0, chunk)], sem_ref.at[slot])

    def body(i, carry):
        @pl.when(i + 1 < n_chunks)
        def _():
            copy(i + 1).start()

        copy(i).wait()
        start = pl.multiple_of(i * chunk, chunk)
        dst_ref[pl.ds(start, chunk), :] = stage_ref[i % 2, pl.ds(0, chunk), :].astype(BF16)
        return carry

    copy(0).start()
    lax.fori_loop(0, n_chunks, body, 0)


def _post_kernel(x_ref, four_ref, am_ref, al_ref, modc_ref, modp_ref, gpm_ref, gpf_ref, gqf_ref,
                 wo_hbm, wg_hbm, wu_hbm, wd_hbm, o_ref,
                 x1_ref, h2_ref, wo_ref, wg_ref, wu_ref, wd_ref, wide_stage, tall_stage, sem_ref):
    t = pl.program_id(0)
    n_four = four_ref.shape[2]

    def mix():
        gt_m = modc_ref[0, 2:3, :]
        sh_f = modc_ref[0, 3:4, :]
        sc_f = modc_ref[0, 4:5, :]
        attn = jnp.concatenate([am_ref[0], al_ref[0]], axis=1)
        y = _dot(four_ref[0], wo_ref[:n_four, :]) + _dot(attn, wo_ref[n_four:, :])
        x1 = x_ref[0] + _rms(y, gt_m * gpm_ref[...])
        x1_ref[...] = x1
        h2_ref[...] = (_rms(x1, gpf_ref[...] * (1.0 + sc_f)) + sh_f).astype(BF16)

    def ffn():
        gt_f = modp_ref[0, 5:6, :]
        h2 = h2_ref[...]
        g = _dot(h2, wg_ref[...])
        up = _dot(h2, wu_ref[...])
        act = (g / (1.0 + jnp.exp(-g)) * up).astype(BF16)
        o_ref[0] = x1_ref[...] + _rms(_dot(act, wd_ref[...]), gt_f * gqf_ref[...])

    @pl.when(t == 0)
    def _():
        _stage_cast(wo_hbm, wo_ref, tall_stage, sem_ref, STAGE_ROWS_OUT)
        mix()
        _stage_cast(wg_hbm, wg_ref, wide_stage, sem_ref, STAGE_ROWS_WIDE)
        _stage_cast(wu_hbm, wu_ref, wide_stage, sem_ref, STAGE_ROWS_WIDE)
        _stage_cast(wd_hbm, wd_ref, tall_stage, sem_ref, STAGE_ROWS_TALL)

    @pl.when(t > 0)
    def _():
        ffn()
        mix()


def _post(x, four, attn_main, attn_last, mod3, gpm, gpf, gqf, wo, wg, wu, wd, tm=512):
    b, s, d = x.shape
    d_ff = wg.shape[1]
    assert wo.shape[0] % STAGE_ROWS_OUT == 0 and STAGE_ROWS_OUT <= STAGE_ROWS_TALL
    assert d % STAGE_ROWS_WIDE == 0 and d_ff % STAGE_ROWS_TALL == 0
    nj = s // tm
    n_blocks = b * nj
    cur = lambda t: jnp.minimum(t, n_blocks - 1)
    prev = lambda t: jnp.maximum(t - 1, 0)
    const = lambda shape: pl.BlockSpec(shape, lambda t: (0,) * len(shape), pipeline_mode=pl.Buffered(1))
    rows = lambda w: pl.BlockSpec((1, tm, w), lambda t: (cur(t) // nj, cur(t) % nj, 0))
    hbm = pl.BlockSpec(memory_space=pl.ANY)
    return pl.pallas_call(
        _post_kernel,
        grid=(n_blocks + 1,),
        in_specs=[
            rows(d), rows(FOURIER_WIDTH), rows(attn_main.shape[2]), rows(attn_last.shape[2]),
            pl.BlockSpec((1, 6, d), lambda t: (cur(t) // nj, 0, 0)),
            pl.BlockSpec((1, 6, d), lambda t: (prev(t) // nj, 0, 0)),
            const(gpm.shape), const(gpf.shape), const(gqf.shape),
            hbm, hbm, hbm, hbm,
        ],
        out_specs=pl.BlockSpec((1, tm, d), lambda t: (prev(t) // nj, prev(t) % nj, 0)),
        out_shape=jax.ShapeDtypeStruct((b, s, d), F32),
        scratch_shapes=[
            pltpu.VMEM((tm, d), F32), pltpu.VMEM((tm, d), BF16),
            pltpu.VMEM(wo.shape, BF16), pltpu.VMEM(wg.shape, BF16), pltpu.VMEM(wu.shape, BF16),
            pltpu.VMEM(wd.shape, BF16),
            pltpu.VMEM((2, STAGE_ROWS_WIDE, d_ff), F32), pltpu.VMEM((2, STAGE_ROWS_TALL, d), F32),
            pltpu.SemaphoreType.DMA((2,)),
        ],
        compiler_params=pltpu.CompilerParams(
            dimension_semantics=("arbitrary",), vmem_limit_bytes=VMEM_LIMIT_BYTES),
        name="post",
    )(x, four, attn_main, attn_last, mod3, mod3, gpm, gpf, gqf, wo, wg, wu, wd)


def _rope_rotate_cols(w):
    a = QK_ROPE_DIM // 2
    hf = a // 2
    blocks = []
    for s0 in (0, a):
        blocks += [-w[..., s0 + hf:s0 + a], w[..., s0:s0 + hf]]
    return jnp.concatenate(blocks, axis=-1)


def _head_slot(nope, rope):
    pad = HEAD_SLOT - QK_NOPE_DIM - QK_ROPE_DIM
    return jnp.concatenate([nope, rope, jnp.zeros(rope.shape[:-1] + (pad,), rope.dtype)], axis=-1)


def _rope_tables(n_lat, q_scale):
    t = np.arange(n_lat)
    hf = QK_ROPE_DIM // 4
    inv_freq = ROPE_BASE ** (-np.arange(hf, dtype=np.float64) / hf)
    ar = (t // GRID_W)[:, None] * inv_freq[None, :]
    ac = (t % GRID_W)[:, None] * inv_freq[None, :]
    z = np.zeros_like(ar)
    cos32 = np.concatenate([np.cos(ar), np.cos(ar), np.cos(ac), np.cos(ac)], axis=-1)
    sin32 = np.concatenate([np.sin(ar), np.sin(ar), np.sin(ac), np.sin(ac)], axis=-1)
    below32 = np.concatenate([z, np.sin(ar), z, np.sin(ac)], axis=-1)
    above32 = np.concatenate([-np.sin(ar), z, -np.sin(ac), z], axis=-1)
    pad = np.zeros((n_lat, HEAD_SLOT - QK_NOPE_DIM - QK_ROPE_DIM))
    ones = np.ones((n_lat, QK_NOPE_DIM))
    zeros = np.zeros((n_lat, QK_NOPE_DIM))
    slot = lambda nope, rope: np.concatenate([nope, rope, pad], axis=-1)
    tables = [slot(ones * q_scale, cos32 * q_scale), slot(zeros, sin32 * q_scale),
              slot(zeros, cos32), slot(zeros, below32), slot(zeros, above32)]
    return jnp.asarray(np.stack(tables).astype(np.float32))


def _dft_tables(n_pos, n_ch):
    n_sub = n_pos // FFT_RADIX
    m = np.arange(n_sub, dtype=np.int64)
    ang = 2.0 * np.pi * ((m[:, None] * m[None, :]) % n_sub) / n_sub
    tmat = np.concatenate([np.cos(ang), np.sin(ang)], axis=0).astype(np.float32)
    tw = np.zeros((FFT_RADIX - 1, 2, n_sub, FOURIER_GROUP_DIM), np.float32)
    for r in range(1, FFT_RADIX):
        a = 2.0 * np.pi * r * m / n_pos
        tw[r - 1, 0] = np.cos(a)[:, None]
        tw[r - 1, 1] = np.sin(a)[:, None]
    c = np.arange(n_ch, dtype=np.int64)
    angc = 2.0 * np.pi * ((c[:, None] * c[None, :]) % n_ch) / n_ch
    norm = 1.0 / np.sqrt(float(n_pos * n_ch))
    cmat = np.concatenate([np.cos(angc) * norm, -np.sin(angc) * norm], axis=0).astype(np.float32)
    return jnp.asarray(tmat), jnp.asarray(tw), jnp.asarray(cmat)


def kernel(x, c, ctx, c_ctx, w_ada, b_ada, g_pre_mix, g_post_mix, g_pre_ffn, g_post_ffn, w_in, g_q_a,
           w_q_b, g_kv_a, w_kv_b, w_fourier, w_out, w_gate, w_up, w_down):
    assert w_ada.shape[0] == 1, "single-layer block"
    batch, n_lat, d = x.shape

    mod_rows = -(-(batch + 1) // 8) * 8
    cc = jnp.concatenate([c, c_ctx[None, :], jnp.zeros((mod_rows - batch - 1, d), F32)], axis=0)
    mod = _adaln(cc, w_ada[0], b_ada[0][None, :])
    mod3 = mod.reshape(mod_rows, 6, d)

    w_in0 = w_in[0]
    w_kr = w_in0[:, ROPE_COL:]
    zeros_d = jnp.zeros((d, QK_NOPE_DIM), F32)
    kr_slot = _head_slot(zeros_d, w_kr)
    win = jnp.concatenate([w_in0[:, :ROPE_COL], kr_slot], axis=1).astype(BF16)
    win_c = jnp.concatenate([w_in0[:, KV_COL:ROPE_COL], kr_slot], axis=1).astype(BF16)

    wq3 = w_q_b[0].reshape(Q_LORA_RANK, MLA_HEADS, QK_NOPE_DIM + QK_ROPE_DIM)
    wq_nope, wq_rope = wq3[..., :QK_NOPE_DIM], wq3[..., QK_NOPE_DIM:]
    wq_a = _head_slot(wq_nope, wq_rope).reshape(Q_LORA_RANK, QK_WIDTH)
    wq_b = _head_slot(jnp.zeros_like(wq_nope), _rope_rotate_cols(wq_rope)).reshape(Q_LORA_RANK, QK_WIDTH)
    wq = jnp.concatenate([wq_a, wq_b], axis=1).astype(BF16)

    wkv3 = w_kv_b[0].reshape(KV_LORA_RANK, MLA_HEADS, QK_NOPE_DIM + V_HEAD_DIM)
    wk_nope, wv = wkv3[..., :QK_NOPE_DIM], wkv3[..., QK_NOPE_DIM:]
    wk_slots = _head_slot(wk_nope, jnp.zeros(wk_nope.shape[:-1] + (QK_ROPE_DIM,), F32))
    wkv = jnp.concatenate([wk_slots.reshape(KV_LORA_RANK, QK_WIDTH),
                           wv.reshape(KV_LORA_RANK, ATTN_WIDTH)], axis=1).astype(BF16)

    q_scale = float((QK_NOPE_DIM + QK_ROPE_DIM) ** -0.5 * np.log2(np.e))
    rope = _rope_tables(n_lat, q_scale)
    tmat, tw, cmat = _dft_tables(n_lat, FOURIER_GROUP_DIM)

    row2 = lambda g: g[0][None, :]
    u_f, q, k_lat, v_lat = _premix(x, mod3, row2(g_pre_mix), win, row2(g_q_a), wq, row2(g_kv_a), wkv,
                                   rope)
    k_ctx, v_ctx = _ctxkv(ctx, mod3, batch, row2(g_pre_mix), win_c, row2(g_kv_a), wkv)
    attn_main, attn_last = _attention(q, k_ctx, k_lat, v_ctx, v_lat)
    four = _fourier(u_f, tmat, tw, cmat, w_fourier[0].astype(BF16))
    return _post(x, four, attn_main, attn_last, mod3, row2(g_post_mix), row2(g_pre_ffn), row2(g_post_ffn),
                 w_out[0], w_gate[0], w_up[0], w_down[0])
```

```python
import functools

import numpy as np
import jax
import jax.numpy as jnp
from jax import lax
from jax.experimental import pallas as pl
from jax.experimental.pallas import tpu as pltpu

F32 = jnp.float32
BF16 = jnp.bfloat16

D_MODEL = 1024
GRID_W = 64
FOURIER_GROUPS = 4
FOURIER_GROUP_DIM = 128
FOURIER_WIDTH = FOURIER_GROUPS * FOURIER_GROUP_DIM
MLA_HEADS = 8
QK_NOPE_DIM = 64
QK_ROPE_DIM = 32
V_HEAD_DIM = 64
Q_LORA_RANK = 256
KV_LORA_RANK = 128
KV_COL = FOURIER_WIDTH + Q_LORA_RANK
ROPE_COL = KV_COL + KV_LORA_RANK
ROPE_BASE = 10000.0
NORM_EPS = 1e-6
FFT_RADIX = 8
HEAD_SLOT = 128
STAGE_ROWS_WIDE = 128
STAGE_ROWS_TALL = 352
STAGE_ROWS_OUT = 256
SHIFT_DEN_MIN = 2.0 ** -40
SHIFT_DEN_MAX = 2.0 ** 40
PV_KEY_TILE = 256
ATTN_WIDTH = MLA_HEADS * V_HEAD_DIM
QK_WIDTH = MLA_HEADS * HEAD_SLOT

VMEM_LIMIT_BYTES = 56 * 1024 * 1024


def _rms(x, g):
    return x * lax.rsqrt(jnp.mean(x * x, axis=-1, keepdims=True) + NORM_EPS) * g


def _dot(a, b):
    return jnp.dot(a, b, preferred_element_type=F32)


def _dot_nt(a, b):
    return lax.dot_general(a, b, (((1,), (1,)), ((), ())), preferred_element_type=F32)


def _rotary(a, table, first):
    half = QK_ROPE_DIM // 4
    below = pltpu.roll(a, half, axis=1)
    above = pltpu.roll(a, HEAD_SLOT - half, axis=1)
    return a * table(first) + below * table(first + 1) + above * table(first + 2)


def _store_value_slots(v_ref, v):
    pair_w = 2 * V_HEAD_DIM
    lower = lax.broadcasted_iota(jnp.int32, (v.shape[0], pair_w), 1) < V_HEAD_DIM
    for pair in range(MLA_HEADS // 2):
        vp = v[:, pair * pair_w:(pair + 1) * pair_w]
        lo = 2 * pair * HEAD_SLOT
        v_ref[0, :, lo:lo + HEAD_SLOT] = jnp.where(lower, vp, 1.0).astype(BF16)
        v_ref[0, :, lo + HEAD_SLOT:lo + 2 * HEAD_SLOT] = jnp.where(lower, 1.0, vp).astype(BF16)


def _adaln_kernel(c_ref, w_ref, b_ref, o_ref):
    c = c_ref[...]
    a = c / (1.0 + jnp.exp(-c))
    o_ref[...] = _dot(a.astype(BF16), w_ref[...].astype(BF16)) + b_ref[...]


def _adaln(cc, w_ada, b_ada, tn=1024):
    rows, d = cc.shape
    n = w_ada.shape[1]
    return pl.pallas_call(
        _adaln_kernel,
        grid=(n // tn,),
        in_specs=[
            pl.BlockSpec((rows, d), lambda j: (0, 0)),
            pl.BlockSpec((d, tn), lambda j: (0, j)),
            pl.BlockSpec((1, tn), lambda j: (0, j)),
        ],
        out_specs=pl.BlockSpec((rows, tn), lambda j: (0, j)),
        out_shape=jax.ShapeDtypeStruct((rows, n), F32),
        name="adaln",
    )(cc, w_ada, b_ada)


def _premix_kernel(x_ref, mod_ref, gpre_ref, win_ref, gq_ref, wq_ref, gkv_ref, wkv_ref,
                   rope_ref, u_ref, q_ref, k_ref, v_ref, us_ref):
    x = x_ref[0]
    shift = mod_ref[0, 0:1, :]
    scale = mod_ref[0, 1:2, :]
    h = _rms(x, gpre_ref[...] * (1.0 + scale)) + shift
    p = _dot(h.astype(BF16), win_ref[...])
    sub = us_ref.shape[1] // FFT_RADIX
    for g in range(FOURIER_GROUPS):
        glo = g * FOURIER_GROUP_DIM
        us_ref[g] = p[:, glo:glo + FOURIER_GROUP_DIM]
        for r in range(FFT_RADIX):
            lo = r * FOURIER_WIDTH + glo
            u_ref[0, :, lo:lo + FOURIER_GROUP_DIM] = (
                us_ref[g, pl.ds(r, sub, stride=FFT_RADIX), :].astype(BF16))

    qn = _rms(p[:, FOURIER_WIDTH:KV_COL], gq_ref[...]).astype(BF16)
    qq = _dot(qn, wq_ref[...])
    rows = x_ref.shape[1]
    row0 = pl.multiple_of(pl.program_id(1) * rows, rows)
    table = lambda k: rope_ref[k, pl.ds(row0, rows), :]
    cosq = table(0)
    sinq = table(1)
    for hd in range(MLA_HEADS):
        lo = hd * HEAD_SLOT
        q_ref[0, :, lo:lo + HEAD_SLOT] = (
            qq[:, lo:lo + HEAD_SLOT] * cosq + qq[:, QK_WIDTH + lo:QK_WIDTH + lo + HEAD_SLOT] * sinq
        ).astype(BF16)

    kvn = _rms(p[:, KV_COL:ROPE_COL], gkv_ref[...]).astype(BF16)
    kv = _dot(kvn, wkv_ref[...])
    kr = _rotary(p[:, ROPE_COL:ROPE_COL + HEAD_SLOT], table, 2)
    for hd in range(MLA_HEADS):
        lo = hd * HEAD_SLOT
        k_ref[0, :, lo:lo + HEAD_SLOT] = (kv[:, lo:lo + HEAD_SLOT] + kr).astype(BF16)
    _store_value_slots(v_ref, kv[:, QK_WIDTH:])


def _premix(x, mod3, gpre, win, gq, wq, gkv, wkv, rope, tm=512):
    b, s, d = x.shape
    const = lambda shape: pl.BlockSpec(shape, lambda i, j: (0,) * len(shape))
    rows = lambda w: pl.BlockSpec((1, tm, w), lambda i, j: (i, j, 0))
    return pl.pallas_call(
        _premix_kernel,
        grid=(b, s // tm),
        in_specs=[
            rows(d),
            pl.BlockSpec((1, 6, d), lambda i, j: (i, 0, 0)),
            const(gpre.shape), const(win.shape), const(gq.shape), const(wq.shape),
            const(gkv.shape), const(wkv.shape),
            pl.BlockSpec(rope.shape, lambda i, j: (0, 0, 0), pipeline_mode=pl.Buffered(1)),
        ],
        out_specs=[
            pl.BlockSpec((1, tm // FFT_RADIX, FFT_RADIX * FOURIER_WIDTH), lambda i, j: (i, j, 0)),
            rows(QK_WIDTH), rows(QK_WIDTH), rows(QK_WIDTH)],
        out_shape=[
            jax.ShapeDtypeStruct((b, s // FFT_RADIX, FFT_RADIX * FOURIER_WIDTH), BF16),
            jax.ShapeDtypeStruct((b, s, QK_WIDTH), BF16),
            jax.ShapeDtypeStruct((b, s, QK_WIDTH), BF16),
            jax.ShapeDtypeStruct((b, s, QK_WIDTH), BF16),
        ],
        scratch_shapes=[pltpu.VMEM((FOURIER_GROUPS, tm, FOURIER_GROUP_DIM), F32)],
        compiler_params=pltpu.CompilerParams(vmem_limit_bytes=VMEM_LIMIT_BYTES),
        name="premix",
    )(x, mod3, gpre, win, gq, wq, gkv, wkv, rope)


def _ctxkv_kernel(x_ref, mod_ref, gpre_ref, win_ref, gkv_ref, wkv_ref, k_ref, v_ref):
    x = x_ref[0]
    shift = mod_ref[0, 0:1, :]
    scale = mod_ref[0, 1:2, :]
    h = _rms(x, gpre_ref[...] * (1.0 + scale)) + shift
    p = _dot(h.astype(BF16), win_ref[...])
    kvn = _rms(p[:, :KV_LORA_RANK], gkv_ref[...]).astype(BF16)
    kv = _dot(kvn, wkv_ref[...])
    kr = p[:, KV_LORA_RANK:]
    for hd in range(MLA_HEADS):
        lo = hd * HEAD_SLOT
        k_ref[0, :, lo:lo + HEAD_SLOT] = (kv[:, lo:lo + HEAD_SLOT] + kr).astype(BF16)
    _store_value_slots(v_ref, kv[:, QK_WIDTH:])


def _ctxkv(ctx, mod3, ctx_row, gpre, win_c, gkv, wkv):
    b, c, d = ctx.shape
    const = lambda shape: pl.BlockSpec(shape, lambda i: (0,) * len(shape))
    rows = lambda w: pl.BlockSpec((1, c, w), lambda i: (i, 0, 0))
    return pl.pallas_call(
        _ctxkv_kernel,
        grid=(b,),
        in_specs=[
            rows(d),
            pl.BlockSpec((1, 6, d), lambda i: (ctx_row, 0, 0)),
            const(gpre.shape), const(win_c.shape), const(gkv.shape), const(wkv.shape),
        ],
        out_specs=[rows(QK_WIDTH), rows(QK_WIDTH)],
        out_shape=[
            jax.ShapeDtypeStruct((b, c, QK_WIDTH), BF16),
            jax.ShapeDtypeStruct((b, c, QK_WIDTH), BF16),
        ],
        name="ctxkv",
    )(ctx, mod3, gpre, win_c, gkv, wkv)


def _attn_kernel(q_ref, kc_ref, kl_ref, vc_ref, vl_ref, vcp_ref, vlp_ref, omain_ref, olast_ref,
                 p_ref, oprev_ref, ohead_ref):
    t = pl.program_id(0)
    n_blocks = pl.num_programs(0) - 1
    n_ctx = kc_ref.shape[1]
    n_lat = kl_ref.shape[1]
    tq = q_ref.shape[1]
    pair_w = 2 * V_HEAD_DIM
    first_half = lax.broadcasted_iota(jnp.int32, (tq, pair_w), 1) < V_HEAD_DIM
    last = MLA_HEADS - 1

    @pl.when(t == 0)
    def _():
        p_ref[...] = jnp.ones_like(p_ref)
        oprev_ref[...] = jnp.ones_like(oprev_ref)

    def join_pair(o_even, o_odd):
        num = jnp.where(first_half, o_even, o_odd)
        den = pltpu.roll(jnp.where(first_half, o_odd, o_even), V_HEAD_DIM, axis=1)
        return num / den

    def den_of(o, hd):
        lane = V_HEAD_DIM if hd % 2 == 0 else 0
        return o[:, lane:lane + 1]

    def drain():
        o = _dot(p_ref[:, :n_ctx], vcp_ref[0]) + _dot(p_ref[:, n_ctx:], vlp_ref[0])
        olast_ref[0] = join_pair(oprev_ref[...], o).astype(BF16)

    def head(hd, exact):
        lo = hd * HEAD_SLOT
        qh = q_ref[0, :, lo:lo + HEAD_SLOT]
        s_c = _dot_nt(qh, kc_ref[0, :, lo:lo + HEAD_SLOT])
        s_l = _dot_nt(qh, kl_ref[0, :, lo:lo + HEAD_SLOT])
        shift = jnp.max(s_c, axis=-1, keepdims=True)
        if exact:
            shift = jnp.maximum(shift, jnp.max(s_l, axis=-1, keepdims=True))
        if hd == last:
            p_c = jnp.exp2(s_c - shift)
            p_l = jnp.exp2(s_l - shift)
            p_ref[:, :n_ctx] = p_c.astype(BF16)
            p_ref[:, n_ctx:] = p_l.astype(BF16)
            return None, jnp.sum(p_c, axis=-1, keepdims=True) + jnp.sum(p_l, axis=-1, keepdims=True)
        o = _dot(jnp.exp2(s_c - shift).astype(BF16), vc_ref[0, :, lo:lo + HEAD_SLOT])
        for k0 in range(0, n_lat, PV_KEY_TILE):
            p = jnp.exp2(s_l[:, k0:k0 + PV_KEY_TILE] - shift).astype(BF16)
            o = o + _dot(p, vl_ref[0, k0:k0 + PV_KEY_TILE, lo:lo + HEAD_SLOT])
        return o, den_of(o, hd)

    def block():
        dens = []
        for pair in range(MLA_HEADS // 2):
            outs = []
            for hd in (2 * pair, 2 * pair + 1):
                o, den = head(hd, False)
                dens.append(den)
                if o is not None:
                    outs.append(o)
            if len(outs) == 2:
                vlo = pair * pair_w
                omain_ref[0, :, vlo:vlo + pair_w] = join_pair(outs[0], outs[1]).astype(BF16)
            else:
                oprev_ref[...] = outs[0]
        return dens

    def redo_block_exact():
        def one_head(hd, carry):
            lo = pl.multiple_of(hd * HEAD_SLOT, HEAD_SLOT)
            qh = q_ref[0, :, pl.ds(lo, HEAD_SLOT)]
            s_c = _dot_nt(qh, kc_ref[0, :, pl.ds(lo, HEAD_SLOT)])
            s_l = _dot_nt(qh, kl_ref[0, :, pl.ds(lo, HEAD_SLOT)])
            shift = jnp.maximum(jnp.max(s_c, axis=-1, keepdims=True), jnp.max(s_l, axis=-1, keepdims=True))
            ohead_ref[hd] = (_dot(jnp.exp2(s_c - shift).astype(BF16), vc_ref[0, :, pl.ds(lo, HEAD_SLOT)])
                             + _dot(jnp.exp2(s_l - shift).astype(BF16), vl_ref[0, :, pl.ds(lo, HEAD_SLOT)]))
            return carry

        lax.fori_loop(0, last, one_head, 0)
        for pair in range(MLA_HEADS // 2 - 1):
            vlo = pair * pair_w
            omain_ref[0, :, vlo:vlo + pair_w] = join_pair(ohead_ref[2 * pair], ohead_ref[2 * pair + 1]).astype(BF16)
        oprev_ref[...] = ohead_ref[last - 1]
        head(last, True)

    @pl.when(t < n_blocks)
    def _():
        drain()
        dens = block()
        lo_den, hi_den = dens[0], dens[0]
        for den in dens[1:]:
            lo_den = jnp.minimum(lo_den, den)
            hi_den = jnp.maximum(hi_den, den)
        trusted = jnp.logical_and(jnp.min(lo_den) >= SHIFT_DEN_MIN, jnp.max(hi_den) <= SHIFT_DEN_MAX)

        @pl.when(jnp.logical_not(trusted))
        def _():
            redo_block_exact()

    @pl.when(t == n_blocks)
    def _():
        drain()


def _attention(q, kc, kl, vc, vl, tq=512):
    b, s, _ = q.shape
    c = kc.shape[1]
    nq = s // tq
    n_blocks = b * nq
    pair_w = 2 * V_HEAD_DIM
    main_w = ATTN_WIDTH - pair_w
    last = MLA_HEADS - 1
    cur = lambda t: jnp.minimum(t, n_blocks - 1)
    prev = lambda t: jnp.maximum(t - 1, 0)
    return pl.pallas_call(
        _attn_kernel,
        grid=(n_blocks + 1,),
        in_specs=[
            pl.BlockSpec((1, tq, QK_WIDTH), lambda t: (cur(t) // nq, cur(t) % nq, 0)),
            pl.BlockSpec((1, c, QK_WIDTH), lambda t: (cur(t) // nq, 0, 0)),
            pl.BlockSpec((1, s, QK_WIDTH), lambda t: (cur(t) // nq, 0, 0)),
            pl.BlockSpec((1, c, QK_WIDTH), lambda t: (cur(t) // nq, 0, 0)),
            pl.BlockSpec((1, s, QK_WIDTH), lambda t: (cur(t) // nq, 0, 0)),
            pl.BlockSpec((1, c, HEAD_SLOT), lambda t: (prev(t) // nq, 0, last)),
            pl.BlockSpec((1, s, HEAD_SLOT), lambda t: (prev(t) // nq, 0, last)),
        ],
        out_specs=[
            pl.BlockSpec((1, tq, main_w), lambda t: (cur(t) // nq, cur(t) % nq, 0)),
            pl.BlockSpec((1, tq, pair_w), lambda t: (prev(t) // nq, prev(t) % nq, 0)),
        ],
        out_shape=[
            jax.ShapeDtypeStruct((b, s, main_w), BF16),
            jax.ShapeDtypeStruct((b, s, pair_w), BF16),
        ],
        scratch_shapes=[
            pltpu.VMEM((tq, c + s), BF16),
            pltpu.VMEM((tq, pair_w), F32),
            pltpu.VMEM((MLA_HEADS - 1, tq, HEAD_SLOT), F32),
        ],
        compiler_params=pltpu.CompilerParams(
            dimension_semantics=("arbitrary",), vmem_limit_bytes=VMEM_LIMIT_BYTES),
        name="attn",
    )(q, kc, kl, vc, vl, vc, vl)


def _fourier_kernel(t_ref, tw_ref, u_ref, cc_ref, wf_ref, o_ref, tb_ref, cw_ref):
    @pl.when(pl.program_id(0) == 0)
    def _():
        tb_ref[...] = t_ref[...].astype(BF16)
        cc = cc_ref[...].astype(BF16)
        for g in range(FOURIER_GROUPS):
            cw_ref[g] = _dot(cc, wf_ref[g]).astype(BF16)

    n_sub = tb_ref.shape[1]
    f = _dot(tb_ref[...], u_ref[0])
    gd = FOURIER_GROUP_DIM
    for g in range(FOURIER_GROUPS):
        gr, gi = [], []
        for r in range(FFT_RADIX):
            lo = r * FOURIER_WIDTH + g * gd
            a = f[:n_sub, lo:lo + gd]
            b = f[n_sub:, lo:lo + gd]
            if r == 0:
                gr.append(a)
                gi.append(b)
            else:
                c = tw_ref[r - 1, 0]
                s = tw_ref[r - 1, 1]
                gr.append(a * c - b * s)
                gi.append(a * s + b * c)
        z = list(zip(gr, gi))
        add = lambda a, b: (a[0] + b[0], a[1] + b[1])
        sub = lambda a, b: (a[0] - b[0], a[1] - b[1])
        times_minus_i = lambda a: (-a[1], a[0])

        def dft4(c):
            e0, e1 = add(c[0], c[2]), sub(c[0], c[2])
            f0, f1 = add(c[1], c[3]), times_minus_i(sub(c[1], c[3]))
            return [add(e0, f0), add(e1, f1), sub(e0, f0), sub(e1, f1)]

        half = FFT_RADIX // 2
        even = dft4([add(z[r], z[r + half]) for r in range(half)])
        d = [sub(z[r], z[r + half]) for r in range(half)]
        rt = np.float32(np.sqrt(0.5))
        odd = dft4([
            d[0],
            ((d[1][0] - d[1][1]) * rt, (d[1][0] + d[1][1]) * rt),
            times_minus_i(d[2]),
            (-(d[3][0] + d[3][1]) * rt, (d[3][0] - d[3][1]) * rt),
        ])
        y = [even[q // 2] if q % 2 == 0 else odd[q // 2] for q in range(FFT_RADIX)]
        xr = jnp.concatenate([v[0] for v in y], axis=0)
        xi = jnp.concatenate([v[1] for v in y], axis=0)
        lhs = jnp.concatenate([xr, xi], axis=1).astype(BF16)
        o_ref[0, :, g * gd:(g + 1) * gd] = _dot(lhs, cw_ref[g]).astype(BF16)


def _fourier(u4, tmat, tw, cmat, wf):
    b, n_sub, _ = u4.shape
    s = n_sub * FFT_RADIX
    full = lambda a: pl.BlockSpec(a.shape, lambda i: (0,) * a.ndim)
    return pl.pallas_call(
        _fourier_kernel,
        grid=(b,),
        in_specs=[
            full(tmat), full(tw),
            pl.BlockSpec((1, n_sub, FFT_RADIX * FOURIER_WIDTH), lambda i: (i, 0, 0)),
            full(cmat), full(wf),
        ],
        out_specs=pl.BlockSpec((1, s, FOURIER_WIDTH), lambda i: (i, 0, 0)),
        out_shape=jax.ShapeDtypeStruct((b, s, FOURIER_WIDTH), BF16),
        scratch_shapes=[
            pltpu.VMEM(tmat.shape, BF16),
            pltpu.VMEM((FOURIER_GROUPS, 2 * FOURIER_GROUP_DIM, FOURIER_GROUP_DIM), BF16),
        ],
        compiler_params=pltpu.CompilerParams(
            dimension_semantics=("arbitrary",), vmem_limit_bytes=VMEM_LIMIT_BYTES),
        name="fourier",
    )(tmat, tw, u4, cmat, wf)


def _stage_cast(src_hbm, dst_ref, stage_ref, sem_ref, chunk):
    n_chunks = src_hbm.shape[0] // chunk

    def copy(i):
        slot = i % 2
        return pltpu.make_async_copy(
            src_hbm.at[pl.ds(i * chunk, chunk)], stage_ref.at[slot, pl.ds(0, chunk)], sem_ref.at[slot])

    def body(i, carry):
        @pl.when(i + 1 < n_chunks)
        def _():
            copy(i + 1).start()

        copy(i).wait()
        start = pl.multiple_of(i * chunk, chunk)
        dst_ref[pl.ds(start, chunk), :] = stage_ref[i % 2, pl.ds(0, chunk), :].astype(BF16)
        return carry

    copy(0).start()
    lax.fori_loop(0, n_chunks, body, 0)


def _post_kernel(x_ref, four_ref, am_ref, al_ref, modc_ref, modp_ref, gpm_ref, gpf_ref, gqf_ref,
                 wo_hbm, wg_hbm, wu_hbm, wd_hbm, o_ref,
                 x1_ref, h2_ref, wo_ref, wg_ref, wu_ref, wd_ref, wide_stage, tall_stage, sem_ref):
    t = pl.program_id(0)
    n_four = four_ref.shape[2]

    def mix():
        gt_m = modc_ref[0, 2:3, :]
        sh_f = modc_ref[0, 3:4, :]
        sc_f = modc_ref[0, 4:5, :]
        attn = jnp.concatenate([am_ref[0], al_ref[0]], axis=1)
        y = _dot(four_ref[0], wo_ref[:n_four, :]) + _dot(attn, wo_ref[n_four:, :])
        x1 = x_ref[0] + _rms(y, gt_m * gpm_ref[...])
        x1_ref[...] = x1
        h2_ref[...] = (_rms(x1, gpf_ref[...] * (1.0 + sc_f)) + sh_f).astype(BF16)

    def ffn():
        gt_f = modp_ref[0, 5:6, :]
        h2 = h2_ref[...]
        g = _dot(h2, wg_ref[...])
        up = _dot(h2, wu_ref[...])
        act = (g / (1.0 + jnp.exp(-g)) * up).astype(BF16)
        o_ref[0] = x1_ref[...] + _rms(_dot(act, wd_ref[...]), gt_f * gqf_ref[...])

    @pl.when(t == 0)
    def _():
        _stage_cast(wo_hbm, wo_ref, tall_stage, sem_ref, STAGE_ROWS_OUT)
        mix()
        _stage_cast(wg_hbm, wg_ref, wide_stage, sem_ref, STAGE_ROWS_WIDE)
        _stage_cast(wu_hbm, wu_ref, wide_stage, sem_ref, STAGE_ROWS_WIDE)
        _stage_cast(wd_hbm, wd_ref, tall_stage, sem_ref, STAGE_ROWS_TALL)

    @pl.when(t > 0)
    def _():
        ffn()
        mix()


def _post(x, four, attn_main, attn_last, mod3, gpm, gpf, gqf, wo, wg, wu, wd, tm=512):
    b, s, d = x.shape
    d_ff = wg.shape[1]
    assert wo.shape[0] % STAGE_ROWS_OUT == 0 and STAGE_ROWS_OUT <= STAGE_ROWS_TALL
    assert d % STAGE_ROWS_WIDE == 0 and d_ff % STAGE_ROWS_TALL == 0
    nj = s // tm
    n_blocks = b * nj
    cur = lambda t: jnp.minimum(t, n_blocks - 1)
    prev = lambda t: jnp.maximum(t - 1, 0)
    const = lambda shape: pl.BlockSpec(shape, lambda t: (0,) * len(shape), pipeline_mode=pl.Buffered(1))
    rows = lambda w: pl.BlockSpec((1, tm, w), lambda t: (cur(t) // nj, cur(t) % nj, 0))
    hbm = pl.BlockSpec(memory_space=pl.ANY)
    return pl.pallas_call(
        _post_kernel,
        grid=(n_blocks + 1,),
        in_specs=[
            rows(d), rows(FOURIER_WIDTH), rows(attn_main.shape[2]), rows(attn_last.shape[2]),
            pl.BlockSpec((1, 6, d), lambda t: (cur(t) // nj, 0, 0)),
            pl.BlockSpec((1, 6, d), lambda t: (prev(t) // nj, 0, 0)),
            const(gpm.shape), const(gpf.shape), const(gqf.shape),
            hbm, hbm, hbm, hbm,
        ],
        out_specs=pl.BlockSpec((1, tm, d), lambda t: (prev(t) // nj, prev(t) % nj, 0)),
        out_shape=jax.ShapeDtypeStruct((b, s, d), F32),
        scratch_shapes=[
            pltpu.VMEM((tm, d), F32), pltpu.VMEM((tm, d), BF16),
            pltpu.VMEM(wo.shape, BF16), pltpu.VMEM(wg.shape, BF16), pltpu.VMEM(wu.shape, BF16),
            pltpu.VMEM(wd.shape, BF16),
            pltpu.VMEM((2, STAGE_ROWS_WIDE, d_ff), F32), pltpu.VMEM((2, STAGE_ROWS_TALL, d), F32),
            pltpu.SemaphoreType.DMA((2,)),
        ],
        compiler_params=pltpu.CompilerParams(
            dimension_semantics=("arbitrary",), vmem_limit_bytes=VMEM_LIMIT_BYTES),
        name="post",
    )(x, four, attn_main, attn_last, mod3, mod3, gpm, gpf, gqf, wo, wg, wu, wd)


def _rope_rotate_cols(w):
    a = QK_ROPE_DIM // 2
    hf = a // 2
    blocks = []
    for s0 in (0, a):
        blocks += [-w[..., s0 + hf:s0 + a], w[..., s0:s0 + hf]]
    return jnp.concatenate(blocks, axis=-1)


def _head_slot(nope, rope):
    pad = HEAD_SLOT - QK_NOPE_DIM - QK_ROPE_DIM
    return jnp.concatenate([nope, rope, jnp.zeros(rope.shape[:-1] + (pad,), rope.dtype)], axis=-1)


def _rope_tables(n_lat, q_scale):
    t = np.arange(n_lat)
    hf = QK_ROPE_DIM // 4
    inv_freq = ROPE_BASE ** (-np.arange(hf, dtype=np.float64) / hf)
    ar = (t // GRID_W)[:, None] * inv_freq[None, :]
    ac = (t % GRID_W)[:, None] * inv_freq[None, :]
    z = np.zeros_like(ar)
    cos32 = np.concatenate([np.cos(ar), np.cos(ar), np.cos(ac), np.cos(ac)], axis=-1)
    sin32 = np.concatenate([np.sin(ar), np.sin(ar), np.sin(ac), np.sin(ac)], axis=-1)
    below32 = np.concatenate([z, np.sin(ar), z, np.sin(ac)], axis=-1)
    above32 = np.concatenate([-np.sin(ar), z, -np.sin(ac), z], axis=-1)
    pad = np.zeros((n_lat, HEAD_SLOT - QK_NOPE_DIM - QK_ROPE_DIM))
    ones = np.ones((n_lat, QK_NOPE_DIM))
    zeros = np.zeros((n_lat, QK_NOPE_DIM))
    slot = lambda nope, rope: np.concatenate([nope, rope, pad], axis=-1)
    tables = [slot(ones * q_scale, cos32 * q_scale), slot(zeros, sin32 * q_scale),
              slot(zeros, cos32), slot(zeros, below32), slot(zeros, above32)]
    return jnp.asarray(np.stack(tables).astype(np.float32))


def _dft_tables(n_pos, n_ch):
    n_sub = n_pos // FFT_RADIX
    m = np.arange(n_sub, dtype=np.int64)
    ang = 2.0 * np.pi * ((m[:, None] * m[None, :]) % n_sub) / n_sub
    tmat = np.concatenate([np.cos(ang), np.sin(ang)], axis=0).astype(np.float32)
    tw = np.zeros((FFT_RADIX - 1, 2, n_sub, FOURIER_GROUP_DIM), np.float32)
    for r in range(1, FFT_RADIX):
        a = 2.0 * np.pi * r * m / n_pos
        tw[r - 1, 0] = np.cos(a)[:, None]
        tw[r - 1, 1] = np.sin(a)[:, None]
    c = np.arange(n_ch, dtype=np.int64)
    angc = 2.0 * np.pi * ((c[:, None] * c[None, :]) % n_ch) / n_ch
    norm = 1.0 / np.sqrt(float(n_pos * n_ch))
    cmat = np.concatenate([np.cos(angc) * norm, -np.sin(angc) * norm], axis=0).astype(np.float32)
    return jnp.asarray(tmat), jnp.asarray(tw), jnp.asarray(cmat)


def kernel(x, c, ctx, c_ctx, w_ada, b_ada, g_pre_mix, g_post_mix, g_pre_ffn, g_post_ffn, w_in, g_q_a,
           w_q_b, g_kv_a, w_kv_b, w_fourier, w_out, w_gate, w_up, w_down):
    assert w_ada.shape[0] == 1, "single-layer block"
    batch, n_lat, d = x.shape

    mod_rows = -(-(batch + 1) // 8) * 8
    cc = jnp.concatenate([c, c_ctx[None, :], jnp.zeros((mod_rows - batch - 1, d), F32)], axis=0)
    mod = _adaln(cc, w_ada[0], b_ada[0][None, :])
    mod3 = mod.reshape(mod_rows, 6, d)

    w_in0 = w_in[0]
    w_kr = w_in0[:, ROPE_COL:]
    zeros_d = jnp.zeros((d, QK_NOPE_DIM), F32)
    kr_slot = _head_slot(zeros_d, w_kr)
    win = jnp.concatenate([w_in0[:, :ROPE_COL], kr_slot], axis=1).astype(BF16)
    win_c = jnp.concatenate([w_in0[:, KV_COL:ROPE_COL], kr_slot], axis=1).astype(BF16)

    wq3 = w_q_b[0].reshape(Q_LORA_RANK, MLA_HEADS, QK_NOPE_DIM + QK_ROPE_DIM)
    wq_nope, wq_rope = wq3[..., :QK_NOPE_DIM], wq3[..., QK_NOPE_DIM:]
    wq_a = _head_slot(wq_nope, wq_rope).reshape(Q_LORA_RANK, QK_WIDTH)
    wq_b = _head_slot(jnp.zeros_like(wq_nope), _rope_rotate_cols(wq_rope)).reshape(Q_LORA_RANK, QK_WIDTH)
    wq = jnp.concatenate([wq_a, wq_b], axis=1).astype(BF16)

    wkv3 = w_kv_b[0].reshape(KV_LORA_RANK, MLA_HEADS, QK_NOPE_DIM + V_HEAD_DIM)
    wk_nope, wv = wkv3[..., :QK_NOPE_DIM], wkv3[..., QK_NOPE_DIM:]
    wk_slots = _head_slot(wk_nope, jnp.zeros(wk_nope.shape[:-1] + (QK_ROPE_DIM,), F32))
    wkv = jnp.concatenate([wk_slots.reshape(KV_LORA_RANK, QK_WIDTH),
                           wv.reshape(KV_LORA_RANK, ATTN_WIDTH)], axis=1).astype(BF16)

    q_scale = float((QK_NOPE_DIM + QK_ROPE_DIM) ** -0.5 * np.log2(np.e))
    rope = _rope_tables(n_lat, q_scale)
    tmat, tw, cmat = _dft_tables(n_lat, FOURIER_GROUP_DIM)

    row2 = lambda g: g[0][None, :]
    u_f, q, k_lat, v_lat = _premix(x, mod3, row2(g_pre_mix), win, row2(g_q_a), wq, row2(g_kv_a), wkv,
                                   rope)
    k_ctx, v_ctx = _ctxkv(ctx, mod3, batch, row2(g_pre_mix), win_c, row2(g_kv_a), wkv)
    attn_main, attn_last = _attention(q, k_ctx, k_lat, v_ctx, v_lat)
    four = _fourier(u_f, tmat, tw, cmat, w_fourier[0].astype(BF16))
    return _post(x, four, attn_main, attn_last, mod3, row2(g_post_mix), row2(g_pre_ffn), row2(g_post_ffn),
                 w_out[0], w_gate[0], w_up[0], w_down[0])
```

```python
import numpy as np
import jax
import jax.numpy as jnp
from jax import lax
from jax.experimental import pallas as pl
from jax.experimental.pallas import tpu as pltpu

F32 = jnp.float32
BF16 = jnp.bfloat16

D_MODEL = 1024
GRID_W = 64
FOURIER_GROUPS = 4
FOURIER_GROUP_DIM = 128
FOURIER_WIDTH = FOURIER_GROUPS * FOURIER_GROUP_DIM
MLA_HEADS = 8
QK_NOPE_DIM = 64
QK_ROPE_DIM = 32
V_HEAD_DIM = 64
Q_LORA_RANK = 256
KV_LORA_RANK = 128
KV_COL = FOURIER_WIDTH + Q_LORA_RANK
ROPE_COL = KV_COL + KV_LORA_RANK
ROPE_BASE = 10000.0
NORM_EPS = 1e-6
FFT_RADIX = 8
HEAD_SLOT = 128
STAGE_ROWS_WIDE = 256
STAGE_ROWS_TALL = 704
STAGE_ROWS_OUT = 512
SHIFT_DEN_MIN = 2.0 ** -40
SHIFT_DEN_MAX = 2.0 ** 40
PV_KEY_TILE = 256
ATTN_WIDTH = MLA_HEADS * V_HEAD_DIM
QK_WIDTH = MLA_HEADS * HEAD_SLOT

VMEM_LIMIT_BYTES = 56 * 1024 * 1024


def _rms(x, g):
    return x * lax.rsqrt(jnp.mean(x * x, axis=-1, keepdims=True) + NORM_EPS) * g


def _dot(a, b):
    return jnp.dot(a, b, preferred_element_type=F32)


def _dot_nt(a, b):
    return lax.dot_general(a, b, (((1,), (1,)), ((), ())), preferred_element_type=F32)


def _rotary(a, table, first):
    half = QK_ROPE_DIM // 4
    below = pltpu.roll(a, half, axis=1)
    above = pltpu.roll(a, HEAD_SLOT - half, axis=1)
    return a * table(first) + below * table(first + 1) + above * table(first + 2)


def _store_value_slots(v_ref, v):
    pair_w = 2 * V_HEAD_DIM
    lower = lax.broadcasted_iota(jnp.int32, (v.shape[0], pair_w), 1) < V_HEAD_DIM
    for pair in range(MLA_HEADS // 2):
        vp = v[:, pair * pair_w:(pair + 1) * pair_w]
        lo = 2 * pair * HEAD_SLOT
        v_ref[0, :, lo:lo + HEAD_SLOT] = jnp.where(lower, vp, 1.0).astype(BF16)
        v_ref[0, :, lo + HEAD_SLOT:lo + 2 * HEAD_SLOT] = jnp.where(lower, 1.0, vp).astype(BF16)


def _adaln_kernel(c_ref, w_ref, b_ref, o_ref):
    c = c_ref[...]
    a = c / (1.0 + jnp.exp(-c))
    o_ref[...] = _dot(a.astype(BF16), w_ref[...].astype(BF16)) + b_ref[...]


def _adaln(cc, w_ada, b_ada, tn=2048):
    rows, d = cc.shape
    n = w_ada.shape[1]
    return pl.pallas_call(
        _adaln_kernel,
        grid=(n // tn,),
        in_specs=[
            pl.BlockSpec((rows, d), lambda j: (0, 0)),
            pl.BlockSpec((d, tn), lambda j: (0, j)),
            pl.BlockSpec((1, tn), lambda j: (0, j)),
        ],
        out_specs=pl.BlockSpec((rows, tn), lambda j: (0, j)),
        out_shape=jax.ShapeDtypeStruct((rows, n), F32),
        compiler_params=pltpu.CompilerParams(vmem_limit_bytes=VMEM_LIMIT_BYTES),
        name="adaln",
    )(cc, w_ada, b_ada)


def _premix_kernel(x_ref, mod_ref, gpre_ref, win_ref, gq_ref, wq_ref, gkv_ref, wkv_ref,
                   rope_ref, u_ref, q_ref, k_ref, v_ref, us_ref):
    x = x_ref[0]
    shift = mod_ref[0, 0:1, :]
    scale = mod_ref[0, 1:2, :]
    h = _rms(x, gpre_ref[...] * (1.0 + scale)) + shift
    p = _dot(h.astype(BF16), win_ref[...])
    sub = us_ref.shape[1] // FFT_RADIX
    for g in range(FOURIER_GROUPS):
        glo = g * FOURIER_GROUP_DIM
        us_ref[g] = p[:, glo:glo + FOURIER_GROUP_DIM]
        for r in range(FFT_RADIX):
            lo = r * FOURIER_WIDTH + glo
            u_ref[0, :, lo:lo + FOURIER_GROUP_DIM] = (
                us_ref[g, pl.ds(r, sub, stride=FFT_RADIX), :].astype(BF16))

    qn = _rms(p[:, FOURIER_WIDTH:KV_COL], gq_ref[...]).astype(BF16)
    qq = _dot(qn, wq_ref[...])
    rows = x_ref.shape[1]
    row0 = pl.multiple_of(pl.program_id(1) * rows, rows)
    table = lambda k: rope_ref[k, pl.ds(row0, rows), :]
    cosq = table(0)
    sinq = table(1)
    for hd in range(MLA_HEADS):
        lo = hd * HEAD_SLOT
        q_ref[0, :, lo:lo + HEAD_SLOT] = (
            qq[:, lo:lo + HEAD_SLOT] * cosq + qq[:, QK_WIDTH + lo:QK_WIDTH + lo + HEAD_SLOT] * sinq
        ).astype(BF16)

    kvn = _rms(p[:, KV_COL:ROPE_COL], gkv_ref[...]).astype(BF16)
    kv = _dot(kvn, wkv_ref[...])
    kr = _rotary(p[:, ROPE_COL:ROPE_COL + HEAD_SLOT], table, 2)
    for hd in range(MLA_HEADS):
        lo = hd * HEAD_SLOT
        k_ref[0, :, lo:lo + HEAD_SLOT] = (kv[:, lo:lo + HEAD_SLOT] + kr).astype(BF16)
    _store_value_slots(v_ref, kv[:, QK_WIDTH:])


def _premix(x, mod3, gpre, win, gq, wq, gkv, wkv, rope, tm=512):
    b, s, d = x.shape
    const = lambda shape: pl.BlockSpec(shape, lambda i, j: (0,) * len(shape))
    rows = lambda w: pl.BlockSpec((1, tm, w), lambda i, j: (i, j, 0))
    return pl.pallas_call(
        _premix_kernel,
        grid=(b, s // tm),
        in_specs=[
            rows(d),
            pl.BlockSpec((1, 6, d), lambda i, j: (i, 0, 0)),
            const(gpre.shape), const(win.shape), const(gq.shape), const(wq.shape),
            const(gkv.shape), const(wkv.shape),
            pl.BlockSpec(rope.shape, lambda i, j: (0, 0, 0), pipeline_mode=pl.Buffered(1)),
        ],
        out_specs=[
            pl.BlockSpec((1, tm // FFT_RADIX, FFT_RADIX * FOURIER_WIDTH), lambda i, j: (i, j, 0)),
            rows(QK_WIDTH), rows(QK_WIDTH), rows(QK_WIDTH)],
        out_shape=[
            jax.ShapeDtypeStruct((b, s // FFT_RADIX, FFT_RADIX * FOURIER_WIDTH), BF16),
            jax.ShapeDtypeStruct((b, s, QK_WIDTH), BF16),
            jax.ShapeDtypeStruct((b, s, QK_WIDTH), BF16),
            jax.ShapeDtypeStruct((b, s, QK_WIDTH), BF16),
        ],
        scratch_shapes=[pltpu.VMEM((FOURIER_GROUPS, tm, FOURIER_GROUP_DIM), F32)],
        compiler_params=pltpu.CompilerParams(vmem_limit_bytes=VMEM_LIMIT_BYTES),
        name="premix",
    )(x, mod3, gpre, win, gq, wq, gkv, wkv, rope)


def _ctxkv_kernel(x_ref, mod_ref, gpre_ref, win_ref, gkv_ref, wkv_ref, k_ref, v_ref):
    x = x_ref[0]
    shift = mod_ref[0, 0:1, :]
    scale = mod_ref[0, 1:2, :]
    h = _rms(x, gpre_ref[...] * (1.0 + scale)) + shift
    p = _dot(h.astype(BF16), win_ref[...])
    kvn = _rms(p[:, :KV_LORA_RANK], gkv_ref[...]).astype(BF16)
    kv = _dot(kvn, wkv_ref[...])
    kr = p[:, KV_LORA_RANK:]
    for hd in range(MLA_HEADS):
        lo = hd * HEAD_SLOT
        k_ref[0, :, lo:lo + HEAD_SLOT] = (kv[:, lo:lo + HEAD_SLOT] + kr).astype(BF16)
    _store_value_slots(v_ref, kv[:, QK_WIDTH:])


def _ctxkv(ctx, mod3, ctx_row, gpre, win_c, gkv, wkv, tm=1024):
    b, c, d = ctx.shape
    n = b * c
    const = lambda shape: pl.BlockSpec(shape, lambda i: (0,) * len(shape))
    rows = lambda w: pl.BlockSpec((1, tm, w), lambda i: (0, i, 0))
    k, v = pl.pallas_call(
        _ctxkv_kernel,
        grid=(n // tm,),
        in_specs=[
            rows(d),
            pl.BlockSpec((1, 6, d), lambda i: (ctx_row, 0, 0)),
            const(gpre.shape), const(win_c.shape), const(gkv.shape), const(wkv.shape),
        ],
        out_specs=[rows(QK_WIDTH), rows(QK_WIDTH)],
        out_shape=[
            jax.ShapeDtypeStruct((1, n, QK_WIDTH), BF16),
            jax.ShapeDtypeStruct((1, n, QK_WIDTH), BF16),
        ],
        compiler_params=pltpu.CompilerParams(vmem_limit_bytes=VMEM_LIMIT_BYTES),
        name="ctxkv",
    )(ctx.reshape(1, n, d), mod3, gpre, win_c, gkv, wkv)
    return k.reshape(b, c, QK_WIDTH), v.reshape(b, c, QK_WIDTH)


def _attn_kernel(q_ref, kc_ref, kl_ref, vc_ref, vl_ref, vcp_ref, vlp_ref, omain_ref, olast_ref,
                 p_ref, oprev_ref, ohead_ref):
    t = pl.program_id(0)
    n_blocks = pl.num_programs(0) - 1
    n_ctx = kc_ref.shape[1]
    n_lat = kl_ref.shape[1]
    tq = q_ref.shape[1]
    pair_w = 2 * V_HEAD_DIM
    first_half = lax.broadcasted_iota(jnp.int32, (tq, pair_w), 1) < V_HEAD_DIM
    last = MLA_HEADS - 1

    @pl.when(t == 0)
    def _():
        p_ref[...] = jnp.ones_like(p_ref)
        oprev_ref[...] = jnp.ones_like(oprev_ref)

    def join_pair(o_even, o_odd):
        num = jnp.where(first_half, o_even, o_odd)
        den = pltpu.roll(jnp.where(first_half, o_odd, o_even), V_HEAD_DIM, axis=1)
        return num / den

    def den_of(o, hd):
        lane = V_HEAD_DIM if hd % 2 == 0 else 0
        return o[:, lane:lane + 1]

    def drain():
        o = _dot(p_ref[:, :n_ctx], vcp_ref[0]) + _dot(p_ref[:, n_ctx:], vlp_ref[0])
        olast_ref[0] = join_pair(oprev_ref[...], o).astype(BF16)

    def head(hd, exact):
        lo = hd * HEAD_SLOT
        qh = q_ref[0, :, lo:lo + HEAD_SLOT]
        s_c = _dot_nt(qh, kc_ref[0, :, lo:lo + HEAD_SLOT])
        s_l = _dot_nt(qh, kl_ref[0, :, lo:lo + HEAD_SLOT])
        shift = jnp.max(s_c, axis=-1, keepdims=True)
        if exact:
            shift = jnp.maximum(shift, jnp.max(s_l, axis=-1, keepdims=True))
        if hd == last:
            p_c = jnp.exp2(s_c - shift)
            p_l = jnp.exp2(s_l - shift)
            p_ref[:, :n_ctx] = p_c.astype(BF16)
            p_ref[:, n_ctx:] = p_l.astype(BF16)
            return None, jnp.sum(p_c, axis=-1, keepdims=True) + jnp.sum(p_l, axis=-1, keepdims=True)
        o = _dot(jnp.exp2(s_c - shift).astype(BF16), vc_ref[0, :, lo:lo + HEAD_SLOT])
        for k0 in range(0, n_lat, PV_KEY_TILE):
            p = jnp.exp2(s_l[:, k0:k0 + PV_KEY_TILE] - shift).astype(BF16)
            o = o + _dot(p, vl_ref[0, k0:k0 + PV_KEY_TILE, lo:lo + HEAD_SLOT])
        return o, den_of(o, hd)

    def block():
        dens = []
        for pair in range(MLA_HEADS // 2):
            outs = []
            for hd in (2 * pair, 2 * pair + 1):
                o, den = head(hd, False)
                dens.append(den)
                if o is not None:
                    outs.append(o)
            if len(outs) == 2:
                vlo = pair * pair_w
                omain_ref[0, :, vlo:vlo + pair_w] = join_pair(outs[0], outs[1]).astype(BF16)
            else:
                oprev_ref[...] = outs[0]
        return dens

    def redo_block_exact():
        def one_head(hd, carry):
            lo = pl.multiple_of(hd * HEAD_SLOT, HEAD_SLOT)
            qh = q_ref[0, :, pl.ds(lo, HEAD_SLOT)]
            s_c = _dot_nt(qh, kc_ref[0, :, pl.ds(lo, HEAD_SLOT)])
            s_l = _dot_nt(qh, kl_ref[0, :, pl.ds(lo, HEAD_SLOT)])
            shift = jnp.maximum(jnp.max(s_c, axis=-1, keepdims=True), jnp.max(s_l, axis=-1, keepdims=True))
            ohead_ref[hd] = (_dot(jnp.exp2(s_c - shift).astype(BF16), vc_ref[0, :, pl.ds(lo, HEAD_SLOT)])
                             + _dot(jnp.exp2(s_l - shift).astype(BF16), vl_ref[0, :, pl.ds(lo, HEAD_SLOT)]))
            return carry

        lax.fori_loop(0, last, one_head, 0)
        for pair in range(MLA_HEADS // 2 - 1):
            vlo = pair * pair_w
            omain_ref[0, :, vlo:vlo + pair_w] = join_pair(ohead_ref[2 * pair], ohead_ref[2 * pair + 1]).astype(BF16)
        oprev_ref[...] = ohead_ref[last - 1]
        head(last, True)

    @pl.when(t < n_blocks)
    def _():
        drain()
        dens = block()
        lo_den, hi_den = dens[0], dens[0]
        for den in dens[1:]:
            lo_den = jnp.minimum(lo_den, den)
            hi_den = jnp.maximum(hi_den, den)
        trusted = jnp.logical_and(jnp.min(lo_den) >= SHIFT_DEN_MIN, jnp.max(hi_den) <= SHIFT_DEN_MAX)

        @pl.when(jnp.logical_not(trusted))
        def _():
            redo_block_exact()

    @pl.when(t == n_blocks)
    def _():
        drain()


def _attention(q, kc, kl, vc, vl, tq=512):
    b, s, _ = q.shape
    c = kc.shape[1]
    nq = s // tq
    n_blocks = b * nq
    pair_w = 2 * V_HEAD_DIM
    main_w = ATTN_WIDTH - pair_w
    last = MLA_HEADS - 1
    cur = lambda t: jnp.minimum(t, n_blocks - 1)
    prev = lambda t: jnp.maximum(t - 1, 0)
    return pl.pallas_call(
        _attn_kernel,
        grid=(n_blocks + 1,),
        in_specs=[
            pl.BlockSpec((1, tq, QK_WIDTH), lambda t: (cur(t) // nq, cur(t) % nq, 0)),
            pl.BlockSpec((1, c, QK_WIDTH), lambda t: (cur(t) // nq, 0, 0)),
            pl.BlockSpec((1, s, QK_WIDTH), lambda t: (cur(t) // nq, 0, 0)),
            pl.BlockSpec((1, c, QK_WIDTH), lambda t: (cur(t) // nq, 0, 0)),
            pl.BlockSpec((1, s, QK_WIDTH), lambda t: (cur(t) // nq, 0, 0)),
            pl.BlockSpec((1, c, HEAD_SLOT), lambda t: (prev(t) // nq, 0, last)),
            pl.BlockSpec((1, s, HEAD_SLOT), lambda t: (prev(t) // nq, 0, last)),
        ],
        out_specs=[
            pl.BlockSpec((1, tq, main_w), lambda t: (cur(t) // nq, cur(t) % nq, 0)),
            pl.BlockSpec((1, tq, pair_w), lambda t: (prev(t) // nq, prev(t) % nq, 0)),
        ],
        out_shape=[
            jax.ShapeDtypeStruct((b, s, main_w), BF16),
            jax.ShapeDtypeStruct((b, s, pair_w), BF16),
        ],
        scratch_shapes=[
            pltpu.VMEM((tq, c + s), BF16),
            pltpu.VMEM((tq, pair_w), F32),
            pltpu.VMEM((MLA_HEADS - 1, tq, HEAD_SLOT), F32),
        ],
        compiler_params=pltpu.CompilerParams(
            dimension_semantics=("arbitrary",), vmem_limit_bytes=VMEM_LIMIT_BYTES),
        name="attn",
    )(q, kc, kl, vc, vl, vc, vl)


def _fourier_kernel(t_ref, tw_ref, u_ref, cc_ref, wf_ref, o_ref, tb_ref, cw_ref):
    @pl.when(pl.program_id(0) == 0)
    def _():
        tb_ref[...] = t_ref[...].astype(BF16)
        cc = cc_ref[...].astype(BF16)
        for g in range(FOURIER_GROUPS):
            cw_ref[g] = _dot(cc, wf_ref[g]).astype(BF16)

    n_sub = tb_ref.shape[1]
    f = _dot(tb_ref[...], u_ref[0])
    gd = FOURIER_GROUP_DIM
    for g in range(FOURIER_GROUPS):
        gr, gi = [], []
        for r in range(FFT_RADIX):
            lo = r * FOURIER_WIDTH + g * gd
            a = f[:n_sub, lo:lo + gd]
            b = f[n_sub:, lo:lo + gd]
            if r == 0:
                gr.append(a)
                gi.append(b)
            else:
                c = tw_ref[r - 1, 0]
                s = tw_ref[r - 1, 1]
                gr.append(a * c - b * s)
                gi.append(a * s + b * c)
        z = list(zip(gr, gi))
        add = lambda a, b: (a[0] + b[0], a[1] + b[1])
        sub = lambda a, b: (a[0] - b[0], a[1] - b[1])
        times_minus_i = lambda a: (-a[1], a[0])

        def dft4(c):
            e0, e1 = add(c[0], c[2]), sub(c[0], c[2])
            f0, f1 = add(c[1], c[3]), times_minus_i(sub(c[1], c[3]))
            return [add(e0, f0), add(e1, f1), sub(e0, f0), sub(e1, f1)]

        half = FFT_RADIX // 2
        even = dft4([add(z[r], z[r + half]) for r in range(half)])
        d = [sub(z[r], z[r + half]) for r in range(half)]
        rt = np.float32(np.sqrt(0.5))
        odd = dft4([
            d[0],
            ((d[1][0] - d[1][1]) * rt, (d[1][0] + d[1][1]) * rt),
            times_minus_i(d[2]),
            (-(d[3][0] + d[3][1]) * rt, (d[3][0] - d[3][1]) * rt),
        ])
        y = [even[q // 2] if q % 2 == 0 else odd[q // 2] for q in range(FFT_RADIX)]
        xr = jnp.concatenate([v[0] for v in y], axis=0)
        xi = jnp.concatenate([v[1] for v in y], axis=0)
        lhs = jnp.concatenate([xr, xi], axis=1).astype(BF16)
        o_ref[0, :, g * gd:(g + 1) * gd] = _dot(lhs, cw_ref[g]).astype(BF16)


def _fourier(u4, tmat, tw, cmat, wf):
    b, n_sub, _ = u4.shape
    s = n_sub * FFT_RADIX
    full = lambda a: pl.BlockSpec(a.shape, lambda i: (0,) * a.ndim)
    return pl.pallas_call(
        _fourier_kernel,
        grid=(b,),
        in_specs=[
            full(tmat), full(tw),
            pl.BlockSpec((1, n_sub, FFT_RADIX * FOURIER_WIDTH), lambda i: (i, 0, 0)),
            full(cmat), full(wf),
        ],
        out_specs=pl.BlockSpec((1, s, FOURIER_WIDTH), lambda i: (i, 0, 0)),
        out_shape=jax.ShapeDtypeStruct((b, s, FOURIER_WIDTH), BF16),
        scratch_shapes=[
            pltpu.VMEM(tmat.shape, BF16),
            pltpu.VMEM((FOURIER_GROUPS, 2 * FOURIER_GROUP_DIM, FOURIER_GROUP_DIM), BF16),
        ],
        compiler_params=pltpu.CompilerParams(
            dimension_semantics=("arbitrary",), vmem_limit_bytes=VMEM_LIMIT_BYTES),
        name="fourier",
    )(tmat, tw, u4, cmat, wf)


def _stage_cast(src_hbm, dst_ref, stage_ref, sem_ref, chunk):
    n_chunks = src_hbm.shape[0] // chunk

    def copy(i):
        slot = i % 2
        return pltpu.make_async_copy(
            src_hbm.at[pl.ds(i * chunk, chunk)], stage_ref.at[slot, pl.ds(0, chunk)], sem_ref.at[slot])

    def body(i, carry):
        @pl.when(i + 1 < n_chunks)
        def _():
            copy(i + 1).start()

        copy(i).wait()
        start = pl.multiple_of(i * chunk, chunk)
        dst_ref[pl.ds(start, chunk), :] = stage_ref[i % 2, pl.ds(0, chunk), :].astype(BF16)
        return carry

    copy(0).start()
    lax.fori_loop(0, n_chunks, body, 0)


def _post_kernel(x_ref, four_ref, am_ref, al_ref, modc_ref, modp_ref, gpm_ref, gpf_ref, gqf_ref,
                 wo_hbm, wg_hbm, wu_hbm, wd_hbm, o_ref,
                 x1_ref, h2_ref, wo_ref, wg_ref, wu_ref, wd_ref, wide_stage, tall_stage, sem_ref):
    t = pl.program_id(0)
    n_four = four_ref.shape[2]

    def mix():
        gt_m = modc_ref[0, 2:3, :]
        sh_f = modc_ref[0, 3:4, :]
        sc_f = modc_ref[0, 4:5, :]
        attn = jnp.concatenate([am_ref[0], al_ref[0]], axis=1)
        y = _dot(four_ref[0], wo_ref[:n_four, :]) + _dot(attn, wo_ref[n_four:, :])
        x1 = x_ref[0] + _rms(y, gt_m * gpm_ref[...])
        x1_ref[...] = x1
        h2_ref[...] = (_rms(x1, gpf_ref[...] * (1.0 + sc_f)) + sh_f).astype(BF16)

    def ffn():
        gt_f = modp_ref[0, 5:6, :]
        h2 = h2_ref[...]
        g = _dot(h2, wg_ref[...])
        up = _dot(h2, wu_ref[...])
        act = (g / (1.0 + jnp.exp(-g)) * up).astype(BF16)
        o_ref[0] = x1_ref[...] + _rms(_dot(act, wd_ref[...]), gt_f * gqf_ref[...])

    @pl.when(t == 0)
    def _():
        _stage_cast(wo_hbm, wo_ref, tall_stage, sem_ref, STAGE_ROWS_OUT)
        mix()
        _stage_cast(wg_hbm, wg_ref, wide_stage, sem_ref, STAGE_ROWS_WIDE)
        _stage_cast(wu_hbm, wu_ref, wide_stage, sem_ref, STAGE_ROWS_WIDE)
        _stage_cast(wd_hbm, wd_ref, tall_stage, sem_ref, STAGE_ROWS_TALL)

    @pl.when(t > 0)
    def _():
        ffn()
        mix()


def _post(x, four, attn_main, attn_last, mod3, gpm, gpf, gqf, wo, wg, wu, wd, tm=512):
    b, s, d = x.shape
    d_ff = wg.shape[1]
    assert wo.shape[0] % STAGE_ROWS_OUT == 0 and STAGE_ROWS_OUT <= STAGE_ROWS_TALL
    assert d % STAGE_ROWS_WIDE == 0 and d_ff % STAGE_ROWS_TALL == 0
    nj = s // tm
    n_blocks = b * nj
    cur = lambda t: jnp.minimum(t, n_blocks - 1)
    prev = lambda t: jnp.maximum(t - 1, 0)
    const = lambda shape: pl.BlockSpec(shape, lambda t: (0,) * len(shape), pipeline_mode=pl.Buffered(1))
    rows = lambda w: pl.BlockSpec((1, tm, w), lambda t: (cur(t) // nj, cur(t) % nj, 0))
    hbm = pl.BlockSpec(memory_space=pl.ANY)
    return pl.pallas_call(
        _post_kernel,
        grid=(n_blocks + 1,),
        in_specs=[
            rows(d), rows(FOURIER_WIDTH), rows(attn_main.shape[2]), rows(attn_last.shape[2]),
            pl.BlockSpec((1, 6, d), lambda t: (cur(t) // nj, 0, 0)),
            pl.BlockSpec((1, 6, d), lambda t: (prev(t) // nj, 0, 0)),
            const(gpm.shape), const(gpf.shape), const(gqf.shape),
            hbm, hbm, hbm, hbm,
        ],
        out_specs=pl.BlockSpec((1, tm, d), lambda t: (prev(t) // nj, prev(t) % nj, 0)),
        out_shape=jax.ShapeDtypeStruct((b, s, d), F32),
        scratch_shapes=[
            pltpu.VMEM((tm, d), F32), pltpu.VMEM((tm, d), BF16),
            pltpu.VMEM(wo.shape, BF16), pltpu.VMEM(wg.shape, BF16), pltpu.VMEM(wu.shape, BF16),
            pltpu.VMEM(wd.shape, BF16),
            pltpu.VMEM((2, STAGE_ROWS_WIDE, d_ff), F32), pltpu.VMEM((2, STAGE_ROWS_TALL, d), F32),
            pltpu.SemaphoreType.DMA((2,)),
        ],
        compiler_params=pltpu.CompilerParams(
            dimension_semantics=("arbitrary",), vmem_limit_bytes=VMEM_LIMIT_BYTES),
        name="post",
    )(x, four, attn_main, attn_last, mod3, mod3, gpm, gpf, gqf, wo, wg, wu, wd)


def _rope_rotate_cols(w):
    a = QK_ROPE_DIM // 2
    hf = a // 2
    blocks = []
    for s0 in (0, a):
        blocks += [-w[..., s0 + hf:s0 + a], w[..., s0:s0 + hf]]
    return jnp.concatenate(blocks, axis=-1)


def _head_slot(nope, rope):
    pad = HEAD_SLOT - QK_NOPE_DIM - QK_ROPE_DIM
    return jnp.concatenate([nope, rope, jnp.zeros(rope.shape[:-1] + (pad,), rope.dtype)], axis=-1)


def _rope_tables(n_lat, q_scale):
    t = np.arange(n_lat)
    hf = QK_ROPE_DIM // 4
    inv_freq = ROPE_BASE ** (-np.arange(hf, dtype=np.float64) / hf)
    ar = (t // GRID_W)[:, None] * inv_freq[None, :]
    ac = (t % GRID_W)[:, None] * inv_freq[None, :]
    z = np.zeros_like(ar)
    cos32 = np.concatenate([np.cos(ar), np.cos(ar), np.cos(ac), np.cos(ac)], axis=-1)
    sin32 = np.concatenate([np.sin(ar), np.sin(ar), np.sin(ac), np.sin(ac)], axis=-1)
    below32 = np.concatenate([z, np.sin(ar), z, np.sin(ac)], axis=-1)
    above32 = np.concatenate([-np.sin(ar), z, -np.sin(ac), z], axis=-1)
    pad = np.zeros((n_lat, HEAD_SLOT - QK_NOPE_DIM - QK_ROPE_DIM))
    ones = np.ones((n_lat, QK_NOPE_DIM))
    zeros = np.zeros((n_lat, QK_NOPE_DIM))
    slot = lambda nope, rope: np.concatenate([nope, rope, pad], axis=-1)
    tables = [slot(ones * q_scale, cos32 * q_scale), slot(zeros, sin32 * q_scale),
              slot(zeros, cos32), slot(zeros, below32), slot(zeros, above32)]
    return jnp.asarray(np.stack(tables).astype(np.float32))


def _dft_tables(n_pos, n_ch):
    n_sub = n_pos // FFT_RADIX
    m = np.arange(n_sub, dtype=np.int64)
    ang = 2.0 * np.pi * ((m[:, None] * m[None, :]) % n_sub) / n_sub
    tmat = np.concatenate([np.cos(ang), np.sin(ang)], axis=0).astype(np.float32)
    tw = np.zeros((FFT_RADIX - 1, 2, n_sub, FOURIER_GROUP_DIM), np.float32)
    for r in range(1, FFT_RADIX):
        a = 2.0 * np.pi * r * m / n_pos
        tw[r - 1, 0] = np.cos(a)[:, None]
        tw[r - 1, 1] = np.sin(a)[:, None]
    c = np.arange(n_ch, dtype=np.int64)
    angc = 2.0 * np.pi * ((c[:, None] * c[None, :]) % n_ch) / n_ch
    norm = 1.0 / np.sqrt(float(n_pos * n_ch))
    cmat = np.concatenate([np.cos(angc) * norm, -np.sin(angc) * norm], axis=0).astype(np.float32)
    return jnp.asarray(tmat), jnp.asarray(tw), jnp.asarray(cmat)


def kernel(x, c, ctx, c_ctx, w_ada, b_ada, g_pre_mix, g_post_mix, g_pre_ffn, g_post_ffn, w_in, g_q_a,
           w_q_b, g_kv_a, w_kv_b, w_fourier, w_out, w_gate, w_up, w_down):
    assert w_ada.shape[0] == 1, "single-layer block"
    batch, n_lat, d = x.shape

    mod_rows = -(-(batch + 1) // 8) * 8
    cc = jnp.concatenate([c, c_ctx[None, :], jnp.zeros((mod_rows - batch - 1, d), F32)], axis=0)
    mod = _adaln(cc, w_ada[0], b_ada[0][None, :])
    mod3 = mod.reshape(mod_rows, 6, d)

    w_in0 = w_in[0]
    w_kr = w_in0[:, ROPE_COL:]
    zeros_d = jnp.zeros((d, QK_NOPE_DIM), F32)
    kr_slot = _head_slot(zeros_d, w_kr)
    win = jnp.concatenate([w_in0[:, :ROPE_COL], kr_slot], axis=1).astype(BF16)
    win_c = jnp.concatenate([w_in0[:, KV_COL:ROPE_COL], kr_slot], axis=1).astype(BF16)

    wq3 = w_q_b[0].reshape(Q_LORA_RANK, MLA_HEADS, QK_NOPE_DIM + QK_ROPE_DIM)
    wq_nope, wq_rope = wq3[..., :QK_NOPE_DIM], wq3[..., QK_NOPE_DIM:]
    wq_a = _head_slot(wq_nope, wq_rope).reshape(Q_LORA_RANK, QK_WIDTH)
    wq_b = _head_slot(jnp.zeros_like(wq_nope), _rope_rotate_cols(wq_rope)).reshape(Q_LORA_RANK, QK_WIDTH)
    wq = jnp.concatenate([wq_a, wq_b], axis=1).astype(BF16)

    wkv3 = w_kv_b[0].reshape(KV_LORA_RANK, MLA_HEADS, QK_NOPE_DIM + V_HEAD_DIM)
    wk_nope, wv = wkv3[..., :QK_NOPE_DIM], wkv3[..., QK_NOPE_DIM:]
    wk_slots = _head_slot(wk_nope, jnp.zeros(wk_nope.shape[:-1] + (QK_ROPE_DIM,), F32))
    wkv = jnp.concatenate([wk_slots.reshape(KV_LORA_RANK, QK_WIDTH),
                           wv.reshape(KV_LORA_RANK, ATTN_WIDTH)], axis=1).astype(BF16)

    q_scale = float((QK_NOPE_DIM + QK_ROPE_DIM) ** -0.5 * np.log2(np.e))
    rope = _rope_tables(n_lat, q_scale)
    tmat, tw, cmat = _dft_tables(n_lat, FOURIER_GROUP_DIM)

    row2 = lambda g: g[0][None, :]
    u_f, q, k_lat, v_lat = _premix(x, mod3, row2(g_pre_mix), win, row2(g_q_a), wq, row2(g_kv_a), wkv,
                                   rope)
    k_ctx, v_ctx = _ctxkv(ctx, mod3, batch, row2(g_pre_mix), win_c, row2(g_kv_a), wkv)
    attn_main, attn_last = _attention(q, k_ctx, k_lat, v_ctx, v_lat)
    four = _fourier(u_f, tmat, tw, cmat, w_fourier[0].astype(BF16))
    return _post(x, four, attn_main, attn_last, mod3, row2(g_post_mix), row2(g_pre_ffn), row2(g_post_ffn),
                 w_out[0], w_gate[0], w_up[0], w_down[0])
```

```python
import numpy as np
import jax
import jax.numpy as jnp
from jax import lax
from jax.experimental import pallas as pl
from jax.experimental.pallas import tpu as pltpu

F32 = jnp.float32
BF16 = jnp.bfloat16

D_MODEL = 1024
GRID_W = 64
FOURIER_GROUPS = 4
FOURIER_GROUP_DIM = 128
FOURIER_WIDTH = FOURIER_GROUPS * FOURIER_GROUP_DIM
MLA_HEADS = 8
QK_NOPE_DIM = 64
QK_ROPE_DIM = 32
V_HEAD_DIM = 64
Q_LORA_RANK = 256
KV_LORA_RANK = 128
KV_COL = FOURIER_WIDTH + Q_LORA_RANK
ROPE_COL = KV_COL + KV_LORA_RANK
ROPE_BASE = 10000.0
NORM_EPS = 1e-6
FFT_RADIX = 8
HEAD_SLOT = 128
STAGE_ROWS_WIDE = 256
STAGE_ROWS_TALL = 704
STAGE_ROWS_OUT = 512
SHIFT_DEN_MIN = 2.0 ** -40
SHIFT_DEN_MAX = 2.0 ** 40
PV_KEY_TILE = 256
ATTN_WIDTH = MLA_HEADS * V_HEAD_DIM
QK_WIDTH = MLA_HEADS * HEAD_SLOT

VMEM_LIMIT_BYTES = 56 * 1024 * 1024


def _rms(x, g):
    return x * lax.rsqrt(jnp.mean(x * x, axis=-1, keepdims=True) + NORM_EPS) * g


def _dot(a, b):
    return jnp.dot(a, b, preferred_element_type=F32)


def _dot_nt(a, b):
    return lax.dot_general(a, b, (((1,), (1,)), ((), ())), preferred_element_type=F32)


def _rotary(a, table, first):
    half = QK_ROPE_DIM // 4
    below = pltpu.roll(a, half, axis=1)
    above = pltpu.roll(a, HEAD_SLOT - half, axis=1)
    return a * table(first) + below * table(first + 1) + above * table(first + 2)


def _store_value_slots(v_ref, v):
    pair_w = 2 * V_HEAD_DIM
    lower = lax.broadcasted_iota(jnp.int32, (v.shape[0], pair_w), 1) < V_HEAD_DIM
    for pair in range(MLA_HEADS // 2):
        vp = v[:, pair * pair_w:(pair + 1) * pair_w]
        lo = 2 * pair * HEAD_SLOT
        v_ref[0, :, lo:lo + HEAD_SLOT] = jnp.where(lower, vp, 1.0).astype(BF16)
        v_ref[0, :, lo + HEAD_SLOT:lo + 2 * HEAD_SLOT] = jnp.where(lower, 1.0, vp).astype(BF16)


def _adaln_kernel(c_ref, w_ref, b_ref, o_ref):
    c = c_ref[...]
    a = c / (1.0 + jnp.exp(-c))
    o_ref[...] = _dot(a.astype(BF16), w_ref[...].astype(BF16)) + b_ref[...]


def _adaln(cc, w_ada, b_ada, tn=1024):
    rows, d = cc.shape
    n = w_ada.shape[1]
    return pl.pallas_call(
        _adaln_kernel,
        grid=(n // tn,),
        in_specs=[
            pl.BlockSpec((rows, d), lambda j: (0, 0)),
            pl.BlockSpec((d, tn), lambda j: (0, j)),
            pl.BlockSpec((1, tn), lambda j: (0, j)),
        ],
        out_specs=pl.BlockSpec((rows, tn), lambda j: (0, j)),
        out_shape=jax.ShapeDtypeStruct((rows, n), F32),
        compiler_params=pltpu.CompilerParams(vmem_limit_bytes=VMEM_LIMIT_BYTES),
        name="adaln",
    )(cc, w_ada, b_ada)


def _premix_kernel(x_ref, mod_ref, gpre_ref, win_ref, gq_ref, wq_ref, gkv_ref, wkv_ref,
                   rope_ref, u_ref, q_ref, k_ref, v_ref, us_ref):
    x = x_ref[0]
    shift = mod_ref[0, 0:1, :]
    scale = mod_ref[0, 1:2, :]
    h = _rms(x, gpre_ref[...] * (1.0 + scale)) + shift
    p = _dot(h.astype(BF16), win_ref[...])
    sub = us_ref.shape[1] // FFT_RADIX
    for g in range(FOURIER_GROUPS):
        glo = g * FOURIER_GROUP_DIM
        us_ref[g] = p[:, glo:glo + FOURIER_GROUP_DIM]
        for r in range(FFT_RADIX):
            lo = r * FOURIER_WIDTH + glo
            u_ref[0, :, lo:lo + FOURIER_GROUP_DIM] = (
                us_ref[g, pl.ds(r, sub, stride=FFT_RADIX), :].astype(BF16))

    qn = _rms(p[:, FOURIER_WIDTH:KV_COL], gq_ref[...]).astype(BF16)
    qq = _dot(qn, wq_ref[...])
    rows = x_ref.shape[1]
    row0 = pl.multiple_of(pl.program_id(1) * rows, rows)
    table = lambda k: rope_ref[k, pl.ds(row0, rows), :]
    cosq = table(0)
    sinq = table(1)
    for hd in range(MLA_HEADS):
        lo = hd * HEAD_SLOT
        q_ref[0, :, lo:lo + HEAD_SLOT] = (
            qq[:, lo:lo + HEAD_SLOT] * cosq + qq[:, QK_WIDTH + lo:QK_WIDTH + lo + HEAD_SLOT] * sinq
        ).astype(BF16)

    kvn = _rms(p[:, KV_COL:ROPE_COL], gkv_ref[...]).astype(BF16)
    kv = _dot(kvn, wkv_ref[...])
    kr = _rotary(p[:, ROPE_COL:ROPE_COL + HEAD_SLOT], table, 2)
    for hd in range(MLA_HEADS):
        lo = hd * HEAD_SLOT
        k_ref[0, :, lo:lo + HEAD_SLOT] = (kv[:, lo:lo + HEAD_SLOT] + kr).astype(BF16)
    _store_value_slots(v_ref, kv[:, QK_WIDTH:])


def _premix(x, mod3, gpre, win, gq, wq, gkv, wkv, rope, tm=512):
    b, s, d = x.shape
    const = lambda shape: pl.BlockSpec(shape, lambda i, j: (0,) * len(shape))
    rows = lambda w: pl.BlockSpec((1, tm, w), lambda i, j: (i, j, 0))
    return pl.pallas_call(
        _premix_kernel,
        grid=(b, s // tm),
        in_specs=[
            rows(d),
            pl.BlockSpec((1, 6, d), lambda i, j: (i, 0, 0)),
            const(gpre.shape), const(win.shape), const(gq.shape), const(wq.shape),
            const(gkv.shape), const(wkv.shape),
            pl.BlockSpec(rope.shape, lambda i, j: (0, 0, 0), pipeline_mode=pl.Buffered(1)),
        ],
        out_specs=[
            pl.BlockSpec((1, tm // FFT_RADIX, FFT_RADIX * FOURIER_WIDTH), lambda i, j: (i, j, 0)),
            rows(QK_WIDTH), rows(QK_WIDTH), rows(QK_WIDTH)],
        out_shape=[
            jax.ShapeDtypeStruct((b, s // FFT_RADIX, FFT_RADIX * FOURIER_WIDTH), BF16),
            jax.ShapeDtypeStruct((b, s, QK_WIDTH), BF16),
            jax.ShapeDtypeStruct((b, s, QK_WIDTH), BF16),
            jax.ShapeDtypeStruct((b, s, QK_WIDTH), BF16),
        ],
        scratch_shapes=[pltpu.VMEM((FOURIER_GROUPS, tm, FOURIER_GROUP_DIM), F32)],
        compiler_params=pltpu.CompilerParams(vmem_limit_bytes=VMEM_LIMIT_BYTES),
        name="premix",
    )(x, mod3, gpre, win, gq, wq, gkv, wkv, rope)


def _ctxkv_kernel(x_ref, mod_ref, gpre_ref, win_ref, gkv_ref, wkv_ref, k_ref, v_ref):
    x = x_ref[0]
    shift = mod_ref[0, 0:1, :]
    scale = mod_ref[0, 1:2, :]
    h = _rms(x, gpre_ref[...] * (1.0 + scale)) + shift
    p = _dot(h.astype(BF16), win_ref[...])
    kvn = _rms(p[:, :KV_LORA_RANK], gkv_ref[...]).astype(BF16)
    kv = _dot(kvn, wkv_ref[...])
    kr = p[:, KV_LORA_RANK:]
    for hd in range(MLA_HEADS):
        lo = hd * HEAD_SLOT
        k_ref[0, :, lo:lo + HEAD_SLOT] = (kv[:, lo:lo + HEAD_SLOT] + kr).astype(BF16)
    _store_value_slots(v_ref, kv[:, QK_WIDTH:])


def _ctxkv(ctx, mod3, ctx_row, gpre, win_c, gkv, wkv):
    b, c, d = ctx.shape
    const = lambda shape: pl.BlockSpec(shape, lambda i: (0,) * len(shape))
    rows = lambda w: pl.BlockSpec((1, c, w), lambda i: (i, 0, 0))
    return pl.pallas_call(
        _ctxkv_kernel,
        grid=(b,),
        in_specs=[
            rows(d),
            pl.BlockSpec((1, 6, d), lambda i: (ctx_row, 0, 0)),
            const(gpre.shape), const(win_c.shape), const(gkv.shape), const(wkv.shape),
        ],
        out_specs=[rows(QK_WIDTH), rows(QK_WIDTH)],
        out_shape=[
            jax.ShapeDtypeStruct((b, c, QK_WIDTH), BF16),
            jax.ShapeDtypeStruct((b, c, QK_WIDTH), BF16),
        ],
        name="ctxkv",
    )(ctx, mod3, gpre, win_c, gkv, wkv)


def _attn_kernel(q_ref, kc_ref, kl_ref, vc_ref, vl_ref, vcp_ref, vlp_ref, omain_ref, olast_ref,
                 p_ref, oprev_ref, ohead_ref):
    t = pl.program_id(0)
    n_blocks = pl.num_programs(0) - 1
    n_ctx = kc_ref.shape[1]
    n_lat = kl_ref.shape[1]
    tq = q_ref.shape[1]
    pair_w = 2 * V_HEAD_DIM
    first_half = lax.broadcasted_iota(jnp.int32, (tq, pair_w), 1) < V_HEAD_DIM
    last = MLA_HEADS - 1

    @pl.when(t == 0)
    def _():
        p_ref[...] = jnp.ones_like(p_ref)
        oprev_ref[...] = jnp.ones_like(oprev_ref)

    def join_pair(o_even, o_odd):
        num = jnp.where(first_half, o_even, o_odd)
        den = pltpu.roll(jnp.where(first_half, o_odd, o_even), V_HEAD_DIM, axis=1)
        return num / den

    def den_of(o, hd):
        lane = V_HEAD_DIM if hd % 2 == 0 else 0
        return o[:, lane:lane + 1]

    def drain():
        o = _dot(p_ref[:, :n_ctx], vcp_ref[0]) + _dot(p_ref[:, n_ctx:], vlp_ref[0])
        olast_ref[0] = join_pair(oprev_ref[...], o).astype(BF16)

    def head(hd, exact):
        lo = hd * HEAD_SLOT
        qh = q_ref[0, :, lo:lo + HEAD_SLOT]
        s_c = _dot_nt(qh, kc_ref[0, :, lo:lo + HEAD_SLOT])
        s_l = _dot_nt(qh, kl_ref[0, :, lo:lo + HEAD_SLOT])
        shift = jnp.max(s_c, axis=-1, keepdims=True)
        if exact:
            shift = jnp.maximum(shift, jnp.max(s_l, axis=-1, keepdims=True))
        if hd == last:
            p_c = jnp.exp2(s_c - shift)
            p_l = jnp.exp2(s_l - shift)
            p_ref[:, :n_ctx] = p_c.astype(BF16)
            p_ref[:, n_ctx:] = p_l.astype(BF16)
            return None, jnp.sum(p_c, axis=-1, keepdims=True) + jnp.sum(p_l, axis=-1, keepdims=True)
        o = _dot(jnp.exp2(s_c - shift).astype(BF16), vc_ref[0, :, lo:lo + HEAD_SLOT])
        for k0 in range(0, n_lat, PV_KEY_TILE):
            p = jnp.exp2(s_l[:, k0:k0 + PV_KEY_TILE] - shift).astype(BF16)
            o = o + _dot(p, vl_ref[0, k0:k0 + PV_KEY_TILE, lo:lo + HEAD_SLOT])
        return o, den_of(o, hd)

    def block():
        dens = []
        for pair in range(MLA_HEADS // 2):
            outs = []
            for hd in (2 * pair, 2 * pair + 1):
                o, den = head(hd, False)
                dens.append(den)
                if o is not None:
                    outs.append(o)
            if len(outs) == 2:
                vlo = pair * pair_w
                omain_ref[0, :, vlo:vlo + pair_w] = join_pair(outs[0], outs[1]).astype(BF16)
            else:
                oprev_ref[...] = outs[0]
        return dens

    def redo_block_exact():
        def one_head(hd, carry):
            lo = pl.multiple_of(hd * HEAD_SLOT, HEAD_SLOT)
            qh = q_ref[0, :, pl.ds(lo, HEAD_SLOT)]
            s_c = _dot_nt(qh, kc_ref[0, :, pl.ds(lo, HEAD_SLOT)])
            s_l = _dot_nt(qh, kl_ref[0, :, pl.ds(lo, HEAD_SLOT)])
            shift = jnp.maximum(jnp.max(s_c, axis=-1, keepdims=True), jnp.max(s_l, axis=-1, keepdims=True))
            ohead_ref[hd] = (_dot(jnp.exp2(s_c - shift).astype(BF16), vc_ref[0, :, pl.ds(lo, HEAD_SLOT)])
                             + _dot(jnp.exp2(s_l - shift).astype(BF16), vl_ref[0, :, pl.ds(lo, HEAD_SLOT)]))
            return carry

        lax.fori_loop(0, last, one_head, 0)
        for pair in range(MLA_HEADS // 2 - 1):
            vlo = pair * pair_w
            omain_ref[0, :, vlo:vlo + pair_w] = join_pair(ohead_ref[2 * pair], ohead_ref[2 * pair + 1]).astype(BF16)
        oprev_ref[...] = ohead_ref[last - 1]
        head(last, True)

    @pl.when(t < n_blocks)
    def _():
        drain()
        dens = block()
        lo_den, hi_den = dens[0], dens[0]
        for den in dens[1:]:
            lo_den = jnp.minimum(lo_den, den)
            hi_den = jnp.maximum(hi_den, den)
        trusted = jnp.logical_and(jnp.min(lo_den) >= SHIFT_DEN_MIN, jnp.max(hi_den) <= SHIFT_DEN_MAX)

        @pl.when(jnp.logical_not(trusted))
        def _():
            redo_block_exact()

    @pl.when(t == n_blocks)
    def _():
        drain()


def _attention(q, kc, kl, vc, vl, tq=512):
    b, s, _ = q.shape
    c = kc.shape[1]
    nq = s // tq
    n_blocks = b * nq
    pair_w = 2 * V_HEAD_DIM
    main_w = ATTN_WIDTH - pair_w
    last = MLA_HEADS - 1
    cur = lambda t: jnp.minimum(t, n_blocks - 1)
    prev = lambda t: jnp.maximum(t - 1, 0)
    return pl.pallas_call(
        _attn_kernel,
        grid=(n_blocks + 1,),
        in_specs=[
            pl.BlockSpec((1, tq, QK_WIDTH), lambda t: (cur(t) // nq, cur(t) % nq, 0)),
            pl.BlockSpec((1, c, QK_WIDTH), lambda t: (cur(t) // nq, 0, 0)),
            pl.BlockSpec((1, s, QK_WIDTH), lambda t: (cur(t) // nq, 0, 0)),
            pl.BlockSpec((1, c, QK_WIDTH), lambda t: (cur(t) // nq, 0, 0)),
            pl.BlockSpec((1, s, QK_WIDTH), lambda t: (cur(t) // nq, 0, 0)),
            pl.BlockSpec((1, c, HEAD_SLOT), lambda t: (prev(t) // nq, 0, last)),
            pl.BlockSpec((1, s, HEAD_SLOT), lambda t: (prev(t) // nq, 0, last)),
        ],
        out_specs=[
            pl.BlockSpec((1, tq, main_w), lambda t: (cur(t) // nq, cur(t) % nq, 0)),
            pl.BlockSpec((1, tq, pair_w), lambda t: (prev(t) // nq, prev(t) % nq, 0)),
        ],
        out_shape=[
            jax.ShapeDtypeStruct((b, s, main_w), BF16),
            jax.ShapeDtypeStruct((b, s, pair_w), BF16),
        ],
        scratch_shapes=[
            pltpu.VMEM((tq, c + s), BF16),
            pltpu.VMEM((tq, pair_w), F32),
            pltpu.VMEM((MLA_HEADS - 1, tq, HEAD_SLOT), F32),
        ],
        compiler_params=pltpu.CompilerParams(
            dimension_semantics=("arbitrary",), vmem_limit_bytes=VMEM_LIMIT_BYTES),
        name="attn",
    )(q, kc, kl, vc, vl, vc, vl)


def _fourier_kernel(t_ref, tw_ref, u_ref, cc_ref, wf_ref, o_ref, tb_ref, cw_ref):
    @pl.when(pl.program_id(0) == 0)
    def _():
        tb_ref[...] = t_ref[...].astype(BF16)
        cc = cc_ref[...].astype(BF16)
        for g in range(FOURIER_GROUPS):
            cw_ref[g] = _dot(cc, wf_ref[g]).astype(BF16)

    n_sub = tb_ref.shape[1]
    f = _dot(tb_ref[...], u_ref[0])
    gd = FOURIER_GROUP_DIM
    for g in range(FOURIER_GROUPS):
        gr, gi = [], []
        for r in range(FFT_RADIX):
            lo = r * FOURIER_WIDTH + g * gd
            a = f[:n_sub, lo:lo + gd]
            b = f[n_sub:, lo:lo + gd]
            if r == 0:
                gr.append(a)
                gi.append(b)
            else:
                c = tw_ref[r - 1, 0]
                s = tw_ref[r - 1, 1]
                gr.append(a * c - b * s)
                gi.append(a * s + b * c)
        z = list(zip(gr, gi))
        add = lambda a, b: (a[0] + b[0], a[1] + b[1])
        sub = lambda a, b: (a[0] - b[0], a[1] - b[1])
        times_minus_i = lambda a: (-a[1], a[0])

        def dft4(c):
            e0, e1 = add(c[0], c[2]), sub(c[0], c[2])
            f0, f1 = add(c[1], c[3]), times_minus_i(sub(c[1], c[3]))
            return [add(e0, f0), add(e1, f1), sub(e0, f0), sub(e1, f1)]

        half = FFT_RADIX // 2
        even = dft4([add(z[r], z[r + half]) for r in range(half)])
        d = [sub(z[r], z[r + half]) for r in range(half)]
        rt = np.float32(np.sqrt(0.5))
        odd = dft4([
            d[0],
            ((d[1][0] - d[1][1]) * rt, (d[1][0] + d[1][1]) * rt),
            times_minus_i(d[2]),
            (-(d[3][0] + d[3][1]) * rt, (d[3][0] - d[3][1]) * rt),
        ])
        y = [even[q // 2] if q % 2 == 0 else odd[q // 2] for q in range(FFT_RADIX)]
        xr = jnp.concatenate([v[0] for v in y], axis=0)
        xi = jnp.concatenate([v[1] for v in y], axis=0)
        lhs = jnp.concatenate([xr, xi], axis=1).astype(BF16)
        o_ref[0, :, g * gd:(g + 1) * gd] = _dot(lhs, cw_ref[g]).astype(BF16)


def _fourier(u4, tmat, tw, cmat, wf):
    b, n_sub, _ = u4.shape
    s = n_sub * FFT_RADIX
    full = lambda a: pl.BlockSpec(a.shape, lambda i: (0,) * a.ndim)
    return pl.pallas_call(
        _fourier_kernel,
        grid=(b,),
        in_specs=[
            full(tmat), full(tw),
            pl.BlockSpec((1, n_sub, FFT_RADIX * FOURIER_WIDTH), lambda i: (i, 0, 0)),
            full(cmat), full(wf),
        ],
        out_specs=pl.BlockSpec((1, s, FOURIER_WIDTH), lambda i: (i, 0, 0)),
        out_shape=jax.ShapeDtypeStruct((b, s, FOURIER_WIDTH), BF16),
        scratch_shapes=[
            pltpu.VMEM(tmat.shape, BF16),
            pltpu.VMEM((FOURIER_GROUPS, 2 * FOURIER_GROUP_DIM, FOURIER_GROUP_DIM), BF16),
        ],
        compiler_params=pltpu.CompilerParams(
            dimension_semantics=("arbitrary",), vmem_limit_bytes=VMEM_LIMIT_BYTES),
        name="fourier",
    )(tmat, tw, u4, cmat, wf)


def _stage_cast(src_hbm, dst_ref, stage_ref, sem_ref, chunk):
    n_chunks = src_hbm.shape[0] // chunk

    def copy(i):
        slot = i % 2
        return pltpu.make_async_copy(
            src_hbm.at[pl.ds(i * chunk, chunk)], stage_ref.at[slot, pl.ds(0, chunk)], sem_ref.at[slot])

    def body(i, carry):
        @pl.when(i + 1 < n_chunks)
        def _():
            copy(i + 1).start()

        copy(i).wait()
        start = pl.multiple_of(i * chunk, chunk)
        dst_ref[pl.ds(start, chunk), :] = stage_ref[i % 2, pl.ds(0, chunk), :].astype(BF16)
        return carry

    copy(0).start()
    lax.fori_loop(0, n_chunks, body, 0)


def _post_kernel(x_ref, four_ref, am_ref, al_ref, modc_ref, modp_ref, gpm_ref, gpf_ref, gqf_ref,
                 wo_hbm, wg_hbm, wu_hbm, wd_hbm, o_ref,
                 x1_ref, h2_ref, wo_ref, wg_ref, wu_ref, wd_ref, wide_stage, tall_stage, sem_ref):
    t = pl.program_id(0)
    n_four = four_ref.shape[2]

    def mix():
        gt_m = modc_ref[0, 2:3, :]
        sh_f = modc_ref[0, 3:4, :]
        sc_f = modc_ref[0, 4:5, :]
        attn = jnp.concatenate([am_ref[0], al_ref[0]], axis=1)
        y = _dot(four_ref[0], wo_ref[:n_four, :]) + _dot(attn, wo_ref[n_four:, :])
        x1 = x_ref[0] + _rms(y, gt_m * gpm_ref[...])
        x1_ref[...] = x1
        h2_ref[...] = (_rms(x1, gpf_ref[...] * (1.0 + sc_f)) + sh_f).astype(BF16)

    def ffn():
        gt_f = modp_ref[0, 5:6, :]
        h2 = h2_ref[...]
        g = _dot(h2, wg_ref[...])
        up = _dot(h2, wu_ref[...])
        act = (g / (1.0 + jnp.exp(-g)) * up).astype(BF16)
        o_ref[0] = x1_ref[...] + _rms(_dot(act, wd_ref[...]), gt_f * gqf_ref[...])

    @pl.when(t == 0)
    def _():
        _stage_cast(wo_hbm, wo_ref, tall_stage, sem_ref, STAGE_ROWS_OUT)
        mix()
        _stage_cast(wg_hbm, wg_ref, wide_stage, sem_ref, STAGE_ROWS_WIDE)
        _stage_cast(wu_hbm, wu_ref, wide_stage, sem_ref, STAGE_ROWS_WIDE)
        _stage_cast(wd_hbm, wd_ref, tall_stage, sem_ref, STAGE_ROWS_TALL)

    @pl.when(t > 0)
    def _():
        ffn()
        mix()


def _post(x, four, attn_main, attn_last, mod3, gpm, gpf, gqf, wo, wg, wu, wd, tm=512):
    b, s, d = x.shape
    d_ff = wg.shape[1]
    assert wo.shape[0] % STAGE_ROWS_OUT == 0 and STAGE_ROWS_OUT <= STAGE_ROWS_TALL
    assert d % STAGE_ROWS_WIDE == 0 and d_ff % STAGE_ROWS_TALL == 0
    nj = s // tm
    n_blocks = b * nj
    cur = lambda t: jnp.minimum(t, n_blocks - 1)
    prev = lambda t: jnp.maximum(t - 1, 0)
    const = lambda shape: pl.BlockSpec(shape, lambda t: (0,) * len(shape), pipeline_mode=pl.Buffered(1))
    rows = lambda w: pl.BlockSpec((1, tm, w), lambda t: (cur(t) // nj, cur(t) % nj, 0))
    hbm = pl.BlockSpec(memory_space=pl.ANY)
    return pl.pallas_call(
        _post_kernel,
        grid=(n_blocks + 1,),
        in_specs=[
            rows(d), rows(FOURIER_WIDTH), rows(attn_main.shape[2]), rows(attn_last.shape[2]),
            pl.BlockSpec((1, 6, d), lambda t: (cur(t) // nj, 0, 0)),
            pl.BlockSpec((1, 6, d), lambda t: (prev(t) // nj, 0, 0)),
            const(gpm.shape), const(gpf.shape), const(gqf.shape),
            hbm, hbm, hbm, hbm,
        ],
        out_specs=pl.BlockSpec((1, tm, d), lambda t: (prev(t) // nj, prev(t) % nj, 0)),
        out_shape=jax.ShapeDtypeStruct((b, s, d), F32),
        scratch_shapes=[
            pltpu.VMEM((tm, d), F32), pltpu.VMEM((tm, d), BF16),
            pltpu.VMEM(wo.shape, BF16), pltpu.VMEM(wg.shape, BF16), pltpu.VMEM(wu.shape, BF16),
            pltpu.VMEM(wd.shape, BF16),
            pltpu.VMEM((2, STAGE_ROWS_WIDE, d_ff), F32), pltpu.VMEM((2, STAGE_ROWS_TALL, d), F32),
            pltpu.SemaphoreType.DMA((2,)),
        ],
        compiler_params=pltpu.CompilerParams(
            dimension_semantics=("arbitrary",), vmem_limit_bytes=VMEM_LIMIT_BYTES),
        name="post",
    )(x, four, attn_main, attn_last, mod3, mod3, gpm, gpf, gqf, wo, wg, wu, wd)


def _rope_rotate_cols(w):
    a = QK_ROPE_DIM // 2
    hf = a // 2
    blocks = []
    for s0 in (0, a):
        blocks += [-w[..., s0 + hf:s0 + a], w[..., s0:s0 + hf]]
    return jnp.concatenate(blocks, axis=-1)


def _head_slot(nope, rope):
    pad = HEAD_SLOT - QK_NOPE_DIM - QK_ROPE_DIM
    return jnp.concatenate([nope, rope, jnp.zeros(rope.shape[:-1] + (pad,), rope.dtype)], axis=-1)


def _rope_tables(n_lat, q_scale):
    t = np.arange(n_lat)
    hf = QK_ROPE_DIM // 4
    inv_freq = ROPE_BASE ** (-np.arange(hf, dtype=np.float64) / hf)
    ar = (t // GRID_W)[:, None] * inv_freq[None, :]
    ac = (t % GRID_W)[:, None] * inv_freq[None, :]
    z = np.zeros_like(ar)
    cos32 = np.concatenate([np.cos(ar), np.cos(ar), np.cos(ac), np.cos(ac)], axis=-1)
    sin32 = np.concatenate([np.sin(ar), np.sin(ar), np.sin(ac), np.sin(ac)], axis=-1)
    below32 = np.concatenate([z, np.sin(ar), z, np.sin(ac)], axis=-1)
    above32 = np.concatenate([-np.sin(ar), z, -np.sin(ac), z], axis=-1)
    pad = np.zeros((n_lat, HEAD_SLOT - QK_NOPE_DIM - QK_ROPE_DIM))
    ones = np.ones((n_lat, QK_NOPE_DIM))
    zeros = np.zeros((n_lat, QK_NOPE_DIM))
    slot = lambda nope, rope: np.concatenate([nope, rope, pad], axis=-1)
    tables = [slot(ones * q_scale, cos32 * q_scale), slot(zeros, sin32 * q_scale),
              slot(zeros, cos32), slot(zeros, below32), slot(zeros, above32)]
    return jnp.asarray(np.stack(tables).astype(np.float32))


def _dft_tables(n_pos, n_ch):
    n_sub = n_pos // FFT_RADIX
    m = np.arange(n_sub, dtype=np.int64)
    ang = 2.0 * np.pi * ((m[:, None] * m[None, :]) % n_sub) / n_sub
    tmat = np.concatenate([np.cos(ang), np.sin(ang)], axis=0).astype(np.float32)
    tw = np.zeros((FFT_RADIX - 1, 2, n_sub, FOURIER_GROUP_DIM), np.float32)
    for r in range(1, FFT_RADIX):
        a = 2.0 * np.pi * r * m / n_pos
        tw[r - 1, 0] = np.cos(a)[:, None]
        tw[r - 1, 1] = np.sin(a)[:, None]
    c = np.arange(n_ch, dtype=np.int64)
    angc = 2.0 * np.pi * ((c[:, None] * c[None, :]) % n_ch) / n_ch
    norm = 1.0 / np.sqrt(float(n_pos * n_ch))
    cmat = np.concatenate([np.cos(angc) * norm, -np.sin(angc) * norm], axis=0).astype(np.float32)
    return jnp.asarray(tmat), jnp.asarray(tw), jnp.asarray(cmat)


def kernel(x, c, ctx, c_ctx, w_ada, b_ada, g_pre_mix, g_post_mix, g_pre_ffn, g_post_ffn, w_in, g_q_a,
           w_q_b, g_kv_a, w_kv_b, w_fourier, w_out, w_gate, w_up, w_down):
    assert w_ada.shape[0] == 1, "single-layer block"
    batch, n_lat, d = x.shape

    mod_rows = -(-(batch + 1) // 8) * 8
    cc = jnp.concatenate([c, c_ctx[None, :], jnp.zeros((mod_rows - batch - 1, d), F32)], axis=0)
    mod = _adaln(cc, w_ada[0], b_ada[0][None, :])
    mod3 = mod.reshape(mod_rows, 6, d)

    w_in0 = w_in[0]
    w_kr = w_in0[:, ROPE_COL:]
    zeros_d = jnp.zeros((d, QK_NOPE_DIM), F32)
    kr_slot = _head_slot(zeros_d, w_kr)
    win = jnp.concatenate([w_in0[:, :ROPE_COL], kr_slot], axis=1).astype(BF16)
    win_c = jnp.concatenate([w_in0[:, KV_COL:ROPE_COL], kr_slot], axis=1).astype(BF16)

    wq3 = w_q_b[0].reshape(Q_LORA_RANK, MLA_HEADS, QK_NOPE_DIM + QK_ROPE_DIM)
    wq_nope, wq_rope = wq3[..., :QK_NOPE_DIM], wq3[..., QK_NOPE_DIM:]
    wq_a = _head_slot(wq_nope, wq_rope).reshape(Q_LORA_RANK, QK_WIDTH)
    wq_b = _head_slot(jnp.zeros_like(wq_nope), _rope_rotate_cols(wq_rope)).reshape(Q_LORA_RANK, QK_WIDTH)
    wq = jnp.concatenate([wq_a, wq_b], axis=1).astype(BF16)

    wkv3 = w_kv_b[0].reshape(KV_LORA_RANK, MLA_HEADS, QK_NOPE_DIM + V_HEAD_DIM)
    wk_nope, wv = wkv3[..., :QK_NOPE_DIM], wkv3[..., QK_NOPE_DIM:]
    wk_slots = _head_slot(wk_nope, jnp.zeros(wk_nope.shape[:-1] + (QK_ROPE_DIM,), F32))
    wkv = jnp.concatenate([wk_slots.reshape(KV_LORA_RANK, QK_WIDTH),
                           wv.reshape(KV_LORA_RANK, ATTN_WIDTH)], axis=1).astype(BF16)

    q_scale = float((QK_NOPE_DIM + QK_ROPE_DIM) ** -0.5 * np.log2(np.e))
    rope = _rope_tables(n_lat, q_scale)
    tmat, tw, cmat = _dft_tables(n_lat, FOURIER_GROUP_DIM)

    row2 = lambda g: g[0][None, :]
    u_f, q, k_lat, v_lat = _premix(x, mod3, row2(g_pre_mix), win, row2(g_q_a), wq, row2(g_kv_a), wkv,
                                   rope)
    k_ctx, v_ctx = _ctxkv(ctx, mod3, batch, row2(g_pre_mix), win_c, row2(g_kv_a), wkv)
    attn_main, attn_last = _attention(q, k_ctx, k_lat, v_ctx, v_lat)
    four = _fourier(u_f, tmat, tw, cmat, w_fourier[0].astype(BF16))
    return _post(x, four, attn_main, attn_last, mod3, row2(g_post_mix), row2(g_pre_ffn), row2(g_post_ffn),
                 w_out[0], w_gate[0], w_up[0], w_down[0])
```

```python
import numpy as np
import jax
import jax.numpy as jnp
from jax import lax
from jax.experimental import pallas as pl
from jax.experimental.pallas import tpu as pltpu

F32 = jnp.float32
BF16 = jnp.bfloat16

D_MODEL = 1024
GRID_W = 64
FOURIER_GROUPS = 4
FOURIER_GROUP_DIM = 128
FOURIER_WIDTH = FOURIER_GROUPS * FOURIER_GROUP_DIM
MLA_HEADS = 8
QK_NOPE_DIM = 64
QK_ROPE_DIM = 32
V_HEAD_DIM = 64
Q_LORA_RANK = 256
KV_LORA_RANK = 128
KV_COL = FOURIER_WIDTH + Q_LORA_RANK
ROPE_COL = KV_COL + KV_LORA_RANK
ROPE_BASE = 10000.0
NORM_EPS = 1e-6
FFT_RADIX = 8
HEAD_SLOT = 128
STAGE_ROWS_WIDE = 256
STAGE_ROWS_TALL = 704
STAGE_ROWS_OUT = 512
SHIFT_DEN_MIN = 2.0 ** -40
SHIFT_DEN_MAX = 2.0 ** 40
PV_KEY_TILE = 256
ATTN_WIDTH = MLA_HEADS * V_HEAD_DIM
QK_WIDTH = MLA_HEADS * HEAD_SLOT

VMEM_LIMIT_BYTES = 56 * 1024 * 1024


def _rms(x, g):
    return x * lax.rsqrt(jnp.mean(x * x, axis=-1, keepdims=True) + NORM_EPS) * g


def _dot(a, b):
    return jnp.dot(a, b, preferred_element_type=F32)


def _dot_nt(a, b):
    return lax.dot_general(a, b, (((1,), (1,)), ((), ())), preferred_element_type=F32)


def _rotary(a, table, first):
    half = QK_ROPE_DIM // 4
    below = pltpu.roll(a, half, axis=1)
    above = pltpu.roll(a, HEAD_SLOT - half, axis=1)
    return a * table(first) + below * table(first + 1) + above * table(first + 2)


def _store_value_slots(v_ref, v):
    pair_w = 2 * V_HEAD_DIM
    lower = lax.broadcasted_iota(jnp.int32, (v.shape[0], pair_w), 1) < V_HEAD_DIM
    for pair in range(MLA_HEADS // 2):
        vp = v[:, pair * pair_w:(pair + 1) * pair_w]
        lo = 2 * pair * HEAD_SLOT
        v_ref[0, :, lo:lo + HEAD_SLOT] = jnp.where(lower, vp, 1.0).astype(BF16)
        v_ref[0, :, lo + HEAD_SLOT:lo + 2 * HEAD_SLOT] = jnp.where(lower, 1.0, vp).astype(BF16)


def _adaln_kernel(c_ref, w_ref, b_ref, o_ref):
    c = c_ref[...]
    a = c / (1.0 + jnp.exp(-c))
    o_ref[...] = _dot(a.astype(BF16), w_ref[...].astype(BF16)) + b_ref[...]


def _adaln(cc, w_ada, b_ada, tn=1024):
    rows, d = cc.shape
    n = w_ada.shape[1]
    return pl.pallas_call(
        _adaln_kernel,
        grid=(n // tn,),
        in_specs=[
            pl.BlockSpec((rows, d), lambda j: (0, 0)),
            pl.BlockSpec((d, tn), lambda j: (0, j)),
            pl.BlockSpec((1, tn), lambda j: (0, j)),
        ],
        out_specs=pl.BlockSpec((rows, tn), lambda j: (0, j)),
        out_shape=jax.ShapeDtypeStruct((rows, n), F32),
        compiler_params=pltpu.CompilerParams(vmem_limit_bytes=VMEM_LIMIT_BYTES),
        name="adaln",
    )(cc, w_ada, b_ada)


def _premix_kernel(x_ref, mod_ref, gpre_ref, win_ref, gq_ref, wq_ref, gkv_ref, wkv_ref,
                   rope_ref, u_ref, q_ref, k_ref, v_ref, us_ref):
    x = x_ref[0]
    shift = mod_ref[0, 0:1, :]
    scale = mod_ref[0, 1:2, :]
    h = _rms(x, gpre_ref[...] * (1.0 + scale)) + shift
    p = _dot(h.astype(BF16), win_ref[...])
    sub = us_ref.shape[1] // FFT_RADIX
    for g in range(FOURIER_GROUPS):
        glo = g * FOURIER_GROUP_DIM
        us_ref[g] = p[:, glo:glo + FOURIER_GROUP_DIM]
        for r in range(FFT_RADIX):
            lo = r * FOURIER_WIDTH + glo
            u_ref[0, :, lo:lo + FOURIER_GROUP_DIM] = (
                us_ref[g, pl.ds(r, sub, stride=FFT_RADIX), :].astype(BF16))

    qn = _rms(p[:, FOURIER_WIDTH:KV_COL], gq_ref[...]).astype(BF16)
    qq = _dot(qn, wq_ref[...])
    rows = x_ref.shape[1]
    row0 = pl.multiple_of(pl.program_id(1) * rows, rows)
    table = lambda k: rope_ref[k, pl.ds(row0, rows), :]
    cosq = table(0)
    sinq = table(1)
    for hd in range(MLA_HEADS):
        lo = hd * HEAD_SLOT
        q_ref[0, :, lo:lo + HEAD_SLOT] = (
            qq[:, lo:lo + HEAD_SLOT] * cosq + qq[:, QK_WIDTH + lo:QK_WIDTH + lo + HEAD_SLOT] * sinq
        ).astype(BF16)

    kvn = _rms(p[:, KV_COL:ROPE_COL], gkv_ref[...]).astype(BF16)
    kv = _dot(kvn, wkv_ref[...])
    kr = _rotary(p[:, ROPE_COL:ROPE_COL + HEAD_SLOT], table, 2)
    for hd in range(MLA_HEADS):
        lo = hd * HEAD_SLOT
        k_ref[0, :, lo:lo + HEAD_SLOT] = (kv[:, lo:lo + HEAD_SLOT] + kr).astype(BF16)
    _store_value_slots(v_ref, kv[:, QK_WIDTH:])


def _premix(x, mod3, gpre, win, gq, wq, gkv, wkv, rope, tm=1024):
    b, s, d = x.shape
    const = lambda shape: pl.BlockSpec(shape, lambda i, j: (0,) * len(shape))
    rows = lambda w: pl.BlockSpec((1, tm, w), lambda i, j: (i, j, 0))
    return pl.pallas_call(
        _premix_kernel,
        grid=(b, s // tm),
        in_specs=[
            rows(d),
            pl.BlockSpec((1, 6, d), lambda i, j: (i, 0, 0)),
            const(gpre.shape), const(win.shape), const(gq.shape), const(wq.shape),
            const(gkv.shape), const(wkv.shape),
            pl.BlockSpec(rope.shape, lambda i, j: (0, 0, 0), pipeline_mode=pl.Buffered(1)),
        ],
        out_specs=[
            pl.BlockSpec((1, tm // FFT_RADIX, FFT_RADIX * FOURIER_WIDTH), lambda i, j: (i, j, 0)),
            rows(QK_WIDTH), rows(QK_WIDTH), rows(QK_WIDTH)],
        out_shape=[
            jax.ShapeDtypeStruct((b, s // FFT_RADIX, FFT_RADIX * FOURIER_WIDTH), BF16),
            jax.ShapeDtypeStruct((b, s, QK_WIDTH), BF16),
            jax.ShapeDtypeStruct((b, s, QK_WIDTH), BF16),
            jax.ShapeDtypeStruct((b, s, QK_WIDTH), BF16),
        ],
        scratch_shapes=[pltpu.VMEM((FOURIER_GROUPS, tm, FOURIER_GROUP_DIM), F32)],
        compiler_params=pltpu.CompilerParams(vmem_limit_bytes=VMEM_LIMIT_BYTES),
        name="premix",
    )(x, mod3, gpre, win, gq, wq, gkv, wkv, rope)


def _ctxkv_kernel(x_ref, mod_ref, gpre_ref, win_ref, gkv_ref, wkv_ref, k_ref, v_ref):
    x = x_ref[0]
    shift = mod_ref[0, 0:1, :]
    scale = mod_ref[0, 1:2, :]
    h = _rms(x, gpre_ref[...] * (1.0 + scale)) + shift
    p = _dot(h.astype(BF16), win_ref[...])
    kvn = _rms(p[:, :KV_LORA_RANK], gkv_ref[...]).astype(BF16)
    kv = _dot(kvn, wkv_ref[...])
    kr = p[:, KV_LORA_RANK:]
    for hd in range(MLA_HEADS):
        lo = hd * HEAD_SLOT
        k_ref[0, :, lo:lo + HEAD_SLOT] = (kv[:, lo:lo + HEAD_SLOT] + kr).astype(BF16)
    _store_value_slots(v_ref, kv[:, QK_WIDTH:])


def _ctxkv(ctx, mod3, ctx_row, gpre, win_c, gkv, wkv):
    b, c, d = ctx.shape
    const = lambda shape: pl.BlockSpec(shape, lambda i: (0,) * len(shape))
    rows = lambda w: pl.BlockSpec((1, c, w), lambda i: (i, 0, 0))
    return pl.pallas_call(
        _ctxkv_kernel,
        grid=(b,),
        in_specs=[
            rows(d),
            pl.BlockSpec((1, 6, d), lambda i: (ctx_row, 0, 0)),
            const(gpre.shape), const(win_c.shape), const(gkv.shape), const(wkv.shape),
        ],
        out_specs=[rows(QK_WIDTH), rows(QK_WIDTH)],
        out_shape=[
            jax.ShapeDtypeStruct((b, c, QK_WIDTH), BF16),
            jax.ShapeDtypeStruct((b, c, QK_WIDTH), BF16),
        ],
        name="ctxkv",
    )(ctx, mod3, gpre, win_c, gkv, wkv)


def _attn_kernel(q_ref, kc_ref, kl_ref, vc_ref, vl_ref, vcp_ref, vlp_ref, omain_ref, olast_ref,
                 p_ref, oprev_ref, ohead_ref):
    t = pl.program_id(0)
    n_blocks = pl.num_programs(0) - 1
    n_ctx = kc_ref.shape[1]
    n_lat = kl_ref.shape[1]
    tq = q_ref.shape[1]
    pair_w = 2 * V_HEAD_DIM
    first_half = lax.broadcasted_iota(jnp.int32, (tq, pair_w), 1) < V_HEAD_DIM
    last = MLA_HEADS - 1

    @pl.when(t == 0)
    def _():
        p_ref[...] = jnp.ones_like(p_ref)
        oprev_ref[...] = jnp.ones_like(oprev_ref)

    def join_pair(o_even, o_odd):
        num = jnp.where(first_half, o_even, o_odd)
        den = pltpu.roll(jnp.where(first_half, o_odd, o_even), V_HEAD_DIM, axis=1)
        return num / den

    def den_of(o, hd):
        lane = V_HEAD_DIM if hd % 2 == 0 else 0
        return o[:, lane:lane + 1]

    def drain():
        o = _dot(p_ref[:, :n_ctx], vcp_ref[0]) + _dot(p_ref[:, n_ctx:], vlp_ref[0])
        olast_ref[0] = join_pair(oprev_ref[...], o).astype(BF16)

    def head(hd, exact):
        lo = hd * HEAD_SLOT
        qh = q_ref[0, :, lo:lo + HEAD_SLOT]
        s_c = _dot_nt(qh, kc_ref[0, :, lo:lo + HEAD_SLOT])
        s_l = _dot_nt(qh, kl_ref[0, :, lo:lo + HEAD_SLOT])
        shift = jnp.max(s_c, axis=-1, keepdims=True)
        if exact:
            shift = jnp.maximum(shift, jnp.max(s_l, axis=-1, keepdims=True))
        if hd == last:
            p_c = jnp.exp2(s_c - shift)
            p_l = jnp.exp2(s_l - shift)
            p_ref[:, :n_ctx] = p_c.astype(BF16)
            p_ref[:, n_ctx:] = p_l.astype(BF16)
            return None, jnp.sum(p_c, axis=-1, keepdims=True) + jnp.sum(p_l, axis=-1, keepdims=True)
        o = _dot(jnp.exp2(s_c - shift).astype(BF16), vc_ref[0, :, lo:lo + HEAD_SLOT])
        for k0 in range(0, n_lat, PV_KEY_TILE):
            p = jnp.exp2(s_l[:, k0:k0 + PV_KEY_TILE] - shift).astype(BF16)
            o = o + _dot(p, vl_ref[0, k0:k0 + PV_KEY_TILE, lo:lo + HEAD_SLOT])
        return o, den_of(o, hd)

    def block():
        dens = []
        for pair in range(MLA_HEADS // 2):
            outs = []
            for hd in (2 * pair, 2 * pair + 1):
                o, den = head(hd, False)
                dens.append(den)
                if o is not None:
                    outs.append(o)
            if len(outs) == 2:
                vlo = pair * pair_w
                omain_ref[0, :, vlo:vlo + pair_w] = join_pair(outs[0], outs[1]).astype(BF16)
            else:
                oprev_ref[...] = outs[0]
        return dens

    def redo_block_exact():
        def one_head(hd, carry):
            lo = pl.multiple_of(hd * HEAD_SLOT, HEAD_SLOT)
            qh = q_ref[0, :, pl.ds(lo, HEAD_SLOT)]
            s_c = _dot_nt(qh, kc_ref[0, :, pl.ds(lo, HEAD_SLOT)])
            s_l = _dot_nt(qh, kl_ref[0, :, pl.ds(lo, HEAD_SLOT)])
            shift = jnp.maximum(jnp.max(s_c, axis=-1, keepdims=True), jnp.max(s_l, axis=-1, keepdims=True))
            ohead_ref[hd] = (_dot(jnp.exp2(s_c - shift).astype(BF16), vc_ref[0, :, pl.ds(lo, HEAD_SLOT)])
                             + _dot(jnp.exp2(s_l - shift).astype(BF16), vl_ref[0, :, pl.ds(lo, HEAD_SLOT)]))
            return carry

        lax.fori_loop(0, last, one_head, 0)
        for pair in range(MLA_HEADS // 2 - 1):
            vlo = pair * pair_w
            omain_ref[0, :, vlo:vlo + pair_w] = join_pair(ohead_ref[2 * pair], ohead_ref[2 * pair + 1]).astype(BF16)
        oprev_ref[...] = ohead_ref[last - 1]
        head(last, True)

    @pl.when(t < n_blocks)
    def _():
        drain()
        dens = block()
        lo_den, hi_den = dens[0], dens[0]
        for den in dens[1:]:
            lo_den = jnp.minimum(lo_den, den)
            hi_den = jnp.maximum(hi_den, den)
        trusted = jnp.logical_and(jnp.min(lo_den) >= SHIFT_DEN_MIN, jnp.max(hi_den) <= SHIFT_DEN_MAX)

        @pl.when(jnp.logical_not(trusted))
        def _():
            redo_block_exact()

    @pl.when(t == n_blocks)
    def _():
        drain()


def _attention(q, kc, kl, vc, vl, tq=512):
    b, s, _ = q.shape
    c = kc.shape[1]
    nq = s // tq
    n_blocks = b * nq
    pair_w = 2 * V_HEAD_DIM
    main_w = ATTN_WIDTH - pair_w
    last = MLA_HEADS - 1
    cur = lambda t: jnp.minimum(t, n_blocks - 1)
    prev = lambda t: jnp.maximum(t - 1, 0)
    return pl.pallas_call(
        _attn_kernel,
        grid=(n_blocks + 1,),
        in_specs=[
            pl.BlockSpec((1, tq, QK_WIDTH), lambda t: (cur(t) // nq, cur(t) % nq, 0)),
            pl.BlockSpec((1, c, QK_WIDTH), lambda t: (cur(t) // nq, 0, 0)),
            pl.BlockSpec((1, s, QK_WIDTH), lambda t: (cur(t) // nq, 0, 0)),
            pl.BlockSpec((1, c, QK_WIDTH), lambda t: (cur(t) // nq, 0, 0)),
            pl.BlockSpec((1, s, QK_WIDTH), lambda t: (cur(t) // nq, 0, 0)),
            pl.BlockSpec((1, c, HEAD_SLOT), lambda t: (prev(t) // nq, 0, last)),
            pl.BlockSpec((1, s, HEAD_SLOT), lambda t: (prev(t) // nq, 0, last)),
        ],
        out_specs=[
            pl.BlockSpec((1, tq, main_w), lambda t: (cur(t) // nq, cur(t) % nq, 0)),
            pl.BlockSpec((1, tq, pair_w), lambda t: (prev(t) // nq, prev(t) % nq, 0)),
        ],
        out_shape=[
            jax.ShapeDtypeStruct((b, s, main_w), BF16),
            jax.ShapeDtypeStruct((b, s, pair_w), BF16),
        ],
        scratch_shapes=[
            pltpu.VMEM((tq, c + s), BF16),
            pltpu.VMEM((tq, pair_w), F32),
            pltpu.VMEM((MLA_HEADS - 1, tq, HEAD_SLOT), F32),
        ],
        compiler_params=pltpu.CompilerParams(
            dimension_semantics=("arbitrary",), vmem_limit_bytes=VMEM_LIMIT_BYTES),
        name="attn",
    )(q, kc, kl, vc, vl, vc, vl)


def _fourier_kernel(t_ref, tw_ref, u_ref, cc_ref, wf_ref, o_ref, tb_ref, cw_ref):
    @pl.when(pl.program_id(0) == 0)
    def _():
        tb_ref[...] = t_ref[...].astype(BF16)
        cc = cc_ref[...].astype(BF16)
        for g in range(FOURIER_GROUPS):
            cw_ref[g] = _dot(cc, wf_ref[g]).astype(BF16)

    n_sub = tb_ref.shape[1]
    f = _dot(tb_ref[...], u_ref[0])
    gd = FOURIER_GROUP_DIM
    for g in range(FOURIER_GROUPS):
        gr, gi = [], []
        for r in range(FFT_RADIX):
            lo = r * FOURIER_WIDTH + g * gd
            a = f[:n_sub, lo:lo + gd]
            b = f[n_sub:, lo:lo + gd]
            if r == 0:
                gr.append(a)
                gi.append(b)
            else:
                c = tw_ref[r - 1, 0]
                s = tw_ref[r - 1, 1]
                gr.append(a * c - b * s)
                gi.append(a * s + b * c)
        z = list(zip(gr, gi))
        add = lambda a, b: (a[0] + b[0], a[1] + b[1])
        sub = lambda a, b: (a[0] - b[0], a[1] - b[1])
        times_minus_i = lambda a: (-a[1], a[0])

        def dft4(c):
            e0, e1 = add(c[0], c[2]), sub(c[0], c[2])
            f0, f1 = add(c[1], c[3]), times_minus_i(sub(c[1], c[3]))
            return [add(e0, f0), add(e1, f1), sub(e0, f0), sub(e1, f1)]

        half = FFT_RADIX // 2
        even = dft4([add(z[r], z[r + half]) for r in range(half)])
        d = [sub(z[r], z[r + half]) for r in range(half)]
        rt = np.float32(np.sqrt(0.5))
        odd = dft4([
            d[0],
            ((d[1][0] - d[1][1]) * rt, (d[1][0] + d[1][1]) * rt),
            times_minus_i(d[2]),
            (-(d[3][0] + d[3][1]) * rt, (d[3][0] - d[3][1]) * rt),
        ])
        y = [even[q // 2] if q % 2 == 0 else odd[q // 2] for q in range(FFT_RADIX)]
        xr = jnp.concatenate([v[0] for v in y], axis=0)
        xi = jnp.concatenate([v[1] for v in y], axis=0)
        lhs = jnp.concatenate([xr, xi], axis=1).astype(BF16)
        o_ref[0, :, g * gd:(g + 1) * gd] = _dot(lhs, cw_ref[g]).astype(BF16)


def _fourier(u4, tmat, tw, cmat, wf):
    b, n_sub, _ = u4.shape
    s = n_sub * FFT_RADIX
    full = lambda a: pl.BlockSpec(a.shape, lambda i: (0,) * a.ndim)
    return pl.pallas_call(
        _fourier_kernel,
        grid=(b,),
        in_specs=[
            full(tmat), full(tw),
            pl.BlockSpec((1, n_sub, FFT_RADIX * FOURIER_WIDTH), lambda i: (i, 0, 0)),
            full(cmat), full(wf),
        ],
        out_specs=pl.BlockSpec((1, s, FOURIER_WIDTH), lambda i: (i, 0, 0)),
        out_shape=jax.ShapeDtypeStruct((b, s, FOURIER_WIDTH), BF16),
        scratch_shapes=[
            pltpu.VMEM(tmat.shape, BF16),
            pltpu.VMEM((FOURIER_GROUPS, 2 * FOURIER_GROUP_DIM, FOURIER_GROUP_DIM), BF16),
        ],
        compiler_params=pltpu.CompilerParams(
            dimension_semantics=("arbitrary",), vmem_limit_bytes=VMEM_LIMIT_BYTES),
        name="fourier",
    )(tmat, tw, u4, cmat, wf)


def _stage_cast(src_hbm, dst_ref, stage_ref, sem_ref, chunk):
    n_chunks = src_hbm.shape[0] // chunk

    def copy(i):
        slot = i % 2
        return pltpu.make_async_copy(
            src_hbm.at[pl.ds(i * chunk, chunk)], stage_ref.at[slot, pl.ds(0, chunk)], sem_ref.at[slot])

    def body(i, carry):
        @pl.when(i + 1 < n_chunks)
        def _():
            copy(i + 1).start()

        copy(i).wait()
        start = pl.multiple_of(i * chunk, chunk)
        dst_ref[pl.ds(start, chunk), :] = stage_ref[i % 2, pl.ds(0, chunk), :].astype(BF16)
        return carry

    copy(0).start()
    lax.fori_loop(0, n_chunks, body, 0)


def _post_kernel(x_ref, four_ref, am_ref, al_ref, modc_ref, modp_ref, gpm_ref, gpf_ref, gqf_ref,
                 wo_hbm, wg_hbm, wu_hbm, wd_hbm, o_ref,
                 x1_ref, h2_ref, wo_ref, wg_ref, wu_ref, wd_ref, wide_stage, tall_stage, sem_ref):
    t = pl.program_id(0)
    n_four = four_ref.shape[2]

    def mix():
        gt_m = modc_ref[0, 2:3, :]
        sh_f = modc_ref[0, 3:4, :]
        sc_f = modc_ref[0, 4:5, :]
        attn = jnp.concatenate([am_ref[0], al_ref[0]], axis=1)
        y = _dot(four_ref[0], wo_ref[:n_four, :]) + _dot(attn, wo_ref[n_four:, :])
        x1 = x_ref[0] + _rms(y, gt_m * gpm_ref[...])
        x1_ref[...] = x1
        h2_ref[...] = (_rms(x1, gpf_ref[...] * (1.0 + sc_f)) + sh_f).astype(BF16)

    def ffn():
        gt_f = modp_ref[0, 5:6, :]
        h2 = h2_ref[...]
        g = _dot(h2, wg_ref[...])
        up = _dot(h2, wu_ref[...])
        act = (g / (1.0 + jnp.exp(-g)) * up).astype(BF16)
        o_ref[0] = x1_ref[...] + _rms(_dot(act, wd_ref[...]), gt_f * gqf_ref[...])

    @pl.when(t == 0)
    def _():
        _stage_cast(wo_hbm, wo_ref, tall_stage, sem_ref, STAGE_ROWS_OUT)
        mix()
        _stage_cast(wg_hbm, wg_ref, wide_stage, sem_ref, STAGE_ROWS_WIDE)
        _stage_cast(wu_hbm, wu_ref, wide_stage, sem_ref, STAGE_ROWS_WIDE)
        _stage_cast(wd_hbm, wd_ref, tall_stage, sem_ref, STAGE_ROWS_TALL)

    @pl.when(t > 0)
    def _():
        ffn()
        mix()


def _post(x, four, attn_main, attn_last, mod3, gpm, gpf, gqf, wo, wg, wu, wd, tm=512):
    b, s, d = x.shape
    d_ff = wg.shape[1]
    assert wo.shape[0] % STAGE_ROWS_OUT == 0 and STAGE_ROWS_OUT <= STAGE_ROWS_TALL
    assert d % STAGE_ROWS_WIDE == 0 and d_ff % STAGE_ROWS_TALL == 0
    nj = s // tm
    n_blocks = b * nj
    cur = lambda t: jnp.minimum(t, n_blocks - 1)
    prev = lambda t: jnp.maximum(t - 1, 0)
    const = lambda shape: pl.BlockSpec(shape, lambda t: (0,) * len(shape), pipeline_mode=pl.Buffered(1))
    rows = lambda w: pl.BlockSpec((1, tm, w), lambda t: (cur(t) // nj, cur(t) % nj, 0))
    hbm = pl.BlockSpec(memory_space=pl.ANY)
    return pl.pallas_call(
        _post_kernel,
        grid=(n_blocks + 1,),
        in_specs=[
            rows(d), rows(FOURIER_WIDTH), rows(attn_main.shape[2]), rows(attn_last.shape[2]),
            pl.BlockSpec((1, 6, d), lambda t: (cur(t) // nj, 0, 0)),
            pl.BlockSpec((1, 6, d), lambda t: (prev(t) // nj, 0, 0)),
            const(gpm.shape), const(gpf.shape), const(gqf.shape),
            hbm, hbm, hbm, hbm,
        ],
        out_specs=pl.BlockSpec((1, tm, d), lambda t: (prev(t) // nj, prev(t) % nj, 0)),
        out_shape=jax.ShapeDtypeStruct((b, s, d), F32),
        scratch_shapes=[
            pltpu.VMEM((tm, d), F32), pltpu.VMEM((tm, d), BF16),
            pltpu.VMEM(wo.shape, BF16), pltpu.VMEM(wg.shape, BF16), pltpu.VMEM(wu.shape, BF16),
            pltpu.VMEM(wd.shape, BF16),
            pltpu.VMEM((2, STAGE_ROWS_WIDE, d_ff), F32), pltpu.VMEM((2, STAGE_ROWS_TALL, d), F32),
            pltpu.SemaphoreType.DMA((2,)),
        ],
        compiler_params=pltpu.CompilerParams(
            dimension_semantics=("arbitrary",), vmem_limit_bytes=VMEM_LIMIT_BYTES),
        name="post",
    )(x, four, attn_main, attn_last, mod3, mod3, gpm, gpf, gqf, wo, wg, wu, wd)


def _rope_rotate_cols(w):
    a = QK_ROPE_DIM // 2
    hf = a // 2
    blocks = []
    for s0 in (0, a):
        blocks += [-w[..., s0 + hf:s0 + a], w[..., s0:s0 + hf]]
    return jnp.concatenate(blocks, axis=-1)


def _head_slot(nope, rope):
    pad = HEAD_SLOT - QK_NOPE_DIM - QK_ROPE_DIM
    return jnp.concatenate([nope, rope, jnp.zeros(rope.shape[:-1] + (pad,), rope.dtype)], axis=-1)


def _rope_tables(n_lat, q_scale):
    t = np.arange(n_lat)
    hf = QK_ROPE_DIM // 4
    inv_freq = ROPE_BASE ** (-np.arange(hf, dtype=np.float64) / hf)
    ar = (t // GRID_W)[:, None] * inv_freq[None, :]
    ac = (t % GRID_W)[:, None] * inv_freq[None, :]
    z = np.zeros_like(ar)
    cos32 = np.concatenate([np.cos(ar), np.cos(ar), np.cos(ac), np.cos(ac)], axis=-1)
    sin32 = np.concatenate([np.sin(ar), np.sin(ar), np.sin(ac), np.sin(ac)], axis=-1)
    below32 = np.concatenate([z, np.sin(ar), z, np.sin(ac)], axis=-1)
    above32 = np.concatenate([-np.sin(ar), z, -np.sin(ac), z], axis=-1)
    pad = np.zeros((n_lat, HEAD_SLOT - QK_NOPE_DIM - QK_ROPE_DIM))
    ones = np.ones((n_lat, QK_NOPE_DIM))
    zeros = np.zeros((n_lat, QK_NOPE_DIM))
    slot = lambda nope, rope: np.concatenate([nope, rope, pad], axis=-1)
    tables = [slot(ones * q_scale, cos32 * q_scale), slot(zeros, sin32 * q_scale),
              slot(zeros, cos32), slot(zeros, below32), slot(zeros, above32)]
    return jnp.asarray(np.stack(tables).astype(np.float32))


def _dft_tables(n_pos, n_ch):
    n_sub = n_pos // FFT_RADIX
    m = np.arange(n_sub, dtype=np.int64)
    ang = 2.0 * np.pi * ((m[:, None] * m[None, :]) % n_sub) / n_sub
    tmat = np.concatenate([np.cos(ang), np.sin(ang)], axis=0).astype(np.float32)
    tw = np.zeros((FFT_RADIX - 1, 2, n_sub, FOURIER_GROUP_DIM), np.float32)
    for r in range(1, FFT_RADIX):
        a = 2.0 * np.pi * r * m / n_pos
        tw[r - 1, 0] = np.cos(a)[:, None]
        tw[r - 1, 1] = np.sin(a)[:, None]
    c = np.arange(n_ch, dtype=np.int64)
    angc = 2.0 * np.pi * ((c[:, None] * c[None, :]) % n_ch) / n_ch
    norm = 1.0 / np.sqrt(float(n_pos * n_ch))
    cmat = np.concatenate([np.cos(angc) * norm, -np.sin(angc) * norm], axis=0).astype(np.float32)
    return jnp.asarray(tmat), jnp.asarray(tw), jnp.asarray(cmat)


def kernel(x, c, ctx, c_ctx, w_ada, b_ada, g_pre_mix, g_post_mix, g_pre_ffn, g_post_ffn, w_in, g_q_a,
           w_q_b, g_kv_a, w_kv_b, w_fourier, w_out, w_gate, w_up, w_down):
    assert w_ada.shape[0] == 1, "single-layer block"
    batch, n_lat, d = x.shape

    mod_rows = -(-(batch + 1) // 8) * 8
    cc = jnp.concatenate([c, c_ctx[None, :], jnp.zeros((mod_rows - batch - 1, d), F32)], axis=0)
    mod = _adaln(cc, w_ada[0], b_ada[0][None, :])
    mod3 = mod.reshape(mod_rows, 6, d)

    w_in0 = w_in[0]
    w_kr = w_in0[:, ROPE_COL:]
    zeros_d = jnp.zeros((d, QK_NOPE_DIM), F32)
    kr_slot = _head_slot(zeros_d, w_kr)
    win = jnp.concatenate([w_in0[:, :ROPE_COL], kr_slot], axis=1).astype(BF16)
    win_c = jnp.concatenate([w_in0[:, KV_COL:ROPE_COL], kr_slot], axis=1).astype(BF16)

    wq3 = w_q_b[0].reshape(Q_LORA_RANK, MLA_HEADS, QK_NOPE_DIM + QK_ROPE_DIM)
    wq_nope, wq_rope = wq3[..., :QK_NOPE_DIM], wq3[..., QK_NOPE_DIM:]
    wq_a = _head_slot(wq_nope, wq_rope).reshape(Q_LORA_RANK, QK_WIDTH)
    wq_b = _head_slot(jnp.zeros_like(wq_nope), _rope_rotate_cols(wq_rope)).reshape(Q_LORA_RANK, QK_WIDTH)
    wq = jnp.concatenate([wq_a, wq_b], axis=1).astype(BF16)

    wkv3 = w_kv_b[0].reshape(KV_LORA_RANK, MLA_HEADS, QK_NOPE_DIM + V_HEAD_DIM)
    wk_nope, wv = wkv3[..., :QK_NOPE_DIM], wkv3[..., QK_NOPE_DIM:]
    wk_slots = _head_slot(wk_nope, jnp.zeros(wk_nope.shape[:-1] + (QK_ROPE_DIM,), F32))
    wkv = jnp.concatenate([wk_slots.reshape(KV_LORA_RANK, QK_WIDTH),
                           wv.reshape(KV_LORA_RANK, ATTN_WIDTH)], axis=1).astype(BF16)

    q_scale = float((QK_NOPE_DIM + QK_ROPE_DIM) ** -0.5 * np.log2(np.e))
    rope = _rope_tables(n_lat, q_scale)
    tmat, tw, cmat = _dft_tables(n_lat, FOURIER_GROUP_DIM)

    row2 = lambda g: g[0][None, :]
    u_f, q, k_lat, v_lat = _premix(x, mod3, row2(g_pre_mix), win, row2(g_q_a), wq, row2(g_kv_a), wkv,
                                   rope)
    k_ctx, v_ctx = _ctxkv(ctx, mod3, batch, row2(g_pre_mix), win_c, row2(g_kv_a), wkv)
    attn_main, attn_last = _attention(q, k_ctx, k_lat, v_ctx, v_lat)
    four = _fourier(u_f, tmat, tw, cmat, w_fourier[0].astype(BF16))
    return _post(x, four, attn_main, attn_last, mod3, row2(g_post_mix), row2(g_pre_ffn), row2(g_post_ffn),
                 w_out[0], w_gate[0], w_up[0], w_down[0])
```

```python
import numpy as np
import jax
import jax.numpy as jnp
from jax import lax
from jax.experimental import pallas as pl
from jax.experimental.pallas import tpu as pltpu

F32 = jnp.float32
BF16 = jnp.bfloat16

D_MODEL = 1024
GRID_W = 64
FOURIER_GROUPS = 4
FOURIER_GROUP_DIM = 128
FOURIER_WIDTH = FOURIER_GROUPS * FOURIER_GROUP_DIM
MLA_HEADS = 8
QK_NOPE_DIM = 64
QK_ROPE_DIM = 32
V_HEAD_DIM = 64
Q_LORA_RANK = 256
KV_LORA_RANK = 128
KV_COL = FOURIER_WIDTH + Q_LORA_RANK
ROPE_COL = KV_COL + KV_LORA_RANK
ROPE_BASE = 10000.0
NORM_EPS = 1e-6
FFT_RADIX = 8
HEAD_SLOT = 128
STAGE_ROWS_WIDE = 256
STAGE_ROWS_TALL = 704
STAGE_ROWS_OUT = 512
SHIFT_DEN_MIN = 2.0 ** -40
SHIFT_DEN_MAX = 2.0 ** 40
PV_KEY_TILE = 256
ATTN_WIDTH = MLA_HEADS * V_HEAD_DIM
QK_WIDTH = MLA_HEADS * HEAD_SLOT

VMEM_LIMIT_BYTES = 56 * 1024 * 1024


def _rms(x, g):
    return x * lax.rsqrt(jnp.mean(x * x, axis=-1, keepdims=True) + NORM_EPS) * g


def _dot(a, b):
    return jnp.dot(a, b, preferred_element_type=F32)


def _dot_nt(a, b):
    return lax.dot_general(a, b, (((1,), (1,)), ((), ())), preferred_element_type=F32)


def _rotary(a, table, first):
    half = QK_ROPE_DIM // 4
    below = pltpu.roll(a, half, axis=1)
    above = pltpu.roll(a, HEAD_SLOT - half, axis=1)
    return a * table(first) + below * table(first + 1) + above * table(first + 2)


def _store_value_slots(v_ref, v):
    pair_w = 2 * V_HEAD_DIM
    lower = lax.broadcasted_iota(jnp.int32, (v.shape[0], pair_w), 1) < V_HEAD_DIM
    for pair in range(MLA_HEADS // 2):
        vp = v[:, pair * pair_w:(pair + 1) * pair_w]
        lo = 2 * pair * HEAD_SLOT
        v_ref[0, :, lo:lo + HEAD_SLOT] = jnp.where(lower, vp, 1.0).astype(BF16)
        v_ref[0, :, lo + HEAD_SLOT:lo + 2 * HEAD_SLOT] = jnp.where(lower, 1.0, vp).astype(BF16)


def _adaln_kernel(c_ref, w_ref, b_ref, o_ref):
    c = c_ref[...]
    a = c / (1.0 + jnp.exp(-c))
    o_ref[...] = _dot(a.astype(BF16), w_ref[...].astype(BF16)) + b_ref[...]


def _adaln(cc, w_ada, b_ada, tn=1024):
    rows, d = cc.shape
    n = w_ada.shape[1]
    return pl.pallas_call(
        _adaln_kernel,
        grid=(n // tn,),
        in_specs=[
            pl.BlockSpec((rows, d), lambda j: (0, 0)),
            pl.BlockSpec((d, tn), lambda j: (0, j)),
            pl.BlockSpec((1, tn), lambda j: (0, j)),
        ],
        out_specs=pl.BlockSpec((rows, tn), lambda j: (0, j)),
        out_shape=jax.ShapeDtypeStruct((rows, n), F32),
        compiler_params=pltpu.CompilerParams(vmem_limit_bytes=VMEM_LIMIT_BYTES),
        name="adaln",
    )(cc, w_ada, b_ada)


def _premix_kernel(x_ref, mod_ref, gpre_ref, win_ref, gq_ref, wq_ref, gkv_ref, wkv_ref,
                   rope_ref, u_ref, q_ref, k_ref, v_ref, us_ref):
    x = x_ref[0]
    shift = mod_ref[0, 0:1, :]
    scale = mod_ref[0, 1:2, :]
    h = _rms(x, gpre_ref[...] * (1.0 + scale)) + shift
    p = _dot(h.astype(BF16), win_ref[...])
    sub = us_ref.shape[1] // FFT_RADIX
    for g in range(FOURIER_GROUPS):
        glo = g * FOURIER_GROUP_DIM
        us_ref[g] = p[:, glo:glo + FOURIER_GROUP_DIM]
        for r in range(FFT_RADIX):
            lo = r * FOURIER_WIDTH + glo
            u_ref[0, :, lo:lo + FOURIER_GROUP_DIM] = (
                us_ref[g, pl.ds(r, sub, stride=FFT_RADIX), :].astype(BF16))

    qn = _rms(p[:, FOURIER_WIDTH:KV_COL], gq_ref[...]).astype(BF16)
    qq = _dot(qn, wq_ref[...])
    rows = x_ref.shape[1]
    row0 = pl.multiple_of(pl.program_id(1) * rows, rows)
    table = lambda k: rope_ref[k, pl.ds(row0, rows), :]
    cosq = table(0)
    sinq = table(1)
    for hd in range(MLA_HEADS):
        lo = hd * HEAD_SLOT
        q_ref[0, :, lo:lo + HEAD_SLOT] = (
            qq[:, lo:lo + HEAD_SLOT] * cosq + qq[:, QK_WIDTH + lo:QK_WIDTH + lo + HEAD_SLOT] * sinq
        ).astype(BF16)

    kvn = _rms(p[:, KV_COL:ROPE_COL], gkv_ref[...]).astype(BF16)
    kv = _dot(kvn, wkv_ref[...])
    kr = _rotary(p[:, ROPE_COL:ROPE_COL + HEAD_SLOT], table, 2)
    for hd in range(MLA_HEADS):
        lo = hd * HEAD_SLOT
        k_ref[0, :, lo:lo + HEAD_SLOT] = (kv[:, lo:lo + HEAD_SLOT] + kr).astype(BF16)
    _store_value_slots(v_ref, kv[:, QK_WIDTH:])


def _premix(x, mod3, gpre, win, gq, wq, gkv, wkv, rope, tm=1024):
    b, s, d = x.shape
    const = lambda shape: pl.BlockSpec(shape, lambda i, j: (0,) * len(shape))
    rows = lambda w: pl.BlockSpec((1, tm, w), lambda i, j: (i, j, 0))
    return pl.pallas_call(
        _premix_kernel,
        grid=(b, s // tm),
        in_specs=[
            rows(d),
            pl.BlockSpec((1, 6, d), lambda i, j: (i, 0, 0)),
            const(gpre.shape), const(win.shape), const(gq.shape), const(wq.shape),
            const(gkv.shape), const(wkv.shape),
            pl.BlockSpec(rope.shape, lambda i, j: (0, 0, 0), pipeline_mode=pl.Buffered(1)),
        ],
        out_specs=[
            pl.BlockSpec((1, tm // FFT_RADIX, FFT_RADIX * FOURIER_WIDTH), lambda i, j: (i, j, 0)),
            rows(QK_WIDTH), rows(QK_WIDTH), rows(QK_WIDTH)],
        out_shape=[
            jax.ShapeDtypeStruct((b, s // FFT_RADIX, FFT_RADIX * FOURIER_WIDTH), BF16),
            jax.ShapeDtypeStruct((b, s, QK_WIDTH), BF16),
            jax.ShapeDtypeStruct((b, s, QK_WIDTH), BF16),
            jax.ShapeDtypeStruct((b, s, QK_WIDTH), BF16),
        ],
        scratch_shapes=[pltpu.VMEM((FOURIER_GROUPS, tm, FOURIER_GROUP_DIM), F32)],
        compiler_params=pltpu.CompilerParams(vmem_limit_bytes=VMEM_LIMIT_BYTES),
        name="premix",
    )(x, mod3, gpre, win, gq, wq, gkv, wkv, rope)


def _ctxkv_kernel(x_ref, mod_ref, gpre_ref, win_ref, gkv_ref, wkv_ref, k_ref, v_ref):
    x = x_ref[0]
    shift = mod_ref[0, 0:1, :]
    scale = mod_ref[0, 1:2, :]
    h = _rms(x, gpre_ref[...] * (1.0 + scale)) + shift
    p = _dot(h.astype(BF16), win_ref[...])
    kvn = _rms(p[:, :KV_LORA_RANK], gkv_ref[...]).astype(BF16)
    kv = _dot(kvn, wkv_ref[...])
    kr = p[:, KV_LORA_RANK:]
    for hd in range(MLA_HEADS):
        lo = hd * HEAD_SLOT
        k_ref[0, :, lo:lo + HEAD_SLOT] = (kv[:, lo:lo + HEAD_SLOT] + kr).astype(BF16)
    _store_value_slots(v_ref, kv[:, QK_WIDTH:])


def _ctxkv(ctx, mod3, ctx_row, gpre, win_c, gkv, wkv):
    b, c, d = ctx.shape
    const = lambda shape: pl.BlockSpec(shape, lambda i: (0,) * len(shape))
    rows = lambda w: pl.BlockSpec((1, c, w), lambda i: (i, 0, 0))
    return pl.pallas_call(
        _ctxkv_kernel,
        grid=(b,),
        in_specs=[
            rows(d),
            pl.BlockSpec((1, 6, d), lambda i: (ctx_row, 0, 0)),
            const(gpre.shape), const(win_c.shape), const(gkv.shape), const(wkv.shape),
        ],
        out_specs=[rows(QK_WIDTH), rows(QK_WIDTH)],
        out_shape=[
            jax.ShapeDtypeStruct((b, c, QK_WIDTH), BF16),
            jax.ShapeDtypeStruct((b, c, QK_WIDTH), BF16),
        ],
        name="ctxkv",
    )(ctx, mod3, gpre, win_c, gkv, wkv)


def _attn_kernel(q_ref, kc_ref, kl_ref, vc_ref, vl_ref, vcp_ref, vlp_ref, omain_ref, olast_ref,
                 p_ref, oprev_ref, ohead_ref):
    t = pl.program_id(0)
    n_blocks = pl.num_programs(0) - 1
    n_ctx = kc_ref.shape[1]
    n_lat = kl_ref.shape[1]
    tq = q_ref.shape[1]
    pair_w = 2 * V_HEAD_DIM
    first_half = lax.broadcasted_iota(jnp.int32, (tq, pair_w), 1) < V_HEAD_DIM
    last = MLA_HEADS - 1

    @pl.when(t == 0)
    def _():
        p_ref[...] = jnp.ones_like(p_ref)
        oprev_ref[...] = jnp.ones_like(oprev_ref)

    def join_pair(o_even, o_odd):
        num = jnp.where(first_half, o_even, o_odd)
        den = pltpu.roll(jnp.where(first_half, o_odd, o_even), V_HEAD_DIM, axis=1)
        return num / den

    def den_of(o, hd):
        lane = V_HEAD_DIM if hd % 2 == 0 else 0
        return o[:, lane:lane + 1]

    def drain():
        o = _dot(p_ref[:, :n_ctx], vcp_ref[0]) + _dot(p_ref[:, n_ctx:], vlp_ref[0])
        olast_ref[0] = join_pair(oprev_ref[...], o).astype(BF16)

    def head(hd, exact):
        lo = hd * HEAD_SLOT
        qh = q_ref[0, :, lo:lo + HEAD_SLOT]
        s_c = _dot_nt(qh, kc_ref[0, :, lo:lo + HEAD_SLOT])
        s_l = _dot_nt(qh, kl_ref[0, :, lo:lo + HEAD_SLOT])
        shift = jnp.max(s_c, axis=-1, keepdims=True)
        if exact:
            shift = jnp.maximum(shift, jnp.max(s_l, axis=-1, keepdims=True))
        if hd == last:
            p_c = jnp.exp2(s_c - shift)
            p_l = jnp.exp2(s_l - shift)
            p_ref[:, :n_ctx] = p_c.astype(BF16)
            p_ref[:, n_ctx:] = p_l.astype(BF16)
            return None, jnp.sum(p_c, axis=-1, keepdims=True) + jnp.sum(p_l, axis=-1, keepdims=True)
        o = _dot(jnp.exp2(s_c - shift).astype(BF16), vc_ref[0, :, lo:lo + HEAD_SLOT])
        for k0 in range(0, n_lat, PV_KEY_TILE):
            p = jnp.exp2(s_l[:, k0:k0 + PV_KEY_TILE] - shift).astype(BF16)
            o = o + _dot(p, vl_ref[0, k0:k0 + PV_KEY_TILE, lo:lo + HEAD_SLOT])
        return o, den_of(o, hd)

    def block():
        dens = []
        for pair in range(MLA_HEADS // 2):
            outs = []
            for hd in (2 * pair, 2 * pair + 1):
                o, den = head(hd, False)
                dens.append(den)
                if o is not None:
                    outs.append(o)
            if len(outs) == 2:
                vlo = pair * pair_w
                omain_ref[0, :, vlo:vlo + pair_w] = join_pair(outs[0], outs[1]).astype(BF16)
            else:
                oprev_ref[...] = outs[0]
        return dens

    def redo_block_exact():
        def one_head(hd, carry):
            lo = pl.multiple_of(hd * HEAD_SLOT, HEAD_SLOT)
            qh = q_ref[0, :, pl.ds(lo, HEAD_SLOT)]
            s_c = _dot_nt(qh, kc_ref[0, :, pl.ds(lo, HEAD_SLOT)])
            s_l = _dot_nt(qh, kl_ref[0, :, pl.ds(lo, HEAD_SLOT)])
            shift = jnp.maximum(jnp.max(s_c, axis=-1, keepdims=True), jnp.max(s_l, axis=-1, keepdims=True))
            ohead_ref[hd] = (_dot(jnp.exp2(s_c - shift).astype(BF16), vc_ref[0, :, pl.ds(lo, HEAD_SLOT)])
                             + _dot(jnp.exp2(s_l - shift).astype(BF16), vl_ref[0, :, pl.ds(lo, HEAD_SLOT)]))
            return carry

        lax.fori_loop(0, last, one_head, 0)
        for pair in range(MLA_HEADS // 2 - 1):
            vlo = pair * pair_w
            omain_ref[0, :, vlo:vlo + pair_w] = join_pair(ohead_ref[2 * pair], ohead_ref[2 * pair + 1]).astype(BF16)
        oprev_ref[...] = ohead_ref[last - 1]
        head(last, True)

    @pl.when(t < n_blocks)
    def _():
        drain()
        dens = block()
        lo_den, hi_den = dens[0], dens[0]
        for den in dens[1:]:
            lo_den = jnp.minimum(lo_den, den)
            hi_den = jnp.maximum(hi_den, den)
        trusted = jnp.logical_and(jnp.min(lo_den) >= SHIFT_DEN_MIN, jnp.max(hi_den) <= SHIFT_DEN_MAX)

        @pl.when(jnp.logical_not(trusted))
        def _():
            redo_block_exact()

    @pl.when(t == n_blocks)
    def _():
        drain()


def _attention(q, kc, kl, vc, vl, tq=512):
    b, s, _ = q.shape
    c = kc.shape[1]
    nq = s // tq
    n_blocks = b * nq
    pair_w = 2 * V_HEAD_DIM
    main_w = ATTN_WIDTH - pair_w
    last = MLA_HEADS - 1
    cur = lambda t: jnp.minimum(t, n_blocks - 1)
    prev = lambda t: jnp.maximum(t - 1, 0)
    return pl.pallas_call(
        _attn_kernel,
        grid=(n_blocks + 1,),
        in_specs=[
            pl.BlockSpec((1, tq, QK_WIDTH), lambda t: (cur(t) // nq, cur(t) % nq, 0)),
            pl.BlockSpec((1, c, QK_WIDTH), lambda t: (cur(t) // nq, 0, 0)),
            pl.BlockSpec((1, s, QK_WIDTH), lambda t: (cur(t) // nq, 0, 0)),
            pl.BlockSpec((1, c, QK_WIDTH), lambda t: (cur(t) // nq, 0, 0)),
            pl.BlockSpec((1, s, QK_WIDTH), lambda t: (cur(t) // nq, 0, 0)),
            pl.BlockSpec((1, c, HEAD_SLOT), lambda t: (prev(t) // nq, 0, last)),
            pl.BlockSpec((1, s, HEAD_SLOT), lambda t: (prev(t) // nq, 0, last)),
        ],
        out_specs=[
            pl.BlockSpec((1, tq, main_w), lambda t: (cur(t) // nq, cur(t) % nq, 0)),
            pl.BlockSpec((1, tq, pair_w), lambda t: (prev(t) // nq, prev(t) % nq, 0)),
        ],
        out_shape=[
            jax.ShapeDtypeStruct((b, s, main_w), BF16),
            jax.ShapeDtypeStruct((b, s, pair_w), BF16),
        ],
        scratch_shapes=[
            pltpu.VMEM((tq, c + s), BF16),
            pltpu.VMEM((tq, pair_w), F32),
            pltpu.VMEM((MLA_HEADS - 1, tq, HEAD_SLOT), F32),
        ],
        compiler_params=pltpu.CompilerParams(
            dimension_semantics=("arbitrary",), vmem_limit_bytes=VMEM_LIMIT_BYTES),
        name="attn",
    )(q, kc, kl, vc, vl, vc, vl)


def _fourier_kernel(t_ref, tw_ref, u_ref, cc_ref, wf_ref, o_ref, tb_ref, cw_ref):
    @pl.when(pl.program_id(0) == 0)
    def _():
        tb_ref[...] = t_ref[...].astype(BF16)
        cc = cc_ref[...].astype(BF16)
        for g in range(FOURIER_GROUPS):
            cw_ref[g] = _dot(cc, wf_ref[g]).astype(BF16)

    n_sub = tb_ref.shape[1]
    f = _dot(tb_ref[...], u_ref[0])
    gd = FOURIER_GROUP_DIM
    for g in range(FOURIER_GROUPS):
        gr, gi = [], []
        for r in range(FFT_RADIX):
            lo = r * FOURIER_WIDTH + g * gd
            a = f[:n_sub, lo:lo + gd]
            b = f[n_sub:, lo:lo + gd]
            if r == 0:
                gr.append(a)
                gi.append(b)
            else:
                c = tw_ref[r - 1, 0]
                s = tw_ref[r - 1, 1]
                gr.append(a * c - b * s)
                gi.append(a * s + b * c)
        z = list(zip(gr, gi))
        add = lambda a, b: (a[0] + b[0], a[1] + b[1])
        sub = lambda a, b: (a[0] - b[0], a[1] - b[1])
        times_minus_i = lambda a: (-a[1], a[0])

        def dft4(c):
            e0, e1 = add(c[0], c[2]), sub(c[0], c[2])
            f0, f1 = add(c[1], c[3]), times_minus_i(sub(c[1], c[3]))
            return [add(e0, f0), add(e1, f1), sub(e0, f0), sub(e1, f1)]

        half = FFT_RADIX // 2
        even = dft4([add(z[r], z[r + half]) for r in range(half)])
        d = [sub(z[r], z[r + half]) for r in range(half)]
        rt = np.float32(np.sqrt(0.5))
        odd = dft4([
            d[0],
            ((d[1][0] - d[1][1]) * rt, (d[1][0] + d[1][1]) * rt),
            times_minus_i(d[2]),
            (-(d[3][0] + d[3][1]) * rt, (d[3][0] - d[3][1]) * rt),
        ])
        y = [even[q // 2] if q % 2 == 0 else odd[q // 2] for q in range(FFT_RADIX)]
        xr = jnp.concatenate([v[0] for v in y], axis=0)
        xi = jnp.concatenate([v[1] for v in y], axis=0)
        lhs = jnp.concatenate([xr, xi], axis=1).astype(BF16)
        o_ref[0, :, g * gd:(g + 1) * gd] = _dot(lhs, cw_ref[g]).astype(BF16)


def _fourier(u4, tmat, tw, cmat, wf):
    b, n_sub, _ = u4.shape
    s = n_sub * FFT_RADIX
    full = lambda a: pl.BlockSpec(a.shape, lambda i: (0,) * a.ndim)
    return pl.pallas_call(
        _fourier_kernel,
        grid=(b,),
        in_specs=[
            full(tmat), full(tw),
            pl.BlockSpec((1, n_sub, FFT_RADIX * FOURIER_WIDTH), lambda i: (i, 0, 0)),
            full(cmat), full(wf),
        ],
        out_specs=pl.BlockSpec((1, s, FOURIER_WIDTH), lambda i: (i, 0, 0)),
        out_shape=jax.ShapeDtypeStruct((b, s, FOURIER_WIDTH), BF16),
        scratch_shapes=[
            pltpu.VMEM(tmat.shape, BF16),
            pltpu.VMEM((FOURIER_GROUPS, 2 * FOURIER_GROUP_DIM, FOURIER_GROUP_DIM), BF16),
        ],
        compiler_params=pltpu.CompilerParams(
            dimension_semantics=("arbitrary",), vmem_limit_bytes=VMEM_LIMIT_BYTES),
        name="fourier",
    )(tmat, tw, u4, cmat, wf)


def _stage_cast(src_hbm, dst_ref, stage_ref, sem_ref, chunk):
    n_chunks = src_hbm.shape[0] // chunk

    def copy(i):
        slot = i % 2
        return pltpu.make_async_copy(
            src_hbm.at[pl.ds(i * chunk, chunk)], stage_ref.at[slot, pl.ds(0, chunk)], sem_ref.at[slot])

    def body(i, carry):
        @pl.when(i + 1 < n_chunks)
        def _():
            copy(i + 1).start()

        copy(i).wait()
        start = pl.multiple_of(i * chunk, chunk)
        dst_ref[pl.ds(start, chunk), :] = stage_ref[i % 2, pl.ds(0, chunk), :].astype(BF16)
        return carry

    copy(0).start()
    lax.fori_loop(0, n_chunks, body, 0)


def _post_kernel(x_ref, four_ref, am_ref, al_ref, modc_ref, modp_ref, gpm_ref, gpf_ref, gqf_ref,
                 wo_hbm, wg_hbm, wu_hbm, wd_hbm, o_ref,
                 x1_ref, h2_ref, wo_ref, wg_ref, wu_ref, wd_ref, wide_stage, tall_stage, sem_ref):
    t = pl.program_id(0)
    n_four = four_ref.shape[2]

    def mix():
        gt_m = modc_ref[0, 2:3, :]
        sh_f = modc_ref[0, 3:4, :]
        sc_f = modc_ref[0, 4:5, :]
        attn = jnp.concatenate([am_ref[0], al_ref[0]], axis=1)
        y = _dot(four_ref[0], wo_ref[:n_four, :]) + _dot(attn, wo_ref[n_four:, :])
        x1 = x_ref[0] + _rms(y, gt_m * gpm_ref[...])
        x1_ref[...] = x1
        h2_ref[...] = (_rms(x1, gpf_ref[...] * (1.0 + sc_f)) + sh_f).astype(BF16)

    def ffn():
        gt_f = modp_ref[0, 5:6, :]
        h2 = h2_ref[...]
        g = _dot(h2, wg_ref[...])
        up = _dot(h2, wu_ref[...])
        act = (g / (1.0 + jnp.exp(-g)) * up).astype(BF16)
        o_ref[0] = x1_ref[...] + _rms(_dot(act, wd_ref[...]), gt_f * gqf_ref[...])

    @pl.when(t == 0)
    def _():
        _stage_cast(wo_hbm, wo_ref, tall_stage, sem_ref, STAGE_ROWS_OUT)
        mix()
        _stage_cast(wg_hbm, wg_ref, wide_stage, sem_ref, STAGE_ROWS_WIDE)
        _stage_cast(wu_hbm, wu_ref, wide_stage, sem_ref, STAGE_ROWS_WIDE)
        _stage_cast(wd_hbm, wd_ref, tall_stage, sem_ref, STAGE_ROWS_TALL)

    @pl.when(t > 0)
    def _():
        ffn()
        mix()


def _post(x, four, attn_main, attn_last, mod3, gpm, gpf, gqf, wo, wg, wu, wd, tm=512):
    b, s, d = x.shape
    d_ff = wg.shape[1]
    assert wo.shape[0] % STAGE_ROWS_OUT == 0 and STAGE_ROWS_OUT <= STAGE_ROWS_TALL
    assert d % STAGE_ROWS_WIDE == 0 and d_ff % STAGE_ROWS_TALL == 0
    nj = s // tm
    n_blocks = b * nj
    cur = lambda t: jnp.minimum(t, n_blocks - 1)
    prev = lambda t: jnp.maximum(t - 1, 0)
    const = lambda shape: pl.BlockSpec(shape, lambda t: (0,) * len(shape), pipeline_mode=pl.Buffered(1))
    rows = lambda w: pl.BlockSpec((1, tm, w), lambda t: (cur(t) // nj, cur(t) % nj, 0))
    hbm = pl.BlockSpec(memory_space=pl.ANY)
    return pl.pallas_call(
        _post_kernel,
        grid=(n_blocks + 1,),
        in_specs=[
            rows(d), rows(FOURIER_WIDTH), rows(attn_main.shape[2]), rows(attn_last.shape[2]),
            pl.BlockSpec((1, 6, d), lambda t: (cur(t) // nj, 0, 0)),
            pl.BlockSpec((1, 6, d), lambda t: (prev(t) // nj, 0, 0)),
            const(gpm.shape), const(gpf.shape), const(gqf.shape),
            hbm, hbm, hbm, hbm,
        ],
        out_specs=pl.BlockSpec((1, tm, d), lambda t: (prev(t) // nj, prev(t) % nj, 0)),
        out_shape=jax.ShapeDtypeStruct((b, s, d), F32),
        scratch_shapes=[
            pltpu.VMEM((tm, d), F32), pltpu.VMEM((tm, d), BF16),
            pltpu.VMEM(wo.shape, BF16), pltpu.VMEM(wg.shape, BF16), pltpu.VMEM(wu.shape, BF16),
            pltpu.VMEM(wd.shape, BF16),
            pltpu.VMEM((2, STAGE_ROWS_WIDE, d_ff), F32), pltpu.VMEM((2, STAGE_ROWS_TALL, d), F32),
            pltpu.SemaphoreType.DMA((2,)),
        ],
        compiler_params=pltpu.CompilerParams(
            dimension_semantics=("arbitrary",), vmem_limit_bytes=VMEM_LIMIT_BYTES),
        name="post",
    )(x, four, attn_main, attn_last, mod3, mod3, gpm, gpf, gqf, wo, wg, wu, wd)


def _rope_rotate_cols(w):
    a = QK_ROPE_DIM // 2
    hf = a // 2
    blocks = []
    for s0 in (0, a):
        blocks += [-w[..., s0 + hf:s0 + a], w[..., s0:s0 + hf]]
    return jnp.concatenate(blocks, axis=-1)


def _head_slot(nope, rope):
    pad = HEAD_SLOT - QK_NOPE_DIM - QK_ROPE_DIM
    return jnp.concatenate([nope, rope, jnp.zeros(rope.shape[:-1] + (pad,), rope.dtype)], axis=-1)


def _rope_tables(n_lat, q_scale):
    t = np.arange(n_lat)
    hf = QK_ROPE_DIM // 4
    inv_freq = ROPE_BASE ** (-np.arange(hf, dtype=np.float64) / hf)
    ar = (t // GRID_W)[:, None] * inv_freq[None, :]
    ac = (t % GRID_W)[:, None] * inv_freq[None, :]
    z = np.zeros_like(ar)
    cos32 = np.concatenate([np.cos(ar), np.cos(ar), np.cos(ac), np.cos(ac)], axis=-1)
    sin32 = np.concatenate([np.sin(ar), np.sin(ar), np.sin(ac), np.sin(ac)], axis=-1)
    below32 = np.concatenate([z, np.sin(ar), z, np.sin(ac)], axis=-1)
    above32 = np.concatenate([-np.sin(ar), z, -np.sin(ac), z], axis=-1)
    pad = np.zeros((n_lat, HEAD_SLOT - QK_NOPE_DIM - QK_ROPE_DIM))
    ones = np.ones((n_lat, QK_NOPE_DIM))
    zeros = np.zeros((n_lat, QK_NOPE_DIM))
    slot = lambda nope, rope: np.concatenate([nope, rope, pad], axis=-1)
    tables = [slot(ones * q_scale, cos32 * q_scale), slot(zeros, sin32 * q_scale),
              slot(zeros, cos32), slot(zeros, below32), slot(zeros, above32)]
    return jnp.asarray(np.stack(tables).astype(np.float32))


def _dft_tables(n_pos, n_ch):
    n_sub = n_pos // FFT_RADIX
    m = np.arange(n_sub, dtype=np.int64)
    ang = 2.0 * np.pi * ((m[:, None] * m[None, :]) % n_sub) / n_sub
    tmat = np.concatenate([np.cos(ang), np.sin(ang)], axis=0).astype(np.float32)
    tw = np.zeros((FFT_RADIX - 1, 2, n_sub, FOURIER_GROUP_DIM), np.float32)
    for r in range(1, FFT_RADIX):
        a = 2.0 * np.pi * r * m / n_pos
        tw[r - 1, 0] = np.cos(a)[:, None]
        tw[r - 1, 1] = np.sin(a)[:, None]
    c = np.arange(n_ch, dtype=np.int64)
    angc = 2.0 * np.pi * ((c[:, None] * c[None, :]) % n_ch) / n_ch
    norm = 1.0 / np.sqrt(float(n_pos * n_ch))
    cmat = np.concatenate([np.cos(angc) * norm, -np.sin(angc) * norm], axis=0).astype(np.float32)
    return jnp.asarray(tmat), jnp.asarray(tw), jnp.asarray(cmat)


def kernel(x, c, ctx, c_ctx, w_ada, b_ada, g_pre_mix, g_post_mix, g_pre_ffn, g_post_ffn, w_in, g_q_a,
           w_q_b, g_kv_a, w_kv_b, w_fourier, w_out, w_gate, w_up, w_down):
    assert w_ada.shape[0] == 1, "single-layer block"
    batch, n_lat, d = x.shape

    mod_rows = -(-(batch + 1) // 8) * 8
    cc = jnp.concatenate([c, c_ctx[None, :], jnp.zeros((mod_rows - batch - 1, d), F32)], axis=0)
    mod = _adaln(cc, w_ada[0], b_ada[0][None, :])
    mod3 = mod.reshape(mod_rows, 6, d)

    w_in0 = w_in[0]
    w_kr = w_in0[:, ROPE_COL:]
    zeros_d = jnp.zeros((d, QK_NOPE_DIM), F32)
    kr_slot = _head_slot(zeros_d, w_kr)
    win = jnp.concatenate([w_in0[:, :ROPE_COL], kr_slot], axis=1).astype(BF16)
    win_c = jnp.concatenate([w_in0[:, KV_COL:ROPE_COL], kr_slot], axis=1).astype(BF16)

    wq3 = w_q_b[0].reshape(Q_LORA_RANK, MLA_HEADS, QK_NOPE_DIM + QK_ROPE_DIM)
    wq_nope, wq_rope = wq3[..., :QK_NOPE_DIM], wq3[..., QK_NOPE_DIM:]
    wq_a = _head_slot(wq_nope, wq_rope).reshape(Q_LORA_RANK, QK_WIDTH)
    wq_b = _head_slot(jnp.zeros_like(wq_nope), _rope_rotate_cols(wq_rope)).reshape(Q_LORA_RANK, QK_WIDTH)
    wq = jnp.concatenate([wq_a, wq_b], axis=1).astype(BF16)

    wkv3 = w_kv_b[0].reshape(KV_LORA_RANK, MLA_HEADS, QK_NOPE_DIM + V_HEAD_DIM)
    wk_nope, wv = wkv3[..., :QK_NOPE_DIM], wkv3[..., QK_NOPE_DIM:]
    wk_slots = _head_slot(wk_nope, jnp.zeros(wk_nope.shape[:-1] + (QK_ROPE_DIM,), F32))
    wkv = jnp.concatenate([wk_slots.reshape(KV_LORA_RANK, QK_WIDTH),
                           wv.reshape(KV_LORA_RANK, ATTN_WIDTH)], axis=1).astype(BF16)

    q_scale = float((QK_NOPE_DIM + QK_ROPE_DIM) ** -0.5 * np.log2(np.e))
    rope = _rope_tables(n_lat, q_scale)
    tmat, tw, cmat = _dft_tables(n_lat, FOURIER_GROUP_DIM)

    row2 = lambda g: g[0][None, :]
    u_f, q, k_lat, v_lat = _premix(x, mod3, row2(g_pre_mix), win, row2(g_q_a), wq, row2(g_kv_a), wkv,
                                   rope)
    four = _fourier(u_f, tmat, tw, cmat, w_fourier[0].astype(BF16))
    k_ctx, v_ctx = _ctxkv(ctx, mod3, batch, row2(g_pre_mix), win_c, row2(g_kv_a), wkv)
    attn_main, attn_last = _attention(q, k_ctx, k_lat, v_ctx, v_lat)
    return _post(x, four, attn_main, attn_last, mod3, row2(g_post_mix), row2(g_pre_ffn), row2(g_post_ffn),
                 w_out[0], w_gate[0], w_up[0], w_down[0])
```

```python
import functools

import numpy as np
import jax
import jax.numpy as jnp
from jax import lax
from jax.experimental import pallas as pl
from jax.experimental.pallas import tpu as pltpu

F32 = jnp.float32
BF16 = jnp.bfloat16

D_MODEL = 1024
GRID_W = 64
FOURIER_GROUPS = 4
FOURIER_GROUP_DIM = 128
FOURIER_WIDTH = FOURIER_GROUPS * FOURIER_GROUP_DIM
MLA_HEADS = 8
QK_NOPE_DIM = 64
QK_ROPE_DIM = 32
V_HEAD_DIM = 64
Q_LORA_RANK = 256
KV_LORA_RANK = 128
KV_COL = FOURIER_WIDTH + Q_LORA_RANK
ROPE_COL = KV_COL + KV_LORA_RANK
ROPE_BASE = 10000.0
NORM_EPS = 1e-6
FFT_RADIX = 8
HEAD_SLOT = 128
STAGE_ROWS_WIDE = 256
STAGE_ROWS_TALL = 704
STAGE_ROWS_OUT = 512
SHIFT_DEN_MIN = 2.0 ** -40
SHIFT_DEN_MAX = 2.0 ** 40
PV_KEY_TILE = 256
ATTN_WIDTH = MLA_HEADS * V_HEAD_DIM
QK_WIDTH = MLA_HEADS * HEAD_SLOT

VMEM_LIMIT_BYTES = 56 * 1024 * 1024


def _rms(x, g):
    return x * lax.rsqrt(jnp.mean(x * x, axis=-1, keepdims=True) + NORM_EPS) * g


def _dot(a, b):
    return jnp.dot(a, b, preferred_element_type=F32)


def _dot_nt(a, b):
    return lax.dot_general(a, b, (((1,), (1,)), ((), ())), preferred_element_type=F32)


def _rotary(a, table, first):
    half = QK_ROPE_DIM // 4
    below = pltpu.roll(a, half, axis=1)
    above = pltpu.roll(a, HEAD_SLOT - half, axis=1)
    return a * table(first) + below * table(first + 1) + above * table(first + 2)


def _store_value_slots(v_ref, v):
    pair_w = 2 * V_HEAD_DIM
    lower = lax.broadcasted_iota(jnp.int32, (v.shape[0], pair_w), 1) < V_HEAD_DIM
    for pair in range(MLA_HEADS // 2):
        vp = v[:, pair * pair_w:(pair + 1) * pair_w]
        lo = 2 * pair * HEAD_SLOT
        v_ref[0, :, lo:lo + HEAD_SLOT] = jnp.where(lower, vp, 1.0).astype(BF16)
        v_ref[0, :, lo + HEAD_SLOT:lo + 2 * HEAD_SLOT] = jnp.where(lower, 1.0, vp).astype(BF16)


def _adaln_kernel(c_ref, cctx_ref, w_ref, b_ref, o_ref, rows_ref):
    n = c_ref.shape[0]
    rows_ref[:n, :] = c_ref[...]
    rows_ref[n:, :] = jnp.broadcast_to(cctx_ref[...], (rows_ref.shape[0] - n, rows_ref.shape[1]))
    c = rows_ref[...]
    a = c / (1.0 + jnp.exp(-c))
    o_ref[0] = _dot(a.astype(BF16), w_ref[...].astype(BF16)) + b_ref[...]


def _adaln(c, c_ctx, w_ada, b_ada):
    batch, d = c.shape
    assert batch % 8 == 0, "the context rows start on a sublane boundary"
    rows = batch + 8
    n_chunks = w_ada.shape[1] // d
    return pl.pallas_call(
        _adaln_kernel,
        grid=(n_chunks,),
        in_specs=[
            pl.BlockSpec((batch, d), lambda j: (0, 0)),
            pl.BlockSpec((1, d), lambda j: (0, 0)),
            pl.BlockSpec((d, d), lambda j: (0, j)),
            pl.BlockSpec((1, d), lambda j: (0, j)),
        ],
        out_specs=pl.BlockSpec((1, rows, d), lambda j: (j, 0, 0)),
        out_shape=jax.ShapeDtypeStruct((n_chunks, rows, d), F32),
        scratch_shapes=[pltpu.VMEM((rows, d), F32)],
        compiler_params=pltpu.CompilerParams(vmem_limit_bytes=VMEM_LIMIT_BYTES),
        name="adaln",
    )(c, c_ctx, w_ada, b_ada)


def _premix_kernel(x_ref, mod_ref, gpre_ref, win_ref, gq_ref, wq_ref, gkv_ref, wkv_ref,
                   rope_ref, u_ref, q_ref, k_ref, v_ref, us_ref):
    x = x_ref[0]
    row = pl.ds(pl.program_id(0), 1)
    shift = mod_ref[0, row, :]
    scale = mod_ref[1, row, :]
    h = _rms(x, gpre_ref[...] * (1.0 + scale)) + shift
    p = _dot(h.astype(BF16), win_ref[...])
    sub = us_ref.shape[1] // FFT_RADIX
    for g in range(FOURIER_GROUPS):
        glo = g * FOURIER_GROUP_DIM
        us_ref[g] = p[:, glo:glo + FOURIER_GROUP_DIM]
        for r in range(FFT_RADIX):
            lo = r * FOURIER_WIDTH + glo
            u_ref[0, :, lo:lo + FOURIER_GROUP_DIM] = (
                us_ref[g, pl.ds(r, sub, stride=FFT_RADIX), :].astype(BF16))

    qn = _rms(p[:, FOURIER_WIDTH:KV_COL], gq_ref[...]).astype(BF16)
    qq = _dot(qn, wq_ref[...])
    rows = x_ref.shape[1]
    row0 = pl.multiple_of(pl.program_id(1) * rows, rows)
    table = lambda k: rope_ref[k, pl.ds(row0, rows), :]
    cosq = table(0)
    sinq = table(1)
    for hd in range(MLA_HEADS):
        lo = hd * HEAD_SLOT
        q_ref[0, :, lo:lo + HEAD_SLOT] = (
            qq[:, lo:lo + HEAD_SLOT] * cosq + qq[:, QK_WIDTH + lo:QK_WIDTH + lo + HEAD_SLOT] * sinq
        ).astype(BF16)

    kvn = _rms(p[:, KV_COL:ROPE_COL], gkv_ref[...]).astype(BF16)
    kv = _dot(kvn, wkv_ref[...])
    kr = _rotary(p[:, ROPE_COL:ROPE_COL + HEAD_SLOT], table, 2)
    for hd in range(MLA_HEADS):
        lo = hd * HEAD_SLOT
        k_ref[0, :, lo:lo + HEAD_SLOT] = (kv[:, lo:lo + HEAD_SLOT] + kr).astype(BF16)
    _store_value_slots(v_ref, kv[:, QK_WIDTH:])


def _premix(x, mod, gpre, win, gq, wq, gkv, wkv, rope, tm=1024):
    b, s, d = x.shape
    const = lambda shape: pl.BlockSpec(shape, lambda i, j: (0,) * len(shape))
    rows = lambda w: pl.BlockSpec((1, tm, w), lambda i, j: (i, j, 0))
    return pl.pallas_call(
        _premix_kernel,
        grid=(b, s // tm),
        in_specs=[
            rows(d),
            pl.BlockSpec(mod.shape, lambda i, j: (0, 0, 0)),
            const(gpre.shape), const(win.shape), const(gq.shape), const(wq.shape),
            const(gkv.shape), const(wkv.shape),
            pl.BlockSpec(rope.shape, lambda i, j: (0, 0, 0), pipeline_mode=pl.Buffered(1)),
        ],
        out_specs=[
            pl.BlockSpec((1, tm // FFT_RADIX, FFT_RADIX * FOURIER_WIDTH), lambda i, j: (i, j, 0)),
            rows(QK_WIDTH), rows(QK_WIDTH), rows(QK_WIDTH)],
        out_shape=[
            jax.ShapeDtypeStruct((b, s // FFT_RADIX, FFT_RADIX * FOURIER_WIDTH), BF16),
            jax.ShapeDtypeStruct((b, s, QK_WIDTH), BF16),
            jax.ShapeDtypeStruct((b, s, QK_WIDTH), BF16),
            jax.ShapeDtypeStruct((b, s, QK_WIDTH), BF16),
        ],
        scratch_shapes=[pltpu.VMEM((FOURIER_GROUPS, tm, FOURIER_GROUP_DIM), F32)],
        compiler_params=pltpu.CompilerParams(vmem_limit_bytes=VMEM_LIMIT_BYTES),
        name="premix",
    )(x, mod, gpre, win, gq, wq, gkv, wkv, rope)


def _ctxkv_kernel(x_ref, mod_ref, gpre_ref, win_ref, gkv_ref, wkv_ref, k_ref, v_ref):
    x = x_ref[0]
    shift = mod_ref[0, 0:1, :]
    scale = mod_ref[1, 0:1, :]
    h = _rms(x, gpre_ref[...] * (1.0 + scale)) + shift
    p = _dot(h.astype(BF16), win_ref[...])
    kvn = _rms(p[:, :KV_LORA_RANK], gkv_ref[...]).astype(BF16)
    kv = _dot(kvn, wkv_ref[...])
    kr = p[:, KV_LORA_RANK:]
    for hd in range(MLA_HEADS):
        lo = hd * HEAD_SLOT
        k_ref[0, :, lo:lo + HEAD_SLOT] = (kv[:, lo:lo + HEAD_SLOT] + kr).astype(BF16)
    _store_value_slots(v_ref, kv[:, QK_WIDTH:])


def _ctxkv(ctx, mod, gpre, win_c, gkv, wkv):
    b, c, d = ctx.shape
    const = lambda shape: pl.BlockSpec(shape, lambda i: (0,) * len(shape))
    rows = lambda w: pl.BlockSpec((1, c, w), lambda i: (i, 0, 0))
    return pl.pallas_call(
        _ctxkv_kernel,
        grid=(b,),
        in_specs=[
            rows(d),
            pl.BlockSpec((mod.shape[0], 8, d), lambda i: (0, b // 8, 0)),
            const(gpre.shape), const(win_c.shape), const(gkv.shape), const(wkv.shape),
        ],
        out_specs=[rows(QK_WIDTH), rows(QK_WIDTH)],
        out_shape=[
            jax.ShapeDtypeStruct((b, c, QK_WIDTH), BF16),
            jax.ShapeDtypeStruct((b, c, QK_WIDTH), BF16),
        ],
        name="ctxkv",
    )(ctx, mod, gpre, win_c, gkv, wkv)


def _attn_kernel(q_ref, kc_ref, kl_ref, vc_ref, vl_ref, vcp_ref, vlp_ref, omain_ref, olast_ref,
                 p_ref, oprev_ref, ohead_ref):
    t = pl.program_id(0)
    n_blocks = pl.num_programs(0) - 1
    n_ctx = kc_ref.shape[1]
    n_lat = kl_ref.shape[1]
    tq = q_ref.shape[1]
    pair_w = 2 * V_HEAD_DIM
    first_half = lax.broadcasted_iota(jnp.int32, (tq, pair_w), 1) < V_HEAD_DIM
    last = MLA_HEADS - 1

    @pl.when(t == 0)
    def _():
        p_ref[...] = jnp.ones_like(p_ref)
        oprev_ref[...] = jnp.ones_like(oprev_ref)

    def join_pair(o_even, o_odd):
        num = jnp.where(first_half, o_even, o_odd)
        den = pltpu.roll(jnp.where(first_half, o_odd, o_even), V_HEAD_DIM, axis=1)
        return num / den

    def den_of(o, hd):
        lane = V_HEAD_DIM if hd % 2 == 0 else 0
        return o[:, lane:lane + 1]

    def drain():
        o = _dot(p_ref[:, :n_ctx], vcp_ref[0]) + _dot(p_ref[:, n_ctx:], vlp_ref[0])
        olast_ref[0] = join_pair(oprev_ref[...], o).astype(BF16)

    def head(hd, exact):
        lo = hd * HEAD_SLOT
        qh = q_ref[0, :, lo:lo + HEAD_SLOT]
        s_c = _dot_nt(qh, kc_ref[0, :, lo:lo + HEAD_SLOT])
        s_l = _dot_nt(qh, kl_ref[0, :, lo:lo + HEAD_SLOT])
        shift = jnp.max(s_c, axis=-1, keepdims=True)
        if exact:
            shift = jnp.maximum(shift, jnp.max(s_l, axis=-1, keepdims=True))
        if hd == last:
            p_c = jnp.exp2(s_c - shift)
            p_l = jnp.exp2(s_l - shift)
            p_ref[:, :n_ctx] = p_c.astype(BF16)
            p_ref[:, n_ctx:] = p_l.astype(BF16)
            return None, jnp.sum(p_c, axis=-1, keepdims=True) + jnp.sum(p_l, axis=-1, keepdims=True)
        o = _dot(jnp.exp2(s_c - shift).astype(BF16), vc_ref[0, :, lo:lo + HEAD_SLOT])
        for k0 in range(0, n_lat, PV_KEY_TILE):
            p = jnp.exp2(s_l[:, k0:k0 + PV_KEY_TILE] - shift).astype(BF16)
            o = o + _dot(p, vl_ref[0, k0:k0 + PV_KEY_TILE, lo:lo + HEAD_SLOT])
        return o, den_of(o, hd)

    def block():
        dens = []
        for pair in range(MLA_HEADS // 2):
            outs = []
            for hd in (2 * pair, 2 * pair + 1):
                o, den = head(hd, False)
                dens.append(den)
                if o is not None:
                    outs.append(o)
            if len(outs) == 2:
                vlo = pair * pair_w
                omain_ref[0, :, vlo:vlo + pair_w] = join_pair(outs[0], outs[1]).astype(BF16)
            else:
                oprev_ref[...] = outs[0]
        return dens

    def redo_block_exact():
        def one_head(hd, carry):
            lo = pl.multiple_of(hd * HEAD_SLOT, HEAD_SLOT)
            qh = q_ref[0, :, pl.ds(lo, HEAD_SLOT)]
            s_c = _dot_nt(qh, kc_ref[0, :, pl.ds(lo, HEAD_SLOT)])
            s_l = _dot_nt(qh, kl_ref[0, :, pl.ds(lo, HEAD_SLOT)])
            shift = jnp.maximum(jnp.max(s_c, axis=-1, keepdims=True), jnp.max(s_l, axis=-1, keepdims=True))
            ohead_ref[hd] = (_dot(jnp.exp2(s_c - shift).astype(BF16), vc_ref[0, :, pl.ds(lo, HEAD_SLOT)])
                             + _dot(jnp.exp2(s_l - shift).astype(BF16), vl_ref[0, :, pl.ds(lo, HEAD_SLOT)]))
            return carry

        lax.fori_loop(0, last, one_head, 0)
        for pair in range(MLA_HEADS // 2 - 1):
            vlo = pair * pair_w
            omain_ref[0, :, vlo:vlo + pair_w] = join_pair(ohead_ref[2 * pair], ohead_ref[2 * pair + 1]).astype(BF16)
        oprev_ref[...] = ohead_ref[last - 1]
        head(last, True)

    @pl.when(t < n_blocks)
    def _():
        drain()
        dens = block()
        lo_den, hi_den = dens[0], dens[0]
        for den in dens[1:]:
            lo_den = jnp.minimum(lo_den, den)
            hi_den = jnp.maximum(hi_den, den)
        trusted = jnp.logical_and(jnp.min(lo_den) >= SHIFT_DEN_MIN, jnp.max(hi_den) <= SHIFT_DEN_MAX)

        @pl.when(jnp.logical_not(trusted))
        def _():
            redo_block_exact()

    @pl.when(t == n_blocks)
    def _():
        drain()


def _attention(q, kc, kl, vc, vl, tq=512):
    b, s, _ = q.shape
    c = kc.shape[1]
    nq = s // tq
    n_blocks = b * nq
    pair_w = 2 * V_HEAD_DIM
    main_w = ATTN_WIDTH - pair_w
    last = MLA_HEADS - 1
    cur = lambda t: jnp.minimum(t, n_blocks - 1)
    prev = lambda t: jnp.maximum(t - 1, 0)
    return pl.pallas_call(
        _attn_kernel,
        grid=(n_blocks + 1,),
        in_specs=[
            pl.BlockSpec((1, tq, QK_WIDTH), lambda t: (cur(t) // nq, cur(t) % nq, 0)),
            pl.BlockSpec((1, c, QK_WIDTH), lambda t: (cur(t) // nq, 0, 0)),
            pl.BlockSpec((1, s, QK_WIDTH), lambda t: (cur(t) // nq, 0, 0)),
            pl.BlockSpec((1, c, QK_WIDTH), lambda t: (cur(t) // nq, 0, 0)),
            pl.BlockSpec((1, s, QK_WIDTH), lambda t: (cur(t) // nq, 0, 0)),
            pl.BlockSpec((1, c, HEAD_SLOT), lambda t: (prev(t) // nq, 0, last)),
            pl.BlockSpec((1, s, HEAD_SLOT), lambda t: (prev(t) // nq, 0, last)),
        ],
        out_specs=[
            pl.BlockSpec((1, tq, main_w), lambda t: (cur(t) // nq, cur(t) % nq, 0)),
            pl.BlockSpec((1, tq, pair_w), lambda t: (prev(t) // nq, prev(t) % nq, 0)),
        ],
        out_shape=[
            jax.ShapeDtypeStruct((b, s, main_w), BF16),
            jax.ShapeDtypeStruct((b, s, pair_w), BF16),
        ],
        scratch_shapes=[
            pltpu.VMEM((tq, c + s), BF16),
            pltpu.VMEM((tq, pair_w), F32),
            pltpu.VMEM((MLA_HEADS - 1, tq, HEAD_SLOT), F32),
        ],
        compiler_params=pltpu.CompilerParams(
            dimension_semantics=("arbitrary",), vmem_limit_bytes=VMEM_LIMIT_BYTES),
        name="attn",
    )(q, kc, kl, vc, vl, vc, vl)


def _fourier_kernel(t_ref, tw_ref, u_ref, cc_ref, wf_ref, o_ref, tb_ref, cw_ref):
    @pl.when(pl.program_id(0) == 0)
    def _():
        tb_ref[...] = t_ref[...].astype(BF16)
        cc = cc_ref[...].astype(BF16)
        for g in range(FOURIER_GROUPS):
            cw_ref[g] = _dot(cc, wf_ref[g]).astype(BF16)

    n_sub = tb_ref.shape[1]
    f = _dot(tb_ref[...], u_ref[0])
    gd = FOURIER_GROUP_DIM
    for g in range(FOURIER_GROUPS):
        gr, gi = [], []
        for r in range(FFT_RADIX):
            lo = r * FOURIER_WIDTH + g * gd
            a = f[:n_sub, lo:lo + gd]
            b = f[n_sub:, lo:lo + gd]
            if r == 0:
                gr.append(a)
                gi.append(b)
            else:
                c = tw_ref[r - 1, 0]
                s = tw_ref[r - 1, 1]
                gr.append(a * c - b * s)
                gi.append(a * s + b * c)
        z = list(zip(gr, gi))
        add = lambda a, b: (a[0] + b[0], a[1] + b[1])
        sub = lambda a, b: (a[0] - b[0], a[1] - b[1])
        times_minus_i = lambda a: (-a[1], a[0])

        def dft4(c):
            e0, e1 = add(c[0], c[2]), sub(c[0], c[2])
            f0, f1 = add(c[1], c[3]), times_minus_i(sub(c[1], c[3]))
            return [add(e0, f0), add(e1, f1), sub(e0, f0), sub(e1, f1)]

        half = FFT_RADIX // 2
        even = dft4([add(z[r], z[r + half]) for r in range(half)])
        d = [sub(z[r], z[r + half]) for r in range(half)]
        rt = np.float32(np.sqrt(0.5))
        odd = dft4([
            d[0],
            ((d[1][0] - d[1][1]) * rt, (d[1][0] + d[1][1]) * rt),
            times_minus_i(d[2]),
            (-(d[3][0] + d[3][1]) * rt, (d[3][0] - d[3][1]) * rt),
        ])
        y = [even[q // 2] if q % 2 == 0 else odd[q // 2] for q in range(FFT_RADIX)]
        xr = jnp.concatenate([v[0] for v in y], axis=0)
        xi = jnp.concatenate([v[1] for v in y], axis=0)
        lhs = jnp.concatenate([xr, xi], axis=1).astype(BF16)
        o_ref[0, :, g * gd:(g + 1) * gd] = _dot(lhs, cw_ref[g]).astype(BF16)


def _fourier(u4, tmat, tw, cmat, wf):
    b, n_sub, _ = u4.shape
    s = n_sub * FFT_RADIX
    full = lambda a: pl.BlockSpec(a.shape, lambda i: (0,) * a.ndim)
    return pl.pallas_call(
        _fourier_kernel,
        grid=(b,),
        in_specs=[
            full(tmat), full(tw),
            pl.BlockSpec((1, n_sub, FFT_RADIX * FOURIER_WIDTH), lambda i: (i, 0, 0)),
            full(cmat), full(wf),
        ],
        out_specs=pl.BlockSpec((1, s, FOURIER_WIDTH), lambda i: (i, 0, 0)),
        out_shape=jax.ShapeDtypeStruct((b, s, FOURIER_WIDTH), BF16),
        scratch_shapes=[
            pltpu.VMEM(tmat.shape, BF16),
            pltpu.VMEM((FOURIER_GROUPS, 2 * FOURIER_GROUP_DIM, FOURIER_GROUP_DIM), BF16),
        ],
        compiler_params=pltpu.CompilerParams(
            dimension_semantics=("arbitrary",), vmem_limit_bytes=VMEM_LIMIT_BYTES),
        name="fourier",
    )(tmat, tw, u4, cmat, wf)


def _stage_cast(src_hbm, dst_ref, stage_ref, sem_ref, chunk):
    n_chunks = src_hbm.shape[0] // chunk

    def copy(i):
        slot = i % 2
        return pltpu.make_async_copy(
            src_hbm.at[pl.ds(i * chunk, chunk)], stage_ref.at[slot, pl.ds(0, chunk)], sem_ref.at[slot])

    def body(i, carry):
        @pl.when(i + 1 < n_chunks)
        def _():
            copy(i + 1).start()

        copy(i).wait()
        start = pl.multiple_of(i * chunk, chunk)
        dst_ref[pl.ds(start, chunk), :] = stage_ref[i % 2, pl.ds(0, chunk), :].astype(BF16)
        return carry

    copy(0).start()
    lax.fori_loop(0, n_chunks, body, 0)


def _post_kernel(x_ref, four_ref, am_ref, al_ref, mod_ref, gpm_ref, gpf_ref, gqf_ref,
                 wo_hbm, wg_hbm, wu_hbm, wd_hbm, o_ref,
                 x1_ref, h2_ref, wo_ref, wg_ref, wu_ref, wd_ref, wide_stage, tall_stage, sem_ref,
                 *, blocks_per_batch):
    t = pl.program_id(0)
    n_four = four_ref.shape[2]
    mix_row = pl.ds(jnp.minimum(t, pl.num_programs(0) - 2) // blocks_per_batch, 1)
    ffn_row = pl.ds(jnp.maximum(t - 1, 0) // blocks_per_batch, 1)

    def mix():
        gt_m = mod_ref[2, mix_row, :]
        sh_f = mod_ref[3, mix_row, :]
        sc_f = mod_ref[4, mix_row, :]
        attn = jnp.concatenate([am_ref[0], al_ref[0]], axis=1)
        y = _dot(four_ref[0], wo_ref[:n_four, :]) + _dot(attn, wo_ref[n_four:, :])
        x1 = x_ref[0] + _rms(y, gt_m * gpm_ref[...])
        x1_ref[...] = x1
        h2_ref[...] = (_rms(x1, gpf_ref[...] * (1.0 + sc_f)) + sh_f).astype(BF16)

    def ffn():
        gt_f = mod_ref[5, ffn_row, :]
        h2 = h2_ref[...]
        g = _dot(h2, wg_ref[...])
        up = _dot(h2, wu_ref[...])
        act = (g / (1.0 + jnp.exp(-g)) * up).astype(BF16)
        o_ref[0] = x1_ref[...] + _rms(_dot(act, wd_ref[...]), gt_f * gqf_ref[...])

    @pl.when(t == 0)
    def _():
        _stage_cast(wo_hbm, wo_ref, tall_stage, sem_ref, STAGE_ROWS_OUT)
        mix()
        _stage_cast(wg_hbm, wg_ref, wide_stage, sem_ref, STAGE_ROWS_WIDE)
        _stage_cast(wu_hbm, wu_ref, wide_stage, sem_ref, STAGE_ROWS_WIDE)
        _stage_cast(wd_hbm, wd_ref, tall_stage, sem_ref, STAGE_ROWS_TALL)

    @pl.when(t > 0)
    def _():
        ffn()
        mix()


def _post(x, four, attn_main, attn_last, mod, gpm, gpf, gqf, wo, wg, wu, wd, tm=512):
    b, s, d = x.shape
    d_ff = wg.shape[1]
    assert wo.shape[0] % STAGE_ROWS_OUT == 0 and STAGE_ROWS_OUT <= STAGE_ROWS_TALL
    assert d % STAGE_ROWS_WIDE == 0 and d_ff % STAGE_ROWS_TALL == 0
    nj = s // tm
    n_blocks = b * nj
    cur = lambda t: jnp.minimum(t, n_blocks - 1)
    prev = lambda t: jnp.maximum(t - 1, 0)
    const = lambda shape: pl.BlockSpec(shape, lambda t: (0,) * len(shape), pipeline_mode=pl.Buffered(1))
    rows = lambda w: pl.BlockSpec((1, tm, w), lambda t: (cur(t) // nj, cur(t) % nj, 0))
    hbm = pl.BlockSpec(memory_space=pl.ANY)
    return pl.pallas_call(
        functools.partial(_post_kernel, blocks_per_batch=nj),
        grid=(n_blocks + 1,),
        in_specs=[
            rows(d), rows(FOURIER_WIDTH), rows(attn_main.shape[2]), rows(attn_last.shape[2]),
            pl.BlockSpec(mod.shape, lambda t: (0, 0, 0)),
            const(gpm.shape), const(gpf.shape), const(gqf.shape),
            hbm, hbm, hbm, hbm,
        ],
        out_specs=pl.BlockSpec((1, tm, d), lambda t: (prev(t) // nj, prev(t) % nj, 0)),
        out_shape=jax.ShapeDtypeStruct((b, s, d), F32),
        scratch_shapes=[
            pltpu.VMEM((tm, d), F32), pltpu.VMEM((tm, d), BF16),
            pltpu.VMEM(wo.shape, BF16), pltpu.VMEM(wg.shape, BF16), pltpu.VMEM(wu.shape, BF16),
            pltpu.VMEM(wd.shape, BF16),
            pltpu.VMEM((2, STAGE_ROWS_WIDE, d_ff), F32), pltpu.VMEM((2, STAGE_ROWS_TALL, d), F32),
            pltpu.SemaphoreType.DMA((2,)),
        ],
        compiler_params=pltpu.CompilerParams(
            dimension_semantics=("arbitrary",), vmem_limit_bytes=VMEM_LIMIT_BYTES),
        name="post",
    )(x, four, attn_main, attn_last, mod, gpm, gpf, gqf, wo, wg, wu, wd)


def _rope_rotate_cols(w):
    a = QK_ROPE_DIM // 2
    hf = a // 2
    blocks = []
    for s0 in (0, a):
        blocks += [-w[..., s0 + hf:s0 + a], w[..., s0:s0 + hf]]
    return jnp.concatenate(blocks, axis=-1)


def _head_slot(nope, rope):
    pad = HEAD_SLOT - QK_NOPE_DIM - QK_ROPE_DIM
    return jnp.concatenate([nope, rope, jnp.zeros(rope.shape[:-1] + (pad,), rope.dtype)], axis=-1)


def _rope_tables(n_lat, q_scale):
    t = np.arange(n_lat)
    hf = QK_ROPE_DIM // 4
    inv_freq = ROPE_BASE ** (-np.arange(hf, dtype=np.float64) / hf)
    ar = (t // GRID_W)[:, None] * inv_freq[None, :]
    ac = (t % GRID_W)[:, None] * inv_freq[None, :]
    z = np.zeros_like(ar)
    cos32 = np.concatenate([np.cos(ar), np.cos(ar), np.cos(ac), np.cos(ac)], axis=-1)
    sin32 = np.concatenate([np.sin(ar), np.sin(ar), np.sin(ac), np.sin(ac)], axis=-1)
    below32 = np.concatenate([z, np.sin(ar), z, np.sin(ac)], axis=-1)
    above32 = np.concatenate([-np.sin(ar), z, -np.sin(ac), z], axis=-1)
    pad = np.zeros((n_lat, HEAD_SLOT - QK_NOPE_DIM - QK_ROPE_DIM))
    ones = np.ones((n_lat, QK_NOPE_DIM))
    zeros = np.zeros((n_lat, QK_NOPE_DIM))
    slot = lambda nope, rope: np.concatenate([nope, rope, pad], axis=-1)
    tables = [slot(ones * q_scale, cos32 * q_scale), slot(zeros, sin32 * q_scale),
              slot(zeros, cos32), slot(zeros, below32), slot(zeros, above32)]
    return jnp.asarray(np.stack(tables).astype(np.float32))


def _dft_tables(n_pos, n_ch):
    n_sub = n_pos // FFT_RADIX
    m = np.arange(n_sub, dtype=np.int64)
    ang = 2.0 * np.pi * ((m[:, None] * m[None, :]) % n_sub) / n_sub
    tmat = np.concatenate([np.cos(ang), np.sin(ang)], axis=0).astype(np.float32)
    tw = np.zeros((FFT_RADIX - 1, 2, n_sub, FOURIER_GROUP_DIM), np.float32)
    for r in range(1, FFT_RADIX):
        a = 2.0 * np.pi * r * m / n_pos
        tw[r - 1, 0] = np.cos(a)[:, None]
        tw[r - 1, 1] = np.sin(a)[:, None]
    c = np.arange(n_ch, dtype=np.int64)
    angc = 2.0 * np.pi * ((c[:, None] * c[None, :]) % n_ch) / n_ch
    norm = 1.0 / np.sqrt(float(n_pos * n_ch))
    cmat = np.concatenate([np.cos(angc) * norm, -np.sin(angc) * norm], axis=0).astype(np.float32)
    return jnp.asarray(tmat), jnp.asarray(tw), jnp.asarray(cmat)


def kernel(x, c, ctx, c_ctx, w_ada, b_ada, g_pre_mix, g_post_mix, g_pre_ffn, g_post_ffn, w_in, g_q_a,
           w_q_b, g_kv_a, w_kv_b, w_fourier, w_out, w_gate, w_up, w_down):
    assert w_ada.shape[0] == 1, "single-layer block"
    batch, n_lat, d = x.shape

    mod = _adaln(c, c_ctx[None, :], w_ada[0], b_ada[0][None, :])

    w_in0 = w_in[0]
    w_kr = w_in0[:, ROPE_COL:]
    zeros_d = jnp.zeros((d, QK_NOPE_DIM), F32)
    kr_slot = _head_slot(zeros_d, w_kr)
    win = jnp.concatenate([w_in0[:, :ROPE_COL], kr_slot], axis=1).astype(BF16)
    win_c = jnp.concatenate([w_in0[:, KV_COL:ROPE_COL], kr_slot], axis=1).astype(BF16)

    wq3 = w_q_b[0].reshape(Q_LORA_RANK, MLA_HEADS, QK_NOPE_DIM + QK_ROPE_DIM)
    wq_nope, wq_rope = wq3[..., :QK_NOPE_DIM], wq3[..., QK_NOPE_DIM:]
    wq_a = _head_slot(wq_nope, wq_rope).reshape(Q_LORA_RANK, QK_WIDTH)
    wq_b = _head_slot(jnp.zeros_like(wq_nope), _rope_rotate_cols(wq_rope)).reshape(Q_LORA_RANK, QK_WIDTH)
    wq = jnp.concatenate([wq_a, wq_b], axis=1).astype(BF16)

    wkv3 = w_kv_b[0].reshape(KV_LORA_RANK, MLA_HEADS, QK_NOPE_DIM + V_HEAD_DIM)
    wk_nope, wv = wkv3[..., :QK_NOPE_DIM], wkv3[..., QK_NOPE_DIM:]
    wk_slots = _head_slot(wk_nope, jnp.zeros(wk_nope.shape[:-1] + (QK_ROPE_DIM,), F32))
    wkv = jnp.concatenate([wk_slots.reshape(KV_LORA_RANK, QK_WIDTH),
                           wv.reshape(KV_LORA_RANK, ATTN_WIDTH)], axis=1).astype(BF16)

    q_scale = float((QK_NOPE_DIM + QK_ROPE_DIM) ** -0.5 * np.log2(np.e))
    rope = _rope_tables(n_lat, q_scale)
    tmat, tw, cmat = _dft_tables(n_lat, FOURIER_GROUP_DIM)

    row2 = lambda g: g[0][None, :]
    u_f, q, k_lat, v_lat = _premix(x, mod, row2(g_pre_mix), win, row2(g_q_a), wq, row2(g_kv_a), wkv,
                                   rope)
    four = _fourier(u_f, tmat, tw, cmat, w_fourier[0].astype(BF16))
    k_ctx, v_ctx = _ctxkv(ctx, mod, row2(g_pre_mix), win_c, row2(g_kv_a), wkv)
    attn_main, attn_last = _attention(q, k_ctx, k_lat, v_ctx, v_lat)
    return _post(x, four, attn_main, attn_last, mod, row2(g_post_mix), row2(g_pre_ffn), row2(g_post_ffn),
                 w_out[0], w_gate[0], w_up[0], w_down[0])
```

```python
import functools

import numpy as np
import jax
import jax.numpy as jnp
from jax import lax
from jax.experimental import pallas as pl
from jax.experimental.pallas import tpu as pltpu

F32 = jnp.float32
BF16 = jnp.bfloat16

D_MODEL = 1024
GRID_W = 64
FOURIER_GROUPS = 4
FOURIER_GROUP_DIM = 128
FOURIER_WIDTH = FOURIER_GROUPS * FOURIER_GROUP_DIM
MLA_HEADS = 8
QK_NOPE_DIM = 64
QK_ROPE_DIM = 32
V_HEAD_DIM = 64
Q_LORA_RANK = 256
KV_LORA_RANK = 128
KV_COL = FOURIER_WIDTH + Q_LORA_RANK
ROPE_COL = KV_COL + KV_LORA_RANK
ROPE_BASE = 10000.0
NORM_EPS = 1e-6
FFT_RADIX = 8
HEAD_SLOT = 128
STAGE_ROWS_WIDE = 256
STAGE_ROWS_TALL = 704
STAGE_ROWS_OUT = 512
SHIFT_DEN_MIN = 2.0 ** -40
SHIFT_DEN_MAX = 2.0 ** 40
PV_KEY_TILE = 256
ATTN_WIDTH = MLA_HEADS * V_HEAD_DIM
QK_WIDTH = MLA_HEADS * HEAD_SLOT

VMEM_LIMIT_BYTES = 56 * 1024 * 1024


def _rms(x, g):
    return x * lax.rsqrt(jnp.mean(x * x, axis=-1, keepdims=True) + NORM_EPS) * g


def _dot(a, b):
    return jnp.dot(a, b, preferred_element_type=F32)


def _dot_nt(a, b):
    return lax.dot_general(a, b, (((1,), (1,)), ((), ())), preferred_element_type=F32)


def _rotary(a, table, first):
    half = QK_ROPE_DIM // 4
    below = pltpu.roll(a, half, axis=1)
    above = pltpu.roll(a, HEAD_SLOT - half, axis=1)
    return a * table(first) + below * table(first + 1) + above * table(first + 2)


def _store_value_slots(v_ref, v):
    pair_w = 2 * V_HEAD_DIM
    lower = lax.broadcasted_iota(jnp.int32, (v.shape[0], pair_w), 1) < V_HEAD_DIM
    for pair in range(MLA_HEADS // 2):
        vp = v[:, pair * pair_w:(pair + 1) * pair_w]
        lo = 2 * pair * HEAD_SLOT
        v_ref[0, :, lo:lo + HEAD_SLOT] = jnp.where(lower, vp, 1.0).astype(BF16)
        v_ref[0, :, lo + HEAD_SLOT:lo + 2 * HEAD_SLOT] = jnp.where(lower, 1.0, vp).astype(BF16)


def _adaln_kernel(c_ref, cctx_ref, w_ref, b_ref, o_ref, rows_ref):
    n = c_ref.shape[0]
    rows_ref[:n, :] = c_ref[...]
    rows_ref[n:, :] = jnp.broadcast_to(cctx_ref[...], (rows_ref.shape[0] - n, rows_ref.shape[1]))
    c = rows_ref[...]
    a = c / (1.0 + jnp.exp(-c))
    o_ref[0] = _dot(a.astype(BF16), w_ref[...].astype(BF16)) + b_ref[...]


def _adaln(c, c_ctx, w_ada, b_ada):
    batch, d = c.shape
    assert batch % 8 == 0, "the context rows start on a sublane boundary"
    rows = batch + 8
    n_chunks = w_ada.shape[1] // d
    return pl.pallas_call(
        _adaln_kernel,
        grid=(n_chunks,),
        in_specs=[
            pl.BlockSpec((batch, d), lambda j: (0, 0)),
            pl.BlockSpec((1, d), lambda j: (0, 0)),
            pl.BlockSpec((d, d), lambda j: (0, j)),
            pl.BlockSpec((1, d), lambda j: (0, j)),
        ],
        out_specs=pl.BlockSpec((1, rows, d), lambda j: (j, 0, 0)),
        out_shape=jax.ShapeDtypeStruct((n_chunks, rows, d), F32),
        scratch_shapes=[pltpu.VMEM((rows, d), F32)],
        compiler_params=pltpu.CompilerParams(vmem_limit_bytes=VMEM_LIMIT_BYTES),
        name="adaln",
    )(c, c_ctx, w_ada, b_ada)


def _premix_kernel(x_ref, mod_ref, gpre_ref, win_ref, gq_ref, wq_ref, gkv_ref, wkv_ref,
                   rope_ref, u_ref, q_ref, k_ref, v_ref, us_ref):
    x = x_ref[0]
    row = pl.ds(pl.program_id(0), 1)
    shift = mod_ref[0, row, :]
    scale = mod_ref[1, row, :]
    h = _rms(x, gpre_ref[...] * (1.0 + scale)) + shift
    p = _dot(h.astype(BF16), win_ref[...])
    sub = us_ref.shape[1] // FFT_RADIX
    for g in range(FOURIER_GROUPS):
        glo = g * FOURIER_GROUP_DIM
        us_ref[g] = p[:, glo:glo + FOURIER_GROUP_DIM]
        for r in range(FFT_RADIX):
            lo = r * FOURIER_WIDTH + glo
            u_ref[0, :, lo:lo + FOURIER_GROUP_DIM] = (
                us_ref[g, pl.ds(r, sub, stride=FFT_RADIX), :].astype(BF16))

    qn = _rms(p[:, FOURIER_WIDTH:KV_COL], gq_ref[...]).astype(BF16)
    qq = _dot(qn, wq_ref[...])
    rows = x_ref.shape[1]
    row0 = pl.multiple_of(pl.program_id(1) * rows, rows)
    table = lambda k: rope_ref[k, pl.ds(row0, rows), :]
    cosq = table(0)
    sinq = table(1)
    for hd in range(MLA_HEADS):
        lo = hd * HEAD_SLOT
        q_ref[0, :, lo:lo + HEAD_SLOT] = (
            qq[:, lo:lo + HEAD_SLOT] * cosq + qq[:, QK_WIDTH + lo:QK_WIDTH + lo + HEAD_SLOT] * sinq
        ).astype(BF16)

    kvn = _rms(p[:, KV_COL:ROPE_COL], gkv_ref[...]).astype(BF16)
    kv = _dot(kvn, wkv_ref[...])
    kr = _rotary(p[:, ROPE_COL:ROPE_COL + HEAD_SLOT], table, 2)
    for hd in range(MLA_HEADS):
        lo = hd * HEAD_SLOT
        k_ref[0, :, lo:lo + HEAD_SLOT] = (kv[:, lo:lo + HEAD_SLOT] + kr).astype(BF16)
    _store_value_slots(v_ref, kv[:, QK_WIDTH:])


def _premix(x, mod, gpre, win, gq, wq, gkv, wkv, rope, tm=1024):
    b, s, d = x.shape
    const = lambda shape: pl.BlockSpec(shape, lambda i, j: (0,) * len(shape))
    rows = lambda w: pl.BlockSpec((1, tm, w), lambda i, j: (i, j, 0))
    return pl.pallas_call(
        _premix_kernel,
        grid=(b, s // tm),
        in_specs=[
            rows(d),
            pl.BlockSpec(mod.shape, lambda i, j: (0, 0, 0)),
            const(gpre.shape), const(win.shape), const(gq.shape), const(wq.shape),
            const(gkv.shape), const(wkv.shape),
            pl.BlockSpec(rope.shape, lambda i, j: (0, 0, 0), pipeline_mode=pl.Buffered(1)),
        ],
        out_specs=[
            pl.BlockSpec((1, tm // FFT_RADIX, FFT_RADIX * FOURIER_WIDTH), lambda i, j: (i, j, 0)),
            rows(QK_WIDTH), rows(QK_WIDTH), rows(QK_WIDTH)],
        out_shape=[
            jax.ShapeDtypeStruct((b, s // FFT_RADIX, FFT_RADIX * FOURIER_WIDTH), BF16),
            jax.ShapeDtypeStruct((b, s, QK_WIDTH), BF16),
            jax.ShapeDtypeStruct((b, s, QK_WIDTH), BF16),
            jax.ShapeDtypeStruct((b, s, QK_WIDTH), BF16),
        ],
        scratch_shapes=[pltpu.VMEM((FOURIER_GROUPS, tm, FOURIER_GROUP_DIM), F32)],
        compiler_params=pltpu.CompilerParams(vmem_limit_bytes=VMEM_LIMIT_BYTES),
        name="premix",
    )(x, mod, gpre, win, gq, wq, gkv, wkv, rope)


def _ctxkv_kernel(x_ref, mod_ref, gpre_ref, win_ref, gkv_ref, wkv_ref, k_ref, v_ref):
    x = x_ref[0]
    shift = mod_ref[0, 0:1, :]
    scale = mod_ref[1, 0:1, :]
    h = _rms(x, gpre_ref[...] * (1.0 + scale)) + shift
    p = _dot(h.astype(BF16), win_ref[...])
    kvn = _rms(p[:, :KV_LORA_RANK], gkv_ref[...]).astype(BF16)
    kv = _dot(kvn, wkv_ref[...])
    kr = p[:, KV_LORA_RANK:]
    for hd in range(MLA_HEADS):
        lo = hd * HEAD_SLOT
        k_ref[0, :, lo:lo + HEAD_SLOT] = (kv[:, lo:lo + HEAD_SLOT] + kr).astype(BF16)
    _store_value_slots(v_ref, kv[:, QK_WIDTH:])


def _ctxkv(ctx, mod, gpre, win, gkv, wkv):
    b, c, d = ctx.shape
    kv_cols = win.shape[1] - KV_COL
    assert KV_COL % kv_cols == 0
    const = lambda shape: pl.BlockSpec(shape, lambda i: (0,) * len(shape))
    rows = lambda w: pl.BlockSpec((1, c, w), lambda i: (i, 0, 0))
    return pl.pallas_call(
        _ctxkv_kernel,
        grid=(b,),
        in_specs=[
            rows(d),
            pl.BlockSpec((mod.shape[0], 8, d), lambda i: (0, b // 8, 0)),
            const(gpre.shape),
            pl.BlockSpec((d, kv_cols), lambda i: (0, KV_COL // kv_cols)),
            const(gkv.shape), const(wkv.shape),
        ],
        out_specs=[rows(QK_WIDTH), rows(QK_WIDTH)],
        out_shape=[
            jax.ShapeDtypeStruct((b, c, QK_WIDTH), BF16),
            jax.ShapeDtypeStruct((b, c, QK_WIDTH), BF16),
        ],
        name="ctxkv",
    )(ctx, mod, gpre, win, gkv, wkv)


def _attn_kernel(q_ref, kc_ref, kl_ref, vc_ref, vl_ref, vcp_ref, vlp_ref, omain_ref, olast_ref,
                 p_ref, oprev_ref, ohead_ref):
    t = pl.program_id(0)
    n_blocks = pl.num_programs(0) - 1
    n_ctx = kc_ref.shape[1]
    n_lat = kl_ref.shape[1]
    tq = q_ref.shape[1]
    pair_w = 2 * V_HEAD_DIM
    first_half = lax.broadcasted_iota(jnp.int32, (tq, pair_w), 1) < V_HEAD_DIM
    last = MLA_HEADS - 1

    @pl.when(t == 0)
    def _():
        p_ref[...] = jnp.ones_like(p_ref)
        oprev_ref[...] = jnp.ones_like(oprev_ref)

    def join_pair(o_even, o_odd):
        num = jnp.where(first_half, o_even, o_odd)
        den = pltpu.roll(jnp.where(first_half, o_odd, o_even), V_HEAD_DIM, axis=1)
        return num / den

    def den_of(o, hd):
        lane = V_HEAD_DIM if hd % 2 == 0 else 0
        return o[:, lane:lane + 1]

    def drain():
        o = _dot(p_ref[:, :n_ctx], vcp_ref[0]) + _dot(p_ref[:, n_ctx:], vlp_ref[0])
        olast_ref[0] = join_pair(oprev_ref[...], o).astype(BF16)

    def head(hd, exact):
        lo = hd * HEAD_SLOT
        qh = q_ref[0, :, lo:lo + HEAD_SLOT]
        s_c = _dot_nt(qh, kc_ref[0, :, lo:lo + HEAD_SLOT])
        s_l = _dot_nt(qh, kl_ref[0, :, lo:lo + HEAD_SLOT])
        shift = jnp.max(s_c, axis=-1, keepdims=True)
        if exact:
            shift = jnp.maximum(shift, jnp.max(s_l, axis=-1, keepdims=True))
        if hd == last:
            p_c = jnp.exp2(s_c - shift)
            p_l = jnp.exp2(s_l - shift)
            p_ref[:, :n_ctx] = p_c.astype(BF16)
            p_ref[:, n_ctx:] = p_l.astype(BF16)
            return None, jnp.sum(p_c, axis=-1, keepdims=True) + jnp.sum(p_l, axis=-1, keepdims=True)
        o = _dot(jnp.exp2(s_c - shift).astype(BF16), vc_ref[0, :, lo:lo + HEAD_SLOT])
        for k0 in range(0, n_lat, PV_KEY_TILE):
            p = jnp.exp2(s_l[:, k0:k0 + PV_KEY_TILE] - shift).astype(BF16)
            o = o + _dot(p, vl_ref[0, k0:k0 + PV_KEY_TILE, lo:lo + HEAD_SLOT])
        return o, den_of(o, hd)

    def block():
        dens = []
        for pair in range(MLA_HEADS // 2):
            outs = []
            for hd in (2 * pair, 2 * pair + 1):
                o, den = head(hd, False)
                dens.append(den)
                if o is not None:
                    outs.append(o)
            if len(outs) == 2:
                vlo = pair * pair_w
                omain_ref[0, :, vlo:vlo + pair_w] = join_pair(outs[0], outs[1]).astype(BF16)
            else:
                oprev_ref[...] = outs[0]
        return dens

    def redo_block_exact():
        def one_head(hd, carry):
            lo = pl.multiple_of(hd * HEAD_SLOT, HEAD_SLOT)
            qh = q_ref[0, :, pl.ds(lo, HEAD_SLOT)]
            s_c = _dot_nt(qh, kc_ref[0, :, pl.ds(lo, HEAD_SLOT)])
            s_l = _dot_nt(qh, kl_ref[0, :, pl.ds(lo, HEAD_SLOT)])
            shift = jnp.maximum(jnp.max(s_c, axis=-1, keepdims=True), jnp.max(s_l, axis=-1, keepdims=True))
            ohead_ref[hd] = (_dot(jnp.exp2(s_c - shift).astype(BF16), vc_ref[0, :, pl.ds(lo, HEAD_SLOT)])
                             + _dot(jnp.exp2(s_l - shift).astype(BF16), vl_ref[0, :, pl.ds(lo, HEAD_SLOT)]))
            return carry

        lax.fori_loop(0, last, one_head, 0)
        for pair in range(MLA_HEADS // 2 - 1):
            vlo = pair * pair_w
            omain_ref[0, :, vlo:vlo + pair_w] = join_pair(ohead_ref[2 * pair], ohead_ref[2 * pair + 1]).astype(BF16)
        oprev_ref[...] = ohead_ref[last - 1]
        head(last, True)

    @pl.when(t < n_blocks)
    def _():
        drain()
        dens = block()
        lo_den, hi_den = dens[0], dens[0]
        for den in dens[1:]:
            lo_den = jnp.minimum(lo_den, den)
            hi_den = jnp.maximum(hi_den, den)
        trusted = jnp.logical_and(jnp.min(lo_den) >= SHIFT_DEN_MIN, jnp.max(hi_den) <= SHIFT_DEN_MAX)

        @pl.when(jnp.logical_not(trusted))
        def _():
            redo_block_exact()

    @pl.when(t == n_blocks)
    def _():
        drain()


def _attention(q, kc, kl, vc, vl, tq=512):
    b, s, _ = q.shape
    c = kc.shape[1]
    nq = s // tq
    n_blocks = b * nq
    pair_w = 2 * V_HEAD_DIM
    main_w = ATTN_WIDTH - pair_w
    last = MLA_HEADS - 1
    cur = lambda t: jnp.minimum(t, n_blocks - 1)
    prev = lambda t: jnp.maximum(t - 1, 0)
    return pl.pallas_call(
        _attn_kernel,
        grid=(n_blocks + 1,),
        in_specs=[
            pl.BlockSpec((1, tq, QK_WIDTH), lambda t: (cur(t) // nq, cur(t) % nq, 0)),
            pl.BlockSpec((1, c, QK_WIDTH), lambda t: (cur(t) // nq, 0, 0)),
            pl.BlockSpec((1, s, QK_WIDTH), lambda t: (cur(t) // nq, 0, 0)),
            pl.BlockSpec((1, c, QK_WIDTH), lambda t: (cur(t) // nq, 0, 0)),
            pl.BlockSpec((1, s, QK_WIDTH), lambda t: (cur(t) // nq, 0, 0)),
            pl.BlockSpec((1, c, HEAD_SLOT), lambda t: (prev(t) // nq, 0, last)),
            pl.BlockSpec((1, s, HEAD_SLOT), lambda t: (prev(t) // nq, 0, last)),
        ],
        out_specs=[
            pl.BlockSpec((1, tq, main_w), lambda t: (cur(t) // nq, cur(t) % nq, 0)),
            pl.BlockSpec((1, tq, pair_w), lambda t: (prev(t) // nq, prev(t) % nq, 0)),
        ],
        out_shape=[
            jax.ShapeDtypeStruct((b, s, main_w), BF16),
            jax.ShapeDtypeStruct((b, s, pair_w), BF16),
        ],
        scratch_shapes=[
            pltpu.VMEM((tq, c + s), BF16),
            pltpu.VMEM((tq, pair_w), F32),
            pltpu.VMEM((MLA_HEADS - 1, tq, HEAD_SLOT), F32),
        ],
        compiler_params=pltpu.CompilerParams(
            dimension_semantics=("arbitrary",), vmem_limit_bytes=VMEM_LIMIT_BYTES),
        name="attn",
    )(q, kc, kl, vc, vl, vc, vl)


def _fourier_kernel(t_ref, tw_ref, u_ref, cc_ref, wf_ref, o_ref, tb_ref, cw_ref):
    @pl.when(pl.program_id(0) == 0)
    def _():
        tb_ref[...] = t_ref[...].astype(BF16)
        cc = cc_ref[...].astype(BF16)
        for g in range(FOURIER_GROUPS):
            cw_ref[g] = _dot(cc, wf_ref[g].astype(BF16)).astype(BF16)

    n_sub = tb_ref.shape[1]
    f = _dot(tb_ref[...], u_ref[0])
    gd = FOURIER_GROUP_DIM
    for g in range(FOURIER_GROUPS):
        gr, gi = [], []
        for r in range(FFT_RADIX):
            lo = r * FOURIER_WIDTH + g * gd
            a = f[:n_sub, lo:lo + gd]
            b = f[n_sub:, lo:lo + gd]
            if r == 0:
                gr.append(a)
                gi.append(b)
            else:
                c = tw_ref[r - 1, 0]
                s = tw_ref[r - 1, 1]
                gr.append(a * c - b * s)
                gi.append(a * s + b * c)
        z = list(zip(gr, gi))
        add = lambda a, b: (a[0] + b[0], a[1] + b[1])
        sub = lambda a, b: (a[0] - b[0], a[1] - b[1])
        times_minus_i = lambda a: (-a[1], a[0])

        def dft4(c):
            e0, e1 = add(c[0], c[2]), sub(c[0], c[2])
            f0, f1 = add(c[1], c[3]), times_minus_i(sub(c[1], c[3]))
            return [add(e0, f0), add(e1, f1), sub(e0, f0), sub(e1, f1)]

        half = FFT_RADIX // 2
        even = dft4([add(z[r], z[r + half]) for r in range(half)])
        d = [sub(z[r], z[r + half]) for r in range(half)]
        rt = np.float32(np.sqrt(0.5))
        odd = dft4([
            d[0],
            ((d[1][0] - d[1][1]) * rt, (d[1][0] + d[1][1]) * rt),
            times_minus_i(d[2]),
            (-(d[3][0] + d[3][1]) * rt, (d[3][0] - d[3][1]) * rt),
        ])
        y = [even[q // 2] if q % 2 == 0 else odd[q // 2] for q in range(FFT_RADIX)]
        xr = jnp.concatenate([v[0] for v in y], axis=0)
        xi = jnp.concatenate([v[1] for v in y], axis=0)
        lhs = jnp.concatenate([xr, xi], axis=1).astype(BF16)
        o_ref[0, :, g * gd:(g + 1) * gd] = _dot(lhs, cw_ref[g]).astype(BF16)


def _fourier(u4, tmat, tw, cmat, wf):
    b, n_sub, _ = u4.shape
    s = n_sub * FFT_RADIX
    full = lambda a: pl.BlockSpec(a.shape, lambda i: (0,) * a.ndim)
    return pl.pallas_call(
        _fourier_kernel,
        grid=(b,),
        in_specs=[
            full(tmat), full(tw),
            pl.BlockSpec((1, n_sub, FFT_RADIX * FOURIER_WIDTH), lambda i: (i, 0, 0)),
            full(cmat), full(wf),
        ],
        out_specs=pl.BlockSpec((1, s, FOURIER_WIDTH), lambda i: (i, 0, 0)),
        out_shape=jax.ShapeDtypeStruct((b, s, FOURIER_WIDTH), BF16),
        scratch_shapes=[
            pltpu.VMEM(tmat.shape, BF16),
            pltpu.VMEM((FOURIER_GROUPS, 2 * FOURIER_GROUP_DIM, FOURIER_GROUP_DIM), BF16),
        ],
        compiler_params=pltpu.CompilerParams(
            dimension_semantics=("arbitrary",), vmem_limit_bytes=VMEM_LIMIT_BYTES),
        name="fourier",
    )(tmat, tw, u4, cmat, wf)


def _stage_cast(src_hbm, dst_ref, stage_ref, sem_ref, chunk):
    n_chunks = src_hbm.shape[0] // chunk

    def copy(i):
        slot = i % 2
        return pltpu.make_async_copy(
            src_hbm.at[pl.ds(i * chunk, chunk)], stage_ref.at[slot, pl.ds(0, chunk)], sem_ref.at[slot])

    def body(i, carry):
        @pl.when(i + 1 < n_chunks)
        def _():
            copy(i + 1).start()

        copy(i).wait()
        start = pl.multiple_of(i * chunk, chunk)
        dst_ref[pl.ds(start, chunk), :] = stage_ref[i % 2, pl.ds(0, chunk), :].astype(BF16)
        return carry

    copy(0).start()
    lax.fori_loop(0, n_chunks, body, 0)


def _post_kernel(x_ref, four_ref, am_ref, al_ref, mod_ref, gpm_ref, gpf_ref, gqf_ref,
                 wo_hbm, wg_hbm, wu_hbm, wd_hbm, o_ref,
                 x1_ref, h2_ref, wo_ref, wg_ref, wu_ref, wd_ref, wide_stage, tall_stage, sem_ref,
                 *, blocks_per_batch):
    t = pl.program_id(0)
    n_four = four_ref.shape[2]
    mix_row = pl.ds(jnp.minimum(t, pl.num_programs(0) - 2) // blocks_per_batch, 1)
    ffn_row = pl.ds(jnp.maximum(t - 1, 0) // blocks_per_batch, 1)

    def mix():
        gt_m = mod_ref[2, mix_row, :]
        sh_f = mod_ref[3, mix_row, :]
        sc_f = mod_ref[4, mix_row, :]
        attn = jnp.concatenate([am_ref[0], al_ref[0]], axis=1)
        y = _dot(four_ref[0], wo_ref[:n_four, :]) + _dot(attn, wo_ref[n_four:, :])
        x1 = x_ref[0] + _rms(y, gt_m * gpm_ref[...])
        x1_ref[...] = x1
        h2_ref[...] = (_rms(x1, gpf_ref[...] * (1.0 + sc_f)) + sh_f).astype(BF16)

    def ffn():
        gt_f = mod_ref[5, ffn_row, :]
        h2 = h2_ref[...]
        g = _dot(h2, wg_ref[...])
        up = _dot(h2, wu_ref[...])
        act = (g / (1.0 + jnp.exp(-g)) * up).astype(BF16)
        o_ref[0] = x1_ref[...] + _rms(_dot(act, wd_ref[...]), gt_f * gqf_ref[...])

    @pl.when(t == 0)
    def _():
        _stage_cast(wo_hbm, wo_ref, tall_stage, sem_ref, STAGE_ROWS_OUT)
        mix()
        _stage_cast(wg_hbm, wg_ref, wide_stage, sem_ref, STAGE_ROWS_WIDE)
        _stage_cast(wu_hbm, wu_ref, wide_stage, sem_ref, STAGE_ROWS_WIDE)
        _stage_cast(wd_hbm, wd_ref, tall_stage, sem_ref, STAGE_ROWS_TALL)

    @pl.when(t > 0)
    def _():
        ffn()
        mix()


def _post(x, four, attn_main, attn_last, mod, gpm, gpf, gqf, wo, wg, wu, wd, tm=512):
    b, s, d = x.shape
    d_ff = wg.shape[1]
    assert wo.shape[0] % STAGE_ROWS_OUT == 0 and STAGE_ROWS_OUT <= STAGE_ROWS_TALL
    assert d % STAGE_ROWS_WIDE == 0 and d_ff % STAGE_ROWS_TALL == 0
    nj = s // tm
    n_blocks = b * nj
    cur = lambda t: jnp.minimum(t, n_blocks - 1)
    prev = lambda t: jnp.maximum(t - 1, 0)
    const = lambda shape: pl.BlockSpec(shape, lambda t: (0,) * len(shape), pipeline_mode=pl.Buffered(1))
    rows = lambda w: pl.BlockSpec((1, tm, w), lambda t: (cur(t) // nj, cur(t) % nj, 0))
    hbm = pl.BlockSpec(memory_space=pl.ANY)
    return pl.pallas_call(
        functools.partial(_post_kernel, blocks_per_batch=nj),
        grid=(n_blocks + 1,),
        in_specs=[
            rows(d), rows(FOURIER_WIDTH), rows(attn_main.shape[2]), rows(attn_last.shape[2]),
            pl.BlockSpec(mod.shape, lambda t: (0, 0, 0)),
            const(gpm.shape), const(gpf.shape), const(gqf.shape),
            hbm, hbm, hbm, hbm,
        ],
        out_specs=pl.BlockSpec((1, tm, d), lambda t: (prev(t) // nj, prev(t) % nj, 0)),
        out_shape=jax.ShapeDtypeStruct((b, s, d), F32),
        scratch_shapes=[
            pltpu.VMEM((tm, d), F32), pltpu.VMEM((tm, d), BF16),
            pltpu.VMEM(wo.shape, BF16), pltpu.VMEM(wg.shape, BF16), pltpu.VMEM(wu.shape, BF16),
            pltpu.VMEM(wd.shape, BF16),
            pltpu.VMEM((2, STAGE_ROWS_WIDE, d_ff), F32), pltpu.VMEM((2, STAGE_ROWS_TALL, d), F32),
            pltpu.SemaphoreType.DMA((2,)),
        ],
        compiler_params=pltpu.CompilerParams(
            dimension_semantics=("arbitrary",), vmem_limit_bytes=VMEM_LIMIT_BYTES),
        name="post",
    )(x, four, attn_main, attn_last, mod, gpm, gpf, gqf, wo, wg, wu, wd)


def _rope_rotate_cols(w):
    a = QK_ROPE_DIM // 2
    hf = a // 2
    blocks = []
    for s0 in (0, a):
        blocks += [-w[..., s0 + hf:s0 + a], w[..., s0:s0 + hf]]
    return jnp.concatenate(blocks, axis=-1)


def _head_slot(nope, rope):
    pad = HEAD_SLOT - QK_NOPE_DIM - QK_ROPE_DIM
    return jnp.concatenate([nope, rope, jnp.zeros(rope.shape[:-1] + (pad,), rope.dtype)], axis=-1)


def _rope_tables(n_lat, q_scale):
    t = np.arange(n_lat)
    hf = QK_ROPE_DIM // 4
    inv_freq = ROPE_BASE ** (-np.arange(hf, dtype=np.float64) / hf)
    ar = (t // GRID_W)[:, None] * inv_freq[None, :]
    ac = (t % GRID_W)[:, None] * inv_freq[None, :]
    z = np.zeros_like(ar)
    cos32 = np.concatenate([np.cos(ar), np.cos(ar), np.cos(ac), np.cos(ac)], axis=-1)
    sin32 = np.concatenate([np.sin(ar), np.sin(ar), np.sin(ac), np.sin(ac)], axis=-1)
    below32 = np.concatenate([z, np.sin(ar), z, np.sin(ac)], axis=-1)
    above32 = np.concatenate([-np.sin(ar), z, -np.sin(ac), z], axis=-1)
    pad = np.zeros((n_lat, HEAD_SLOT - QK_NOPE_DIM - QK_ROPE_DIM))
    ones = np.ones((n_lat, QK_NOPE_DIM))
    zeros = np.zeros((n_lat, QK_NOPE_DIM))
    slot = lambda nope, rope: np.concatenate([nope, rope, pad], axis=-1)
    tables = [slot(ones * q_scale, cos32 * q_scale), slot(zeros, sin32 * q_scale),
              slot(zeros, cos32), slot(zeros, below32), slot(zeros, above32)]
    return jnp.asarray(np.stack(tables).astype(np.float32))


def _dft_tables(n_pos, n_ch):
    n_sub = n_pos // FFT_RADIX
    m = np.arange(n_sub, dtype=np.int64)
    ang = 2.0 * np.pi * ((m[:, None] * m[None, :]) % n_sub) / n_sub
    tmat = np.concatenate([np.cos(ang), np.sin(ang)], axis=0).astype(np.float32)
    tw = np.zeros((FFT_RADIX - 1, 2, n_sub, FOURIER_GROUP_DIM), np.float32)
    for r in range(1, FFT_RADIX):
        a = 2.0 * np.pi * r * m / n_pos
        tw[r - 1, 0] = np.cos(a)[:, None]
        tw[r - 1, 1] = np.sin(a)[:, None]
    c = np.arange(n_ch, dtype=np.int64)
    angc = 2.0 * np.pi * ((c[:, None] * c[None, :]) % n_ch) / n_ch
    norm = 1.0 / np.sqrt(float(n_pos * n_ch))
    cmat = np.concatenate([np.cos(angc) * norm, -np.sin(angc) * norm], axis=0).astype(np.float32)
    return jnp.asarray(tmat), jnp.asarray(tw), jnp.asarray(cmat)


def kernel(x, c, ctx, c_ctx, w_ada, b_ada, g_pre_mix, g_post_mix, g_pre_ffn, g_post_ffn, w_in, g_q_a,
           w_q_b, g_kv_a, w_kv_b, w_fourier, w_out, w_gate, w_up, w_down):
    assert w_ada.shape[0] == 1, "single-layer block"
    batch, n_lat, d = x.shape

    mod = _adaln(c, c_ctx[None, :], w_ada[0], b_ada[0][None, :])

    w_in0 = w_in[0]
    w_kr = w_in0[:, ROPE_COL:]
    zeros_d = jnp.zeros((d, QK_NOPE_DIM), F32)
    kr_slot = _head_slot(zeros_d, w_kr)
    win = jnp.concatenate([w_in0[:, :ROPE_COL], kr_slot], axis=1).astype(BF16)

    wq3 = w_q_b[0].reshape(Q_LORA_RANK, MLA_HEADS, QK_NOPE_DIM + QK_ROPE_DIM)
    wq_nope, wq_rope = wq3[..., :QK_NOPE_DIM], wq3[..., QK_NOPE_DIM:]
    wq_a = _head_slot(wq_nope, wq_rope).reshape(Q_LORA_RANK, QK_WIDTH)
    wq_b = _head_slot(jnp.zeros_like(wq_nope), _rope_rotate_cols(wq_rope)).reshape(Q_LORA_RANK, QK_WIDTH)
    wq = jnp.concatenate([wq_a, wq_b], axis=1).astype(BF16)

    wkv3 = w_kv_b[0].reshape(KV_LORA_RANK, MLA_HEADS, QK_NOPE_DIM + V_HEAD_DIM)
    wk_nope, wv = wkv3[..., :QK_NOPE_DIM], wkv3[..., QK_NOPE_DIM:]
    wk_slots = _head_slot(wk_nope, jnp.zeros(wk_nope.shape[:-1] + (QK_ROPE_DIM,), F32))
    wkv = jnp.concatenate([wk_slots.reshape(KV_LORA_RANK, QK_WIDTH),
                           wv.reshape(KV_LORA_RANK, ATTN_WIDTH)], axis=1).astype(BF16)

    q_scale = float((QK_NOPE_DIM + QK_ROPE_DIM) ** -0.5 * np.log2(np.e))
    rope = _rope_tables(n_lat, q_scale)
    tmat, tw, cmat = _dft_tables(n_lat, FOURIER_GROUP_DIM)

    row2 = lambda g: g[0][None, :]
    u_f, q, k_lat, v_lat = _premix(x, mod, row2(g_pre_mix), win, row2(g_q_a), wq, row2(g_kv_a), wkv,
                                   rope)
    four = _fourier(u_f, tmat, tw, cmat, w_fourier[0])
    k_ctx, v_ctx = _ctxkv(ctx, mod, row2(g_pre_mix), win, row2(g_kv_a), wkv)
    attn_main, attn_last = _attention(q, k_ctx, k_lat, v_ctx, v_lat)
    return _post(x, four, attn_main, attn_last, mod, row2(g_post_mix), row2(g_pre_ffn), row2(g_post_ffn),
                 w_out[0], w_gate[0], w_up[0], w_down[0])
```

```python
import functools

import numpy as np
import jax
import jax.numpy as jnp
from jax import lax
from jax.experimental import pallas as pl
from jax.experimental.pallas import tpu as pltpu

F32 = jnp.float32
BF16 = jnp.bfloat16

D_MODEL = 1024
GRID_W = 64
FOURIER_GROUPS = 4
FOURIER_GROUP_DIM = 128
FOURIER_WIDTH = FOURIER_GROUPS * FOURIER_GROUP_DIM
MLA_HEADS = 8
QK_NOPE_DIM = 64
QK_ROPE_DIM = 32
V_HEAD_DIM = 64
Q_LORA_RANK = 256
KV_LORA_RANK = 128
KV_COL = FOURIER_WIDTH + Q_LORA_RANK
ROPE_COL = KV_COL + KV_LORA_RANK
ROPE_BASE = 10000.0
NORM_EPS = 1e-6
FFT_RADIX = 8
HEAD_SLOT = 128
STAGE_ROWS_WIDE = 256
STAGE_ROWS_TALL = 704
STAGE_ROWS_OUT = 512
SHIFT_DEN_MAX = 2.0 ** 40
PV_KEY_TILE = 256
ATTN_WIDTH = MLA_HEADS * V_HEAD_DIM
QK_WIDTH = MLA_HEADS * HEAD_SLOT

VMEM_LIMIT_BYTES = 56 * 1024 * 1024


def _rms(x, g):
    return x * lax.rsqrt(jnp.mean(x * x, axis=-1, keepdims=True) + NORM_EPS) * g


def _dot(a, b):
    return jnp.dot(a, b, preferred_element_type=F32)


def _dot_nt(a, b):
    return lax.dot_general(a, b, (((1,), (1,)), ((), ())), preferred_element_type=F32)


def _rotary(a, table, first):
    half = QK_ROPE_DIM // 4
    below = pltpu.roll(a, half, axis=1)
    above = pltpu.roll(a, HEAD_SLOT - half, axis=1)
    return a * table(first) + below * table(first + 1) + above * table(first + 2)


def _store_value_slots(v_ref, v):
    pair_w = 2 * V_HEAD_DIM
    lower = lax.broadcasted_iota(jnp.int32, (v.shape[0], pair_w), 1) < V_HEAD_DIM
    for pair in range(MLA_HEADS // 2):
        vp = v[:, pair * pair_w:(pair + 1) * pair_w]
        lo = 2 * pair * HEAD_SLOT
        v_ref[0, :, lo:lo + HEAD_SLOT] = jnp.where(lower, vp, 1.0).astype(BF16)
        v_ref[0, :, lo + HEAD_SLOT:lo + 2 * HEAD_SLOT] = jnp.where(lower, 1.0, vp).astype(BF16)


def _adaln_kernel(c_ref, cctx_ref, w_ref, b_ref, o_ref, rows_ref):
    n = c_ref.shape[0]
    rows_ref[:n, :] = c_ref[...]
    rows_ref[n:, :] = jnp.broadcast_to(cctx_ref[...], (rows_ref.shape[0] - n, rows_ref.shape[1]))
    c = rows_ref[...]
    a = c / (1.0 + jnp.exp(-c))
    o_ref[0] = _dot(a.astype(BF16), w_ref[...].astype(BF16)) + b_ref[...]


def _adaln(c, c_ctx, w_ada, b_ada):
    batch, d = c.shape
    assert batch % 8 == 0, "the context rows start on a sublane boundary"
    rows = batch + 8
    n_chunks = w_ada.shape[1] // d
    return pl.pallas_call(
        _adaln_kernel,
        grid=(n_chunks,),
        in_specs=[
            pl.BlockSpec((batch, d), lambda j: (0, 0)),
            pl.BlockSpec((1, d), lambda j: (0, 0)),
            pl.BlockSpec((d, d), lambda j: (0, j)),
            pl.BlockSpec((1, d), lambda j: (0, j)),
        ],
        out_specs=pl.BlockSpec((1, rows, d), lambda j: (j, 0, 0)),
        out_shape=jax.ShapeDtypeStruct((n_chunks, rows, d), F32),
        scratch_shapes=[pltpu.VMEM((rows, d), F32)],
        compiler_params=pltpu.CompilerParams(vmem_limit_bytes=VMEM_LIMIT_BYTES),
        name="adaln",
    )(c, c_ctx, w_ada, b_ada)


def _premix_kernel(x_ref, mod_ref, gpre_ref, win_ref, gq_ref, wq_ref, gkv_ref, wkv_ref,
                   rope_ref, u_ref, q_ref, k_ref, v_ref, us_ref):
    x = x_ref[0]
    row = pl.ds(pl.program_id(0), 1)
    shift = mod_ref[0, row, :]
    scale = mod_ref[1, row, :]
    h = _rms(x, gpre_ref[...] * (1.0 + scale)) + shift
    p = _dot(h.astype(BF16), win_ref[...])
    sub = us_ref.shape[1] // FFT_RADIX
    for g in range(FOURIER_GROUPS):
        glo = g * FOURIER_GROUP_DIM
        us_ref[g] = p[:, glo:glo + FOURIER_GROUP_DIM]
        for r in range(FFT_RADIX):
            lo = r * FOURIER_WIDTH + glo
            u_ref[0, :, lo:lo + FOURIER_GROUP_DIM] = (
                us_ref[g, pl.ds(r, sub, stride=FFT_RADIX), :].astype(BF16))

    qn = _rms(p[:, FOURIER_WIDTH:KV_COL], gq_ref[...]).astype(BF16)
    qq = _dot(qn, wq_ref[...])
    rows = x_ref.shape[1]
    row0 = pl.multiple_of(pl.program_id(1) * rows, rows)
    table = lambda k: rope_ref[k, pl.ds(row0, rows), :]
    cosq = table(0)
    sinq = table(1)
    for hd in range(MLA_HEADS):
        lo = hd * HEAD_SLOT
        q_ref[0, :, lo:lo + HEAD_SLOT] = (
            qq[:, lo:lo + HEAD_SLOT] * cosq + qq[:, QK_WIDTH + lo:QK_WIDTH + lo + HEAD_SLOT] * sinq
        ).astype(BF16)

    kvn = _rms(p[:, KV_COL:ROPE_COL], gkv_ref[...]).astype(BF16)
    kv = _dot(kvn, wkv_ref[...])
    kr = _rotary(p[:, ROPE_COL:ROPE_COL + HEAD_SLOT], table, 2)
    for hd in range(MLA_HEADS):
        lo = hd * HEAD_SLOT
        k_ref[0, :, lo:lo + HEAD_SLOT] = (kv[:, lo:lo + HEAD_SLOT] + kr).astype(BF16)
    _store_value_slots(v_ref, kv[:, QK_WIDTH:])


def _premix(x, mod, gpre, win, gq, wq, gkv, wkv, rope, tm=1024):
    b, s, d = x.shape
    const = lambda shape: pl.BlockSpec(shape, lambda i, j: (0,) * len(shape))
    rows = lambda w: pl.BlockSpec((1, tm, w), lambda i, j: (i, j, 0))
    return pl.pallas_call(
        _premix_kernel,
        grid=(b, s // tm),
        in_specs=[
            rows(d),
            pl.BlockSpec(mod.shape, lambda i, j: (0, 0, 0)),
            const(gpre.shape), const(win.shape), const(gq.shape), const(wq.shape),
            const(gkv.shape), const(wkv.shape),
            pl.BlockSpec(rope.shape, lambda i, j: (0, 0, 0), pipeline_mode=pl.Buffered(1)),
        ],
        out_specs=[
            pl.BlockSpec((1, tm // FFT_RADIX, FFT_RADIX * FOURIER_WIDTH), lambda i, j: (i, j, 0)),
            rows(QK_WIDTH), rows(QK_WIDTH), rows(QK_WIDTH)],
        out_shape=[
            jax.ShapeDtypeStruct((b, s // FFT_RADIX, FFT_RADIX * FOURIER_WIDTH), BF16),
            jax.ShapeDtypeStruct((b, s, QK_WIDTH), BF16),
            jax.ShapeDtypeStruct((b, s, QK_WIDTH), BF16),
            jax.ShapeDtypeStruct((b, s, QK_WIDTH), BF16),
        ],
        scratch_shapes=[pltpu.VMEM((FOURIER_GROUPS, tm, FOURIER_GROUP_DIM), F32)],
        compiler_params=pltpu.CompilerParams(vmem_limit_bytes=VMEM_LIMIT_BYTES),
        name="premix",
    )(x, mod, gpre, win, gq, wq, gkv, wkv, rope)


def _ctxkv_kernel(x_ref, mod_ref, gpre_ref, win_ref, gkv_ref, wkv_ref, k_ref, v_ref):
    x = x_ref[0]
    shift = mod_ref[0, 0:1, :]
    scale = mod_ref[1, 0:1, :]
    h = _rms(x, gpre_ref[...] * (1.0 + scale)) + shift
    p = _dot(h.astype(BF16), win_ref[...])
    kvn = _rms(p[:, :KV_LORA_RANK], gkv_ref[...]).astype(BF16)
    kv = _dot(kvn, wkv_ref[...])
    kr = p[:, KV_LORA_RANK:]
    for hd in range(MLA_HEADS):
        lo = hd * HEAD_SLOT
        k_ref[0, :, lo:lo + HEAD_SLOT] = (kv[:, lo:lo + HEAD_SLOT] + kr).astype(BF16)
    _store_value_slots(v_ref, kv[:, QK_WIDTH:])


def _ctxkv(ctx, mod, gpre, win, gkv, wkv):
    b, c, d = ctx.shape
    kv_cols = win.shape[1] - KV_COL
    assert KV_COL % kv_cols == 0
    const = lambda shape: pl.BlockSpec(shape, lambda i: (0,) * len(shape))
    rows = lambda w: pl.BlockSpec((1, c, w), lambda i: (i, 0, 0))
    return pl.pallas_call(
        _ctxkv_kernel,
        grid=(b,),
        in_specs=[
            rows(d),
            pl.BlockSpec((mod.shape[0], 8, d), lambda i: (0, b // 8, 0)),
            const(gpre.shape),
            pl.BlockSpec((d, kv_cols), lambda i: (0, KV_COL // kv_cols)),
            const(gkv.shape), const(wkv.shape),
        ],
        out_specs=[rows(QK_WIDTH), rows(QK_WIDTH)],
        out_shape=[
            jax.ShapeDtypeStruct((b, c, QK_WIDTH), BF16),
            jax.ShapeDtypeStruct((b, c, QK_WIDTH), BF16),
        ],
        name="ctxkv",
    )(ctx, mod, gpre, win, gkv, wkv)


def _attn_kernel(q_ref, kc_ref, kl_ref, vc_ref, vl_ref, vcp_ref, vlp_ref, omain_ref, olast_ref,
                 p_ref, oprev_ref, ohead_ref):
    t = pl.program_id(0)
    n_blocks = pl.num_programs(0) - 1
    n_ctx = kc_ref.shape[1]
    n_lat = kl_ref.shape[1]
    tq = q_ref.shape[1]
    pair_w = 2 * V_HEAD_DIM
    first_half = lax.broadcasted_iota(jnp.int32, (tq, pair_w), 1) < V_HEAD_DIM
    last = MLA_HEADS - 1

    @pl.when(t == 0)
    def _():
        p_ref[...] = jnp.ones_like(p_ref)
        oprev_ref[...] = jnp.ones_like(oprev_ref)

    def join_pair(o_even, o_odd):
        num = jnp.where(first_half, o_even, o_odd)
        den = pltpu.roll(jnp.where(first_half, o_odd, o_even), V_HEAD_DIM, axis=1)
        return num / den

    def den_of(o, hd):
        lane = V_HEAD_DIM if hd % 2 == 0 else 0
        return o[:, lane:lane + 1]

    def drain():
        o = _dot(p_ref[:, :n_ctx], vcp_ref[0]) + _dot(p_ref[:, n_ctx:], vlp_ref[0])
        olast_ref[0] = join_pair(oprev_ref[...], o).astype(BF16)

    def head(hd, exact):
        lo = hd * HEAD_SLOT
        qh = q_ref[0, :, lo:lo + HEAD_SLOT]
        s_c = _dot_nt(qh, kc_ref[0, :, lo:lo + HEAD_SLOT])
        s_l = _dot_nt(qh, kl_ref[0, :, lo:lo + HEAD_SLOT])
        shift = jnp.max(s_c, axis=-1, keepdims=True)
        if exact:
            shift = jnp.maximum(shift, jnp.max(s_l, axis=-1, keepdims=True))
        if hd == last:
            p_c = jnp.exp2(s_c - shift)
            p_l = jnp.exp2(s_l - shift)
            p_ref[:, :n_ctx] = p_c.astype(BF16)
            p_ref[:, n_ctx:] = p_l.astype(BF16)
            return None, jnp.sum(p_c, axis=-1, keepdims=True) + jnp.sum(p_l, axis=-1, keepdims=True)
        o = _dot(jnp.exp2(s_c - shift).astype(BF16), vc_ref[0, :, lo:lo + HEAD_SLOT])
        for k0 in range(0, n_lat, PV_KEY_TILE):
            p = jnp.exp2(s_l[:, k0:k0 + PV_KEY_TILE] - shift).astype(BF16)
            o = o + _dot(p, vl_ref[0, k0:k0 + PV_KEY_TILE, lo:lo + HEAD_SLOT])
        return o, den_of(o, hd)

    def block():
        dens = []
        for pair in range(MLA_HEADS // 2):
            outs = []
            for hd in (2 * pair, 2 * pair + 1):
                o, den = head(hd, False)
                dens.append(den)
                if o is not None:
                    outs.append(o)
            if len(outs) == 2:
                vlo = pair * pair_w
                omain_ref[0, :, vlo:vlo + pair_w] = join_pair(outs[0], outs[1]).astype(BF16)
            else:
                oprev_ref[...] = outs[0]
        return dens

    def redo_block_exact():
        def one_head(hd, carry):
            lo = pl.multiple_of(hd * HEAD_SLOT, HEAD_SLOT)
            qh = q_ref[0, :, pl.ds(lo, HEAD_SLOT)]
            s_c = _dot_nt(qh, kc_ref[0, :, pl.ds(lo, HEAD_SLOT)])
            s_l = _dot_nt(qh, kl_ref[0, :, pl.ds(lo, HEAD_SLOT)])
            shift = jnp.maximum(jnp.max(s_c, axis=-1, keepdims=True), jnp.max(s_l, axis=-1, keepdims=True))
            ohead_ref[hd] = (_dot(jnp.exp2(s_c - shift).astype(BF16), vc_ref[0, :, pl.ds(lo, HEAD_SLOT)])
                             + _dot(jnp.exp2(s_l - shift).astype(BF16), vl_ref[0, :, pl.ds(lo, HEAD_SLOT)]))
            return carry

        lax.fori_loop(0, last, one_head, 0)
        for pair in range(MLA_HEADS // 2 - 1):
            vlo = pair * pair_w
            omain_ref[0, :, vlo:vlo + pair_w] = join_pair(ohead_ref[2 * pair], ohead_ref[2 * pair + 1]).astype(BF16)
        oprev_ref[...] = ohead_ref[last - 1]
        head(last, True)

    @pl.when(t < n_blocks)
    def _():
        drain()
        dens = block()
        worst = dens[0]
        for den in dens[1:]:
            worst = jnp.maximum(worst, den)
        trusted = jnp.max(worst) <= SHIFT_DEN_MAX

        @pl.when(jnp.logical_not(trusted))
        def _():
            redo_block_exact()

    @pl.when(t == n_blocks)
    def _():
        drain()


def _attention(q, kc, kl, vc, vl, tq=512):
    b, s, _ = q.shape
    c = kc.shape[1]
    nq = s // tq
    n_blocks = b * nq
    pair_w = 2 * V_HEAD_DIM
    main_w = ATTN_WIDTH - pair_w
    last = MLA_HEADS - 1
    cur = lambda t: jnp.minimum(t, n_blocks - 1)
    prev = lambda t: jnp.maximum(t - 1, 0)
    return pl.pallas_call(
        _attn_kernel,
        grid=(n_blocks + 1,),
        in_specs=[
            pl.BlockSpec((1, tq, QK_WIDTH), lambda t: (cur(t) // nq, cur(t) % nq, 0)),
            pl.BlockSpec((1, c, QK_WIDTH), lambda t: (cur(t) // nq, 0, 0)),
            pl.BlockSpec((1, s, QK_WIDTH), lambda t: (cur(t) // nq, 0, 0)),
            pl.BlockSpec((1, c, QK_WIDTH), lambda t: (cur(t) // nq, 0, 0)),
            pl.BlockSpec((1, s, QK_WIDTH), lambda t: (cur(t) // nq, 0, 0)),
            pl.BlockSpec((1, c, HEAD_SLOT), lambda t: (prev(t) // nq, 0, last)),
            pl.BlockSpec((1, s, HEAD_SLOT), lambda t: (prev(t) // nq, 0, last)),
        ],
        out_specs=[
            pl.BlockSpec((1, tq, main_w), lambda t: (cur(t) // nq, cur(t) % nq, 0)),
            pl.BlockSpec((1, tq, pair_w), lambda t: (prev(t) // nq, prev(t) % nq, 0)),
        ],
        out_shape=[
            jax.ShapeDtypeStruct((b, s, main_w), BF16),
            jax.ShapeDtypeStruct((b, s, pair_w), BF16),
        ],
        scratch_shapes=[
            pltpu.VMEM((tq, c + s), BF16),
            pltpu.VMEM((tq, pair_w), F32),
            pltpu.VMEM((MLA_HEADS - 1, tq, HEAD_SLOT), F32),
        ],
        compiler_params=pltpu.CompilerParams(
            dimension_semantics=("arbitrary",), vmem_limit_bytes=VMEM_LIMIT_BYTES),
        name="attn",
    )(q, kc, kl, vc, vl, vc, vl)


def _fourier_kernel(t_ref, tw_ref, u_ref, cc_ref, wf_ref, o_ref, tb_ref, cw_ref):
    @pl.when(pl.program_id(0) == 0)
    def _():
        tb_ref[...] = t_ref[...].astype(BF16)
        cc = cc_ref[...].astype(BF16)
        for g in range(FOURIER_GROUPS):
            cw_ref[g] = _dot(cc, wf_ref[g].astype(BF16)).astype(BF16)

    n_sub = tb_ref.shape[1]
    f = _dot(tb_ref[...], u_ref[0])
    gd = FOURIER_GROUP_DIM
    for g in range(FOURIER_GROUPS):
        gr, gi = [], []
        for r in range(FFT_RADIX):
            lo = r * FOURIER_WIDTH + g * gd
            a = f[:n_sub, lo:lo + gd]
            b = f[n_sub:, lo:lo + gd]
            if r == 0:
                gr.append(a)
                gi.append(b)
            else:
                c = tw_ref[r - 1, 0]
                s = tw_ref[r - 1, 1]
                gr.append(a * c - b * s)
                gi.append(a * s + b * c)
        z = list(zip(gr, gi))
        add = lambda a, b: (a[0] + b[0], a[1] + b[1])
        sub = lambda a, b: (a[0] - b[0], a[1] - b[1])
        times_minus_i = lambda a: (-a[1], a[0])

        def dft4(c):
            e0, e1 = add(c[0], c[2]), sub(c[0], c[2])
            f0, f1 = add(c[1], c[3]), times_minus_i(sub(c[1], c[3]))
            return [add(e0, f0), add(e1, f1), sub(e0, f0), sub(e1, f1)]

        half = FFT_RADIX // 2
        even = dft4([add(z[r], z[r + half]) for r in range(half)])
        d = [sub(z[r], z[r + half]) for r in range(half)]
        rt = np.float32(np.sqrt(0.5))
        odd = dft4([
            d[0],
            ((d[1][0] - d[1][1]) * rt, (d[1][0] + d[1][1]) * rt),
            times_minus_i(d[2]),
            (-(d[3][0] + d[3][1]) * rt, (d[3][0] - d[3][1]) * rt),
        ])
        y = [even[q // 2] if q % 2 == 0 else odd[q // 2] for q in range(FFT_RADIX)]
        xr = jnp.concatenate([v[0] for v in y], axis=0)
        xi = jnp.concatenate([v[1] for v in y], axis=0)
        lhs = jnp.concatenate([xr, xi], axis=1).astype(BF16)
        o_ref[0, :, g * gd:(g + 1) * gd] = _dot(lhs, cw_ref[g]).astype(BF16)


def _fourier(u4, tmat, tw, cmat, wf):
    b, n_sub, _ = u4.shape
    s = n_sub * FFT_RADIX
    full = lambda a: pl.BlockSpec(a.shape, lambda i: (0,) * a.ndim)
    return pl.pallas_call(
        _fourier_kernel,
        grid=(b,),
        in_specs=[
            full(tmat), full(tw),
            pl.BlockSpec((1, n_sub, FFT_RADIX * FOURIER_WIDTH), lambda i: (i, 0, 0)),
            full(cmat), full(wf),
        ],
        out_specs=pl.BlockSpec((1, s, FOURIER_WIDTH), lambda i: (i, 0, 0)),
        out_shape=jax.ShapeDtypeStruct((b, s, FOURIER_WIDTH), BF16),
        scratch_shapes=[
            pltpu.VMEM(tmat.shape, BF16),
            pltpu.VMEM((FOURIER_GROUPS, 2 * FOURIER_GROUP_DIM, FOURIER_GROUP_DIM), BF16),
        ],
        compiler_params=pltpu.CompilerParams(
            dimension_semantics=("arbitrary",), vmem_limit_bytes=VMEM_LIMIT_BYTES),
        name="fourier",
    )(tmat, tw, u4, cmat, wf)


def _stage_cast(src_hbm, dst_ref, stage_ref, sem_ref, chunk):
    n_chunks = src_hbm.shape[0] // chunk

    def copy(i):
        slot = i % 2
        return pltpu.make_async_copy(
            src_hbm.at[pl.ds(i * chunk, chunk)], stage_ref.at[slot, pl.ds(0, chunk)], sem_ref.at[slot])

    def body(i, carry):
        @pl.when(i + 1 < n_chunks)
        def _():
            copy(i + 1).start()

        copy(i).wait()
        start = pl.multiple_of(i * chunk, chunk)
        dst_ref[pl.ds(start, chunk), :] = stage_ref[i % 2, pl.ds(0, chunk), :].astype(BF16)
        return carry

    copy(0).start()
    lax.fori_loop(0, n_chunks, body, 0)


def _post_kernel(x_ref, four_ref, am_ref, al_ref, mod_ref, gpm_ref, gpf_ref, gqf_ref,
                 wo_hbm, wg_hbm, wu_hbm, wd_hbm, o_ref,
                 x1_ref, h2_ref, wo_ref, wg_ref, wu_ref, wd_ref, wide_stage, tall_stage, sem_ref,
                 *, blocks_per_batch):
    t = pl.program_id(0)
    n_four = four_ref.shape[2]
    mix_row = pl.ds(jnp.minimum(t, pl.num_programs(0) - 2) // blocks_per_batch, 1)
    ffn_row = pl.ds(jnp.maximum(t - 1, 0) // blocks_per_batch, 1)

    def mix():
        gt_m = mod_ref[2, mix_row, :]
        sh_f = mod_ref[3, mix_row, :]
        sc_f = mod_ref[4, mix_row, :]
        attn = jnp.concatenate([am_ref[0], al_ref[0]], axis=1)
        y = _dot(four_ref[0], wo_ref[:n_four, :]) + _dot(attn, wo_ref[n_four:, :])
        x1 = x_ref[0] + _rms(y, gt_m * gpm_ref[...])
        x1_ref[...] = x1
        h2_ref[...] = (_rms(x1, gpf_ref[...] * (1.0 + sc_f)) + sh_f).astype(BF16)

    def ffn():
        gt_f = mod_ref[5, ffn_row, :]
        h2 = h2_ref[...]
        g = _dot(h2, wg_ref[...])
        up = _dot(h2, wu_ref[...])
        act = (g / (1.0 + jnp.exp(-g)) * up).astype(BF16)
        o_ref[0] = x1_ref[...] + _rms(_dot(act, wd_ref[...]), gt_f * gqf_ref[...])

    @pl.when(t == 0)
    def _():
        _stage_cast(wo_hbm, wo_ref, tall_stage, sem_ref, STAGE_ROWS_OUT)
        mix()
        _stage_cast(wg_hbm, wg_ref, wide_stage, sem_ref, STAGE_ROWS_WIDE)
        _stage_cast(wu_hbm, wu_ref, wide_stage, sem_ref, STAGE_ROWS_WIDE)
        _stage_cast(wd_hbm, wd_ref, tall_stage, sem_ref, STAGE_ROWS_TALL)

    @pl.when(t > 0)
    def _():
        ffn()
        mix()


def _post(x, four, attn_main, attn_last, mod, gpm, gpf, gqf, wo, wg, wu, wd, tm=512):
    b, s, d = x.shape
    d_ff = wg.shape[1]
    assert wo.shape[0] % STAGE_ROWS_OUT == 0 and STAGE_ROWS_OUT <= STAGE_ROWS_TALL
    assert d % STAGE_ROWS_WIDE == 0 and d_ff % STAGE_ROWS_TALL == 0
    nj = s // tm
    n_blocks = b * nj
    cur = lambda t: jnp.minimum(t, n_blocks - 1)
    prev = lambda t: jnp.maximum(t - 1, 0)
    const = lambda shape: pl.BlockSpec(shape, lambda t: (0,) * len(shape), pipeline_mode=pl.Buffered(1))
    rows = lambda w: pl.BlockSpec((1, tm, w), lambda t: (cur(t) // nj, cur(t) % nj, 0))
    hbm = pl.BlockSpec(memory_space=pl.ANY)
    return pl.pallas_call(
        functools.partial(_post_kernel, blocks_per_batch=nj),
        grid=(n_blocks + 1,),
        in_specs=[
            rows(d), rows(FOURIER_WIDTH), rows(attn_main.shape[2]), rows(attn_last.shape[2]),
            pl.BlockSpec(mod.shape, lambda t: (0, 0, 0)),
            const(gpm.shape), const(gpf.shape), const(gqf.shape),
            hbm, hbm, hbm, hbm,
        ],
        out_specs=pl.BlockSpec((1, tm, d), lambda t: (prev(t) // nj, prev(t) % nj, 0)),
        out_shape=jax.ShapeDtypeStruct((b, s, d), F32),
        scratch_shapes=[
            pltpu.VMEM((tm, d), F32), pltpu.VMEM((tm, d), BF16),
            pltpu.VMEM(wo.shape, BF16), pltpu.VMEM(wg.shape, BF16), pltpu.VMEM(wu.shape, BF16),
            pltpu.VMEM(wd.shape, BF16),
            pltpu.VMEM((2, STAGE_ROWS_WIDE, d_ff), F32), pltpu.VMEM((2, STAGE_ROWS_TALL, d), F32),
            pltpu.SemaphoreType.DMA((2,)),
        ],
        compiler_params=pltpu.CompilerParams(
            dimension_semantics=("arbitrary",), vmem_limit_bytes=VMEM_LIMIT_BYTES),
        name="post",
    )(x, four, attn_main, attn_last, mod, gpm, gpf, gqf, wo, wg, wu, wd)


def _rope_rotate_cols(w):
    a = QK_ROPE_DIM // 2
    hf = a // 2
    blocks = []
    for s0 in (0, a):
        blocks += [-w[..., s0 + hf:s0 + a], w[..., s0:s0 + hf]]
    return jnp.concatenate(blocks, axis=-1)


def _head_slot(nope, rope):
    pad = HEAD_SLOT - QK_NOPE_DIM - QK_ROPE_DIM
    return jnp.concatenate([nope, rope, jnp.zeros(rope.shape[:-1] + (pad,), rope.dtype)], axis=-1)


def _rope_tables(n_lat, q_scale):
    t = np.arange(n_lat)
    hf = QK_ROPE_DIM // 4
    inv_freq = ROPE_BASE ** (-np.arange(hf, dtype=np.float64) / hf)
    ar = (t // GRID_W)[:, None] * inv_freq[None, :]
    ac = (t % GRID_W)[:, None] * inv_freq[None, :]
    z = np.zeros_like(ar)
    cos32 = np.concatenate([np.cos(ar), np.cos(ar), np.cos(ac), np.cos(ac)], axis=-1)
    sin32 = np.concatenate([np.sin(ar), np.sin(ar), np.sin(ac), np.sin(ac)], axis=-1)
    below32 = np.concatenate([z, np.sin(ar), z, np.sin(ac)], axis=-1)
    above32 = np.concatenate([-np.sin(ar), z, -np.sin(ac), z], axis=-1)
    pad = np.zeros((n_lat, HEAD_SLOT - QK_NOPE_DIM - QK_ROPE_DIM))
    ones = np.ones((n_lat, QK_NOPE_DIM))
    zeros = np.zeros((n_lat, QK_NOPE_DIM))
    slot = lambda nope, rope: np.concatenate([nope, rope, pad], axis=-1)
    tables = [slot(ones * q_scale, cos32 * q_scale), slot(zeros, sin32 * q_scale),
              slot(zeros, cos32), slot(zeros, below32), slot(zeros, above32)]
    return jnp.asarray(np.stack(tables).astype(np.float32))


def _dft_tables(n_pos, n_ch):
    n_sub = n_pos // FFT_RADIX
    m = np.arange(n_sub, dtype=np.int64)
    ang = 2.0 * np.pi * ((m[:, None] * m[None, :]) % n_sub) / n_sub
    tmat = np.concatenate([np.cos(ang), np.sin(ang)], axis=0).astype(np.float32)
    tw = np.zeros((FFT_RADIX - 1, 2, n_sub, FOURIER_GROUP_DIM), np.float32)
    for r in range(1, FFT_RADIX):
        a = 2.0 * np.pi * r * m / n_pos
        tw[r - 1, 0] = np.cos(a)[:, None]
        tw[r - 1, 1] = np.sin(a)[:, None]
    c = np.arange(n_ch, dtype=np.int64)
    angc = 2.0 * np.pi * ((c[:, None] * c[None, :]) % n_ch) / n_ch
    norm = 1.0 / np.sqrt(float(n_pos * n_ch))
    cmat = np.concatenate([np.cos(angc) * norm, -np.sin(angc) * norm], axis=0).astype(np.float32)
    return jnp.asarray(tmat), jnp.asarray(tw), jnp.asarray(cmat)


def kernel(x, c, ctx, c_ctx, w_ada, b_ada, g_pre_mix, g_post_mix, g_pre_ffn, g_post_ffn, w_in, g_q_a,
           w_q_b, g_kv_a, w_kv_b, w_fourier, w_out, w_gate, w_up, w_down):
    assert w_ada.shape[0] == 1, "single-layer block"
    batch, n_lat, d = x.shape

    mod = _adaln(c, c_ctx[None, :], w_ada[0], b_ada[0][None, :])

    w_in0 = w_in[0]
    w_kr = w_in0[:, ROPE_COL:]
    zeros_d = jnp.zeros((d, QK_NOPE_DIM), F32)
    kr_slot = _head_slot(zeros_d, w_kr)
    win = jnp.concatenate([w_in0[:, :ROPE_COL], kr_slot], axis=1).astype(BF16)

    wq3 = w_q_b[0].reshape(Q_LORA_RANK, MLA_HEADS, QK_NOPE_DIM + QK_ROPE_DIM)
    wq_nope, wq_rope = wq3[..., :QK_NOPE_DIM], wq3[..., QK_NOPE_DIM:]
    wq_a = _head_slot(wq_nope, wq_rope).reshape(Q_LORA_RANK, QK_WIDTH)
    wq_b = _head_slot(jnp.zeros_like(wq_nope), _rope_rotate_cols(wq_rope)).reshape(Q_LORA_RANK, QK_WIDTH)
    wq = jnp.concatenate([wq_a, wq_b], axis=1).astype(BF16)

    wkv3 = w_kv_b[0].reshape(KV_LORA_RANK, MLA_HEADS, QK_NOPE_DIM + V_HEAD_DIM)
    wk_nope, wv = wkv3[..., :QK_NOPE_DIM], wkv3[..., QK_NOPE_DIM:]
    wk_slots = _head_slot(wk_nope, jnp.zeros(wk_nope.shape[:-1] + (QK_ROPE_DIM,), F32))
    wkv = jnp.concatenate([wk_slots.reshape(KV_LORA_RANK, QK_WIDTH),
                           wv.reshape(KV_LORA_RANK, ATTN_WIDTH)], axis=1).astype(BF16)

    q_scale = float((QK_NOPE_DIM + QK_ROPE_DIM) ** -0.5 * np.log2(np.e))
    rope = _rope_tables(n_lat, q_scale)
    tmat, tw, cmat = _dft_tables(n_lat, FOURIER_GROUP_DIM)

    row2 = lambda g: g[0][None, :]
    u_f, q, k_lat, v_lat = _premix(x, mod, row2(g_pre_mix), win, row2(g_q_a), wq, row2(g_kv_a), wkv,
                                   rope)
    four = _fourier(u_f, tmat, tw, cmat, w_fourier[0])
    k_ctx, v_ctx = _ctxkv(ctx, mod, row2(g_pre_mix), win, row2(g_kv_a), wkv)
    attn_main, attn_last = _attention(q, k_ctx, k_lat, v_ctx, v_lat)
    return _post(x, four, attn_main, attn_last, mod, row2(g_post_mix), row2(g_pre_ffn), row2(g_post_ffn),
                 w_out[0], w_gate[0], w_up[0], w_down[0])
```

```python
import functools

import numpy as np
import jax
import jax.numpy as jnp
from jax import lax
from jax.experimental import pallas as pl
from jax.experimental.pallas import tpu as pltpu

F32 = jnp.float32
BF16 = jnp.bfloat16

D_MODEL = 1024
GRID_W = 64
FOURIER_GROUPS = 4
FOURIER_GROUP_DIM = 128
FOURIER_WIDTH = FOURIER_GROUPS * FOURIER_GROUP_DIM
MLA_HEADS = 8
QK_NOPE_DIM = 64
QK_ROPE_DIM = 32
V_HEAD_DIM = 64
Q_LORA_RANK = 256
KV_LORA_RANK = 128
KV_COL = FOURIER_WIDTH + Q_LORA_RANK
ROPE_COL = KV_COL + KV_LORA_RANK
ROPE_BASE = 10000.0
NORM_EPS = 1e-6
FFT_RADIX = 8
HEAD_SLOT = 128
STAGE_ROWS_WIDE = 256
STAGE_ROWS_TALL = 704
STAGE_ROWS_OUT = 512
SHIFT_DEN_MAX = 2.0 ** 40
PV_KEY_TILE = 256
ATTN_WIDTH = MLA_HEADS * V_HEAD_DIM
QK_WIDTH = MLA_HEADS * HEAD_SLOT

VMEM_LIMIT_BYTES = 56 * 1024 * 1024


def _rms(x, g):
    return x * lax.rsqrt(jnp.mean(x * x, axis=-1, keepdims=True) + NORM_EPS) * g


def _dot(a, b):
    return jnp.dot(a, b, preferred_element_type=F32)


def _dot_nt(a, b):
    return lax.dot_general(a, b, (((1,), (1,)), ((), ())), preferred_element_type=F32)


def _rotary(a, table, first):
    half = QK_ROPE_DIM // 4
    below = pltpu.roll(a, half, axis=1)
    above = pltpu.roll(a, HEAD_SLOT - half, axis=1)
    return a * table(first) + below * table(first + 1) + above * table(first + 2)


def _store_value_slots(v_ref, v):
    pair_w = 2 * V_HEAD_DIM
    lower = lax.broadcasted_iota(jnp.int32, (v.shape[0], pair_w), 1) < V_HEAD_DIM
    for pair in range(MLA_HEADS // 2):
        vp = v[:, pair * pair_w:(pair + 1) * pair_w]
        lo = 2 * pair * HEAD_SLOT
        v_ref[0, :, lo:lo + HEAD_SLOT] = jnp.where(lower, vp, 1.0).astype(BF16)
        v_ref[0, :, lo + HEAD_SLOT:lo + 2 * HEAD_SLOT] = jnp.where(lower, 1.0, vp).astype(BF16)


def _adaln_kernel(c_ref, cctx_ref, w_ref, b_ref, o_ref, rows_ref):
    n = c_ref.shape[0]
    rows_ref[:n, :] = c_ref[...]
    rows_ref[n:, :] = jnp.broadcast_to(cctx_ref[...], (rows_ref.shape[0] - n, rows_ref.shape[1]))
    c = rows_ref[...]
    a = c / (1.0 + jnp.exp(-c))
    o_ref[0] = _dot(a.astype(BF16), w_ref[...].astype(BF16)) + b_ref[...]


def _adaln(c, c_ctx, w_ada, b_ada):
    batch, d = c.shape
    assert batch % 8 == 0, "the context rows start on a sublane boundary"
    rows = batch + 8
    n_chunks = w_ada.shape[1] // d
    return pl.pallas_call(
        _adaln_kernel,
        grid=(n_chunks,),
        in_specs=[
            pl.BlockSpec((batch, d), lambda j: (0, 0)),
            pl.BlockSpec((1, d), lambda j: (0, 0)),
            pl.BlockSpec((d, d), lambda j: (0, j)),
            pl.BlockSpec((1, d), lambda j: (0, j)),
        ],
        out_specs=pl.BlockSpec((1, rows, d), lambda j: (j, 0, 0)),
        out_shape=jax.ShapeDtypeStruct((n_chunks, rows, d), F32),
        scratch_shapes=[pltpu.VMEM((rows, d), F32)],
        compiler_params=pltpu.CompilerParams(vmem_limit_bytes=VMEM_LIMIT_BYTES),
        name="adaln",
    )(c, c_ctx, w_ada, b_ada)


def _premix_kernel(x_ref, mod_ref, gpre_ref, win_ref, gq_ref, wq_ref, gkv_ref, wkv_ref,
                   rope_ref, u_ref, q_ref, k_ref, v_ref, us_ref):
    x = x_ref[0]
    row = pl.ds(pl.program_id(0), 1)
    shift = mod_ref[0, row, :]
    scale = mod_ref[1, row, :]
    h = _rms(x, gpre_ref[...] * (1.0 + scale)) + shift
    p = _dot(h.astype(BF16), win_ref[...])
    sub = us_ref.shape[1] // FFT_RADIX
    for g in range(FOURIER_GROUPS):
        glo = g * FOURIER_GROUP_DIM
        us_ref[g] = p[:, glo:glo + FOURIER_GROUP_DIM]
        for r in range(FFT_RADIX):
            lo = r * FOURIER_WIDTH + glo
            u_ref[0, :, lo:lo + FOURIER_GROUP_DIM] = (
                us_ref[g, pl.ds(r, sub, stride=FFT_RADIX), :].astype(BF16))

    qn = _rms(p[:, FOURIER_WIDTH:KV_COL], gq_ref[...]).astype(BF16)
    qq = _dot(qn, wq_ref[...])
    rows = x_ref.shape[1]
    row0 = pl.multiple_of(pl.program_id(1) * rows, rows)
    table = lambda k: rope_ref[k, pl.ds(row0, rows), :]
    cosq = table(0)
    sinq = table(1)
    for hd in range(MLA_HEADS):
        lo = hd * HEAD_SLOT
        q_ref[0, :, lo:lo + HEAD_SLOT] = (
            qq[:, lo:lo + HEAD_SLOT] * cosq + qq[:, QK_WIDTH + lo:QK_WIDTH + lo + HEAD_SLOT] * sinq
        ).astype(BF16)

    kvn = _rms(p[:, KV_COL:ROPE_COL], gkv_ref[...]).astype(BF16)
    kv = _dot(kvn, wkv_ref[...])
    kr = _rotary(p[:, ROPE_COL:ROPE_COL + HEAD_SLOT], table, 2)
    for hd in range(MLA_HEADS):
        lo = hd * HEAD_SLOT
        k_ref[0, :, lo:lo + HEAD_SLOT] = (kv[:, lo:lo + HEAD_SLOT] + kr).astype(BF16)
    _store_value_slots(v_ref, kv[:, QK_WIDTH:])


def _premix(x, mod, gpre, win, gq, wq, gkv, wkv, rope, tm=1024):
    b, s, d = x.shape
    const = lambda shape: pl.BlockSpec(shape, lambda i, j: (0,) * len(shape))
    rows = lambda w: pl.BlockSpec((1, tm, w), lambda i, j: (i, j, 0))
    return pl.pallas_call(
        _premix_kernel,
        grid=(b, s // tm),
        in_specs=[
            rows(d),
            pl.BlockSpec(mod.shape, lambda i, j: (0, 0, 0)),
            const(gpre.shape), const(win.shape), const(gq.shape), const(wq.shape),
            const(gkv.shape), const(wkv.shape),
            pl.BlockSpec(rope.shape, lambda i, j: (0, 0, 0), pipeline_mode=pl.Buffered(1)),
        ],
        out_specs=[
            pl.BlockSpec((1, tm // FFT_RADIX, FFT_RADIX * FOURIER_WIDTH), lambda i, j: (i, j, 0)),
            rows(QK_WIDTH), rows(QK_WIDTH), rows(QK_WIDTH)],
        out_shape=[
            jax.ShapeDtypeStruct((b, s // FFT_RADIX, FFT_RADIX * FOURIER_WIDTH), BF16),
            jax.ShapeDtypeStruct((b, s, QK_WIDTH), BF16),
            jax.ShapeDtypeStruct((b, s, QK_WIDTH), BF16),
            jax.ShapeDtypeStruct((b, s, QK_WIDTH), BF16),
        ],
        scratch_shapes=[pltpu.VMEM((FOURIER_GROUPS, tm, FOURIER_GROUP_DIM), F32)],
        compiler_params=pltpu.CompilerParams(vmem_limit_bytes=VMEM_LIMIT_BYTES),
        name="premix",
    )(x, mod, gpre, win, gq, wq, gkv, wkv, rope)


def _ctxkv_kernel(x_ref, mod_ref, gpre_ref, win_ref, gkv_ref, wkv_ref, k_ref, v_ref):
    x = x_ref[0]
    shift = mod_ref[0, 0:1, :]
    scale = mod_ref[1, 0:1, :]
    h = _rms(x, gpre_ref[...] * (1.0 + scale)) + shift
    p = _dot(h.astype(BF16), win_ref[...])
    kvn = _rms(p[:, :KV_LORA_RANK], gkv_ref[...]).astype(BF16)
    kv = _dot(kvn, wkv_ref[...])
    kr = p[:, KV_LORA_RANK:]
    for hd in range(MLA_HEADS):
        lo = hd * HEAD_SLOT
        k_ref[0, :, lo:lo + HEAD_SLOT] = (kv[:, lo:lo + HEAD_SLOT] + kr).astype(BF16)
    _store_value_slots(v_ref, kv[:, QK_WIDTH:])


def _ctxkv(ctx, mod, gpre, win, gkv, wkv):
    b, c, d = ctx.shape
    kv_cols = win.shape[1] - KV_COL
    assert KV_COL % kv_cols == 0
    const = lambda shape: pl.BlockSpec(shape, lambda i: (0,) * len(shape))
    rows = lambda w: pl.BlockSpec((1, c, w), lambda i: (i, 0, 0))
    return pl.pallas_call(
        _ctxkv_kernel,
        grid=(b,),
        in_specs=[
            rows(d),
            pl.BlockSpec((mod.shape[0], 8, d), lambda i: (0, b // 8, 0)),
            const(gpre.shape),
            pl.BlockSpec((d, kv_cols), lambda i: (0, KV_COL // kv_cols)),
            const(gkv.shape), const(wkv.shape),
        ],
        out_specs=[rows(QK_WIDTH), rows(QK_WIDTH)],
        out_shape=[
            jax.ShapeDtypeStruct((b, c, QK_WIDTH), BF16),
            jax.ShapeDtypeStruct((b, c, QK_WIDTH), BF16),
        ],
        name="ctxkv",
    )(ctx, mod, gpre, win, gkv, wkv)


def _attn_kernel(q_ref, kc_ref, kl_ref, vc_ref, vl_ref, vcp_ref, vlp_ref, omain_ref, olast_ref,
                 p_ref, oprev_ref, ohead_ref):
    t = pl.program_id(0)
    n_blocks = pl.num_programs(0) - 1
    n_ctx = kc_ref.shape[1]
    n_lat = kl_ref.shape[1]
    tq = q_ref.shape[1]
    pair_w = 2 * V_HEAD_DIM
    first_half = lax.broadcasted_iota(jnp.int32, (tq, pair_w), 1) < V_HEAD_DIM
    last = MLA_HEADS - 1

    @pl.when(t == 0)
    def _():
        p_ref[...] = jnp.ones_like(p_ref)
        oprev_ref[...] = jnp.ones_like(oprev_ref)

    def join_pair(o_even, o_odd):
        num = jnp.where(first_half, o_even, o_odd)
        den = pltpu.roll(jnp.where(first_half, o_odd, o_even), V_HEAD_DIM, axis=1)
        return num / den

    def den_of(o, hd):
        lane = V_HEAD_DIM if hd % 2 == 0 else 0
        return o[:, lane:lane + 1]

    def drain():
        o = _dot(p_ref[:, :n_ctx], vcp_ref[0]) + _dot(p_ref[:, n_ctx:], vlp_ref[0])
        olast_ref[0] = join_pair(oprev_ref[...], o).astype(BF16)

    def head(hd, exact):
        lo = hd * HEAD_SLOT
        qh = q_ref[0, :, lo:lo + HEAD_SLOT]
        s_c = _dot_nt(qh, kc_ref[0, :, lo:lo + HEAD_SLOT])
        s_l = _dot_nt(qh, kl_ref[0, :, lo:lo + HEAD_SLOT])
        shift = jnp.max(s_c, axis=-1, keepdims=True)
        if exact:
            shift = jnp.maximum(shift, jnp.max(s_l, axis=-1, keepdims=True))
        if hd == last:
            p_c = jnp.exp2(s_c - shift)
            p_l = jnp.exp2(s_l - shift)
            p_ref[:, :n_ctx] = p_c.astype(BF16)
            p_ref[:, n_ctx:] = p_l.astype(BF16)
            return None, jnp.sum(p_c, axis=-1, keepdims=True) + jnp.sum(p_l, axis=-1, keepdims=True)
        o = _dot(jnp.exp2(s_c - shift).astype(BF16), vc_ref[0, :, lo:lo + HEAD_SLOT])
        for k0 in range(0, n_lat, PV_KEY_TILE):
            p = jnp.exp2(s_l[:, k0:k0 + PV_KEY_TILE] - shift).astype(BF16)
            o = o + _dot(p, vl_ref[0, k0:k0 + PV_KEY_TILE, lo:lo + HEAD_SLOT])
        return o, den_of(o, hd)

    def block():
        dens = []
        for pair in range(MLA_HEADS // 2):
            outs = []
            for hd in (2 * pair, 2 * pair + 1):
                o, den = head(hd, False)
                dens.append(den)
                if o is not None:
                    outs.append(o)
            if len(outs) == 2:
                vlo = pair * pair_w
                omain_ref[0, :, vlo:vlo + pair_w] = join_pair(outs[0], outs[1]).astype(BF16)
            else:
                oprev_ref[...] = outs[0]
        return dens

    def redo_block_exact():
        def one_head(hd, carry):
            lo = pl.multiple_of(hd * HEAD_SLOT, HEAD_SLOT)
            qh = q_ref[0, :, pl.ds(lo, HEAD_SLOT)]
            s_c = _dot_nt(qh, kc_ref[0, :, pl.ds(lo, HEAD_SLOT)])
            s_l = _dot_nt(qh, kl_ref[0, :, pl.ds(lo, HEAD_SLOT)])
            shift = jnp.maximum(jnp.max(s_c, axis=-1, keepdims=True), jnp.max(s_l, axis=-1, keepdims=True))
            ohead_ref[hd] = (_dot(jnp.exp2(s_c - shift).astype(BF16), vc_ref[0, :, pl.ds(lo, HEAD_SLOT)])
                             + _dot(jnp.exp2(s_l - shift).astype(BF16), vl_ref[0, :, pl.ds(lo, HEAD_SLOT)]))
            return carry

        lax.fori_loop(0, last, one_head, 0)
        for pair in range(MLA_HEADS // 2 - 1):
            vlo = pair * pair_w
            omain_ref[0, :, vlo:vlo + pair_w] = join_pair(ohead_ref[2 * pair], ohead_ref[2 * pair + 1]).astype(BF16)
        oprev_ref[...] = ohead_ref[last - 1]
        head(last, True)

    @pl.when(t < n_blocks)
    def _():
        drain()
        dens = block()
        worst = dens[0]
        for den in dens[1:]:
            worst = jnp.maximum(worst, den)
        trusted = jnp.max(worst) <= SHIFT_DEN_MAX

        @pl.when(jnp.logical_not(trusted))
        def _():
            redo_block_exact()

    @pl.when(t == n_blocks)
    def _():
        drain()


def _attention(q, kc, kl, vc, vl, tq=512):
    b, s, _ = q.shape
    c = kc.shape[1]
    nq = s // tq
    n_blocks = b * nq
    pair_w = 2 * V_HEAD_DIM
    main_w = ATTN_WIDTH - pair_w
    last = MLA_HEADS - 1
    cur = lambda t: jnp.minimum(t, n_blocks - 1)
    prev = lambda t: jnp.maximum(t - 1, 0)
    return pl.pallas_call(
        _attn_kernel,
        grid=(n_blocks + 1,),
        in_specs=[
            pl.BlockSpec((1, tq, QK_WIDTH), lambda t: (cur(t) // nq, cur(t) % nq, 0)),
            pl.BlockSpec((1, c, QK_WIDTH), lambda t: (cur(t) // nq, 0, 0)),
            pl.BlockSpec((1, s, QK_WIDTH), lambda t: (cur(t) // nq, 0, 0)),
            pl.BlockSpec((1, c, QK_WIDTH), lambda t: (cur(t) // nq, 0, 0)),
            pl.BlockSpec((1, s, QK_WIDTH), lambda t: (cur(t) // nq, 0, 0)),
            pl.BlockSpec((1, c, HEAD_SLOT), lambda t: (prev(t) // nq, 0, last)),
            pl.BlockSpec((1, s, HEAD_SLOT), lambda t: (prev(t) // nq, 0, last)),
        ],
        out_specs=[
            pl.BlockSpec((1, tq, main_w), lambda t: (cur(t) // nq, cur(t) % nq, 0)),
            pl.BlockSpec((1, tq, pair_w), lambda t: (prev(t) // nq, prev(t) % nq, 0)),
        ],
        out_shape=[
            jax.ShapeDtypeStruct((b, s, main_w), BF16),
            jax.ShapeDtypeStruct((b, s, pair_w), BF16),
        ],
        scratch_shapes=[
            pltpu.VMEM((tq, c + s), BF16),
            pltpu.VMEM((tq, pair_w), F32),
            pltpu.VMEM((MLA_HEADS - 1, tq, HEAD_SLOT), F32),
        ],
        compiler_params=pltpu.CompilerParams(
            dimension_semantics=("arbitrary",), vmem_limit_bytes=VMEM_LIMIT_BYTES),
        name="attn",
    )(q, kc, kl, vc, vl, vc, vl)


def _fourier_kernel(t_ref, tw_ref, u_ref, cc_ref, wf_ref, o_ref, tb_ref, cw_ref):
    @pl.when(pl.program_id(0) == 0)
    def _():
        tb_ref[...] = t_ref[...].astype(BF16)
        cc = cc_ref[...].astype(BF16)
        for g in range(FOURIER_GROUPS):
            cw_ref[g] = _dot(cc, wf_ref[g].astype(BF16)).astype(BF16)

    n_sub = tb_ref.shape[1]
    f = _dot(tb_ref[...], u_ref[0])
    gd = FOURIER_GROUP_DIM
    for g in range(FOURIER_GROUPS):
        gr, gi = [], []
        for r in range(FFT_RADIX):
            lo = r * FOURIER_WIDTH + g * gd
            a = f[:n_sub, lo:lo + gd]
            b = f[n_sub:, lo:lo + gd]
            if r == 0:
                gr.append(a)
                gi.append(b)
            else:
                c = tw_ref[r - 1, 0]
                s = tw_ref[r - 1, 1]
                gr.append(a * c - b * s)
                gi.append(a * s + b * c)
        z = list(zip(gr, gi))
        add = lambda a, b: (a[0] + b[0], a[1] + b[1])
        sub = lambda a, b: (a[0] - b[0], a[1] - b[1])
        times_minus_i = lambda a: (-a[1], a[0])

        def dft4(c):
            e0, e1 = add(c[0], c[2]), sub(c[0], c[2])
            f0, f1 = add(c[1], c[3]), times_minus_i(sub(c[1], c[3]))
            return [add(e0, f0), add(e1, f1), sub(e0, f0), sub(e1, f1)]

        half = FFT_RADIX // 2
        even = dft4([add(z[r], z[r + half]) for r in range(half)])
        d = [sub(z[r], z[r + half]) for r in range(half)]
        rt = np.float32(np.sqrt(0.5))
        odd = dft4([
            d[0],
            ((d[1][0] - d[1][1]) * rt, (d[1][0] + d[1][1]) * rt),
            times_minus_i(d[2]),
            (-(d[3][0] + d[3][1]) * rt, (d[3][0] - d[3][1]) * rt),
        ])
        y = [even[q // 2] if q % 2 == 0 else odd[q // 2] for q in range(FFT_RADIX)]
        xr = jnp.concatenate([v[0] for v in y], axis=0)
        xi = jnp.concatenate([v[1] for v in y], axis=0)
        lhs = jnp.concatenate([xr, xi], axis=1).astype(BF16)
        o_ref[0, :, g * gd:(g + 1) * gd] = _dot(lhs, cw_ref[g]).astype(BF16)


def _fourier(u4, tmat, tw, cmat, wf):
    b, n_sub, _ = u4.shape
    s = n_sub * FFT_RADIX
    full = lambda a: pl.BlockSpec(a.shape, lambda i: (0,) * a.ndim)
    return pl.pallas_call(
        _fourier_kernel,
        grid=(b,),
        in_specs=[
            full(tmat), full(tw),
            pl.BlockSpec((1, n_sub, FFT_RADIX * FOURIER_WIDTH), lambda i: (i, 0, 0)),
            full(cmat), full(wf),
        ],
        out_specs=pl.BlockSpec((1, s, FOURIER_WIDTH), lambda i: (i, 0, 0)),
        out_shape=jax.ShapeDtypeStruct((b, s, FOURIER_WIDTH), BF16),
        scratch_shapes=[
            pltpu.VMEM(tmat.shape, BF16),
            pltpu.VMEM((FOURIER_GROUPS, 2 * FOURIER_GROUP_DIM, FOURIER_GROUP_DIM), BF16),
        ],
        compiler_params=pltpu.CompilerParams(
            dimension_semantics=("arbitrary",), vmem_limit_bytes=VMEM_LIMIT_BYTES),
        name="fourier",
    )(tmat, tw, u4, cmat, wf)


def _stage_cast(src_hbm, dst_ref, stage_ref, sem_ref, chunk):
    n_chunks = src_hbm.shape[0] // chunk

    def copy(i):
        slot = i % 2
        return pltpu.make_async_copy(
            src_hbm.at[pl.ds(i * chunk, chunk)], stage_ref.at[slot, pl.ds(0, chunk)], sem_ref.at[slot])

    def body(i, carry):
        @pl.when(i + 1 < n_chunks)
        def _():
            copy(i + 1).start()

        copy(i).wait()
        start = pl.multiple_of(i * chunk, chunk)
        dst_ref[pl.ds(start, chunk), :] = stage_ref[i % 2, pl.ds(0, chunk), :].astype(BF16)
        return carry

    copy(0).start()
    lax.fori_loop(0, n_chunks, body, 0)


def _post_kernel(x_ref, four_ref, am_ref, al_ref, mod_ref, gpm_ref, gpf_ref, gqf_ref,
                 wo_hbm, wg_hbm, wu_hbm, wd_hbm, o_ref,
                 x1_ref, h2_ref, wo_ref, wg_ref, wu_ref, wd_ref, wide_stage, tall_stage, sem_ref,
                 *, blocks_per_batch):
    t = pl.program_id(0)
    mix_row = pl.ds(jnp.minimum(t, pl.num_programs(0) - 2) // blocks_per_batch, 1)
    ffn_row = pl.ds(jnp.maximum(t - 1, 0) // blocks_per_batch, 1)

    def mix():
        gt_m = mod_ref[2, mix_row, :]
        sh_f = mod_ref[3, mix_row, :]
        sc_f = mod_ref[4, mix_row, :]
        mixed = jnp.concatenate([four_ref[0], am_ref[0], al_ref[0]], axis=1)
        y = _dot(mixed, wo_ref[...])
        x1 = x_ref[0] + _rms(y, gt_m * gpm_ref[...])
        x1_ref[...] = x1
        h2_ref[...] = (_rms(x1, gpf_ref[...] * (1.0 + sc_f)) + sh_f).astype(BF16)

    def ffn():
        gt_f = mod_ref[5, ffn_row, :]
        h2 = h2_ref[...]
        g = _dot(h2, wg_ref[...])
        up = _dot(h2, wu_ref[...])
        act = (g / (1.0 + jnp.exp(-g)) * up).astype(BF16)
        o_ref[0] = x1_ref[...] + _rms(_dot(act, wd_ref[...]), gt_f * gqf_ref[...])

    @pl.when(t == 0)
    def _():
        _stage_cast(wo_hbm, wo_ref, tall_stage, sem_ref, STAGE_ROWS_OUT)
        mix()
        _stage_cast(wg_hbm, wg_ref, wide_stage, sem_ref, STAGE_ROWS_WIDE)
        _stage_cast(wu_hbm, wu_ref, wide_stage, sem_ref, STAGE_ROWS_WIDE)
        _stage_cast(wd_hbm, wd_ref, tall_stage, sem_ref, STAGE_ROWS_TALL)

    @pl.when(t > 0)
    def _():
        ffn()
        mix()


def _post(x, four, attn_main, attn_last, mod, gpm, gpf, gqf, wo, wg, wu, wd, tm=512):
    b, s, d = x.shape
    d_ff = wg.shape[1]
    assert wo.shape[0] % STAGE_ROWS_OUT == 0 and STAGE_ROWS_OUT <= STAGE_ROWS_TALL
    assert d % STAGE_ROWS_WIDE == 0 and d_ff % STAGE_ROWS_TALL == 0
    nj = s // tm
    n_blocks = b * nj
    cur = lambda t: jnp.minimum(t, n_blocks - 1)
    prev = lambda t: jnp.maximum(t - 1, 0)
    const = lambda shape: pl.BlockSpec(shape, lambda t: (0,) * len(shape), pipeline_mode=pl.Buffered(1))
    rows = lambda w: pl.BlockSpec((1, tm, w), lambda t: (cur(t) // nj, cur(t) % nj, 0))
    hbm = pl.BlockSpec(memory_space=pl.ANY)
    return pl.pallas_call(
        functools.partial(_post_kernel, blocks_per_batch=nj),
        grid=(n_blocks + 1,),
        in_specs=[
            rows(d), rows(FOURIER_WIDTH), rows(attn_main.shape[2]), rows(attn_last.shape[2]),
            pl.BlockSpec(mod.shape, lambda t: (0, 0, 0)),
            const(gpm.shape), const(gpf.shape), const(gqf.shape),
            hbm, hbm, hbm, hbm,
        ],
        out_specs=pl.BlockSpec((1, tm, d), lambda t: (prev(t) // nj, prev(t) % nj, 0)),
        out_shape=jax.ShapeDtypeStruct((b, s, d), F32),
        scratch_shapes=[
            pltpu.VMEM((tm, d), F32), pltpu.VMEM((tm, d), BF16),
            pltpu.VMEM(wo.shape, BF16), pltpu.VMEM(wg.shape, BF16), pltpu.VMEM(wu.shape, BF16),
            pltpu.VMEM(wd.shape, BF16),
            pltpu.VMEM((2, STAGE_ROWS_WIDE, d_ff), F32), pltpu.VMEM((2, STAGE_ROWS_TALL, d), F32),
            pltpu.SemaphoreType.DMA((2,)),
        ],
        compiler_params=pltpu.CompilerParams(
            dimension_semantics=("arbitrary",), vmem_limit_bytes=VMEM_LIMIT_BYTES),
        name="post",
    )(x, four, attn_main, attn_last, mod, gpm, gpf, gqf, wo, wg, wu, wd)


def _rope_rotate_cols(w):
    a = QK_ROPE_DIM // 2
    hf = a // 2
    blocks = []
    for s0 in (0, a):
        blocks += [-w[..., s0 + hf:s0 + a], w[..., s0:s0 + hf]]
    return jnp.concatenate(blocks, axis=-1)


def _head_slot(nope, rope):
    pad = HEAD_SLOT - QK_NOPE_DIM - QK_ROPE_DIM
    return jnp.concatenate([nope, rope, jnp.zeros(rope.shape[:-1] + (pad,), rope.dtype)], axis=-1)


def _rope_tables(n_lat, q_scale):
    t = np.arange(n_lat)
    hf = QK_ROPE_DIM // 4
    inv_freq = ROPE_BASE ** (-np.arange(hf, dtype=np.float64) / hf)
    ar = (t // GRID_W)[:, None] * inv_freq[None, :]
    ac = (t % GRID_W)[:, None] * inv_freq[None, :]
    z = np.zeros_like(ar)
    cos32 = np.concatenate([np.cos(ar), np.cos(ar), np.cos(ac), np.cos(ac)], axis=-1)
    sin32 = np.concatenate([np.sin(ar), np.sin(ar), np.sin(ac), np.sin(ac)], axis=-1)
    below32 = np.concatenate([z, np.sin(ar), z, np.sin(ac)], axis=-1)
    above32 = np.concatenate([-np.sin(ar), z, -np.sin(ac), z], axis=-1)
    pad = np.zeros((n_lat, HEAD_SLOT - QK_NOPE_DIM - QK_ROPE_DIM))
    ones = np.ones((n_lat, QK_NOPE_DIM))
    zeros = np.zeros((n_lat, QK_NOPE_DIM))
    slot = lambda nope, rope: np.concatenate([nope, rope, pad], axis=-1)
    tables = [slot(ones * q_scale, cos32 * q_scale), slot(zeros, sin32 * q_scale),
              slot(zeros, cos32), slot(zeros, below32), slot(zeros, above32)]
    return jnp.asarray(np.stack(tables).astype(np.float32))


def _dft_tables(n_pos, n_ch):
    n_sub = n_pos // FFT_RADIX
    m = np.arange(n_sub, dtype=np.int64)
    ang = 2.0 * np.pi * ((m[:, None] * m[None, :]) % n_sub) / n_sub
    tmat = np.concatenate([np.cos(ang), np.sin(ang)], axis=0).astype(np.float32)
    tw = np.zeros((FFT_RADIX - 1, 2, n_sub, FOURIER_GROUP_DIM), np.float32)
    for r in range(1, FFT_RADIX):
        a = 2.0 * np.pi * r * m / n_pos
        tw[r - 1, 0] = np.cos(a)[:, None]
        tw[r - 1, 1] = np.sin(a)[:, None]
    c = np.arange(n_ch, dtype=np.int64)
    angc = 2.0 * np.pi * ((c[:, None] * c[None, :]) % n_ch) / n_ch
    norm = 1.0 / np.sqrt(float(n_pos * n_ch))
    cmat = np.concatenate([np.cos(angc) * norm, -np.sin(angc) * norm], axis=0).astype(np.float32)
    return jnp.asarray(tmat), jnp.asarray(tw), jnp.asarray(cmat)


def kernel(x, c, ctx, c_ctx, w_ada, b_ada, g_pre_mix, g_post_mix, g_pre_ffn, g_post_ffn, w_in, g_q_a,
           w_q_b, g_kv_a, w_kv_b, w_fourier, w_out, w_gate, w_up, w_down):
    assert w_ada.shape[0] == 1, "single-layer block"
    batch, n_lat, d = x.shape

    mod = _adaln(c, c_ctx[None, :], w_ada[0], b_ada[0][None, :])

    w_in0 = w_in[0]
    w_kr = w_in0[:, ROPE_COL:]
    zeros_d = jnp.zeros((d, QK_NOPE_DIM), F32)
    kr_slot = _head_slot(zeros_d, w_kr)
    win = jnp.concatenate([w_in0[:, :ROPE_COL], kr_slot], axis=1).astype(BF16)

    wq3 = w_q_b[0].reshape(Q_LORA_RANK, MLA_HEADS, QK_NOPE_DIM + QK_ROPE_DIM)
    wq_nope, wq_rope = wq3[..., :QK_NOPE_DIM], wq3[..., QK_NOPE_DIM:]
    wq_a = _head_slot(wq_nope, wq_rope).reshape(Q_LORA_RANK, QK_WIDTH)
    wq_b = _head_slot(jnp.zeros_like(wq_nope), _rope_rotate_cols(wq_rope)).reshape(Q_LORA_RANK, QK_WIDTH)
    wq = jnp.concatenate([wq_a, wq_b], axis=1).astype(BF16)

    wkv3 = w_kv_b[0].reshape(KV_LORA_RANK, MLA_HEADS, QK_NOPE_DIM + V_HEAD_DIM)
    wk_nope, wv = wkv3[..., :QK_NOPE_DIM], wkv3[..., QK_NOPE_DIM:]
    wk_slots = _head_slot(wk_nope, jnp.zeros(wk_nope.shape[:-1] + (QK_ROPE_DIM,), F32))
    wkv = jnp.concatenate([wk_slots.reshape(KV_LORA_RANK, QK_WIDTH),
                           wv.reshape(KV_LORA_RANK, ATTN_WIDTH)], axis=1).astype(BF16)

    q_scale = float((QK_NOPE_DIM + QK_ROPE_DIM) ** -0.5 * np.log2(np.e))
    rope = _rope_tables(n_lat, q_scale)
    tmat, tw, cmat = _dft_tables(n_lat, FOURIER_GROUP_DIM)

    row2 = lambda g: g[0][None, :]
    u_f, q, k_lat, v_lat = _premix(x, mod, row2(g_pre_mix), win, row2(g_q_a), wq, row2(g_kv_a), wkv,
                                   rope)
    four = _fourier(u_f, tmat, tw, cmat, w_fourier[0])
    k_ctx, v_ctx = _ctxkv(ctx, mod, row2(g_pre_mix), win, row2(g_kv_a), wkv)
    attn_main, attn_last = _attention(q, k_ctx, k_lat, v_ctx, v_lat)
    return _post(x, four, attn_main, attn_last, mod, row2(g_post_mix), row2(g_pre_ffn), row2(g_post_ffn),
                 w_out[0], w_gate[0], w_up[0], w_down[0])
```

```python
import functools

import numpy as np
import jax
import jax.numpy as jnp
from jax import lax
from jax.experimental import pallas as pl
from jax.experimental.pallas import tpu as pltpu

F32 = jnp.float32
BF16 = jnp.bfloat16

GRID_W = 64
FOURIER_GROUPS = 4
FOURIER_GROUP_DIM = 128
FOURIER_WIDTH = FOURIER_GROUPS * FOURIER_GROUP_DIM
MLA_HEADS = 8
QK_NOPE_DIM = 64
QK_ROPE_DIM = 32
V_HEAD_DIM = 64
Q_LORA_RANK = 256
KV_LORA_RANK = 128
KV_COL = FOURIER_WIDTH + Q_LORA_RANK
ROPE_COL = KV_COL + KV_LORA_RANK
ROPE_BASE = 10000.0
NORM_EPS = 1e-6
FFT_RADIX = 8
HEAD_SLOT = 128
STAGE_ROWS_WIDE = 256
STAGE_ROWS_TALL = 704
STAGE_ROWS_OUT = 512
SHIFT_DEN_MAX = 2.0 ** 40
PV_KEY_TILE = 256
ATTN_WIDTH = MLA_HEADS * V_HEAD_DIM
QK_WIDTH = MLA_HEADS * HEAD_SLOT

VMEM_LIMIT_BYTES = 56 * 1024 * 1024


def _rms(x, g):
    return x * lax.rsqrt(jnp.mean(x * x, axis=-1, keepdims=True) + NORM_EPS) * g


def _dot(a, b):
    return jnp.dot(a, b, preferred_element_type=F32)


def _dot_nt(a, b):
    return lax.dot_general(a, b, (((1,), (1,)), ((), ())), preferred_element_type=F32)


def _rotary(a, table, first):
    half = QK_ROPE_DIM // 4
    below = pltpu.roll(a, half, axis=1)
    above = pltpu.roll(a, HEAD_SLOT - half, axis=1)
    return a * table(first) + below * table(first + 1) + above * table(first + 2)


def _store_value_slots(v_ref, v):
    pair_w = 2 * V_HEAD_DIM
    lower = lax.broadcasted_iota(jnp.int32, (v.shape[0], pair_w), 1) < V_HEAD_DIM
    for pair in range(MLA_HEADS // 2):
        vp = v[:, pair * pair_w:(pair + 1) * pair_w]
        lo = 2 * pair * HEAD_SLOT
        v_ref[0, :, lo:lo + HEAD_SLOT] = jnp.where(lower, vp, 1.0).astype(BF16)
        v_ref[0, :, lo + HEAD_SLOT:lo + 2 * HEAD_SLOT] = jnp.where(lower, 1.0, vp).astype(BF16)


def _adaln_kernel(c_ref, cctx_ref, w_ref, b_ref, o_ref, rows_ref):
    n = c_ref.shape[0]
    rows_ref[:n, :] = c_ref[...]
    rows_ref[n:, :] = jnp.broadcast_to(cctx_ref[...], (rows_ref.shape[0] - n, rows_ref.shape[1]))
    c = rows_ref[...]
    a = c / (1.0 + jnp.exp(-c))
    o_ref[0] = _dot(a.astype(BF16), w_ref[...].astype(BF16)) + b_ref[...]


def _adaln(c, c_ctx, w_ada, b_ada):
    batch, d = c.shape
    assert batch % 8 == 0, "the context rows start on a sublane boundary"
    rows = batch + 8
    n_chunks = w_ada.shape[1] // d
    return pl.pallas_call(
        _adaln_kernel,
        grid=(n_chunks,),
        in_specs=[
            pl.BlockSpec((batch, d), lambda j: (0, 0)),
            pl.BlockSpec((1, d), lambda j: (0, 0)),
            pl.BlockSpec((d, d), lambda j: (0, j)),
            pl.BlockSpec((1, d), lambda j: (0, j)),
        ],
        out_specs=pl.BlockSpec((1, rows, d), lambda j: (j, 0, 0)),
        out_shape=jax.ShapeDtypeStruct((n_chunks, rows, d), F32),
        scratch_shapes=[pltpu.VMEM((rows, d), F32)],
        compiler_params=pltpu.CompilerParams(vmem_limit_bytes=VMEM_LIMIT_BYTES),
        name="adaln",
    )(c, c_ctx, w_ada, b_ada)


def _premix_kernel(x_ref, mod_ref, gpre_ref, win_ref, gq_ref, wq_ref, gkv_ref, wkv_ref,
                   rope_ref, u_ref, q_ref, k_ref, v_ref, us_ref):
    x = x_ref[0]
    row = pl.ds(pl.program_id(0), 1)
    shift = mod_ref[0, row, :]
    scale = mod_ref[1, row, :]
    h = _rms(x, gpre_ref[...] * (1.0 + scale)) + shift
    p = _dot(h.astype(BF16), win_ref[...])
    sub = us_ref.shape[1] // FFT_RADIX
    for g in range(FOURIER_GROUPS):
        glo = g * FOURIER_GROUP_DIM
        us_ref[g] = p[:, glo:glo + FOURIER_GROUP_DIM]
        for r in range(FFT_RADIX):
            lo = r * FOURIER_WIDTH + glo
            u_ref[0, :, lo:lo + FOURIER_GROUP_DIM] = (
                us_ref[g, pl.ds(r, sub, stride=FFT_RADIX), :].astype(BF16))

    qn = _rms(p[:, FOURIER_WIDTH:KV_COL], gq_ref[...]).astype(BF16)
    qq = _dot(qn, wq_ref[...])
    rows = x_ref.shape[1]
    row0 = pl.multiple_of(pl.program_id(1) * rows, rows)
    table = lambda k: rope_ref[k, pl.ds(row0, rows), :]
    cosq = table(0)
    sinq = table(1)
    for hd in range(MLA_HEADS):
        lo = hd * HEAD_SLOT
        q_ref[0, :, lo:lo + HEAD_SLOT] = (
            qq[:, lo:lo + HEAD_SLOT] * cosq + qq[:, QK_WIDTH + lo:QK_WIDTH + lo + HEAD_SLOT] * sinq
        ).astype(BF16)

    kvn = _rms(p[:, KV_COL:ROPE_COL], gkv_ref[...]).astype(BF16)
    kv = _dot(kvn, wkv_ref[...])
    kr = _rotary(p[:, ROPE_COL:ROPE_COL + HEAD_SLOT], table, 2)
    for hd in range(MLA_HEADS):
        lo = hd * HEAD_SLOT
        k_ref[0, :, lo:lo + HEAD_SLOT] = (kv[:, lo:lo + HEAD_SLOT] + kr).astype(BF16)
    _store_value_slots(v_ref, kv[:, QK_WIDTH:])


def _premix(x, mod, gpre, win, gq, wq, gkv, wkv, rope, tm=1024):
    b, s, d = x.shape
    const = lambda shape: pl.BlockSpec(shape, lambda i, j: (0,) * len(shape))
    rows = lambda w: pl.BlockSpec((1, tm, w), lambda i, j: (i, j, 0))
    return pl.pallas_call(
        _premix_kernel,
        grid=(b, s // tm),
        in_specs=[
            rows(d),
            pl.BlockSpec(mod.shape, lambda i, j: (0, 0, 0)),
            const(gpre.shape), const(win.shape), const(gq.shape), const(wq.shape),
            const(gkv.shape), const(wkv.shape),
            pl.BlockSpec(rope.shape, lambda i, j: (0, 0, 0), pipeline_mode=pl.Buffered(1)),
        ],
        out_specs=[
            pl.BlockSpec((1, tm // FFT_RADIX, FFT_RADIX * FOURIER_WIDTH), lambda i, j: (i, j, 0)),
            rows(QK_WIDTH), rows(QK_WIDTH), rows(QK_WIDTH)],
        out_shape=[
            jax.ShapeDtypeStruct((b, s // FFT_RADIX, FFT_RADIX * FOURIER_WIDTH), BF16),
            jax.ShapeDtypeStruct((b, s, QK_WIDTH), BF16),
            jax.ShapeDtypeStruct((b, s, QK_WIDTH), BF16),
            jax.ShapeDtypeStruct((b, s, QK_WIDTH), BF16),
        ],
        scratch_shapes=[pltpu.VMEM((FOURIER_GROUPS, tm, FOURIER_GROUP_DIM), F32)],
        compiler_params=pltpu.CompilerParams(vmem_limit_bytes=VMEM_LIMIT_BYTES),
        name="premix",
    )(x, mod, gpre, win, gq, wq, gkv, wkv, rope)


def _ctxkv_kernel(x_ref, mod_ref, gpre_ref, win_ref, gkv_ref, wkv_ref, k_ref, v_ref):
    x = x_ref[0]
    shift = mod_ref[0, 0:1, :]
    scale = mod_ref[1, 0:1, :]
    h = _rms(x, gpre_ref[...] * (1.0 + scale)) + shift
    p = _dot(h.astype(BF16), win_ref[...])
    kvn = _rms(p[:, :KV_LORA_RANK], gkv_ref[...]).astype(BF16)
    kv = _dot(kvn, wkv_ref[...])
    kr = p[:, KV_LORA_RANK:]
    for hd in range(MLA_HEADS):
        lo = hd * HEAD_SLOT
        k_ref[0, :, lo:lo + HEAD_SLOT] = (kv[:, lo:lo + HEAD_SLOT] + kr).astype(BF16)
    _store_value_slots(v_ref, kv[:, QK_WIDTH:])


def _ctxkv(ctx, mod, gpre, win, gkv, wkv):
    b, c, d = ctx.shape
    kv_cols = win.shape[1] - KV_COL
    assert KV_COL % kv_cols == 0
    const = lambda shape: pl.BlockSpec(shape, lambda i: (0,) * len(shape))
    rows = lambda w: pl.BlockSpec((1, c, w), lambda i: (i, 0, 0))
    return pl.pallas_call(
        _ctxkv_kernel,
        grid=(b,),
        in_specs=[
            rows(d),
            pl.BlockSpec((mod.shape[0], 8, d), lambda i: (0, b // 8, 0)),
            const(gpre.shape),
            pl.BlockSpec((d, kv_cols), lambda i: (0, KV_COL // kv_cols)),
            const(gkv.shape), const(wkv.shape),
        ],
        out_specs=[rows(QK_WIDTH), rows(QK_WIDTH)],
        out_shape=[
            jax.ShapeDtypeStruct((b, c, QK_WIDTH), BF16),
            jax.ShapeDtypeStruct((b, c, QK_WIDTH), BF16),
        ],
        name="ctxkv",
    )(ctx, mod, gpre, win, gkv, wkv)


def _attn_kernel(q_ref, kc_ref, kl_ref, vc_ref, vl_ref, vcp_ref, vlp_ref, omain_ref, olast_ref,
                 p_ref, oprev_ref, ohead_ref):
    t = pl.program_id(0)
    n_blocks = pl.num_programs(0) - 1
    n_ctx = kc_ref.shape[1]
    n_lat = kl_ref.shape[1]
    tq = q_ref.shape[1]
    pair_w = 2 * V_HEAD_DIM
    first_half = lax.broadcasted_iota(jnp.int32, (tq, pair_w), 1) < V_HEAD_DIM
    last = MLA_HEADS - 1

    @pl.when(t == 0)
    def _():
        p_ref[...] = jnp.ones_like(p_ref)
        oprev_ref[...] = jnp.ones_like(oprev_ref)

    def join_pair(o_even, o_odd):
        num = jnp.where(first_half, o_even, o_odd)
        den = pltpu.roll(jnp.where(first_half, o_odd, o_even), V_HEAD_DIM, axis=1)
        return num / den

    def den_of(o, hd):
        lane = V_HEAD_DIM if hd % 2 == 0 else 0
        return o[:, lane:lane + 1]

    def drain():
        o = _dot(p_ref[:, :n_ctx], vcp_ref[0]) + _dot(p_ref[:, n_ctx:], vlp_ref[0])
        olast_ref[0] = join_pair(oprev_ref[...], o).astype(BF16)

    def head(hd, exact):
        lo = hd * HEAD_SLOT
        qh = q_ref[0, :, lo:lo + HEAD_SLOT]
        s_c = _dot_nt(qh, kc_ref[0, :, lo:lo + HEAD_SLOT])
        s_l = _dot_nt(qh, kl_ref[0, :, lo:lo + HEAD_SLOT])
        shift = jnp.max(s_c, axis=-1, keepdims=True)
        if exact:
            shift = jnp.maximum(shift, jnp.max(s_l, axis=-1, keepdims=True))
        if hd == last:
            p_c = jnp.exp2(s_c - shift)
            p_l = jnp.exp2(s_l - shift)
            p_ref[:, :n_ctx] = p_c.astype(BF16)
            p_ref[:, n_ctx:] = p_l.astype(BF16)
            return None, jnp.sum(p_c, axis=-1, keepdims=True) + jnp.sum(p_l, axis=-1, keepdims=True)
        o = _dot(jnp.exp2(s_c - shift).astype(BF16), vc_ref[0, :, lo:lo + HEAD_SLOT])
        for k0 in range(0, n_lat, PV_KEY_TILE):
            p = jnp.exp2(s_l[:, k0:k0 + PV_KEY_TILE] - shift).astype(BF16)
            o = o + _dot(p, vl_ref[0, k0:k0 + PV_KEY_TILE, lo:lo + HEAD_SLOT])
        return o, den_of(o, hd)

    def block():
        dens = []
        for pair in range(MLA_HEADS // 2):
            outs = []
            for hd in (2 * pair, 2 * pair + 1):
                o, den = head(hd, False)
                dens.append(den)
                if o is not None:
                    outs.append(o)
            if len(outs) == 2:
                vlo = pair * pair_w
                omain_ref[0, :, vlo:vlo + pair_w] = join_pair(outs[0], outs[1]).astype(BF16)
            else:
                oprev_ref[...] = outs[0]
        return dens

    def redo_block_exact():
        def one_head(hd, carry):
            lo = pl.multiple_of(hd * HEAD_SLOT, HEAD_SLOT)
            qh = q_ref[0, :, pl.ds(lo, HEAD_SLOT)]
            s_c = _dot_nt(qh, kc_ref[0, :, pl.ds(lo, HEAD_SLOT)])
            s_l = _dot_nt(qh, kl_ref[0, :, pl.ds(lo, HEAD_SLOT)])
            shift = jnp.maximum(jnp.max(s_c, axis=-1, keepdims=True), jnp.max(s_l, axis=-1, keepdims=True))
            ohead_ref[hd] = (_dot(jnp.exp2(s_c - shift).astype(BF16), vc_ref[0, :, pl.ds(lo, HEAD_SLOT)])
                             + _dot(jnp.exp2(s_l - shift).astype(BF16), vl_ref[0, :, pl.ds(lo, HEAD_SLOT)]))
            return carry

        lax.fori_loop(0, last, one_head, 0)
        for pair in range(MLA_HEADS // 2 - 1):
            vlo = pair * pair_w
            omain_ref[0, :, vlo:vlo + pair_w] = join_pair(ohead_ref[2 * pair], ohead_ref[2 * pair + 1]).astype(BF16)
        oprev_ref[...] = ohead_ref[last - 1]
        head(last, True)

    @pl.when(t < n_blocks)
    def _():
        drain()
        dens = block()
        worst = dens[0]
        for den in dens[1:]:
            worst = jnp.maximum(worst, den)
        trusted = jnp.max(worst) <= SHIFT_DEN_MAX

        @pl.when(jnp.logical_not(trusted))
        def _():
            redo_block_exact()

    @pl.when(t == n_blocks)
    def _():
        drain()


def _attention(q, kc, kl, vc, vl, tq=512):
    b, s, _ = q.shape
    c = kc.shape[1]
    nq = s // tq
    n_blocks = b * nq
    pair_w = 2 * V_HEAD_DIM
    main_w = ATTN_WIDTH - pair_w
    last = MLA_HEADS - 1
    cur = lambda t: jnp.minimum(t, n_blocks - 1)
    prev = lambda t: jnp.maximum(t - 1, 0)
    return pl.pallas_call(
        _attn_kernel,
        grid=(n_blocks + 1,),
        in_specs=[
            pl.BlockSpec((1, tq, QK_WIDTH), lambda t: (cur(t) // nq, cur(t) % nq, 0)),
            pl.BlockSpec((1, c, QK_WIDTH), lambda t: (cur(t) // nq, 0, 0)),
            pl.BlockSpec((1, s, QK_WIDTH), lambda t: (cur(t) // nq, 0, 0)),
            pl.BlockSpec((1, c, QK_WIDTH), lambda t: (cur(t) // nq, 0, 0)),
            pl.BlockSpec((1, s, QK_WIDTH), lambda t: (cur(t) // nq, 0, 0)),
            pl.BlockSpec((1, c, HEAD_SLOT), lambda t: (prev(t) // nq, 0, last)),
            pl.BlockSpec((1, s, HEAD_SLOT), lambda t: (prev(t) // nq, 0, last)),
        ],
        out_specs=[
            pl.BlockSpec((1, tq, main_w), lambda t: (cur(t) // nq, cur(t) % nq, 0)),
            pl.BlockSpec((1, tq, pair_w), lambda t: (prev(t) // nq, prev(t) % nq, 0)),
        ],
        out_shape=[
            jax.ShapeDtypeStruct((b, s, main_w), BF16),
            jax.ShapeDtypeStruct((b, s, pair_w), BF16),
        ],
        scratch_shapes=[
            pltpu.VMEM((tq, c + s), BF16),
            pltpu.VMEM((tq, pair_w), F32),
            pltpu.VMEM((MLA_HEADS - 1, tq, HEAD_SLOT), F32),
        ],
        compiler_params=pltpu.CompilerParams(
            dimension_semantics=("arbitrary",), vmem_limit_bytes=VMEM_LIMIT_BYTES),
        name="attn",
    )(q, kc, kl, vc, vl, vc, vl)


def _fourier_kernel(t_ref, tw_ref, u_ref, cc_ref, wf_ref, o_ref, tb_ref, cw_ref):
    @pl.when(pl.program_id(0) == 0)
    def _():
        tb_ref[...] = t_ref[...].astype(BF16)
        cc = cc_ref[...].astype(BF16)
        for g in range(FOURIER_GROUPS):
            cw_ref[g] = _dot(cc, wf_ref[g].astype(BF16)).astype(BF16)

    n_sub = tb_ref.shape[1]
    f = _dot(tb_ref[...], u_ref[0])
    gd = FOURIER_GROUP_DIM
    for g in range(FOURIER_GROUPS):
        gr, gi = [], []
        for r in range(FFT_RADIX):
            lo = r * FOURIER_WIDTH + g * gd
            a = f[:n_sub, lo:lo + gd]
            b = f[n_sub:, lo:lo + gd]
            if r == 0:
                gr.append(a)
                gi.append(b)
            else:
                c = tw_ref[r - 1, 0]
                s = tw_ref[r - 1, 1]
                gr.append(a * c - b * s)
                gi.append(a * s + b * c)
        z = list(zip(gr, gi))
        add = lambda a, b: (a[0] + b[0], a[1] + b[1])
        sub = lambda a, b: (a[0] - b[0], a[1] - b[1])
        times_minus_i = lambda a: (-a[1], a[0])

        def dft4(c):
            e0, e1 = add(c[0], c[2]), sub(c[0], c[2])
            f0, f1 = add(c[1], c[3]), times_minus_i(sub(c[1], c[3]))
            return [add(e0, f0), add(e1, f1), sub(e0, f0), sub(e1, f1)]

        half = FFT_RADIX // 2
        even = dft4([add(z[r], z[r + half]) for r in range(half)])
        d = [sub(z[r], z[r + half]) for r in range(half)]
        rt = np.float32(np.sqrt(0.5))
        odd = dft4([
            d[0],
            ((d[1][0] - d[1][1]) * rt, (d[1][0] + d[1][1]) * rt),
            times_minus_i(d[2]),
            (-(d[3][0] + d[3][1]) * rt, (d[3][0] - d[3][1]) * rt),
        ])
        y = [even[q // 2] if q % 2 == 0 else odd[q // 2] for q in range(FFT_RADIX)]
        xr = jnp.concatenate([v[0] for v in y], axis=0)
        xi = jnp.concatenate([v[1] for v in y], axis=0)
        lhs = jnp.concatenate([xr, xi], axis=1).astype(BF16)
        o_ref[0, :, g * gd:(g + 1) * gd] = _dot(lhs, cw_ref[g]).astype(BF16)


def _fourier(u4, tmat, tw, cmat, wf):
    b, n_sub, _ = u4.shape
    s = n_sub * FFT_RADIX
    full = lambda a: pl.BlockSpec(a.shape, lambda i: (0,) * a.ndim)
    return pl.pallas_call(
        _fourier_kernel,
        grid=(b,),
        in_specs=[
            full(tmat), full(tw),
            pl.BlockSpec((1, n_sub, FFT_RADIX * FOURIER_WIDTH), lambda i: (i, 0, 0)),
            full(cmat), full(wf),
        ],
        out_specs=pl.BlockSpec((1, s, FOURIER_WIDTH), lambda i: (i, 0, 0)),
        out_shape=jax.ShapeDtypeStruct((b, s, FOURIER_WIDTH), BF16),
        scratch_shapes=[
            pltpu.VMEM(tmat.shape, BF16),
            pltpu.VMEM((FOURIER_GROUPS, 2 * FOURIER_GROUP_DIM, FOURIER_GROUP_DIM), BF16),
        ],
        compiler_params=pltpu.CompilerParams(
            dimension_semantics=("arbitrary",), vmem_limit_bytes=VMEM_LIMIT_BYTES),
        name="fourier",
    )(tmat, tw, u4, cmat, wf)


def _stage_cast(src_hbm, dst_ref, stage_ref, sem_ref, chunk):
    n_chunks = src_hbm.shape[0] // chunk

    def copy(i):
        slot = i % 2
        return pltpu.make_async_copy(
            src_hbm.at[pl.ds(i * chunk, chunk)], stage_ref.at[slot, pl.ds(0, chunk)], sem_ref.at[slot])

    def body(i, carry):
        @pl.when(i + 1 < n_chunks)
        def _():
            copy(i + 1).start()

        copy(i).wait()
        start = pl.multiple_of(i * chunk, chunk)
        dst_ref[pl.ds(start, chunk), :] = stage_ref[i % 2, pl.ds(0, chunk), :].astype(BF16)
        return carry

    copy(0).start()
    lax.fori_loop(0, n_chunks, body, 0)


def _post_kernel(x_ref, four_ref, am_ref, al_ref, mod_ref, gpm_ref, gpf_ref, gqf_ref,
                 wo_hbm, wg_hbm, wu_hbm, wd_hbm, o_ref,
                 x1_ref, h2_ref, wo_ref, wg_ref, wu_ref, wd_ref, wide_stage, tall_stage, sem_ref,
                 *, blocks_per_batch):
    t = pl.program_id(0)
    n_four = four_ref.shape[2]
    mix_row = pl.ds(jnp.minimum(t, pl.num_programs(0) - 2) // blocks_per_batch, 1)
    ffn_row = pl.ds(jnp.maximum(t - 1, 0) // blocks_per_batch, 1)

    def mix():
        gt_m = mod_ref[2, mix_row, :]
        sh_f = mod_ref[3, mix_row, :]
        sc_f = mod_ref[4, mix_row, :]
        attn = jnp.concatenate([am_ref[0], al_ref[0]], axis=1)
        y = _dot(four_ref[0], wo_ref[:n_four, :]) + _dot(attn, wo_ref[n_four:, :])
        x1 = x_ref[0] + _rms(y, gt_m * gpm_ref[...])
        x1_ref[...] = x1
        h2_ref[...] = (_rms(x1, gpf_ref[...] * (1.0 + sc_f)) + sh_f).astype(BF16)

    def ffn():
        gt_f = mod_ref[5, ffn_row, :]
        h2 = h2_ref[...]
        g = _dot(h2, wg_ref[...])
        up = _dot(h2, wu_ref[...])
        act = (g / (1.0 + jnp.exp(-g)) * up).astype(BF16)
        o_ref[0] = x1_ref[...] + _rms(_dot(act, wd_ref[...]), gt_f * gqf_ref[...])

    @pl.when(t == 0)
    def _():
        _stage_cast(wo_hbm, wo_ref, tall_stage, sem_ref, STAGE_ROWS_OUT)
        mix()
        _stage_cast(wg_hbm, wg_ref, wide_stage, sem_ref, STAGE_ROWS_WIDE)
        _stage_cast(wu_hbm, wu_ref, wide_stage, sem_ref, STAGE_ROWS_WIDE)
        _stage_cast(wd_hbm, wd_ref, tall_stage, sem_ref, STAGE_ROWS_TALL)

    @pl.when(t > 0)
    def _():
        ffn()
        mix()


def _post(x, four, attn_main, attn_last, mod, gpm, gpf, gqf, wo, wg, wu, wd, tm=512):
    b, s, d = x.shape
    d_ff = wg.shape[1]
    assert wo.shape[0] % STAGE_ROWS_OUT == 0 and STAGE_ROWS_OUT <= STAGE_ROWS_TALL
    assert d % STAGE_ROWS_WIDE == 0 and d_ff % STAGE_ROWS_TALL == 0
    nj = s // tm
    n_blocks = b * nj
    cur = lambda t: jnp.minimum(t, n_blocks - 1)
    prev = lambda t: jnp.maximum(t - 1, 0)
    const = lambda shape: pl.BlockSpec(shape, lambda t: (0,) * len(shape), pipeline_mode=pl.Buffered(1))
    rows = lambda w: pl.BlockSpec((1, tm, w), lambda t: (cur(t) // nj, cur(t) % nj, 0))
    hbm = pl.BlockSpec(memory_space=pl.ANY)
    return pl.pallas_call(
        functools.partial(_post_kernel, blocks_per_batch=nj),
        grid=(n_blocks + 1,),
        in_specs=[
            rows(d), rows(FOURIER_WIDTH), rows(attn_main.shape[2]), rows(attn_last.shape[2]),
            pl.BlockSpec(mod.shape, lambda t: (0, 0, 0)),
            const(gpm.shape), const(gpf.shape), const(gqf.shape),
            hbm, hbm, hbm, hbm,
        ],
        out_specs=pl.BlockSpec((1, tm, d), lambda t: (prev(t) // nj, prev(t) % nj, 0)),
        out_shape=jax.ShapeDtypeStruct((b, s, d), F32),
        scratch_shapes=[
            pltpu.VMEM((tm, d), F32), pltpu.VMEM((tm, d), BF16),
            pltpu.VMEM(wo.shape, BF16), pltpu.VMEM(wg.shape, BF16), pltpu.VMEM(wu.shape, BF16),
            pltpu.VMEM(wd.shape, BF16),
            pltpu.VMEM((2, STAGE_ROWS_WIDE, d_ff), F32), pltpu.VMEM((2, STAGE_ROWS_TALL, d), F32),
            pltpu.SemaphoreType.DMA((2,)),
        ],
        compiler_params=pltpu.CompilerParams(
            dimension_semantics=("arbitrary",), vmem_limit_bytes=VMEM_LIMIT_BYTES),
        name="post",
    )(x, four, attn_main, attn_last, mod, gpm, gpf, gqf, wo, wg, wu, wd)


def _rope_rotate_cols(w):
    a = QK_ROPE_DIM // 2
    hf = a // 2
    blocks = []
    for s0 in (0, a):
        blocks += [-w[..., s0 + hf:s0 + a], w[..., s0:s0 + hf]]
    return jnp.concatenate(blocks, axis=-1)


def _head_slot(nope, rope):
    pad = HEAD_SLOT - QK_NOPE_DIM - QK_ROPE_DIM
    return jnp.concatenate([nope, rope, jnp.zeros(rope.shape[:-1] + (pad,), rope.dtype)], axis=-1)


def _rope_tables(n_lat, q_scale):
    t = np.arange(n_lat)
    hf = QK_ROPE_DIM // 4
    inv_freq = ROPE_BASE ** (-np.arange(hf, dtype=np.float64) / hf)
    ar = (t // GRID_W)[:, None] * inv_freq[None, :]
    ac = (t % GRID_W)[:, None] * inv_freq[None, :]
    z = np.zeros_like(ar)
    cos32 = np.concatenate([np.cos(ar), np.cos(ar), np.cos(ac), np.cos(ac)], axis=-1)
    sin32 = np.concatenate([np.sin(ar), np.sin(ar), np.sin(ac), np.sin(ac)], axis=-1)
    below32 = np.concatenate([z, np.sin(ar), z, np.sin(ac)], axis=-1)
    above32 = np.concatenate([-np.sin(ar), z, -np.sin(ac), z], axis=-1)
    pad = np.zeros((n_lat, HEAD_SLOT - QK_NOPE_DIM - QK_ROPE_DIM))
    ones = np.ones((n_lat, QK_NOPE_DIM))
    zeros = np.zeros((n_lat, QK_NOPE_DIM))
    slot = lambda nope, rope: np.concatenate([nope, rope, pad], axis=-1)
    tables = [slot(ones * q_scale, cos32 * q_scale), slot(zeros, sin32 * q_scale),
              slot(zeros, cos32), slot(zeros, below32), slot(zeros, above32)]
    return jnp.asarray(np.stack(tables).astype(np.float32))


def _dft_tables(n_pos, n_ch):
    n_sub = n_pos // FFT_RADIX
    m = np.arange(n_sub, dtype=np.int64)
    ang = 2.0 * np.pi * ((m[:, None] * m[None, :]) % n_sub) / n_sub
    tmat = np.concatenate([np.cos(ang), np.sin(ang)], axis=0).astype(np.float32)
    tw = np.zeros((FFT_RADIX - 1, 2, n_sub, FOURIER_GROUP_DIM), np.float32)
    for r in range(1, FFT_RADIX):
        a = 2.0 * np.pi * r * m / n_pos
        tw[r - 1, 0] = np.cos(a)[:, None]
        tw[r - 1, 1] = np.sin(a)[:, None]
    c = np.arange(n_ch, dtype=np.int64)
    angc = 2.0 * np.pi * ((c[:, None] * c[None, :]) % n_ch) / n_ch
    norm = 1.0 / np.sqrt(float(n_pos * n_ch))
    cmat = np.concatenate([np.cos(angc) * norm, -np.sin(angc) * norm], axis=0).astype(np.float32)
    return jnp.asarray(tmat), jnp.asarray(tw), jnp.asarray(cmat)


def kernel(x, c, ctx, c_ctx, w_ada, b_ada, g_pre_mix, g_post_mix, g_pre_ffn, g_post_ffn, w_in, g_q_a,
           w_q_b, g_kv_a, w_kv_b, w_fourier, w_out, w_gate, w_up, w_down):
    assert w_ada.shape[0] == 1, "single-layer block"
    batch, n_lat, d = x.shape

    mod = _adaln(c, c_ctx[None, :], w_ada[0], b_ada[0][None, :])

    w_in0 = w_in[0]
    w_kr = w_in0[:, ROPE_COL:]
    zeros_d = jnp.zeros((d, QK_NOPE_DIM), F32)
    kr_slot = _head_slot(zeros_d, w_kr)
    win = jnp.concatenate([w_in0[:, :ROPE_COL], kr_slot], axis=1).astype(BF16)

    wq3 = w_q_b[0].reshape(Q_LORA_RANK, MLA_HEADS, QK_NOPE_DIM + QK_ROPE_DIM)
    wq_nope, wq_rope = wq3[..., :QK_NOPE_DIM], wq3[..., QK_NOPE_DIM:]
    wq_a = _head_slot(wq_nope, wq_rope).reshape(Q_LORA_RANK, QK_WIDTH)
    wq_b = _head_slot(jnp.zeros_like(wq_nope), _rope_rotate_cols(wq_rope)).reshape(Q_LORA_RANK, QK_WIDTH)
    wq = jnp.concatenate([wq_a, wq_b], axis=1).astype(BF16)

    wkv3 = w_kv_b[0].reshape(KV_LORA_RANK, MLA_HEADS, QK_NOPE_DIM + V_HEAD_DIM)
    wk_nope, wv = wkv3[..., :QK_NOPE_DIM], wkv3[..., QK_NOPE_DIM:]
    wk_slots = _head_slot(wk_nope, jnp.zeros(wk_nope.shape[:-1] + (QK_ROPE_DIM,), F32))
    wkv = jnp.concatenate([wk_slots.reshape(KV_LORA_RANK, QK_WIDTH),
                           wv.reshape(KV_LORA_RANK, ATTN_WIDTH)], axis=1).astype(BF16)

    q_scale = float((QK_NOPE_DIM + QK_ROPE_DIM) ** -0.5 * np.log2(np.e))
    rope = _rope_tables(n_lat, q_scale)
    tmat, tw, cmat = _dft_tables(n_lat, FOURIER_GROUP_DIM)

    row2 = lambda g: g[0][None, :]
    u_f, q, k_lat, v_lat = _premix(x, mod, row2(g_pre_mix), win, row2(g_q_a), wq, row2(g_kv_a), wkv,
                                   rope)
    four = _fourier(u_f, tmat, tw, cmat, w_fourier[0])
    k_ctx, v_ctx = _ctxkv(ctx, mod, row2(g_pre_mix), win, row2(g_kv_a), wkv)
    attn_main, attn_last = _attention(q, k_ctx, k_lat, v_ctx, v_lat)
    return _post(x, four, attn_main, attn_last, mod, row2(g_post_mix), row2(g_pre_ffn), row2(g_post_ffn),
                 w_out[0], w_gate[0], w_up[0], w_down[0])
```

```python
import functools

import numpy as np
import jax
import jax.numpy as jnp
from jax import lax
from jax.experimental import pallas as pl
from jax.experimental.pallas import tpu as pltpu

F32 = jnp.float32
BF16 = jnp.bfloat16

D_MODEL = 1024
GRID_W = 64
FOURIER_GROUPS = 4
FOURIER_GROUP_DIM = 128
FOURIER_WIDTH = FOURIER_GROUPS * FOURIER_GROUP_DIM
MLA_HEADS = 8
QK_NOPE_DIM = 64
QK_ROPE_DIM = 32
V_HEAD_DIM = 64
Q_LORA_RANK = 256
KV_LORA_RANK = 128
KV_COL = FOURIER_WIDTH + Q_LORA_RANK
ROPE_COL = KV_COL + KV_LORA_RANK
ROPE_BASE = 10000.0
NORM_EPS = 1e-6
FFT_RADIX = 8
HEAD_SLOT = 128
STAGE_ROWS_WIDE = 256
STAGE_ROWS_TALL = 704
STAGE_ROWS_OUT = 512
SHIFT_DEN_MAX = 2.0 ** 40
U_RING = 3
PV_KEY_TILE = 256
ATTN_WIDTH = MLA_HEADS * V_HEAD_DIM
QK_WIDTH = MLA_HEADS * HEAD_SLOT

VMEM_LIMIT_BYTES = 56 * 1024 * 1024


def _rms(x, g):
    return x * lax.rsqrt(jnp.mean(x * x, axis=-1, keepdims=True) + NORM_EPS) * g


def _dot(a, b):
    return jnp.dot(a, b, preferred_element_type=F32)


def _dot_nt(a, b):
    return lax.dot_general(a, b, (((1,), (1,)), ((), ())), preferred_element_type=F32)


def _rotary(a, table, first):
    half = QK_ROPE_DIM // 4
    below = pltpu.roll(a, half, axis=1)
    above = pltpu.roll(a, HEAD_SLOT - half, axis=1)
    return a * table(first) + below * table(first + 1) + above * table(first + 2)


def _store_value_slots(v_ref, v):
    pair_w = 2 * V_HEAD_DIM
    lower = lax.broadcasted_iota(jnp.int32, (v.shape[0], pair_w), 1) < V_HEAD_DIM
    for pair in range(MLA_HEADS // 2):
        vp = v[:, pair * pair_w:(pair + 1) * pair_w]
        lo = 2 * pair * HEAD_SLOT
        v_ref[0, :, lo:lo + HEAD_SLOT] = jnp.where(lower, vp, 1.0).astype(BF16)
        v_ref[0, :, lo + HEAD_SLOT:lo + 2 * HEAD_SLOT] = jnp.where(lower, 1.0, vp).astype(BF16)


def _adaln_kernel(c_ref, cctx_ref, w_ref, b_ref, o_ref, rows_ref):
    n = c_ref.shape[0]
    rows_ref[:n, :] = c_ref[...]
    rows_ref[n:, :] = jnp.broadcast_to(cctx_ref[...], (rows_ref.shape[0] - n, rows_ref.shape[1]))
    c = rows_ref[...]
    a = c / (1.0 + jnp.exp(-c))
    o_ref[0] = _dot(a.astype(BF16), w_ref[...].astype(BF16)) + b_ref[...]


def _adaln(c, c_ctx, w_ada, b_ada):
    batch, d = c.shape
    assert batch % 8 == 0, "the context rows start on a sublane boundary"
    rows = batch + 8
    n_chunks = w_ada.shape[1] // d
    return pl.pallas_call(
        _adaln_kernel,
        grid=(n_chunks,),
        in_specs=[
            pl.BlockSpec((batch, d), lambda j: (0, 0)),
            pl.BlockSpec((1, d), lambda j: (0, 0)),
            pl.BlockSpec((d, d), lambda j: (0, j)),
            pl.BlockSpec((1, d), lambda j: (0, j)),
        ],
        out_specs=pl.BlockSpec((1, rows, d), lambda j: (j, 0, 0)),
        out_shape=jax.ShapeDtypeStruct((n_chunks, rows, d), F32),
        scratch_shapes=[pltpu.VMEM((rows, d), F32)],
        compiler_params=pltpu.CompilerParams(vmem_limit_bytes=VMEM_LIMIT_BYTES),
        name="adaln",
    )(c, c_ctx, w_ada, b_ada)


def _premix_kernel(x_ref, mod_ref, gpre_ref, win_ref, gq_ref, wq_ref, gkv_ref, wkv_ref,
                   rope_ref, u_ref, q_ref, k_ref, v_ref, us_ref):
    x = x_ref[0]
    row = pl.ds(pl.program_id(0), 1)
    shift = mod_ref[0, row, :]
    scale = mod_ref[1, row, :]
    h = _rms(x, gpre_ref[...] * (1.0 + scale)) + shift
    p = _dot(h.astype(BF16), win_ref[...])
    sub = us_ref.shape[1] // FFT_RADIX
    for g in range(FOURIER_GROUPS):
        glo = g * FOURIER_GROUP_DIM
        us_ref[g] = p[:, glo:glo + FOURIER_GROUP_DIM]
        for r in range(FFT_RADIX):
            lo = r * FOURIER_WIDTH + glo
            u_ref[0, :, lo:lo + FOURIER_GROUP_DIM] = (
                us_ref[g, pl.ds(r, sub, stride=FFT_RADIX), :].astype(BF16))

    qn = _rms(p[:, FOURIER_WIDTH:KV_COL], gq_ref[...]).astype(BF16)
    qq = _dot(qn, wq_ref[...])
    rows = x_ref.shape[1]
    row0 = pl.multiple_of(pl.program_id(1) * rows, rows)
    table = lambda k: rope_ref[k, pl.ds(row0, rows), :]
    cosq = table(0)
    sinq = table(1)
    for hd in range(MLA_HEADS):
        lo = hd * HEAD_SLOT
        q_ref[0, :, lo:lo + HEAD_SLOT] = (
            qq[:, lo:lo + HEAD_SLOT] * cosq + qq[:, QK_WIDTH + lo:QK_WIDTH + lo + HEAD_SLOT] * sinq
        ).astype(BF16)

    kvn = _rms(p[:, KV_COL:ROPE_COL], gkv_ref[...]).astype(BF16)
    kv = _dot(kvn, wkv_ref[...])
    kr = _rotary(p[:, ROPE_COL:ROPE_COL + HEAD_SLOT], table, 2)
    for hd in range(MLA_HEADS):
        lo = hd * HEAD_SLOT
        k_ref[0, :, lo:lo + HEAD_SLOT] = (kv[:, lo:lo + HEAD_SLOT] + kr).astype(BF16)
    _store_value_slots(v_ref, kv[:, QK_WIDTH:])


def _premix(x, mod, gpre, win, gq, wq, gkv, wkv, rope, tm=1024):
    b, s, d = x.shape
    const = lambda shape: pl.BlockSpec(shape, lambda i, j: (0,) * len(shape))
    rows = lambda w: pl.BlockSpec((1, tm, w), lambda i, j: (i, j, 0))
    return pl.pallas_call(
        _premix_kernel,
        grid=(b, s // tm),
        in_specs=[
            rows(d),
            pl.BlockSpec(mod.shape, lambda i, j: (0, 0, 0)),
            const(gpre.shape), const(win.shape), const(gq.shape), const(wq.shape),
            const(gkv.shape), const(wkv.shape),
            pl.BlockSpec(rope.shape, lambda i, j: (0, 0, 0), pipeline_mode=pl.Buffered(1)),
        ],
        out_specs=[
            pl.BlockSpec((1, tm // FFT_RADIX, FFT_RADIX * FOURIER_WIDTH), lambda i, j: (i, j, 0)),
            rows(QK_WIDTH), rows(QK_WIDTH), rows(QK_WIDTH)],
        out_shape=[
            jax.ShapeDtypeStruct((b, s // FFT_RADIX, FFT_RADIX * FOURIER_WIDTH), BF16),
            jax.ShapeDtypeStruct((b, s, QK_WIDTH), BF16),
            jax.ShapeDtypeStruct((b, s, QK_WIDTH), BF16),
            jax.ShapeDtypeStruct((b, s, QK_WIDTH), BF16),
        ],
        scratch_shapes=[pltpu.VMEM((FOURIER_GROUPS, tm, FOURIER_GROUP_DIM), F32)],
        compiler_params=pltpu.CompilerParams(vmem_limit_bytes=VMEM_LIMIT_BYTES),
        name="premix",
    )(x, mod, gpre, win, gq, wq, gkv, wkv, rope)


def _ctxkv_kernel(x_ref, mod_ref, gpre_ref, win_ref, gkv_ref, wkv_ref, k_ref, v_ref):
    x = x_ref[0]
    shift = mod_ref[0, 0:1, :]
    scale = mod_ref[1, 0:1, :]
    h = _rms(x, gpre_ref[...] * (1.0 + scale)) + shift
    p = _dot(h.astype(BF16), win_ref[...])
    kvn = _rms(p[:, :KV_LORA_RANK], gkv_ref[...]).astype(BF16)
    kv = _dot(kvn, wkv_ref[...])
    kr = p[:, KV_LORA_RANK:]
    for hd in range(MLA_HEADS):
        lo = hd * HEAD_SLOT
        k_ref[0, :, lo:lo + HEAD_SLOT] = (kv[:, lo:lo + HEAD_SLOT] + kr).astype(BF16)
    _store_value_slots(v_ref, kv[:, QK_WIDTH:])


def _ctxkv(ctx, mod, gpre, win, gkv, wkv):
    b, c, d = ctx.shape
    kv_cols = win.shape[1] - KV_COL
    assert KV_COL % kv_cols == 0
    const = lambda shape: pl.BlockSpec(shape, lambda i: (0,) * len(shape))
    rows = lambda w: pl.BlockSpec((1, c, w), lambda i: (i, 0, 0))
    return pl.pallas_call(
        _ctxkv_kernel,
        grid=(b,),
        in_specs=[
            rows(d),
            pl.BlockSpec((mod.shape[0], 8, d), lambda i: (0, b // 8, 0)),
            const(gpre.shape),
            pl.BlockSpec((d, kv_cols), lambda i: (0, KV_COL // kv_cols)),
            const(gkv.shape), const(wkv.shape),
        ],
        out_specs=[rows(QK_WIDTH), rows(QK_WIDTH)],
        out_shape=[
            jax.ShapeDtypeStruct((b, c, QK_WIDTH), BF16),
            jax.ShapeDtypeStruct((b, c, QK_WIDTH), BF16),
        ],
        name="ctxkv",
    )(ctx, mod, gpre, win, gkv, wkv)


def _attn_kernel(q_ref, kc_ref, kl_ref, vc_ref, vl_ref, vcp_ref, vlp_ref, omain_ref, olast_ref,
                 p_ref, oprev_ref, ohead_ref):
    t = pl.program_id(0)
    n_blocks = pl.num_programs(0) - 1
    n_ctx = kc_ref.shape[1]
    n_lat = kl_ref.shape[1]
    tq = q_ref.shape[1]
    pair_w = 2 * V_HEAD_DIM
    first_half = lax.broadcasted_iota(jnp.int32, (tq, pair_w), 1) < V_HEAD_DIM
    last = MLA_HEADS - 1

    @pl.when(t == 0)
    def _():
        p_ref[...] = jnp.ones_like(p_ref)
        oprev_ref[...] = jnp.ones_like(oprev_ref)

    def join_pair(o_even, o_odd):
        num = jnp.where(first_half, o_even, o_odd)
        den = pltpu.roll(jnp.where(first_half, o_odd, o_even), V_HEAD_DIM, axis=1)
        return num / den

    def den_of(o, hd):
        lane = V_HEAD_DIM if hd % 2 == 0 else 0
        return o[:, lane:lane + 1]

    def drain():
        o = _dot(p_ref[:, :n_ctx], vcp_ref[0]) + _dot(p_ref[:, n_ctx:], vlp_ref[0])
        olast_ref[0] = join_pair(oprev_ref[...], o).astype(BF16)

    def head(hd, exact):
        lo = hd * HEAD_SLOT
        qh = q_ref[0, :, lo:lo + HEAD_SLOT]
        s_c = _dot_nt(qh, kc_ref[0, :, lo:lo + HEAD_SLOT])
        s_l = _dot_nt(qh, kl_ref[0, :, lo:lo + HEAD_SLOT])
        shift = jnp.max(s_c, axis=-1, keepdims=True)
        if exact:
            shift = jnp.maximum(shift, jnp.max(s_l, axis=-1, keepdims=True))
        if hd == last:
            p_c = jnp.exp2(s_c - shift)
            p_l = jnp.exp2(s_l - shift)
            p_ref[:, :n_ctx] = p_c.astype(BF16)
            p_ref[:, n_ctx:] = p_l.astype(BF16)
            return None, jnp.sum(p_c, axis=-1, keepdims=True) + jnp.sum(p_l, axis=-1, keepdims=True)
        o = _dot(jnp.exp2(s_c - shift).astype(BF16), vc_ref[0, :, lo:lo + HEAD_SLOT])
        for k0 in range(0, n_lat, PV_KEY_TILE):
            p = jnp.exp2(s_l[:, k0:k0 + PV_KEY_TILE] - shift).astype(BF16)
            o = o + _dot(p, vl_ref[0, k0:k0 + PV_KEY_TILE, lo:lo + HEAD_SLOT])
        return o, den_of(o, hd)

    def block():
        dens = []
        for pair in range(MLA_HEADS // 2):
            outs = []
            for hd in (2 * pair, 2 * pair + 1):
                o, den = head(hd, False)
                dens.append(den)
                if o is not None:
                    outs.append(o)
            if len(outs) == 2:
                vlo = pair * pair_w
                omain_ref[0, :, vlo:vlo + pair_w] = join_pair(outs[0], outs[1]).astype(BF16)
            else:
                oprev_ref[...] = outs[0]
        return dens

    def redo_block_exact():
        def one_head(hd, carry):
            lo = pl.multiple_of(hd * HEAD_SLOT, HEAD_SLOT)
            qh = q_ref[0, :, pl.ds(lo, HEAD_SLOT)]
            s_c = _dot_nt(qh, kc_ref[0, :, pl.ds(lo, HEAD_SLOT)])
            s_l = _dot_nt(qh, kl_ref[0, :, pl.ds(lo, HEAD_SLOT)])
            shift = jnp.maximum(jnp.max(s_c, axis=-1, keepdims=True), jnp.max(s_l, axis=-1, keepdims=True))
            ohead_ref[hd] = (_dot(jnp.exp2(s_c - shift).astype(BF16), vc_ref[0, :, pl.ds(lo, HEAD_SLOT)])
                             + _dot(jnp.exp2(s_l - shift).astype(BF16), vl_ref[0, :, pl.ds(lo, HEAD_SLOT)]))
            return carry

        lax.fori_loop(0, last, one_head, 0)
        for pair in range(MLA_HEADS // 2 - 1):
            vlo = pair * pair_w
            omain_ref[0, :, vlo:vlo + pair_w] = join_pair(ohead_ref[2 * pair], ohead_ref[2 * pair + 1]).astype(BF16)
        oprev_ref[...] = ohead_ref[last - 1]
        head(last, True)

    @pl.when(t < n_blocks)
    def _():
        drain()
        dens = block()
        worst = dens[0]
        for den in dens[1:]:
            worst = jnp.maximum(worst, den)
        trusted = jnp.max(worst) <= SHIFT_DEN_MAX

        @pl.when(jnp.logical_not(trusted))
        def _():
            redo_block_exact()

    @pl.when(t == n_blocks)
    def _():
        drain()


def _attention(q, kc, kl, vc, vl, tq=512):
    b, s, _ = q.shape
    c = kc.shape[1]
    nq = s // tq
    n_blocks = b * nq
    pair_w = 2 * V_HEAD_DIM
    main_w = ATTN_WIDTH - pair_w
    last = MLA_HEADS - 1
    cur = lambda t: jnp.minimum(t, n_blocks - 1)
    prev = lambda t: jnp.maximum(t - 1, 0)
    return pl.pallas_call(
        _attn_kernel,
        grid=(n_blocks + 1,),
        in_specs=[
            pl.BlockSpec((1, tq, QK_WIDTH), lambda t: (cur(t) // nq, cur(t) % nq, 0)),
            pl.BlockSpec((1, c, QK_WIDTH), lambda t: (cur(t) // nq, 0, 0)),
            pl.BlockSpec((1, s, QK_WIDTH), lambda t: (cur(t) // nq, 0, 0)),
            pl.BlockSpec((1, c, QK_WIDTH), lambda t: (cur(t) // nq, 0, 0)),
            pl.BlockSpec((1, s, QK_WIDTH), lambda t: (cur(t) // nq, 0, 0)),
            pl.BlockSpec((1, c, HEAD_SLOT), lambda t: (prev(t) // nq, 0, last)),
            pl.BlockSpec((1, s, HEAD_SLOT), lambda t: (prev(t) // nq, 0, last)),
        ],
        out_specs=[
            pl.BlockSpec((1, tq, main_w), lambda t: (cur(t) // nq, cur(t) % nq, 0)),
            pl.BlockSpec((1, tq, pair_w), lambda t: (prev(t) // nq, prev(t) % nq, 0)),
        ],
        out_shape=[
            jax.ShapeDtypeStruct((b, s, main_w), BF16),
            jax.ShapeDtypeStruct((b, s, pair_w), BF16),
        ],
        scratch_shapes=[
            pltpu.VMEM((tq, c + s), BF16),
            pltpu.VMEM((tq, pair_w), F32),
            pltpu.VMEM((MLA_HEADS - 1, tq, HEAD_SLOT), F32),
        ],
        compiler_params=pltpu.CompilerParams(
            dimension_semantics=("arbitrary",), vmem_limit_bytes=VMEM_LIMIT_BYTES),
        name="attn",
    )(q, kc, kl, vc, vl, vc, vl)


def _fourier_kernel(t_ref, tw_ref, u_hbm, cc_ref, wf_ref, o_ref, tb_ref, cw_ref, ubuf_ref, usem_ref):
    step = pl.program_id(0)
    n_steps = pl.num_programs(0)

    def fetch(batch, slot):
        return pltpu.make_async_copy(u_hbm.at[batch], ubuf_ref.at[slot], usem_ref.at[slot])

    @pl.when(step == 0)
    def _():
        for b in range(U_RING - 1):
            fetch(b, b).start()
        tb_ref[...] = t_ref[...].astype(BF16)
        cc = cc_ref[...].astype(BF16)
        for g in range(FOURIER_GROUPS):
            cw_ref[g] = _dot(cc, wf_ref[g].astype(BF16)).astype(BF16)

    ahead = step + (U_RING - 1)

    @pl.when(ahead < n_steps)
    def _():
        fetch(ahead, lax.rem(ahead, U_RING)).start()

    slot = lax.rem(step, U_RING)
    fetch(step, slot).wait()

    n_sub = tb_ref.shape[1]
    f = _dot(tb_ref[...], ubuf_ref[slot])
    gd = FOURIER_GROUP_DIM
    for g in range(FOURIER_GROUPS):
        gr, gi = [], []
        for r in range(FFT_RADIX):
            lo = r * FOURIER_WIDTH + g * gd
            a = f[:n_sub, lo:lo + gd]
            b = f[n_sub:, lo:lo + gd]
            if r == 0:
                gr.append(a)
                gi.append(b)
            else:
                c = tw_ref[r - 1, 0]
                s = tw_ref[r - 1, 1]
                gr.append(a * c - b * s)
                gi.append(a * s + b * c)
        z = list(zip(gr, gi))
        add = lambda a, b: (a[0] + b[0], a[1] + b[1])
        sub = lambda a, b: (a[0] - b[0], a[1] - b[1])
        times_minus_i = lambda a: (-a[1], a[0])

        def dft4(c):
            e0, e1 = add(c[0], c[2]), sub(c[0], c[2])
            f0, f1 = add(c[1], c[3]), times_minus_i(sub(c[1], c[3]))
            return [add(e0, f0), add(e1, f1), sub(e0, f0), sub(e1, f1)]

        half = FFT_RADIX // 2
        even = dft4([add(z[r], z[r + half]) for r in range(half)])
        d = [sub(z[r], z[r + half]) for r in range(half)]
        rt = np.float32(np.sqrt(0.5))
        odd = dft4([
            d[0],
            ((d[1][0] - d[1][1]) * rt, (d[1][0] + d[1][1]) * rt),
            times_minus_i(d[2]),
            (-(d[3][0] + d[3][1]) * rt, (d[3][0] - d[3][1]) * rt),
        ])
        y = [even[q // 2] if q % 2 == 0 else odd[q // 2] for q in range(FFT_RADIX)]
        xr = jnp.concatenate([v[0] for v in y], axis=0)
        xi = jnp.concatenate([v[1] for v in y], axis=0)
        lhs = jnp.concatenate([xr, xi], axis=1).astype(BF16)
        o_ref[0, :, g * gd:(g + 1) * gd] = _dot(lhs, cw_ref[g]).astype(BF16)


def _fourier(u4, tmat, tw, cmat, wf):
    b, n_sub, _ = u4.shape
    assert b >= U_RING - 1
    s = n_sub * FFT_RADIX
    full = lambda a: pl.BlockSpec(a.shape, lambda i: (0,) * a.ndim)
    return pl.pallas_call(
        _fourier_kernel,
        grid=(b,),
        in_specs=[
            full(tmat), full(tw),
            pl.BlockSpec(memory_space=pl.ANY),
            full(cmat), full(wf),
        ],
        out_specs=pl.BlockSpec((1, s, FOURIER_WIDTH), lambda i: (i, 0, 0)),
        out_shape=jax.ShapeDtypeStruct((b, s, FOURIER_WIDTH), BF16),
        scratch_shapes=[
            pltpu.VMEM(tmat.shape, BF16),
            pltpu.VMEM((FOURIER_GROUPS, 2 * FOURIER_GROUP_DIM, FOURIER_GROUP_DIM), BF16),
            pltpu.VMEM((U_RING,) + u4.shape[1:], BF16),
            pltpu.SemaphoreType.DMA((U_RING,)),
        ],
        compiler_params=pltpu.CompilerParams(
            dimension_semantics=("arbitrary",), vmem_limit_bytes=VMEM_LIMIT_BYTES),
        name="fourier",
    )(tmat, tw, u4, cmat, wf)


def _stage_cast(src_hbm, dst_ref, stage_ref, sem_ref, chunk):
    n_chunks = src_hbm.shape[0] // chunk

    def copy(i):
        slot = i % 2
        return pltpu.make_async_copy(
            src_hbm.at[pl.ds(i * chunk, chunk)], stage_ref.at[slot, pl.ds(0, chunk)], sem_ref.at[slot])

    def body(i, carry):
        @pl.when(i + 1 < n_chunks)
        def _():
            copy(i + 1).start()

        copy(i).wait()
        start = pl.multiple_of(i * chunk, chunk)
        dst_ref[pl.ds(start, chunk), :] = stage_ref[i % 2, pl.ds(0, chunk), :].astype(BF16)
        return carry

    copy(0).start()
    lax.fori_loop(0, n_chunks, body, 0)


def _post_kernel(x_ref, four_ref, am_ref, al_ref, mod_ref, gpm_ref, gpf_ref, gqf_ref,
                 wo_hbm, wg_hbm, wu_hbm, wd_hbm, o_ref,
                 x1_ref, h2_ref, wo_ref, wg_ref, wu_ref, wd_ref, wide_stage, tall_stage, sem_ref,
                 *, blocks_per_batch):
    t = pl.program_id(0)
    n_four = four_ref.shape[2]
    mix_row = pl.ds(jnp.minimum(t, pl.num_programs(0) - 2) // blocks_per_batch, 1)
    ffn_row = pl.ds(jnp.maximum(t - 1, 0) // blocks_per_batch, 1)

    def mix():
        gt_m = mod_ref[2, mix_row, :]
        sh_f = mod_ref[3, mix_row, :]
        sc_f = mod_ref[4, mix_row, :]
        attn = jnp.concatenate([am_ref[0], al_ref[0]], axis=1)
        y = _dot(four_ref[0], wo_ref[:n_four, :]) + _dot(attn, wo_ref[n_four:, :])
        x1 = x_ref[0] + _rms(y, gt_m * gpm_ref[...])
        x1_ref[...] = x1
        h2_ref[...] = (_rms(x1, gpf_ref[...] * (1.0 + sc_f)) + sh_f).astype(BF16)

    def ffn():
        gt_f = mod_ref[5, ffn_row, :]
        h2 = h2_ref[...]
        g = _dot(h2, wg_ref[...])
        up = _dot(h2, wu_ref[...])
        act = (g / (1.0 + jnp.exp(-g)) * up).astype(BF16)
        o_ref[0] = x1_ref[...] + _rms(_dot(act, wd_ref[...]), gt_f * gqf_ref[...])

    @pl.when(t == 0)
    def _():
        _stage_cast(wo_hbm, wo_ref, tall_stage, sem_ref, STAGE_ROWS_OUT)
        mix()
        _stage_cast(wg_hbm, wg_ref, wide_stage, sem_ref, STAGE_ROWS_WIDE)
        _stage_cast(wu_hbm, wu_ref, wide_stage, sem_ref, STAGE_ROWS_WIDE)
        _stage_cast(wd_hbm, wd_ref, tall_stage, sem_ref, STAGE_ROWS_TALL)

    @pl.when(t > 0)
    def _():
        ffn()
        mix()


def _post(x, four, attn_main, attn_last, mod, gpm, gpf, gqf, wo, wg, wu, wd, tm=512):
    b, s, d = x.shape
    d_ff = wg.shape[1]
    assert wo.shape[0] % STAGE_ROWS_OUT == 0 and STAGE_ROWS_OUT <= STAGE_ROWS_TALL
    assert d % STAGE_ROWS_WIDE == 0 and d_ff % STAGE_ROWS_TALL == 0
    nj = s // tm
    n_blocks = b * nj
    cur = lambda t: jnp.minimum(t, n_blocks - 1)
    prev = lambda t: jnp.maximum(t - 1, 0)
    const = lambda shape: pl.BlockSpec(shape, lambda t: (0,) * len(shape), pipeline_mode=pl.Buffered(1))
    rows = lambda w: pl.BlockSpec((1, tm, w), lambda t: (cur(t) // nj, cur(t) % nj, 0))
    hbm = pl.BlockSpec(memory_space=pl.ANY)
    return pl.pallas_call(
        functools.partial(_post_kernel, blocks_per_batch=nj),
        grid=(n_blocks + 1,),
        in_specs=[
            rows(d), rows(FOURIER_WIDTH), rows(attn_main.shape[2]), rows(attn_last.shape[2]),
            pl.BlockSpec(mod.shape, lambda t: (0, 0, 0)),
            const(gpm.shape), const(gpf.shape), const(gqf.shape),
            hbm, hbm, hbm, hbm,
        ],
        out_specs=pl.BlockSpec((1, tm, d), lambda t: (prev(t) // nj, prev(t) % nj, 0)),
        out_shape=jax.ShapeDtypeStruct((b, s, d), F32),
        scratch_shapes=[
            pltpu.VMEM((tm, d), F32), pltpu.VMEM((tm, d), BF16),
            pltpu.VMEM(wo.shape, BF16), pltpu.VMEM(wg.shape, BF16), pltpu.VMEM(wu.shape, BF16),
            pltpu.VMEM(wd.shape, BF16),
            pltpu.VMEM((2, STAGE_ROWS_WIDE, d_ff), F32), pltpu.VMEM((2, STAGE_ROWS_TALL, d), F32),
            pltpu.SemaphoreType.DMA((2,)),
        ],
        compiler_params=pltpu.CompilerParams(
            dimension_semantics=("arbitrary",), vmem_limit_bytes=VMEM_LIMIT_BYTES),
        name="post",
    )(x, four, attn_main, attn_last, mod, gpm, gpf, gqf, wo, wg, wu, wd)


def _rope_rotate_cols(w):
    a = QK_ROPE_DIM // 2
    hf = a // 2
    blocks = []
    for s0 in (0, a):
        blocks += [-w[..., s0 + hf:s0 + a], w[..., s0:s0 + hf]]
    return jnp.concatenate(blocks, axis=-1)


def _head_slot(nope, rope):
    pad = HEAD_SLOT - QK_NOPE_DIM - QK_ROPE_DIM
    return jnp.concatenate([nope, rope, jnp.zeros(rope.shape[:-1] + (pad,), rope.dtype)], axis=-1)


def _rope_tables(n_lat, q_scale):
    t = np.arange(n_lat)
    hf = QK_ROPE_DIM // 4
    inv_freq = ROPE_BASE ** (-np.arange(hf, dtype=np.float64) / hf)
    ar = (t // GRID_W)[:, None] * inv_freq[None, :]
    ac = (t % GRID_W)[:, None] * inv_freq[None, :]
    z = np.zeros_like(ar)
    cos32 = np.concatenate([np.cos(ar), np.cos(ar), np.cos(ac), np.cos(ac)], axis=-1)
    sin32 = np.concatenate([np.sin(ar), np.sin(ar), np.sin(ac), np.sin(ac)], axis=-1)
    below32 = np.concatenate([z, np.sin(ar), z, np.sin(ac)], axis=-1)
    above32 = np.concatenate([-np.sin(ar), z, -np.sin(ac), z], axis=-1)
    pad = np.zeros((n_lat, HEAD_SLOT - QK_NOPE_DIM - QK_ROPE_DIM))
    ones = np.ones((n_lat, QK_NOPE_DIM))
    zeros = np.zeros((n_lat, QK_NOPE_DIM))
    slot = lambda nope, rope: np.concatenate([nope, rope, pad], axis=-1)
    tables = [slot(ones * q_scale, cos32 * q_scale), slot(zeros, sin32 * q_scale),
              slot(zeros, cos32), slot(zeros, below32), slot(zeros, above32)]
    return jnp.asarray(np.stack(tables).astype(np.float32))


def _dft_tables(n_pos, n_ch):
    n_sub = n_pos // FFT_RADIX
    m = np.arange(n_sub, dtype=np.int64)
    ang = 2.0 * np.pi * ((m[:, None] * m[None, :]) % n_sub) / n_sub
    tmat = np.concatenate([np.cos(ang), np.sin(ang)], axis=0).astype(np.float32)
    tw = np.zeros((FFT_RADIX - 1, 2, n_sub, FOURIER_GROUP_DIM), np.float32)
    for r in range(1, FFT_RADIX):
        a = 2.0 * np.pi * r * m / n_pos
        tw[r - 1, 0] = np.cos(a)[:, None]
        tw[r - 1, 1] = np.sin(a)[:, None]
    c = np.arange(n_ch, dtype=np.int64)
    angc = 2.0 * np.pi * ((c[:, None] * c[None, :]) % n_ch) / n_ch
    norm = 1.0 / np.sqrt(float(n_pos * n_ch))
    cmat = np.concatenate([np.cos(angc) * norm, -np.sin(angc) * norm], axis=0).astype(np.float32)
    return jnp.asarray(tmat), jnp.asarray(tw), jnp.asarray(cmat)


def kernel(x, c, ctx, c_ctx, w_ada, b_ada, g_pre_mix, g_post_mix, g_pre_ffn, g_post_ffn, w_in, g_q_a,
           w_q_b, g_kv_a, w_kv_b, w_fourier, w_out, w_gate, w_up, w_down):
    assert w_ada.shape[0] == 1, "single-layer block"
    batch, n_lat, d = x.shape

    mod = _adaln(c, c_ctx[None, :], w_ada[0], b_ada[0][None, :])

    w_in0 = w_in[0]
    w_kr = w_in0[:, ROPE_COL:]
    zeros_d = jnp.zeros((d, QK_NOPE_DIM), F32)
    kr_slot = _head_slot(zeros_d, w_kr)
    win = jnp.concatenate([w_in0[:, :ROPE_COL], kr_slot], axis=1).astype(BF16)

    wq3 = w_q_b[0].reshape(Q_LORA_RANK, MLA_HEADS, QK_NOPE_DIM + QK_ROPE_DIM)
    wq_nope, wq_rope = wq3[..., :QK_NOPE_DIM], wq3[..., QK_NOPE_DIM:]
    wq_a = _head_slot(wq_nope, wq_rope).reshape(Q_LORA_RANK, QK_WIDTH)
    wq_b = _head_slot(jnp.zeros_like(wq_nope), _rope_rotate_cols(wq_rope)).reshape(Q_LORA_RANK, QK_WIDTH)
    wq = jnp.concatenate([wq_a, wq_b], axis=1).astype(BF16)

    wkv3 = w_kv_b[0].reshape(KV_LORA_RANK, MLA_HEADS, QK_NOPE_DIM + V_HEAD_DIM)
    wk_nope, wv = wkv3[..., :QK_NOPE_DIM], wkv3[..., QK_NOPE_DIM:]
    wk_slots = _head_slot(wk_nope, jnp.zeros(wk_nope.shape[:-1] + (QK_ROPE_DIM,), F32))
    wkv = jnp.concatenate([wk_slots.reshape(KV_LORA_RANK, QK_WIDTH),
                           wv.reshape(KV_LORA_RANK, ATTN_WIDTH)], axis=1).astype(BF16)

    q_scale = float((QK_NOPE_DIM + QK_ROPE_DIM) ** -0.5 * np.log2(np.e))
    rope = _rope_tables(n_lat, q_scale)
    tmat, tw, cmat = _dft_tables(n_lat, FOURIER_GROUP_DIM)

    row2 = lambda g: g[0][None, :]
    u_f, q, k_lat, v_lat = _premix(x, mod, row2(g_pre_mix), win, row2(g_q_a), wq, row2(g_kv_a), wkv,
                                   rope)
    four = _fourier(u_f, tmat, tw, cmat, w_fourier[0])
    k_ctx, v_ctx = _ctxkv(ctx, mod, row2(g_pre_mix), win, row2(g_kv_a), wkv)
    attn_main, attn_last = _attention(q, k_ctx, k_lat, v_ctx, v_lat)
    return _post(x, four, attn_main, attn_last, mod, row2(g_post_mix), row2(g_pre_ffn), row2(g_post_ffn),
                 w_out[0], w_gate[0], w_up[0], w_down[0])
```
